```python
import math
import jax
import jax.numpy as jnp
from jax import lax
import numpy as np

D_MODEL = 2048
BATCH = 2
SEQ = 16384
DEPTH = 4

MLSTM_HEADS = 4
MLSTM_HEAD_DIM = 256
MLSTM_WIDTH = MLSTM_HEADS * MLSTM_HEAD_DIM
MLSTM_CHUNK = 64
CONV_WIDTH = 4
ATTN_GROUPS = ((128, 1), (512, 4), (2048, 16))
ATTN_HEADS = 8
ATTN_HEAD_DIM = 128
ATTN_WIDTH = ATTN_HEADS * ATTN_HEAD_DIM
ATTN_HEADS_TOTAL = len(ATTN_GROUPS) * ATTN_HEADS
ATTN_BLOCK = 128
REL_BUCKETS = 32
REL_MAX_DISTANCE = 2048
S5_WIDTH = 1024
S5_GROUP = 16
S5_GROUPS = S5_WIDTH // S5_GROUP
S5_STATE = 64
FF_HIDDEN = 4 * D_MODEL
N_BRANCHES = 3
NORM_EPS = 1e-6

IN_SPLITS = (
    MLSTM_WIDTH, MLSTM_WIDTH, MLSTM_WIDTH, MLSTM_WIDTH,
    MLSTM_HEADS, MLSTM_HEADS,
    3 * ATTN_HEADS_TOTAL * ATTN_HEAD_DIM,
    S5_WIDTH,
    N_BRANCHES * D_MODEL,
)
N_IN = sum(IN_SPLITS)
IN_OFFSETS = tuple(int(o) for o in np.cumsum(IN_SPLITS)[:-1])

kernel_name = "hybrid_mlstm_dilated_s5_trunk"


def rms_norm(x, g):
    xf = x.astype(jnp.float32)
    y = xf * lax.rsqrt(jnp.mean(xf * xf, axis=-1, keepdims=True) + NORM_EPS)
    return (y * g.astype(jnp.float32)).astype(x.dtype)


def split_heads(t, h):
    return t.reshape(t.shape[:-1] + (h, t.shape[-1] // h))


def causal_depthwise_conv(x, w, b):
    k = w.shape[0]
    s = x.shape[1]
    xp = jnp.pad(x, ((0, 0), (k - 1, 0), (0, 0)))
    y = b
    for j in range(k):
        y = y + w[j] * xp[:, j:j + s]
    return y


def mlstm_chunkwise(q, k, v, ig, lf):
    b, s, h, e = q.shape
    L = MLSTM_CHUNK
    nc = s // L
    k = k * (e ** -0.5)

    def chunks(t):
        t = t.reshape((b, nc, L) + t.shape[2:])
        return jnp.moveaxis(t, 1, 0).swapaxes(2, 3)

    causal = jnp.tril(jnp.ones((L, L), dtype=bool))

    def step(carry, inp):
        c_state, n_state, m_state = carry
        qc, kc, vc, igc, lfc = inp
        bcum = jnp.cumsum(lfc, axis=-1)
        dlog = bcum[..., :, None] - bcum[..., None, :] + igc[..., None, :]
        dlog = jnp.where(causal, dlog, -jnp.inf)
        m_inter = bcum + m_state[..., None]
        m_t = jnp.maximum(m_inter, jnp.max(dlog, axis=-1))
        w_intra = jnp.exp(dlog - m_t[..., None])
        w_inter = jnp.exp(m_inter - m_t)
        sc = jnp.einsum('bhtd,bhsd->bhts', qc, kc) * w_intra
        num = (jnp.einsum('bhts,bhse->bhte', sc, vc)
               + w_inter[..., None] * jnp.einsum('bhtd,bhed->bhte', qc, c_state))
        den = jnp.sum(sc, axis=-1) + w_inter * jnp.einsum('bhtd,bhd->bht', qc, n_state)
        h_out = num / jnp.maximum(jnp.abs(den), jnp.exp(-m_t))[..., None]
        b_last = bcum[..., -1]
        g = b_last[..., None] - bcum + igc
        m_new = jnp.maximum(b_last + m_state, jnp.max(g, axis=-1))
        w_s = jnp.exp(g - m_new[..., None])
        decay = jnp.exp(b_last + m_state - m_new)
        c_new = decay[..., None, None] * c_state + jnp.einsum('bhs,bhse,bhsd->bhed', w_s, vc, kc)
        n_new = decay[..., None] * n_state + jnp.einsum('bhs,bhsd->bhd', w_s, kc)
        return (c_new, n_new, m_new), h_out

    init = (jnp.zeros((b, h, e, e), jnp.float32), jnp.zeros((b, h, e), jnp.float32),
            jnp.zeros((b, h), jnp.float32))
    _, hs = lax.scan(step, init, (chunks(q), chunks(k), chunks(v), chunks(ig), chunks(lf)))
    hs = jnp.moveaxis(hs.swapaxes(2, 3), 0, 1)
    return hs.reshape(b, s, h, e)


def t5_bucket(n):
    max_exact = REL_BUCKETS // 2
    nf = jnp.maximum(n, max_exact).astype(jnp.float32)
    large = max_exact + (jnp.log(nf / max_exact) / math.log(REL_MAX_DISTANCE / max_exact)
                         * (REL_BUCKETS - max_exact)).astype(jnp.int32)
    large = jnp.minimum(large, REL_BUCKETS - 1)
    return jnp.where(n < max_exact, n, large)


def band_bias(table, dilation):
    i = jnp.arange(ATTN_BLOCK)[:, None]
    j = jnp.arange(2 * ATTN_BLOCK)[None, :]
    dist = jnp.maximum(ATTN_BLOCK + i - j, 0) * dilation
    return jnp.transpose(table[t5_bucket(dist)], (2, 0, 1)).astype(jnp.float32)


def dilated_group_attention(q, k, v, bias, window, dilation):
    b, s, h, e = q.shape
    r = dilation
    span = window // dilation
    l = s // r
    nb = -(-l // ATTN_BLOCK)
    lp = nb * ATTN_BLOCK

    def to_blocks(t):
        t = t.reshape(b, l, r, h, e).swapaxes(1, 2)
        t = jnp.pad(t, ((0, 0), (0, 0), (0, lp - l), (0, 0), (0, 0)))
        return t.reshape(b, r, nb, ATTN_BLOCK, h, e)

    def with_prev(t):
        prev = jnp.pad(t, ((0, 0), (0, 0), (1, 0), (0, 0), (0, 0), (0, 0)))[:, :, :-1]
        return jnp.concatenate([prev, t], axis=3)

    qb = to_blocks(q)
    kk = with_prev(to_blocks(k))
    vv = with_prev(to_blocks(v))
    logits = jnp.einsum('bcnqhe,bcnkhe->bcnhqk', qb, kk) * (e ** -0.5) + bias
    i = jnp.arange(ATTN_BLOCK)[:, None]
    j = jnp.arange(2 * ATTN_BLOCK)[None, :]
    rel = ATTN_BLOCK + i - j
    band = (rel >= 0) & (rel <= span)
    has_prev = (jnp.arange(nb) > 0)[:, None, None] | (j >= ATTN_BLOCK)[None]
    valid = band[None] & has_prev
    logits = jnp.where(valid[None, None, :, None], logits, -jnp.inf)
    mx = jnp.max(logits, axis=-1, keepdims=True)
    p = jnp.exp(logits - mx)
    den = jnp.sum(p, axis=-1)
    out = jnp.einsum('bcnhqk,bcnkhe->bcnqhe', p, vv) / den.swapaxes(-1, -2)[..., None]
    lse = (mx[..., 0] + jnp.log(den)).swapaxes(-1, -2)
    out = out.reshape(b, r, lp, h, e)[:, :, :l].swapaxes(1, 2).reshape(b, s, h, e)
    lse = lse.reshape(b, r, lp, h)[:, :, :l].swapaxes(1, 2).reshape(b, s, h)
    return out, lse


def dilated_attention(q, k, v, rel_bias):
    outs, lses = [], []
    for g, (window, dilation) in enumerate(ATTN_GROUPS):
        hs = slice(g * ATTN_HEADS, (g + 1) * ATTN_HEADS)
        o, lse = dilated_group_attention(q[:, :, hs], k[:, :, hs], v[:, :, hs],
                                         band_bias(rel_bias[:, hs], dilation), window, dilation)
        outs.append(o)
        lses.append(lse)
    wts = jax.nn.softmax(jnp.stack(lses), axis=0)
    return jnp.sum(wts[..., None] * jnp.stack(outs), axis=0)


def s5_combine(e1, e2):
    a1r, a1i, b1r, b1i = e1
    a2r, a2i, b2r, b2i = e2
    return (a2r * a1r - a2i * a1i, a2r * a1i + a2i * a1r,
            a2r * b1r - a2i * b1i + b2r, a2r * b1i + a2i * b1r + b2i)


def s5_ssm(u, lam_re, lam_im, log_dt, b_re, b_im, c_re, c_im, d_skip):
    bsz, s, _ = u.shape
    ug = u.reshape(bsz, s, S5_GROUPS, S5_GROUP)
    dt = jnp.exp(log_dt)[:, None]
    mag = jnp.exp(lam_re * dt)
    ab_re = mag * jnp.cos(lam_im * dt)
    ab_im = mag * jnp.sin(lam_im * dt)
    nr = ab_re - 1.0
    ni = ab_im
    den = lam_re * lam_re + lam_im * lam_im
    f_re = (nr * lam_re + ni * lam_im) / den
    f_im = (ni * lam_re - nr * lam_im) / den
    bb_re = f_re[..., None] * b_re - f_im[..., None] * b_im
    bb_im = f_re[..., None] * b_im + f_im[..., None] * b_re
    bu_re = jnp.einsum('bsgi,gpi->bsgp', ug, bb_re)
    bu_im = jnp.einsum('bsgi,gpi->bsgp', ug, bb_im)
    a_re = jnp.broadcast_to(ab_re, (1, s) + ab_re.shape)
    a_im = jnp.broadcast_to(ab_im, (1, s) + ab_im.shape)
    _, _, xr, xi = lax.associative_scan(s5_combine, (a_re, a_im, bu_re, bu_im), axis=1)
    y = (jnp.einsum('bsgp,gop->bsgo', xr, c_re) - jnp.einsum('bsgp,gop->bsgo', xi, c_im)
         + d_skip * ug)
    return y.reshape(bsz, s, S5_WIDTH)


def setup_inputs(seed: int = 0) -> dict:
    key = jax.random.key(seed)
    ks = jax.random.split(key, 26)
    f32 = jnp.float32
    L, D = DEPTH, D_MODEL

    def normal(k, shape, scale):
        return jax.random.normal(k, shape, f32) * scale

    return {
        "x": normal(ks[0], (BATCH, SEQ, D), 1.0),
        "norm1_g": 1.0 + normal(ks[1], (L, D), 0.02),
        "w_in": normal(ks[2], (L, D, N_IN), D ** -0.5),
        "conv_w": normal(ks[3], (L, CONV_WIDTH, 2 * MLSTM_WIDTH), CONV_WIDTH ** -0.5),
        "conv_b": normal(ks[4], (L, 2 * MLSTM_WIDTH), 0.02),
        "b_igate": normal(ks[5], (L, MLSTM_HEADS), 0.1),
        "b_fgate": jnp.linspace(3.0, 6.0, MLSTM_HEADS, dtype=f32) + normal(ks[6], (L, MLSTM_HEADS), 0.1),
        "mh_norm_g": 1.0 + normal(ks[7], (L, MLSTM_WIDTH), 0.02),
        "rel_bias": normal(ks[8], (REL_BUCKETS, ATTN_HEADS_TOTAL), 0.5),
        "lam_re": -0.5 + normal(ks[9], (L, S5_GROUPS, S5_STATE), 0.01),
        "lam_im": math.pi * jnp.arange(S5_STATE, dtype=f32) + normal(ks[10], (L, S5_GROUPS, S5_STATE), 0.01),
        "log_dt": jax.random.uniform(ks[11], (L, S5_GROUPS), f32, math.log(1e-3), math.log(1e-1)),
        "b_re": normal(ks[12], (L, S5_GROUPS, S5_STATE, S5_GROUP), (2 * S5_GROUP) ** -0.5),
        "b_im": normal(ks[13], (L, S5_GROUPS, S5_STATE, S5_GROUP), (2 * S5_GROUP) ** -0.5),
        "c_re": normal(ks[14], (L, S5_GROUPS, S5_GROUP, S5_STATE), S5_STATE ** -0.5),
        "c_im": normal(ks[15], (L, S5_GROUPS, S5_GROUP, S5_STATE), S5_STATE ** -0.5),
        "d_skip": normal(ks[16], (L, S5_GROUPS, S5_GROUP), 1.0),
        "w_glu": normal(ks[17], (L, S5_WIDTH, 2 * S5_WIDTH), S5_WIDTH ** -0.5),
        "w_br_a": normal(ks[18], (L, MLSTM_WIDTH, D), MLSTM_WIDTH ** -0.5),
        "w_br_b": normal(ks[19], (L, ATTN_WIDTH, D), ATTN_WIDTH ** -0.5),
        "w_br_c": normal(ks[20], (L, S5_WIDTH, D), S5_WIDTH ** -0.5),
        "w_out": normal(ks[21], (L, D, D), D ** -0.5),
        "norm2_g": 1.0 + normal(ks[22], (L, D), 0.02),
        "w_ff1": normal(ks[23], (L, D, FF_HIDDEN), D ** -0.5),
        "w_ff2": normal(ks[24], (L, FF_HIDDEN, D), FF_HIDDEN ** -0.5),
        "final_g": 1.0 + normal(ks[25], (D,), 0.02),
    }


def reference(x, norm1_g, w_in, conv_w, conv_b, b_igate, b_fgate, mh_norm_g, rel_bias,
              lam_re, lam_im, log_dt, b_re, b_im, c_re, c_im, d_skip, w_glu,
              w_br_a, w_br_b, w_br_c, w_out, norm2_g, w_ff1, w_ff2, final_g):
    dt = x.dtype
    f32 = jnp.float32
    bsz, s, _ = x.shape
    for l in range(DEPTH):
        xn = rms_norm(x, norm1_g[l])
        proj = xn @ w_in[l]
        q_m, k_m, v_m, o_m, i_m, f_m, qkv_att, u_s5, gates = jnp.split(proj, IN_OFFSETS, axis=-1)

        qk = jax.nn.silu(causal_depthwise_conv(jnp.concatenate([q_m, k_m], axis=-1),
                                               conv_w[l], conv_b[l]).astype(f32))
        q_a, k_a = jnp.split(qk, 2, axis=-1)
        h_m = mlstm_chunkwise(split_heads(q_a, MLSTM_HEADS), split_heads(k_a, MLSTM_HEADS),
                              split_heads(v_m.astype(f32), MLSTM_HEADS),
                              i_m.astype(f32) + b_igate[l].astype(f32),
                              jax.nn.log_sigmoid(f_m.astype(f32) + b_fgate[l].astype(f32)))
        h_m = h_m * lax.rsqrt(jnp.mean(h_m * h_m, axis=-1, keepdims=True) + NORM_EPS)
        h_m = h_m.reshape(bsz, s, MLSTM_WIDTH) * mh_norm_g[l].astype(f32)
        y_a = (jax.nn.sigmoid(o_m.astype(f32)) * h_m).astype(dt)

        qkv = qkv_att.astype(f32).reshape(bsz, s, 3, ATTN_HEADS_TOTAL, ATTN_HEAD_DIM)
        y_b = dilated_attention(qkv[:, :, 0], qkv[:, :, 1], qkv[:, :, 2], rel_bias)
        y_b = y_b.reshape(bsz, s, ATTN_WIDTH).astype(dt)

        y_s = s5_ssm(u_s5.astype(f32), lam_re[l].astype(f32), lam_im[l].astype(f32),
                     log_dt[l].astype(f32), b_re[l].astype(f32), b_im[l].astype(f32),
                     c_re[l].astype(f32), c_im[l].astype(f32), d_skip[l].astype(f32))
        z = jax.nn.gelu(y_s.astype(dt))
        z_lin, z_gate = jnp.split(z @ w_glu[l], 2, axis=-1)
        y_c = z_lin * jax.nn.sigmoid(z_gate)

        g_a, g_b, g_c = jnp.split(jax.nn.sigmoid(gates), 3, axis=-1)
        mix = g_a * (y_a @ w_br_a[l]) + g_b * (y_b @ w_br_b[l]) + g_c * (y_c @ w_br_c[l])
        x = x + mix @ w_out[l]

        hn = rms_norm(x, norm2_g[l])
        x = x + jnp.square(jax.nn.relu(hn @ w_ff1[l])) @ w_ff2[l]
    return rms_norm(x, final_g)
```

```python
import functools
import math

import jax
import jax.numpy as jnp
from jax import lax
from jax.experimental import pallas as pl
from jax.experimental.pallas import tpu as pltpu

F32 = jnp.float32
BF16 = jnp.bfloat16
SDS = jax.ShapeDtypeStruct

NORM_EPS = 1e-6
NEG = -1e30

MLSTM_HEADS = 4
MLSTM_HEAD_DIM = 256
MLSTM_WIDTH = MLSTM_HEADS * MLSTM_HEAD_DIM
CONV_WIDTH = 4
ATTN_GROUPS = ((128, 1), (512, 4), (2048, 16))
ATTN_HEADS = 8
ATTN_HEAD_DIM = 128
ATTN_WIDTH = ATTN_HEADS * ATTN_HEAD_DIM
ATTN_HEADS_TOTAL = len(ATTN_GROUPS) * ATTN_HEADS
ATTN_BLOCK = 128
REL_BUCKETS = 32
REL_MAX_DISTANCE = 2048
S5_WIDTH = 1024
S5_GROUP = 16
S5_GROUPS = S5_WIDTH // S5_GROUP
S5_STATE = 64
S5_TOK = 16
N_GATE_PAD = 128

COL_MQ = 0
COL_AQ = 4 * MLSTM_WIDTH
COL_U = COL_AQ + 3 * ATTN_HEADS_TOTAL * ATTN_HEAD_DIM
COL_G = COL_U + S5_WIDTH

MLSTM_CHUNK = 256
VMEM_LIMIT = 56 * 2**20


def _params(sem):
    return pltpu.CompilerParams(dimension_semantics=sem, vmem_limit_bytes=VMEM_LIMIT)


def _sigmoid(x):
    return 1.0 / (1.0 + jnp.exp(-x))


def _log_sigmoid(x):
    return jnp.minimum(x, 0.0) - jnp.log(1.0 + jnp.exp(-jnp.abs(x)))


def _rms(x, g):
    ms = jnp.mean(x * x, axis=-1, keepdims=True)
    return x * lax.rsqrt(ms + NORM_EPS) * g


def _dot(a, b):
    return jnp.dot(a, b, preferred_element_type=F32)


def _dot_nt(a, b):
    return lax.dot_general(a, b, (((1,), (1,)), ((), ())), preferred_element_type=F32)


def _dot_tn(a, b):
    return lax.dot_general(a, b, (((0,), (0,)), ((), ())), preferred_element_type=F32)


def _inproj_body(x_ref, g_ref, w_ref, wif_ref, o_ref, oif_ref, xn_ref):
    @pl.when(pl.program_id(1) == 0)
    def _():
        xn = _rms(x_ref[...], g_ref[...]).astype(BF16)
        xn_ref[...] = xn
        oif_ref[...] = _dot(xn, wif_ref[...])

    o_ref[...] = _dot(xn_ref[...], w_ref[...]).astype(o_ref.dtype)


def in_proj(x2d, g, w, wif, tm, tn):
    t, d = x2d.shape
    n = w.shape[1]
    return pl.pallas_call(
        _inproj_body,
        grid=(t // tm, n // tn),
        in_specs=[pl.BlockSpec((tm, d), lambda i, j: (i, 0)),
                  pl.BlockSpec((1, d), lambda i, j: (0, 0)),
                  pl.BlockSpec((d, tn), lambda i, j: (0, j)),
                  pl.BlockSpec((d, N_GATE_PAD), lambda i, j: (0, 0))],
        out_specs=[pl.BlockSpec((tm, tn), lambda i, j: (i, j)),
                   pl.BlockSpec((tm, N_GATE_PAD), lambda i, j: (i, 0))],
        out_shape=[SDS((t, n), BF16), SDS((t, N_GATE_PAD), F32)],
        scratch_shapes=[pltpu.VMEM((tm, d), BF16)],
        compiler_params=_params(("parallel", "arbitrary")),
        name="in_proj",
    )(x2d, g, w, wif)


def _mlstm_body(q_ref, k_ref, v_ref, og_ref, gcol_ref, grow_ref, bcol_ref, brow_ref,
                cwq_ref, cwk_ref, cbq_ref, cbk_ref, ng_ref, y_ref,
                ct_ref, m_ref, tq_ref, tk_ref):
    h = pl.program_id(1)
    c = pl.program_id(2)
    L, E = q_ref.shape

    @pl.when(c == 0)
    def _():
        ct_ref[...] = jnp.zeros_like(ct_ref)
        m_ref[...] = jnp.zeros_like(m_ref)
        tq_ref[...] = jnp.zeros_like(tq_ref)
        tk_ref[...] = jnp.zeros_like(tk_ref)

    row8 = lax.broadcasted_iota(jnp.int32, (8, E), 0)

    def conv_silu(x_ref, tail_ref, w_ref, b_ref):
        x = x_ref[...].astype(F32)
        tail = tail_ref[...]
        w = w_ref[...]
        acc = b_ref[...] + w[CONV_WIDTH - 1:CONV_WIDTH] * x
        for d in range(1, CONV_WIDTH):
            r = pltpu.roll(x, d, 0)
            head = jnp.where(row8 < d, pltpu.roll(tail, d, 0), r[:8])
            xs = jnp.concatenate([head, r[8:]], axis=0)
            acc = acc + w[CONV_WIDTH - 1 - d:CONV_WIDTH - d] * xs
        tail_ref[...] = x[L - 8:]
        return acc * _sigmoid(acc)

    q = conv_silu(q_ref, tq_ref, cwq_ref, cbq_ref)
    k = conv_silu(k_ref, tk_ref, cwk_ref, cbk_ref) * (E ** -0.5)

    gc = gcol_ref[...] + bcol_ref[...]
    lane = lax.broadcasted_iota(jnp.int32, gc.shape, 1)
    ig_col = jnp.sum(jnp.where(lane == h, gc, 0.0), axis=1, keepdims=True)
    lf_col = _log_sigmoid(jnp.sum(jnp.where(lane == h + MLSTM_HEADS, gc, 0.0), axis=1, keepdims=True))
    gr = grow_ref[...] + brow_ref[...]
    sub = lax.broadcasted_iota(jnp.int32, gr.shape, 0)
    ig_row = jnp.sum(jnp.where(sub == h, gr, 0.0), axis=0, keepdims=True)
    lf_row = _log_sigmoid(jnp.sum(jnp.where(sub == h + MLSTM_HEADS, gr, 0.0), axis=0, keepdims=True))

    tt = lax.broadcasted_iota(jnp.int32, (L, L), 0)
    ss = lax.broadcasted_iota(jnp.int32, (L, L), 1)
    causal = ss <= tt
    bcum_col = jnp.sum(jnp.where(causal, lf_row, 0.0), axis=1, keepdims=True)
    bcum_row = jnp.sum(jnp.where(tt <= ss, lf_col, 0.0), axis=0, keepdims=True)
    a_row = ig_row - bcum_row
    a_col = ig_col - bcum_col

    m_prev = m_ref[...]
    amat = jnp.where(causal, a_row, NEG)
    mrow = jnp.maximum(m_prev, jnp.max(amat, axis=1, keepdims=True))
    w_intra = jnp.exp(amat - mrow)
    w_inter = jnp.exp(m_prev - mrow)

    qb = q.astype(BF16)
    kb = k.astype(BF16)
    vaug = jnp.concatenate([v_ref[...], jnp.ones((L, 128), BF16)], axis=1)
    s = _dot_nt(qb, kb) * w_intra
    ct = ct_ref[...]
    num_aug = _dot(s.astype(BF16), vaug) + w_inter * _dot(qb, ct.astype(BF16))
    num = num_aug[:, :E]
    den = num_aug[:, E:E + 1]
    m_t = bcum_col + mrow
    hout = num / jnp.maximum(jnp.abs(den), jnp.exp(-m_t))
    hn = _rms(hout, ng_ref[...])
    y_ref[...] = (_sigmoid(og_ref[...].astype(F32)) * hn).astype(y_ref.dtype)

    b_last = jnp.sum(lf_col, axis=0, keepdims=True)
    g_col = b_last + a_col
    m_new = jnp.maximum(b_last + m_prev, jnp.max(g_col, axis=0, keepdims=True))
    w_s = jnp.exp(g_col - m_new)
    decay = jnp.exp(b_last + m_prev - m_new)
    kw = (k * w_s).astype(BF16)
    ct_ref[...] = decay * ct + _dot_tn(kw, vaug)
    m_ref[...] = m_new


def mlstm(proj, gif, gif_t, bcol, brow, conv_w, conv_b, ng, batch, seq):
    L = min(MLSTM_CHUNK, seq)
    E = MLSTM_HEAD_DIM
    nc = seq // L
    H = MLSTM_HEADS
    t = batch * seq
    row = lambda b, h, c: b * nc + c
    return pl.pallas_call(
        _mlstm_body,
        grid=(batch, H, nc),
        in_specs=[pl.BlockSpec((L, E), lambda b, h, c: (row(b, h, c), h)),
                  pl.BlockSpec((L, E), lambda b, h, c: (row(b, h, c), H + h)),
                  pl.BlockSpec((L, E), lambda b, h, c: (row(b, h, c), 2 * H + h)),
                  pl.BlockSpec((L, E), lambda b, h, c: (row(b, h, c), 3 * H + h)),
                  pl.BlockSpec((L, N_GATE_PAD), lambda b, h, c: (row(b, h, c), 0)),
                  pl.BlockSpec((8, L), lambda b, h, c: (0, row(b, h, c))),
                  pl.BlockSpec((1, N_GATE_PAD), lambda b, h, c: (0, 0)),
                  pl.BlockSpec((8, 1), lambda b, h, c: (0, 0)),
                  pl.BlockSpec((CONV_WIDTH, E), lambda b, h, c: (0, h)),
                  pl.BlockSpec((CONV_WIDTH, E), lambda b, h, c: (0, H + h)),
                  pl.BlockSpec((1, E), lambda b, h, c: (0, h)),
                  pl.BlockSpec((1, E), lambda b, h, c: (0, H + h)),
                  pl.BlockSpec((1, E), lambda b, h, c: (0, h))],
        out_specs=pl.BlockSpec((L, E), lambda b, h, c: (row(b, h, c), h)),
        out_shape=SDS((t, MLSTM_WIDTH), BF16),
        scratch_shapes=[pltpu.VMEM((E, E + 128), F32), pltpu.VMEM((1, 1), F32),
                        pltpu.VMEM((8, E), F32), pltpu.VMEM((8, E), F32)],
        compiler_params=_params(("parallel", "parallel", "arbitrary")),
        name="mlstm",
    )(proj, proj, proj, proj, gif, gif_t, bcol, brow, conv_w, conv_w, conv_b, conv_b, ng)


def _bias_body(table_ref, o_ref):
    h = pl.program_id(0)
    dil = jnp.where(h < ATTN_HEADS, ATTN_GROUPS[0][1],
                    jnp.where(h < 2 * ATTN_HEADS, ATTN_GROUPS[1][1], ATTN_GROUPS[2][1]))
    shape = (ATTN_BLOCK, 2 * ATTN_BLOCK)
    i = lax.broadcasted_iota(jnp.int32, shape, 0)
    j = lax.broadcasted_iota(jnp.int32, shape, 1)
    rel = ATTN_BLOCK + i - j
    dist = jnp.maximum(rel, 0) * dil
    max_exact = REL_BUCKETS // 2
    nf = jnp.maximum(dist, max_exact).astype(F32)
    large = max_exact + (jnp.log(nf / max_exact) / math.log(REL_MAX_DISTANCE / max_exact)
                         * (REL_BUCKETS - max_exact)).astype(jnp.int32)
    large = jnp.minimum(large, REL_BUCKETS - 1)
    bucket = jnp.where(dist < max_exact, dist, large)
    acc = jnp.zeros(shape, F32)
    for b in range(REL_BUCKETS):
        acc = jnp.where(bucket == b, table_ref[b, h], acc)
    o_ref[0] = jnp.where(rel >= 0, jnp.where(rel <= ATTN_BLOCK, acc, NEG), NEG)


def attn_bias(rel_bias):
    return pl.pallas_call(
        _bias_body,
        grid=(ATTN_HEADS_TOTAL,),
        in_specs=[pl.BlockSpec(memory_space=pltpu.SMEM)],
        out_specs=pl.BlockSpec((1, ATTN_BLOCK, 2 * ATTN_BLOCK), lambda h: (h, 0, 0)),
        out_shape=SDS((ATTN_HEADS_TOTAL, ATTN_BLOCK, 2 * ATTN_BLOCK), F32),
        compiler_params=_params(("arbitrary",)),
        name="attn_bias",
    )(rel_bias)


def _attn_body(q_ref, kp_ref, kc_ref, vp_ref, vc_ref, bias_ref, o_ref, lse_ref):
    n = pl.program_id(2)
    B = ATTN_BLOCK
    E = ATTN_HEAD_DIM
    scale = E ** -0.5
    no_prev = jnp.where(n > 0, 0.0, NEG)
    lane = lax.broadcasted_iota(jnp.int32, (B, 128), 1)
    lse = jnp.zeros((B, 128), F32)
    for h in range(ATTN_HEADS):
        sl = slice(h * E, (h + 1) * E)
        q = q_ref[0, :, sl]
        sc = _dot_nt(q, kc_ref[0, :, sl]) * scale + bias_ref[h, :, B:]
        sp = _dot_nt(q, kp_ref[0, :, sl]) * scale + bias_ref[h, :, :B] + no_prev
        m = jnp.maximum(jnp.max(sc, axis=1, keepdims=True), jnp.max(sp, axis=1, keepdims=True))
        pc = jnp.exp(sc - m)
        pp = jnp.exp(sp - m)
        den = jnp.sum(pc, axis=1, keepdims=True) + jnp.sum(pp, axis=1, keepdims=True)
        o = (_dot(pc.astype(BF16), vc_ref[0, :, sl]) + _dot(pp.astype(BF16), vp_ref[0, :, sl])) / den
        o_ref[0, :, sl] = o.astype(o_ref.dtype)
        lse = jnp.where(lane == h, m + jnp.log(den), lse)
    lse_ref[0] = lse


def attn_group(proj, bias, g, batch, seq):
    _, r = ATTN_GROUPS[g]
    n_cols = proj.shape[1]
    W = ATTN_WIDTH
    cb = n_cols // W
    l = seq // r
    nb = l // ATTN_BLOCK
    pv = proj.reshape(batch, l, r * n_cols)
    qo = COL_AQ // W + g
    ko = qo + ATTN_HEADS_TOTAL * ATTN_HEAD_DIM // W
    vo = ko + ATTN_HEADS_TOTAL * ATTN_HEAD_DIM // W
    blk = (1, ATTN_BLOCK, W)
    prev = lambda n: jnp.maximum(n - 1, 0)
    out, lse = pl.pallas_call(
        _attn_body,
        grid=(batch, r, nb),
        in_specs=[pl.BlockSpec(blk, lambda b, c, n: (b, n, c * cb + qo)),
                  pl.BlockSpec(blk, lambda b, c, n: (b, prev(n), c * cb + ko)),
                  pl.BlockSpec(blk, lambda b, c, n: (b, n, c * cb + ko)),
                  pl.BlockSpec(blk, lambda b, c, n: (b, prev(n), c * cb + vo)),
                  pl.BlockSpec(blk, lambda b, c, n: (b, n, c * cb + vo)),
                  pl.BlockSpec((ATTN_HEADS, ATTN_BLOCK, 2 * ATTN_BLOCK), lambda b, c, n: (g, 0, 0))],
        out_specs=[pl.BlockSpec(blk, lambda b, c, n: (b, n, c)),
                   pl.BlockSpec((1, ATTN_BLOCK, 128), lambda b, c, n: (b, n, c))],
        out_shape=[SDS((batch, l, r * W), BF16), SDS((batch, l, r * 128), F32)],
        compiler_params=_params(("parallel", "parallel", "arbitrary")),
        name=f"attn_g{g}",
    )(pv, pv, pv, pv, pv, bias)
    return out.reshape(batch * seq, W), lse.reshape(batch * seq, 128)


def _s5_prep_body(lam_ref, lamc_ref, dt_ref, bt_r_ref, bt_i_ref, ce_r_ref, ce_i_ref, d_ref,
                  t_ref, w_ref, v_ref, coef_ref):
    P = S5_STATE
    lr = lam_ref[0, 0:1, :]
    li = lam_ref[0, 1:2, :]
    dt = jnp.exp(dt_ref[0])

    def apow(e, lr_, li_):
        mag = jnp.exp(lr_ * dt * e)
        ang = li_ * dt * e
        return mag * jnp.cos(ang), mag * jnp.sin(ang)

    one = jnp.ones((1, 1), F32)
    ar, ai = apow(one, lr, li)
    nr = ar - 1.0
    den = lr * lr + li * li
    f_re = (nr * lr + ai * li) / den
    f_im = (ai * lr - nr * li) / den
    bt_r = bt_r_ref[0]
    bt_i = bt_i_ref[0]
    bb_r = f_re * bt_r - f_im * bt_i
    bb_i = f_re * bt_i + f_im * bt_r

    lrc = lamc_ref[0, :, 0:1]
    lic = lamc_ref[0, :, 1:2]
    lag = (lax.broadcasted_iota(jnp.int32, (P, 256), 1) // S5_GROUP).astype(F32)
    adr, adi = apow(lag, lrc, lic)
    ce_r = ce_r_ref[0]
    ce_i = ce_i_ref[0]
    ca_r = ce_r * adr - ce_i * adi
    ca_i = ce_r * adi + ce_i * adr
    hp = lax.Precision.HIGHEST
    ks = (jnp.dot(bb_r[:, :P], ca_r, precision=hp, preferred_element_type=F32)
          - jnp.dot(bb_i[:, :P], ca_i, precision=hp, preferred_element_type=F32))
    si = lax.broadcasted_iota(jnp.int32, (S5_GROUP, 256), 0)
    lj = lax.broadcasted_iota(jnp.int32, (S5_GROUP, 256), 1)
    ks = ks + jnp.where(si == lj, d_ref[0], 0.0)
    for s in range(S5_TOK):
        sh = s * S5_GROUP
        blk = ks if s == 0 else jnp.where(lj >= sh, pltpu.roll(ks, sh, 1), 0.0)
        t_ref[0, sh:sh + S5_GROUP, :] = blk.astype(t_ref.dtype)

    lane128 = lax.broadcasted_iota(jnp.int32, (1, 128), 1)
    for s in range(S5_TOK):
        pr, pi = apow(float(S5_TOK - 1 - s) * one, lr, li)
        p1 = jnp.where(lane128 < P, pr, pi)
        p2 = jnp.where(lane128 < P, -pi, pr)
        w_ref[0, s * S5_GROUP:(s + 1) * S5_GROUP, :] = (bb_r * p1 + bb_i * p2).astype(w_ref.dtype)

    adr1, adi1 = apow(lag + 1.0, lrc, lic)
    v_ref[0, 0:P, :] = (ce_r * adr1 - ce_i * adi1).astype(v_ref.dtype)
    v_ref[0, P:2 * P, :] = (-(ce_r * adi1 + ce_i * adr1)).astype(v_ref.dtype)

    ek = jnp.left_shift(S5_TOK, lax.broadcasted_iota(jnp.int32, (16, 1), 0)).astype(F32)
    cr, ci = apow(ek, lr, li)
    coef_ref[0, 0:16, :] = cr
    coef_ref[0, 16:32, :] = jnp.where(lane128 < P, -ci, ci)


def s5_prep(lam_re, lam_im, log_dt, b_re, b_im, c_re, c_im, d_skip):
    G, P, I = b_re.shape
    lam = jnp.stack([jnp.tile(lam_re, (1, 2)), jnp.tile(lam_im, (1, 2))], axis=1)
    lam = jnp.pad(lam, ((0, 0), (0, 6), (0, 0)))
    lamc = jnp.stack([lam_re, lam_im], axis=2)
    dt = log_dt.reshape(G, 1, 1)
    bt_r = jnp.tile(jnp.swapaxes(b_re, 1, 2), (1, 1, 2))
    bt_i = jnp.tile(jnp.swapaxes(b_im, 1, 2), (1, 1, 2))
    ce_r = jnp.tile(jnp.swapaxes(c_re, 1, 2), (1, 1, S5_TOK))
    ce_i = jnp.tile(jnp.swapaxes(c_im, 1, 2), (1, 1, S5_TOK))
    d = jnp.pad(d_skip, ((0, 0), (0, 256 - I))).reshape(G, 1, 256)
    blk = lambda *s: pl.BlockSpec((1,) + s, lambda g: (g, 0, 0))
    return pl.pallas_call(
        _s5_prep_body,
        grid=(G,),
        in_specs=[blk(8, 128), blk(P, 2), blk(1, 1), blk(16, 128), blk(16, 128), blk(P, 256), blk(P, 256),
                  blk(1, 256)],
        out_specs=[blk(256, 256), blk(256, 128), blk(128, 256), blk(32, 128)],
        out_shape=[SDS((G, 256, 256), BF16), SDS((G, 256, 128), BF16), SDS((G, 128, 256), BF16),
                   SDS((G, 32, 128), F32)],
        compiler_params=_params(("arbitrary",)),
        name="s5_prep",
    )(lam, lamc, dt, bt_r, bt_i, ce_r, ce_i, d)


def _gelu_tanh(x):
    return 0.5 * x * (1.0 + jnp.tanh(math.sqrt(2.0 / math.pi) * (x + 0.044715 * (x * x * x))))


def _s5_body(u_ref, t_ref, w_ref, v_ref, coef_ref, z_ref):
    u = u_ref[0, 0]
    R = u.shape[0]
    P = S5_STATE
    y = _dot(u, t_ref[0])
    x = _dot(u, w_ref[0])
    row = lax.broadcasted_iota(jnp.int32, (R, 128), 0)

    def shift_down(a, sh):
        if sh % 8 == 0:
            return jnp.concatenate([jnp.zeros((sh, 128), F32), a[:R - sh]], axis=0)
        return jnp.where(row < sh, 0.0, pltpu.roll(a, sh, 0))

    k = 0
    while (1 << k) < R:
        xs = shift_down(x, 1 << k)
        x = x + coef_ref[0, k:k + 1, :] * xs + coef_ref[0, 16 + k:17 + k, :] * pltpu.roll(xs, P, 1)
        k += 1
    xprev = shift_down(x, 1)
    y = y + _dot(xprev.astype(BF16), v_ref[0])
    z_ref[0, 0] = _gelu_tanh(y).astype(z_ref.dtype)


def s5_scan(u_flat, tm, wm, vm, coef):
    b, g, r, _ = u_flat.shape
    per_g = lambda *s: pl.BlockSpec((1,) + s, lambda bi, gi: (gi, 0, 0))
    return pl.pallas_call(
        _s5_body,
        grid=(b, g),
        in_specs=[pl.BlockSpec((1, 1, r, 256), lambda bi, gi: (bi, gi, 0, 0)),
                  per_g(256, 256), per_g(256, 128), per_g(128, 256), per_g(32, 128)],
        out_specs=pl.BlockSpec((1, 1, r, 256), lambda bi, gi: (bi, gi, 0, 0)),
        out_shape=SDS(u_flat.shape, BF16),
        compiler_params=_params(("parallel", "parallel")),
        name="s5_scan",
    )(u_flat, tm, wm, vm, coef)


def _glu_body(z_ref, wl_ref, wg_ref, o_ref):
    z = z_ref[...]
    o_ref[...] = (_dot(z, wl_ref[...]) * _sigmoid(_dot(z, wg_ref[...]))).astype(o_ref.dtype)


def glu(z, w, tm, tn):
    t, k = z.shape
    n = w.shape[1] // 2
    nj = n // tn
    return pl.pallas_call(
        _glu_body,
        grid=(t // tm, nj),
        in_specs=[pl.BlockSpec((tm, k), lambda i, j: (i, 0)),
                  pl.BlockSpec((k, tn), lambda i, j: (0, j)),
                  pl.BlockSpec((k, tn), lambda i, j: (0, nj + j))],
        out_specs=pl.BlockSpec((tm, tn), lambda i, j: (i, j)),
        out_shape=SDS((t, n), BF16),
        compiler_params=_params(("parallel", "arbitrary")),
        name="glu",
    )(z, w, w)


def _mix_body(ya_ref, o0_ref, o1_ref, o2_ref, l0_ref, l1_ref, l2_ref, yc_ref,
              wa_ref, wb_ref, wc_ref, ga_ref, gb_ref, gc_ref, out_ref, yb_ref):
    @pl.when(pl.program_id(1) == 0)
    def _():
        l0, l1, l2 = l0_ref[...], l1_ref[...], l2_ref[...]
        m = jnp.maximum(jnp.maximum(l0, l1), l2)
        e0, e1, e2 = jnp.exp(l0 - m), jnp.exp(l1 - m), jnp.exp(l2 - m)
        inv = 1.0 / (e0 + e1 + e2)
        w0, w1, w2 = e0 * inv, e1 * inv, e2 * inv
        for h in range(ATTN_HEADS):
            sl = slice(h * ATTN_HEAD_DIM, (h + 1) * ATTN_HEAD_DIM)
            yb = (w0[:, h:h + 1] * o0_ref[:, sl].astype(F32) + w1[:, h:h + 1] * o1_ref[:, sl].astype(F32)
                  + w2[:, h:h + 1] * o2_ref[:, sl].astype(F32))
            yb_ref[:, sl] = yb.astype(yb_ref.dtype)

    mix = (_sigmoid(ga_ref[...].astype(F32)) * _dot(ya_ref[...], wa_ref[...])
           + _sigmoid(gb_ref[...].astype(F32)) * _dot(yb_ref[...], wb_ref[...])
           + _sigmoid(gc_ref[...].astype(F32)) * _dot(yc_ref[...], wc_ref[...]))
    out_ref[...] = mix.astype(out_ref.dtype)


def gated_mix(ya, outs, lses, yc, wa, wb, wc, proj, tm, tn):
    t, kw = ya.shape
    d = wa.shape[1]
    go = COL_G // tn
    row = lambda w: pl.BlockSpec((tm, w), lambda i, j: (i, 0))
    wsp = pl.BlockSpec((kw, tn), lambda i, j: (0, j))
    gate = lambda o: pl.BlockSpec((tm, tn), lambda i, j: (i, go + o * (d // tn) + j))
    return pl.pallas_call(
        _mix_body,
        grid=(t // tm, d // tn),
        in_specs=[row(kw), row(kw), row(kw), row(kw), row(128), row(128), row(128), row(kw),
                  wsp, wsp, wsp, gate(0), gate(1), gate(2)],
        out_specs=pl.BlockSpec((tm, tn), lambda i, j: (i, j)),
        out_shape=SDS((t, d), BF16),
        scratch_shapes=[pltpu.VMEM((tm, kw), BF16)],
        compiler_params=_params(("parallel", "arbitrary")),
        name="gated_mix",
    )(ya, *outs, *lses, yc, wa, wb, wc, proj, proj, proj)


def _outproj_body(x_ref, m_ref, w_ref, o_ref):
    o_ref[...] = x_ref[...] + _dot(m_ref[...], w_ref[...])


def out_proj(x2d, mix, w, tm, tn):
    t, d = x2d.shape
    return pl.pallas_call(
        _outproj_body,
        grid=(t // tm, d // tn),
        in_specs=[pl.BlockSpec((tm, tn), lambda i, j: (i, j)),
                  pl.BlockSpec((tm, d), lambda i, j: (i, 0)),
                  pl.BlockSpec((d, tn), lambda i, j: (0, j))],
        out_specs=pl.BlockSpec((tm, tn), lambda i, j: (i, j)),
        out_shape=SDS((t, d), F32),
        compiler_params=_params(("parallel", "arbitrary")),
        name="out_proj",
    )(x2d, mix, w)


def _ffn_body(x_ref, g_ref, w1_ref, w2_ref, o_ref, xn_ref):
    @pl.when(pl.program_id(1) == 0)
    def _():
        x = x_ref[...]
        xn_ref[...] = _rms(x, g_ref[...]).astype(BF16)
        o_ref[...] = x

    hid = jnp.maximum(_dot(xn_ref[...], w1_ref[...]), 0.0)
    o_ref[...] += _dot((hid * hid).astype(BF16), w2_ref[...])


def ffn(x2d, g, w1, w2, tm, th):
    t, d = x2d.shape
    hdim = w1.shape[1]
    return pl.pallas_call(
        _ffn_body,
        grid=(t // tm, hdim // th),
        in_specs=[pl.BlockSpec((tm, d), lambda i, j: (i, 0)),
                  pl.BlockSpec((1, d), lambda i, j: (0, 0)),
                  pl.BlockSpec((d, th), lambda i, j: (0, j)),
                  pl.BlockSpec((th, d), lambda i, j: (j, 0))],
        out_specs=pl.BlockSpec((tm, d), lambda i, j: (i, 0)),
        out_shape=SDS((t, d), F32),
        scratch_shapes=[pltpu.VMEM((tm, d), BF16)],
        compiler_params=_params(("parallel", "arbitrary")),
        name="ffn",
    )(x2d, g, w1, w2)


def _norm_body(x_ref, g_ref, o_ref):
    o_ref[...] = _rms(x_ref[...], g_ref[...])


def final_norm(x2d, g, tm):
    t, d = x2d.shape
    return pl.pallas_call(
        _norm_body,
        grid=(t // tm,),
        in_specs=[pl.BlockSpec((tm, d), lambda i: (i, 0)), pl.BlockSpec((1, d), lambda i: (0, 0))],
        out_specs=pl.BlockSpec((tm, d), lambda i: (i, 0)),
        out_shape=SDS((t, d), F32),
        compiler_params=_params(("parallel",)),
        name="final_norm",
    )(x2d, g)


def _tile(n, want):
    t = min(n, want)
    assert n % t == 0, (n, want)
    return t


def _layer(x2d, bias, batch, seq, p):
    t, d = x2d.shape
    tm = _tile(t, 1024)
    n_gate0 = 4 * MLSTM_WIDTH
    w_in = p["w_in"]
    w_main = jnp.concatenate([w_in[:, :n_gate0], w_in[:, n_gate0 + 2 * MLSTM_HEADS:]], axis=1).astype(BF16)
    w_if = jnp.pad(w_in[:, n_gate0:n_gate0 + 2 * MLSTM_HEADS],
                   ((0, 0), (0, N_GATE_PAD - 2 * MLSTM_HEADS))).astype(BF16)
    proj, gif = in_proj(x2d, p["norm1_g"].reshape(1, d), w_main, w_if, tm, 1024)

    bif = jnp.concatenate([p["b_igate"], p["b_fgate"]]).astype(F32)
    bcol = jnp.pad(bif, (0, N_GATE_PAD - bif.shape[0])).reshape(1, N_GATE_PAD)
    ya = mlstm(proj, gif, gif[:, :8].T, bcol, bif.reshape(8, 1), p["conv_w"], p["conv_b"].reshape(1, -1),
               p["mh_norm_g"].reshape(1, -1), batch, seq)

    outs, lses = zip(*[attn_group(proj, bias, g, batch, seq) for g in range(len(ATTN_GROUPS))])

    tmat, wmat, vmat, coef = s5_prep(p["lam_re"], p["lam_im"], p["log_dt"], p["b_re"], p["b_im"],
                                     p["c_re"], p["c_im"], p["d_skip"])
    nblk = seq // S5_TOK
    u = proj[:, COL_U:COL_U + S5_WIDTH].reshape(batch, nblk, S5_TOK, S5_GROUPS, S5_GROUP)
    u = u.transpose(0, 3, 1, 2, 4).reshape(batch, S5_GROUPS, nblk, S5_TOK * S5_GROUP)
    z = s5_scan(u, tmat, wmat, vmat, coef)
    z = z.reshape(batch, S5_GROUPS, nblk, S5_TOK, S5_GROUP).transpose(0, 2, 3, 1, 4).reshape(t, S5_WIDTH)
    yc = glu(z, p["w_glu"].astype(BF16), tm, 512)

    mix = gated_mix(ya, outs, lses, yc, p["w_br_a"].astype(BF16), p["w_br_b"].astype(BF16),
                    p["w_br_c"].astype(BF16), proj, tm, 512)
    x2d = out_proj(x2d, mix, p["w_out"].astype(BF16), tm, 512)
    return ffn(x2d, p["norm2_g"].reshape(1, d), p["w_ff1"].astype(BF16), p["w_ff2"].astype(BF16),
               _tile(t, 512), 1024)


_PER_LAYER = ("norm1_g", "w_in", "conv_w", "conv_b", "b_igate", "b_fgate", "mh_norm_g", "lam_re", "lam_im",
              "log_dt", "b_re", "b_im", "c_re", "c_im", "d_skip", "w_glu", "w_br_a", "w_br_b", "w_br_c",
              "w_out", "norm2_g", "w_ff1", "w_ff2")


def kernel(x, norm1_g, w_in, conv_w, conv_b, b_igate, b_fgate, mh_norm_g, rel_bias, lam_re, lam_im, log_dt,
           b_re, b_im, c_re, c_im, d_skip, w_glu, w_br_a, w_br_b, w_br_c, w_out, norm2_g, w_ff1, w_ff2,
           final_g):
    stacked = dict(norm1_g=norm1_g, w_in=w_in, conv_w=conv_w, conv_b=conv_b, b_igate=b_igate,
                   b_fgate=b_fgate, mh_norm_g=mh_norm_g, lam_re=lam_re, lam_im=lam_im, log_dt=log_dt,
                   b_re=b_re, b_im=b_im, c_re=c_re, c_im=c_im, d_skip=d_skip, w_glu=w_glu, w_br_a=w_br_a,
                   w_br_b=w_br_b, w_br_c=w_br_c, w_out=w_out, norm2_g=norm2_g, w_ff1=w_ff1, w_ff2=w_ff2)
    batch, seq, d = x.shape
    x2d = x.astype(F32).reshape(batch * seq, d)
    bias = attn_bias(rel_bias.astype(F32))
    for l in range(w_in.shape[0]):
        x2d = _layer(x2d, bias, batch, seq, {k: stacked[k][l] for k in _PER_LAYER})
    out = final_norm(x2d, final_g.reshape(1, d), _tile(batch * seq, 1024))
    return out.reshape(batch, seq, d).astype(x.dtype)
```

```python
import functools
import math

import jax
import jax.numpy as jnp
from jax import lax
from jax.experimental import pallas as pl
from jax.experimental.pallas import tpu as pltpu

F32 = jnp.float32
BF16 = jnp.bfloat16
SDS = jax.ShapeDtypeStruct

NORM_EPS = 1e-6
NEG = -1e30

MLSTM_HEADS = 4
MLSTM_HEAD_DIM = 256
MLSTM_WIDTH = MLSTM_HEADS * MLSTM_HEAD_DIM
CONV_WIDTH = 4
ATTN_GROUPS = ((128, 1), (512, 4), (2048, 16))
ATTN_HEADS = 8
ATTN_HEAD_DIM = 128
ATTN_WIDTH = ATTN_HEADS * ATTN_HEAD_DIM
ATTN_HEADS_TOTAL = len(ATTN_GROUPS) * ATTN_HEADS
ATTN_BLOCK = 128
REL_BUCKETS = 32
REL_MAX_DISTANCE = 2048
S5_WIDTH = 1024
S5_GROUP = 16
S5_GROUPS = S5_WIDTH // S5_GROUP
S5_STATE = 64
S5_TOK = 16
N_GATE_PAD = 128

COL_U = 4 * MLSTM_WIDTH
COL_G = COL_U + S5_WIDTH
D_MODEL = 2048
N_MAIN = COL_G + 3 * D_MODEL
LANES = 128

MLSTM_CHUNK = 256
VMEM_LIMIT = 56 * 2**20


def _params(sem):
    return pltpu.CompilerParams(dimension_semantics=sem, vmem_limit_bytes=VMEM_LIMIT)


def _sigmoid(x):
    return 1.0 / (1.0 + jnp.exp(-x))


def _log_sigmoid(x):
    return jnp.minimum(x, 0.0) - jnp.log(1.0 + jnp.exp(-jnp.abs(x)))


def _rms(x, g):
    ms = jnp.mean(x * x, axis=-1, keepdims=True)
    return x * lax.rsqrt(ms + NORM_EPS) * g


def _dot(a, b):
    return jnp.dot(a, b, preferred_element_type=F32)


def _dot_nt(a, b):
    return lax.dot_general(a, b, (((1,), (1,)), ((), ())), preferred_element_type=F32)


def _dot_tn(a, b):
    return lax.dot_general(a, b, (((0,), (0,)), ((), ())), preferred_element_type=F32)


def _inproj_body(nm, x_ref, g_ref, w_ref, wif_ref, o_ref, oif_ref, a0_ref, a1_ref, a2_ref, xn_ref, acc_ref):
    j = pl.program_id(1)
    tm = x_ref.shape[0]
    tn = w_ref.shape[1]

    @pl.when(j == 0)
    def _():
        xn = _rms(x_ref[...], g_ref[...]).astype(BF16)
        xn_ref[...] = xn
        oif_ref[...] = _dot(xn, wif_ref[...])

    @pl.when(j < nm)
    def _():
        o_ref[...] = _dot(xn_ref[...], w_ref[...]).astype(o_ref.dtype)

    for g, a_ref in enumerate((a0_ref, a1_ref, a2_ref)):
        r = ATTN_GROUPS[g][1]
        lo = nm + 3 * g

        @pl.when(jnp.logical_and(j >= lo, j < lo + 3))
        def _(a_ref=a_ref, r=r):
            res = _dot(xn_ref[...], w_ref[...])
            if r == 1:
                a_ref[0, 0] = res.astype(a_ref.dtype)
            else:
                for s in range(tn // LANES):
                    acc_ref[s] = res[:, s * LANES:(s + 1) * LANES]
                for c in range(r):
                    for s in range(tn // LANES):
                        a_ref[0, c, :, s * LANES:(s + 1) * LANES] = (
                            acc_ref[s, pl.ds(c, tm // r, stride=r), :].astype(a_ref.dtype))


def in_proj(x2d, g, w, wif, batch, seq, tm):
    t, d = x2d.shape
    tn = ATTN_WIDTH
    nm = N_MAIN // tn
    nbt = seq // tm
    n_groups = len(ATTN_GROUPS)

    def a_spec(gi):
        r = ATTN_GROUPS[gi][1]
        return pl.BlockSpec((1, r, tm // r, tn),
                            lambda i, j: (i // nbt, 0, i % nbt, jnp.clip(j - nm - 3 * gi, 0, 2)))

    return pl.pallas_call(
        functools.partial(_inproj_body, nm),
        grid=(t // tm, nm + 3 * n_groups),
        in_specs=[pl.BlockSpec((tm, d), lambda i, j: (i, 0)),
                  pl.BlockSpec((1, d), lambda i, j: (0, 0)),
                  pl.BlockSpec((d, tn), lambda i, j: (0, j)),
                  pl.BlockSpec((d, N_GATE_PAD), lambda i, j: (0, 0))],
        out_specs=[pl.BlockSpec((tm, tn), lambda i, j: (i, jnp.minimum(j, nm - 1))),
                   pl.BlockSpec((tm, N_GATE_PAD), lambda i, j: (i, 0))] + [a_spec(gi) for gi in range(n_groups)],
        out_shape=[SDS((t, N_MAIN), BF16), SDS((t, N_GATE_PAD), F32)]
        + [SDS((batch, r, seq // r, 3 * tn), BF16) for _, r in ATTN_GROUPS],
        scratch_shapes=[pltpu.VMEM((tm, d), BF16), pltpu.VMEM((tn // LANES, tm, LANES), F32)],
        compiler_params=_params(("parallel", "arbitrary")),
        name="in_proj",
    )(x2d, g, w, wif)


def _mlstm_body(q_ref, k_ref, v_ref, og_ref, gcol_ref, grow_ref, bcol_ref, brow_ref,
                cwq_ref, cwk_ref, cbq_ref, cbk_ref, ng_ref, y_ref,
                ct_ref, m_ref, tq_ref, tk_ref):
    h = pl.program_id(1)
    c = pl.program_id(2)
    L, E = q_ref.shape

    @pl.when(c == 0)
    def _():
        ct_ref[...] = jnp.zeros_like(ct_ref)
        m_ref[...] = jnp.zeros_like(m_ref)
        tq_ref[...] = jnp.zeros_like(tq_ref)
        tk_ref[...] = jnp.zeros_like(tk_ref)

    row8 = lax.broadcasted_iota(jnp.int32, (8, E), 0)

    def conv_silu(x_ref, tail_ref, w_ref, b_ref):
        x = x_ref[...].astype(F32)
        tail = tail_ref[...]
        w = w_ref[...]
        acc = b_ref[...] + w[CONV_WIDTH - 1:CONV_WIDTH] * x
        for d in range(1, CONV_WIDTH):
            r = pltpu.roll(x, d, 0)
            head = jnp.where(row8 < d, pltpu.roll(tail, d, 0), r[:8])
            xs = jnp.concatenate([head, r[8:]], axis=0)
            acc = acc + w[CONV_WIDTH - 1 - d:CONV_WIDTH - d] * xs
        tail_ref[...] = x[L - 8:]
        return acc * _sigmoid(acc)

    q = conv_silu(q_ref, tq_ref, cwq_ref, cbq_ref)
    k = conv_silu(k_ref, tk_ref, cwk_ref, cbk_ref) * (E ** -0.5)

    gc = gcol_ref[...] + bcol_ref[...]
    lane = lax.broadcasted_iota(jnp.int32, gc.shape, 1)
    ig_col = jnp.sum(jnp.where(lane == h, gc, 0.0), axis=1, keepdims=True)
    lf_col = _log_sigmoid(jnp.sum(jnp.where(lane == h + MLSTM_HEADS, gc, 0.0), axis=1, keepdims=True))
    gr = grow_ref[...] + brow_ref[...]
    sub = lax.broadcasted_iota(jnp.int32, gr.shape, 0)
    ig_row = jnp.sum(jnp.where(sub == h, gr, 0.0), axis=0, keepdims=True)
    lf_row = _log_sigmoid(jnp.sum(jnp.where(sub == h + MLSTM_HEADS, gr, 0.0), axis=0, keepdims=True))

    tt = lax.broadcasted_iota(jnp.int32, (L, L), 0)
    ss = lax.broadcasted_iota(jnp.int32, (L, L), 1)
    causal = ss <= tt
    bcum_col = jnp.sum(jnp.where(causal, lf_row, 0.0), axis=1, keepdims=True)
    bcum_row = jnp.sum(jnp.where(tt <= ss, lf_col, 0.0), axis=0, keepdims=True)
    a_row = ig_row - bcum_row
    a_col = ig_col - bcum_col

    m_prev = m_ref[...]
    amat = jnp.where(causal, a_row, NEG)
    mrow = jnp.maximum(m_prev, jnp.max(amat, axis=1, keepdims=True))
    w_intra = jnp.exp(amat - mrow)
    w_inter = jnp.exp(m_prev - mrow)

    qb = q.astype(BF16)
    kb = k.astype(BF16)
    vaug = jnp.concatenate([v_ref[...], jnp.ones((L, 128), BF16)], axis=1)
    s = _dot_nt(qb, kb) * w_intra
    ct = ct_ref[...]
    num_aug = _dot(s.astype(BF16), vaug) + w_inter * _dot(qb, ct.astype(BF16))
    num = num_aug[:, :E]
    den = num_aug[:, E:E + 1]
    m_t = bcum_col + mrow
    hout = num / jnp.maximum(jnp.abs(den), jnp.exp(-m_t))
    hn = _rms(hout, ng_ref[...])
    y_ref[...] = (_sigmoid(og_ref[...].astype(F32)) * hn).astype(y_ref.dtype)

    b_last = jnp.sum(lf_col, axis=0, keepdims=True)
    g_col = b_last + a_col
    m_new = jnp.maximum(b_last + m_prev, jnp.max(g_col, axis=0, keepdims=True))
    w_s = jnp.exp(g_col - m_new)
    decay = jnp.exp(b_last + m_prev - m_new)
    kw = (k * w_s).astype(BF16)
    ct_ref[...] = decay * ct + _dot_tn(kw, vaug)
    m_ref[...] = m_new


def mlstm(proj, gif, gif_t, bcol, brow, conv_w, conv_b, ng, batch, seq):
    L = min(MLSTM_CHUNK, seq)
    E = MLSTM_HEAD_DIM
    nc = seq // L
    H = MLSTM_HEADS
    t = batch * seq
    row = lambda b, h, c: b * nc + c
    return pl.pallas_call(
        _mlstm_body,
        grid=(batch, H, nc),
        in_specs=[pl.BlockSpec((L, E), lambda b, h, c: (row(b, h, c), h)),
                  pl.BlockSpec((L, E), lambda b, h, c: (row(b, h, c), H + h)),
                  pl.BlockSpec((L, E), lambda b, h, c: (row(b, h, c), 2 * H + h)),
                  pl.BlockSpec((L, E), lambda b, h, c: (row(b, h, c), 3 * H + h)),
                  pl.BlockSpec((L, N_GATE_PAD), lambda b, h, c: (row(b, h, c), 0)),
                  pl.BlockSpec((8, L), lambda b, h, c: (0, row(b, h, c))),
                  pl.BlockSpec((1, N_GATE_PAD), lambda b, h, c: (0, 0)),
                  pl.BlockSpec((8, 1), lambda b, h, c: (0, 0)),
                  pl.BlockSpec((CONV_WIDTH, E), lambda b, h, c: (0, h)),
                  pl.BlockSpec((CONV_WIDTH, E), lambda b, h, c: (0, H + h)),
                  pl.BlockSpec((1, E), lambda b, h, c: (0, h)),
                  pl.BlockSpec((1, E), lambda b, h, c: (0, H + h)),
                  pl.BlockSpec((1, E), lambda b, h, c: (0, h))],
        out_specs=pl.BlockSpec((L, E), lambda b, h, c: (row(b, h, c), h)),
        out_shape=SDS((t, MLSTM_WIDTH), BF16),
        scratch_shapes=[pltpu.VMEM((E, E + 128), F32), pltpu.VMEM((1, 1), F32),
                        pltpu.VMEM((8, E), F32), pltpu.VMEM((8, E), F32)],
        compiler_params=_params(("parallel", "parallel", "arbitrary")),
        name="mlstm",
    )(proj, proj, proj, proj, gif, gif_t, bcol, brow, conv_w, conv_w, conv_b, conv_b, ng)


def _bias_body(table_ref, o_ref):
    h = pl.program_id(0)
    dil = jnp.where(h < ATTN_HEADS, ATTN_GROUPS[0][1],
                    jnp.where(h < 2 * ATTN_HEADS, ATTN_GROUPS[1][1], ATTN_GROUPS[2][1]))
    shape = (ATTN_BLOCK, 2 * ATTN_BLOCK)
    i = lax.broadcasted_iota(jnp.int32, shape, 0)
    j = lax.broadcasted_iota(jnp.int32, shape, 1)
    rel = ATTN_BLOCK + i - j
    dist = jnp.maximum(rel, 0) * dil
    max_exact = REL_BUCKETS // 2
    nf = jnp.maximum(dist, max_exact).astype(F32)
    large = max_exact + (jnp.log(nf / max_exact) / math.log(REL_MAX_DISTANCE / max_exact)
                         * (REL_BUCKETS - max_exact)).astype(jnp.int32)
    large = jnp.minimum(large, REL_BUCKETS - 1)
    bucket = jnp.where(dist < max_exact, dist, large)
    acc = jnp.zeros(shape, F32)
    for b in range(REL_BUCKETS):
        acc = jnp.where(bucket == b, table_ref[b, h], acc)
    o_ref[0] = jnp.where(rel >= 0, jnp.where(rel <= ATTN_BLOCK, acc, NEG), NEG)


def attn_bias(rel_bias):
    return pl.pallas_call(
        _bias_body,
        grid=(ATTN_HEADS_TOTAL,),
        in_specs=[pl.BlockSpec(memory_space=pltpu.SMEM)],
        out_specs=pl.BlockSpec((1, ATTN_BLOCK, 2 * ATTN_BLOCK), lambda h: (h, 0, 0)),
        out_shape=SDS((ATTN_HEADS_TOTAL, ATTN_BLOCK, 2 * ATTN_BLOCK), F32),
        compiler_params=_params(("arbitrary",)),
        name="attn_bias",
    )(rel_bias)


def _attn_body(q_ref, kp_ref, kc_ref, vp_ref, vc_ref, bias_ref, o_ref, lse_ref):
    n = pl.program_id(2)
    B = ATTN_BLOCK
    E = ATTN_HEAD_DIM
    scale = E ** -0.5
    no_prev = jnp.where(n > 0, 0.0, NEG)
    lane = lax.broadcasted_iota(jnp.int32, (B, LANES), 1)
    lse = jnp.zeros((B, LANES), F32)
    for h in range(ATTN_HEADS):
        sl = slice(h * E, (h + 1) * E)
        q = q_ref[0, 0, :, sl]
        sc = _dot_nt(q, kc_ref[0, 0, :, sl]) * scale + bias_ref[h, :, B:]
        sp = _dot_nt(q, kp_ref[0, 0, :, sl]) * scale + bias_ref[h, :, :B] + no_prev
        m = jnp.maximum(jnp.max(sc, axis=1, keepdims=True), jnp.max(sp, axis=1, keepdims=True))
        pc = jnp.exp(sc - m)
        pp = jnp.exp(sp - m)
        den = jnp.sum(pc, axis=1, keepdims=True) + jnp.sum(pp, axis=1, keepdims=True)
        o = (_dot(pc.astype(BF16), vc_ref[0, 0, :, sl]) + _dot(pp.astype(BF16), vp_ref[0, 0, :, sl])) / den
        o_ref[0, 0, :, sl] = o.astype(o_ref.dtype)
        lse = jnp.where(lane == h, m + jnp.log(den), lse)
    lse_ref[0, 0] = lse


def attn_group(qkv, bias, g):
    batch, r, l, _ = qkv.shape
    W = ATTN_WIDTH
    nb = l // ATTN_BLOCK
    blk = (1, 1, ATTN_BLOCK, W)
    prev = lambda n: jnp.maximum(n - 1, 0)
    return pl.pallas_call(
        _attn_body,
        grid=(batch, r, nb),
        in_specs=[pl.BlockSpec(blk, lambda b, c, n: (b, c, n, 0)),
                  pl.BlockSpec(blk, lambda b, c, n: (b, c, prev(n), 1)),
                  pl.BlockSpec(blk, lambda b, c, n: (b, c, n, 1)),
                  pl.BlockSpec(blk, lambda b, c, n: (b, c, prev(n), 2)),
                  pl.BlockSpec(blk, lambda b, c, n: (b, c, n, 2)),
                  pl.BlockSpec((ATTN_HEADS, ATTN_BLOCK, 2 * ATTN_BLOCK), lambda b, c, n: (g, 0, 0))],
        out_specs=[pl.BlockSpec(blk, lambda b, c, n: (b, c, n, 0)),
                   pl.BlockSpec((1, 1, ATTN_BLOCK, LANES), lambda b, c, n: (b, c, n, 0))],
        out_shape=[SDS((batch, r, l, W), BF16), SDS((batch, r, l, LANES), F32)],
        compiler_params=_params(("parallel", "parallel", "arbitrary")),
        name=f"attn_g{g}",
    )(qkv, qkv, qkv, qkv, qkv, bias)


def _s5_prep_body(lam_ref, lamc_ref, dt_ref, bt_r_ref, bt_i_ref, ce_r_ref, ce_i_ref, d_ref,
                  t_ref, w_ref, v_ref, coef_ref):
    P = S5_STATE
    lr = lam_ref[0, 0:1, :]
    li = lam_ref[0, 1:2, :]
    dt = jnp.exp(dt_ref[0])

    def apow(e, lr_, li_):
        mag = jnp.exp(lr_ * dt * e)
        ang = li_ * dt * e
        return mag * jnp.cos(ang), mag * jnp.sin(ang)

    one = jnp.ones((1, 1), F32)
    ar, ai = apow(one, lr, li)
    nr = ar - 1.0
    den = lr * lr + li * li
    f_re = (nr * lr + ai * li) / den
    f_im = (ai * lr - nr * li) / den
    bt_r = bt_r_ref[0]
    bt_i = bt_i_ref[0]
    bb_r = f_re * bt_r - f_im * bt_i
    bb_i = f_re * bt_i + f_im * bt_r

    lrc = lamc_ref[0, :, 0:1]
    lic = lamc_ref[0, :, 1:2]
    lag = (lax.broadcasted_iota(jnp.int32, (P, 256), 1) // S5_GROUP).astype(F32)
    adr, adi = apow(lag, lrc, lic)
    ce_r = ce_r_ref[0]
    ce_i = ce_i_ref[0]
    ca_r = ce_r * adr - ce_i * adi
    ca_i = ce_r * adi + ce_i * adr
    hp = lax.Precision.HIGHEST
    ks = (jnp.dot(bb_r[:, :P], ca_r, precision=hp, preferred_element_type=F32)
          - jnp.dot(bb_i[:, :P], ca_i, precision=hp, preferred_element_type=F32))
    si = lax.broadcasted_iota(jnp.int32, (S5_GROUP, 256), 0)
    lj = lax.broadcasted_iota(jnp.int32, (S5_GROUP, 256), 1)
    ks = ks + jnp.where(si == lj, d_ref[0], 0.0)
    for s in range(S5_TOK):
        sh = s * S5_GROUP
        blk = ks if s == 0 else jnp.where(lj >= sh, pltpu.roll(ks, sh, 1), 0.0)
        t_ref[0, sh:sh + S5_GROUP, :] = blk.astype(t_ref.dtype)

    lane128 = lax.broadcasted_iota(jnp.int32, (1, 128), 1)
    for s in range(S5_TOK):
        pr, pi = apow(float(S5_TOK - 1 - s) * one, lr, li)
        p1 = jnp.where(lane128 < P, pr, pi)
        p2 = jnp.where(lane128 < P, -pi, pr)
        w_ref[0, s * S5_GROUP:(s + 1) * S5_GROUP, :] = (bb_r * p1 + bb_i * p2).astype(w_ref.dtype)

    adr1, adi1 = apow(lag + 1.0, lrc, lic)
    v_ref[0, 0:P, :] = (ce_r * adr1 - ce_i * adi1).astype(v_ref.dtype)
    v_ref[0, P:2 * P, :] = (-(ce_r * adi1 + ce_i * adr1)).astype(v_ref.dtype)

    ek = jnp.left_shift(S5_TOK, lax.broadcasted_iota(jnp.int32, (16, 1), 0)).astype(F32)
    cr, ci = apow(ek, lr, li)
    coef_ref[0, 0:16, :] = cr
    coef_ref[0, 16:32, :] = jnp.where(lane128 < P, -ci, ci)


def s5_prep(lam_re, lam_im, log_dt, b_re, b_im, c_re, c_im, d_skip):
    G, P, I = b_re.shape
    lam = jnp.stack([jnp.tile(lam_re, (1, 2)), jnp.tile(lam_im, (1, 2))], axis=1)
    lam = jnp.pad(lam, ((0, 0), (0, 6), (0, 0)))
    lamc = jnp.stack([lam_re, lam_im], axis=2)
    dt = log_dt.reshape(G, 1, 1)
    bt_r = jnp.tile(jnp.swapaxes(b_re, 1, 2), (1, 1, 2))
    bt_i = jnp.tile(jnp.swapaxes(b_im, 1, 2), (1, 1, 2))
    ce_r = jnp.tile(jnp.swapaxes(c_re, 1, 2), (1, 1, S5_TOK))
    ce_i = jnp.tile(jnp.swapaxes(c_im, 1, 2), (1, 1, S5_TOK))
    d = jnp.pad(d_skip, ((0, 0), (0, 256 - I))).reshape(G, 1, 256)
    blk = lambda *s: pl.BlockSpec((1,) + s, lambda g: (g, 0, 0))
    return pl.pallas_call(
        _s5_prep_body,
        grid=(G,),
        in_specs=[blk(8, 128), blk(P, 2), blk(1, 1), blk(16, 128), blk(16, 128), blk(P, 256), blk(P, 256),
                  blk(1, 256)],
        out_specs=[blk(256, 256), blk(256, 128), blk(128, 256), blk(32, 128)],
        out_shape=[SDS((G, 256, 256), BF16), SDS((G, 256, 128), BF16), SDS((G, 128, 256), BF16),
                   SDS((G, 32, 128), F32)],
        compiler_params=_params(("arbitrary",)),
        name="s5_prep",
    )(lam, lamc, dt, bt_r, bt_i, ce_r, ce_i, d)


def _gelu_tanh(x):
    return 0.5 * x * (1.0 + jnp.tanh(math.sqrt(2.0 / math.pi) * (x + 0.044715 * (x * x * x))))


def _s5_body(u_ref, t_ref, w_ref, v_ref, coef_ref, z_ref):
    u = u_ref[0, 0]
    R = u.shape[0]
    P = S5_STATE
    y = _dot(u, t_ref[0])
    x = _dot(u, w_ref[0])
    row = lax.broadcasted_iota(jnp.int32, (R, 128), 0)

    def shift_down(a, sh):
        if sh % 8 == 0:
            return jnp.concatenate([jnp.zeros((sh, 128), F32), a[:R - sh]], axis=0)
        return jnp.where(row < sh, 0.0, pltpu.roll(a, sh, 0))

    k = 0
    while (1 << k) < R:
        xs = shift_down(x, 1 << k)
        x = x + coef_ref[0, k:k + 1, :] * xs + coef_ref[0, 16 + k:17 + k, :] * pltpu.roll(xs, P, 1)
        k += 1
    xprev = shift_down(x, 1)
    y = y + _dot(xprev.astype(BF16), v_ref[0])
    z_ref[0, 0] = _gelu_tanh(y).astype(z_ref.dtype)


def s5_scan(u_flat, tm, wm, vm, coef):
    b, g, r, _ = u_flat.shape
    per_g = lambda *s: pl.BlockSpec((1,) + s, lambda bi, gi: (gi, 0, 0))
    return pl.pallas_call(
        _s5_body,
        grid=(b, g),
        in_specs=[pl.BlockSpec((1, 1, r, 256), lambda bi, gi: (bi, gi, 0, 0)),
                  per_g(256, 256), per_g(256, 128), per_g(128, 256), per_g(32, 128)],
        out_specs=pl.BlockSpec((1, 1, r, 256), lambda bi, gi: (bi, gi, 0, 0)),
        out_shape=SDS(u_flat.shape, BF16),
        compiler_params=_params(("parallel", "parallel")),
        name="s5_scan",
    )(u_flat, tm, wm, vm, coef)


def _glu_body(z_ref, wl_ref, wg_ref, o_ref):
    z = z_ref[...]
    o_ref[...] = (_dot(z, wl_ref[...]) * _sigmoid(_dot(z, wg_ref[...]))).astype(o_ref.dtype)


def glu(z, w, tm, tn):
    t, k = z.shape
    n = w.shape[1] // 2
    nj = n // tn
    return pl.pallas_call(
        _glu_body,
        grid=(t // tm, nj),
        in_specs=[pl.BlockSpec((tm, k), lambda i, j: (i, 0)),
                  pl.BlockSpec((k, tn), lambda i, j: (0, j)),
                  pl.BlockSpec((k, tn), lambda i, j: (0, nj + j))],
        out_specs=pl.BlockSpec((tm, tn), lambda i, j: (i, j)),
        out_shape=SDS((t, n), BF16),
        compiler_params=_params(("parallel", "arbitrary")),
        name="glu",
    )(z, w, w)


def _mix_body(ya_ref, o0_ref, o1_ref, o2_ref, l0_ref, l1_ref, l2_ref, yc_ref,
              wa_ref, wb_ref, wc_ref, ga_ref, gb_ref, gc_ref, out_ref, yb_ref, wt_ref, acc_ref):
    tm = ya_ref.shape[0]
    E = ATTN_HEAD_DIM

    @pl.when(pl.program_id(1) == 0)
    def _():
        o_refs = (o0_ref, o1_ref, o2_ref)
        l_refs = (l0_ref, l1_ref, l2_ref)
        for g, (_, r) in enumerate(ATTN_GROUPS):
            for c in range(r):
                wt_ref[g, pl.ds(c, tm // r, stride=r), :] = l_refs[g][0, c]
        l0, l1, l2 = wt_ref[0], wt_ref[1], wt_ref[2]
        m = jnp.maximum(jnp.maximum(l0, l1), l2)
        e0, e1, e2 = jnp.exp(l0 - m), jnp.exp(l1 - m), jnp.exp(l2 - m)
        inv = 1.0 / (e0 + e1 + e2)
        wt_ref[0] = e0 * inv
        wt_ref[1] = e1 * inv
        wt_ref[2] = e2 * inv
        for g, (_, r) in enumerate(ATTN_GROUPS):
            for c in range(r):
                rows = pl.ds(c, tm // r, stride=r)
                w = wt_ref[g, rows, :]
                for h in range(ATTN_HEADS):
                    part = w[:, h:h + 1] * o_refs[g][0, c, :, h * E:(h + 1) * E].astype(F32)
                    if g == 0:
                        acc_ref[h, rows, :] = part
                    else:
                        acc_ref[h, rows, :] += part
        for h in range(ATTN_HEADS):
            yb_ref[:, h * E:(h + 1) * E] = acc_ref[h].astype(yb_ref.dtype)

    mix = (_sigmoid(ga_ref[...].astype(F32)) * _dot(ya_ref[...], wa_ref[...])
           + _sigmoid(gb_ref[...].astype(F32)) * _dot(yb_ref[...], wb_ref[...])
           + _sigmoid(gc_ref[...].astype(F32)) * _dot(yc_ref[...], wc_ref[...]))
    out_ref[...] = mix.astype(out_ref.dtype)


def gated_mix(ya, outs, lses, yc, wa, wb, wc, proj, seq, tm, tn):
    t, kw = ya.shape
    d = wa.shape[1]
    go = COL_G // tn
    nbt = seq // tm
    row = lambda w: pl.BlockSpec((tm, w), lambda i, j: (i, 0))
    grp = lambda r, w: pl.BlockSpec((1, r, tm // r, w), lambda i, j: (i // nbt, 0, i % nbt, 0))
    wsp = pl.BlockSpec((kw, tn), lambda i, j: (0, j))
    gate = lambda o: pl.BlockSpec((tm, tn), lambda i, j: (i, go + o * (d // tn) + j))
    dils = [r for _, r in ATTN_GROUPS]
    return pl.pallas_call(
        _mix_body,
        grid=(t // tm, d // tn),
        in_specs=[row(kw)] + [grp(r, kw) for r in dils] + [grp(r, LANES) for r in dils] + [row(kw)]
        + [wsp, wsp, wsp, gate(0), gate(1), gate(2)],
        out_specs=pl.BlockSpec((tm, tn), lambda i, j: (i, j)),
        out_shape=SDS((t, d), BF16),
        scratch_shapes=[pltpu.VMEM((tm, kw), BF16), pltpu.VMEM((len(dils), tm, LANES), F32),
                        pltpu.VMEM((ATTN_HEADS, tm, LANES), F32)],
        compiler_params=_params(("parallel", "arbitrary")),
        name="gated_mix",
    )(ya, *outs, *lses, yc, wa, wb, wc, proj, proj, proj)


def _outproj_body(x_ref, m_ref, w_ref, o_ref):
    o_ref[...] = x_ref[...] + _dot(m_ref[...], w_ref[...])


def out_proj(x2d, mix, w, tm, tn):
    t, d = x2d.shape
    return pl.pallas_call(
        _outproj_body,
        grid=(t // tm, d // tn),
        in_specs=[pl.BlockSpec((tm, tn), lambda i, j: (i, j)),
                  pl.BlockSpec((tm, d), lambda i, j: (i, 0)),
                  pl.BlockSpec((d, tn), lambda i, j: (0, j))],
        out_specs=pl.BlockSpec((tm, tn), lambda i, j: (i, j)),
        out_shape=SDS((t, d), F32),
        compiler_params=_params(("parallel", "arbitrary")),
        name="out_proj",
    )(x2d, mix, w)


def _ffn_body(x_ref, g_ref, w1_ref, w2_ref, o_ref, xn_ref):
    @pl.when(pl.program_id(1) == 0)
    def _():
        x = x_ref[...]
        xn_ref[...] = _rms(x, g_ref[...]).astype(BF16)
        o_ref[...] = x

    hid = jnp.maximum(_dot(xn_ref[...], w1_ref[...]), 0.0)
    o_ref[...] += _dot((hid * hid).astype(BF16), w2_ref[...])


def ffn(x2d, g, w1, w2, tm, th):
    t, d = x2d.shape
    hdim = w1.shape[1]
    return pl.pallas_call(
        _ffn_body,
        grid=(t // tm, hdim // th),
        in_specs=[pl.BlockSpec((tm, d), lambda i, j: (i, 0)),
                  pl.BlockSpec((1, d), lambda i, j: (0, 0)),
                  pl.BlockSpec((d, th), lambda i, j: (0, j)),
                  pl.BlockSpec((th, d), lambda i, j: (j, 0))],
        out_specs=pl.BlockSpec((tm, d), lambda i, j: (i, 0)),
        out_shape=SDS((t, d), F32),
        scratch_shapes=[pltpu.VMEM((tm, d), BF16)],
        compiler_params=_params(("parallel", "arbitrary")),
        name="ffn",
    )(x2d, g, w1, w2)


def _norm_body(x_ref, g_ref, o_ref):
    o_ref[...] = _rms(x_ref[...], g_ref[...])


def final_norm(x2d, g, tm):
    t, d = x2d.shape
    return pl.pallas_call(
        _norm_body,
        grid=(t // tm,),
        in_specs=[pl.BlockSpec((tm, d), lambda i: (i, 0)), pl.BlockSpec((1, d), lambda i: (0, 0))],
        out_specs=pl.BlockSpec((tm, d), lambda i: (i, 0)),
        out_shape=SDS((t, d), F32),
        compiler_params=_params(("parallel",)),
        name="final_norm",
    )(x2d, g)


def _tile(n, want):
    t = min(n, want)
    assert n % t == 0, (n, want)
    return t


def _layer(x2d, bias, batch, seq, p):
    t, d = x2d.shape
    tm = _tile(seq, 1024)
    n_gate0 = 4 * MLSTM_WIDTH
    n_att0 = n_gate0 + 2 * MLSTM_HEADS
    n_att = 3 * ATTN_HEADS_TOTAL * ATTN_HEAD_DIM
    n_groups = len(ATTN_GROUPS)
    w_in = p["w_in"]
    w_att = w_in[:, n_att0:n_att0 + n_att].reshape(d, 3, n_groups, ATTN_WIDTH)
    w_att = w_att.transpose(0, 2, 1, 3).reshape(d, n_att)
    w_main = jnp.concatenate([w_in[:, :n_gate0], w_in[:, n_att0 + n_att:], w_att], axis=1).astype(BF16)
    w_if = jnp.pad(w_in[:, n_gate0:n_att0], ((0, 0), (0, N_GATE_PAD - 2 * MLSTM_HEADS))).astype(BF16)
    proj, gif, *qkvs = in_proj(x2d, p["norm1_g"].reshape(1, d), w_main, w_if, batch, seq, tm)

    bif = jnp.concatenate([p["b_igate"], p["b_fgate"]]).astype(F32)
    bcol = jnp.pad(bif, (0, N_GATE_PAD - bif.shape[0])).reshape(1, N_GATE_PAD)
    ya = mlstm(proj, gif, gif[:, :8].T, bcol, bif.reshape(8, 1), p["conv_w"], p["conv_b"].reshape(1, -1),
               p["mh_norm_g"].reshape(1, -1), batch, seq)

    outs, lses = zip(*[attn_group(qkvs[g], bias, g) for g in range(n_groups)])

    tmat, wmat, vmat, coef = s5_prep(p["lam_re"], p["lam_im"], p["log_dt"], p["b_re"], p["b_im"],
                                     p["c_re"], p["c_im"], p["d_skip"])
    nblk = seq // S5_TOK
    u = proj[:, COL_U:COL_U + S5_WIDTH].reshape(batch, nblk, S5_TOK, S5_GROUPS, S5_GROUP)
    u = u.transpose(0, 3, 1, 2, 4).reshape(batch, S5_GROUPS, nblk, S5_TOK * S5_GROUP)
    z = s5_scan(u, tmat, wmat, vmat, coef)
    z = z.reshape(batch, S5_GROUPS, nblk, S5_TOK, S5_GROUP).transpose(0, 2, 3, 1, 4).reshape(t, S5_WIDTH)
    yc = glu(z, p["w_glu"].astype(BF16), tm, 512)

    mix = gated_mix(ya, outs, lses, yc, p["w_br_a"].astype(BF16), p["w_br_b"].astype(BF16),
                    p["w_br_c"].astype(BF16), proj, seq, tm, 512)
    x2d = out_proj(x2d, mix, p["w_out"].astype(BF16), tm, 512)
    return ffn(x2d, p["norm2_g"].reshape(1, d), p["w_ff1"].astype(BF16), p["w_ff2"].astype(BF16),
               _tile(t, 512), 1024)


_PER_LAYER = ("norm1_g", "w_in", "conv_w", "conv_b", "b_igate", "b_fgate", "mh_norm_g", "lam_re", "lam_im",
              "log_dt", "b_re", "b_im", "c_re", "c_im", "d_skip", "w_glu", "w_br_a", "w_br_b", "w_br_c",
              "w_out", "norm2_g", "w_ff1", "w_ff2")


def kernel(x, norm1_g, w_in, conv_w, conv_b, b_igate, b_fgate, mh_norm_g, rel_bias, lam_re, lam_im, log_dt,
           b_re, b_im, c_re, c_im, d_skip, w_glu, w_br_a, w_br_b, w_br_c, w_out, norm2_g, w_ff1, w_ff2,
           final_g):
    stacked = dict(norm1_g=norm1_g, w_in=w_in, conv_w=conv_w, conv_b=conv_b, b_igate=b_igate,
                   b_fgate=b_fgate, mh_norm_g=mh_norm_g, lam_re=lam_re, lam_im=lam_im, log_dt=log_dt,
                   b_re=b_re, b_im=b_im, c_re=c_re, c_im=c_im, d_skip=d_skip, w_glu=w_glu, w_br_a=w_br_a,
                   w_br_b=w_br_b, w_br_c=w_br_c, w_out=w_out, norm2_g=norm2_g, w_ff1=w_ff1, w_ff2=w_ff2)
    batch, seq, d = x.shape
    x2d = x.astype(F32).reshape(batch * seq, d)
    bias = attn_bias(rel_bias.astype(F32))
    for l in range(w_in.shape[0]):
        x2d = _layer(x2d, bias, batch, seq, {k: stacked[k][l] for k in _PER_LAYER})
    out = final_norm(x2d, final_g.reshape(1, d), _tile(batch * seq, 1024))
    return out.reshape(batch, seq, d).astype(x.dtype)
```

```python
import functools
import math

import jax
import jax.numpy as jnp
from jax import lax
from jax.experimental import pallas as pl
from jax.experimental.pallas import tpu as pltpu

F32 = jnp.float32
BF16 = jnp.bfloat16
SDS = jax.ShapeDtypeStruct

NORM_EPS = 1e-6
NEG = -1e30

MLSTM_HEADS = 4
MLSTM_HEAD_DIM = 256
MLSTM_WIDTH = MLSTM_HEADS * MLSTM_HEAD_DIM
CONV_WIDTH = 4
ATTN_GROUPS = ((128, 1), (512, 4), (2048, 16))
ATTN_HEADS = 8
ATTN_HEAD_DIM = 128
ATTN_WIDTH = ATTN_HEADS * ATTN_HEAD_DIM
ATTN_HEADS_TOTAL = len(ATTN_GROUPS) * ATTN_HEADS
ATTN_BLOCK = 128
REL_BUCKETS = 32
REL_MAX_DISTANCE = 2048
S5_WIDTH = 1024
S5_GROUP = 16
S5_GROUPS = S5_WIDTH // S5_GROUP
S5_STATE = 64
S5_TOK = 16
N_GATE_PAD = 128

COL_U = 4 * MLSTM_WIDTH
COL_G = COL_U + S5_WIDTH
D_MODEL = 2048
N_MAIN = COL_G + 3 * D_MODEL
LANES = 128

MLSTM_CHUNK = 256
VMEM_LIMIT = 56 * 2**20


def _params(sem):
    return pltpu.CompilerParams(dimension_semantics=sem, vmem_limit_bytes=VMEM_LIMIT)


def _sigmoid(x):
    return 1.0 / (1.0 + jnp.exp(-x))


def _log_sigmoid(x):
    return jnp.minimum(x, 0.0) - jnp.log(1.0 + jnp.exp(-jnp.abs(x)))


def _rms(x, g):
    ms = jnp.mean(x * x, axis=-1, keepdims=True)
    return x * lax.rsqrt(ms + NORM_EPS) * g


def _dot(a, b):
    return jnp.dot(a, b, preferred_element_type=F32)


def _dot_nt(a, b):
    return lax.dot_general(a, b, (((1,), (1,)), ((), ())), preferred_element_type=F32)


def _dot_tn(a, b):
    return lax.dot_general(a, b, (((0,), (0,)), ((), ())), preferred_element_type=F32)


def _inproj_body(nm, x_ref, g_ref, w_ref, wif_ref, o_ref, oif_ref, a0_ref, a1_ref, a2_ref, xn_ref, acc_ref):
    j = pl.program_id(1)
    tm = x_ref.shape[0]
    tn = w_ref.shape[1]

    @pl.when(j == 0)
    def _():
        xn = _rms(x_ref[...], g_ref[...]).astype(BF16)
        xn_ref[...] = xn
        oif_ref[...] = _dot(xn, wif_ref[...])

    @pl.when(j < nm)
    def _():
        o_ref[...] = _dot(xn_ref[...], w_ref[...]).astype(o_ref.dtype)

    for g, a_ref in enumerate((a0_ref, a1_ref, a2_ref)):
        r = ATTN_GROUPS[g][1]
        lo = nm + 3 * g

        @pl.when(jnp.logical_and(j >= lo, j < lo + 3))
        def _(a_ref=a_ref, r=r):
            res = _dot(xn_ref[...], w_ref[...])
            if r == 1:
                a_ref[0, 0] = res.astype(a_ref.dtype)
            else:
                for s in range(tn // LANES):
                    acc_ref[s] = res[:, s * LANES:(s + 1) * LANES]
                for c in range(r):
                    for s in range(tn // LANES):
                        a_ref[0, c, :, s * LANES:(s + 1) * LANES] = (
                            acc_ref[s, pl.ds(c, tm // r, stride=r), :].astype(a_ref.dtype))


def in_proj(x2d, g, w, wif, batch, seq, tm):
    t, d = x2d.shape
    tn = ATTN_WIDTH
    nm = N_MAIN // tn
    nbt = seq // tm
    n_groups = len(ATTN_GROUPS)

    def a_spec(gi):
        r = ATTN_GROUPS[gi][1]
        return pl.BlockSpec((1, r, tm // r, tn),
                            lambda i, j: (i // nbt, 0, i % nbt, jnp.clip(j - nm - 3 * gi, 0, 2)))

    return pl.pallas_call(
        functools.partial(_inproj_body, nm),
        grid=(t // tm, nm + 3 * n_groups),
        in_specs=[pl.BlockSpec((tm, d), lambda i, j: (i, 0)),
                  pl.BlockSpec((1, d), lambda i, j: (0, 0)),
                  pl.BlockSpec((d, tn), lambda i, j: (0, j)),
                  pl.BlockSpec((d, N_GATE_PAD), lambda i, j: (0, 0))],
        out_specs=[pl.BlockSpec((tm, tn), lambda i, j: (i, jnp.minimum(j, nm - 1))),
                   pl.BlockSpec((tm, N_GATE_PAD), lambda i, j: (i, 0))] + [a_spec(gi) for gi in range(n_groups)],
        out_shape=[SDS((t, N_MAIN), BF16), SDS((t, N_GATE_PAD), F32)]
        + [SDS((batch, r, seq // r, 3 * tn), BF16) for _, r in ATTN_GROUPS],
        scratch_shapes=[pltpu.VMEM((tm, d), BF16), pltpu.VMEM((tn // LANES, tm, LANES), F32)],
        compiler_params=_params(("parallel", "arbitrary")),
        name="in_proj",
    )(x2d, g, w, wif)


def _mlstm_body(q_ref, k_ref, v_ref, og_ref, gcol_ref, grow_ref, bcol_ref, brow_ref,
                cwq_ref, cwk_ref, cbq_ref, cbk_ref, ng_ref, y_ref,
                ct_ref, m_ref, tq_ref, tk_ref):
    h = pl.program_id(1)
    c = pl.program_id(2)
    L, E = q_ref.shape

    @pl.when(c == 0)
    def _():
        ct_ref[...] = jnp.zeros_like(ct_ref)
        m_ref[...] = jnp.zeros_like(m_ref)
        tq_ref[...] = jnp.zeros_like(tq_ref)
        tk_ref[...] = jnp.zeros_like(tk_ref)

    row8 = lax.broadcasted_iota(jnp.int32, (8, E), 0)

    def conv_silu(x_ref, tail_ref, w_ref, b_ref):
        x = x_ref[...].astype(F32)
        tail = tail_ref[...]
        w = w_ref[...]
        acc = b_ref[...] + w[CONV_WIDTH - 1:CONV_WIDTH] * x
        for d in range(1, CONV_WIDTH):
            r = pltpu.roll(x, d, 0)
            head = jnp.where(row8 < d, pltpu.roll(tail, d, 0), r[:8])
            xs = jnp.concatenate([head, r[8:]], axis=0)
            acc = acc + w[CONV_WIDTH - 1 - d:CONV_WIDTH - d] * xs
        tail_ref[...] = x[L - 8:]
        return acc * _sigmoid(acc)

    q = conv_silu(q_ref, tq_ref, cwq_ref, cbq_ref)
    k = conv_silu(k_ref, tk_ref, cwk_ref, cbk_ref) * (E ** -0.5)

    gc = gcol_ref[...] + bcol_ref[...]
    lane = lax.broadcasted_iota(jnp.int32, gc.shape, 1)
    ig_col = jnp.sum(jnp.where(lane == h, gc, 0.0), axis=1, keepdims=True)
    lf_col = _log_sigmoid(jnp.sum(jnp.where(lane == h + MLSTM_HEADS, gc, 0.0), axis=1, keepdims=True))
    gr = grow_ref[...] + brow_ref[...]
    sub = lax.broadcasted_iota(jnp.int32, gr.shape, 0)
    ig_row = jnp.sum(jnp.where(sub == h, gr, 0.0), axis=0, keepdims=True)
    lf_row = _log_sigmoid(jnp.sum(jnp.where(sub == h + MLSTM_HEADS, gr, 0.0), axis=0, keepdims=True))

    tt = lax.broadcasted_iota(jnp.int32, (L, L), 0)
    ss = lax.broadcasted_iota(jnp.int32, (L, L), 1)
    causal = ss <= tt
    bcum_col = jnp.sum(jnp.where(causal, lf_row, 0.0), axis=1, keepdims=True)
    bcum_row = jnp.sum(jnp.where(tt <= ss, lf_col, 0.0), axis=0, keepdims=True)
    a_row = ig_row - bcum_row
    a_col = ig_col - bcum_col

    m_prev = m_ref[...]
    amat = jnp.where(causal, a_row, NEG)
    mrow = jnp.maximum(m_prev, jnp.max(amat, axis=1, keepdims=True))
    w_intra = jnp.exp(amat - mrow)
    w_inter = jnp.exp(m_prev - mrow)

    qb = q.astype(BF16)
    kb = k.astype(BF16)
    vaug = jnp.concatenate([v_ref[...], jnp.ones((L, 128), BF16)], axis=1)
    s = _dot_nt(qb, kb) * w_intra
    ct = ct_ref[...]
    num_aug = _dot(s.astype(BF16), vaug) + w_inter * _dot(qb, ct.astype(BF16))
    num = num_aug[:, :E]
    den = num_aug[:, E:E + 1]
    m_t = bcum_col + mrow
    hout = num / jnp.maximum(jnp.abs(den), jnp.exp(-m_t))
    hn = _rms(hout, ng_ref[...])
    y_ref[...] = (_sigmoid(og_ref[...].astype(F32)) * hn).astype(y_ref.dtype)

    b_last = jnp.sum(lf_col, axis=0, keepdims=True)
    g_col = b_last + a_col
    m_new = jnp.maximum(b_last + m_prev, jnp.max(g_col, axis=0, keepdims=True))
    w_s = jnp.exp(g_col - m_new)
    decay = jnp.exp(b_last + m_prev - m_new)
    kw = (k * w_s).astype(BF16)
    ct_ref[...] = decay * ct + _dot_tn(kw, vaug)
    m_ref[...] = m_new


def mlstm(proj, gif, gif_t, bcol, brow, conv_w, conv_b, ng, batch, seq):
    L = min(MLSTM_CHUNK, seq)
    E = MLSTM_HEAD_DIM
    nc = seq // L
    H = MLSTM_HEADS
    t = batch * seq
    row = lambda b, h, c: b * nc + c
    return pl.pallas_call(
        _mlstm_body,
        grid=(batch, H, nc),
        in_specs=[pl.BlockSpec((L, E), lambda b, h, c: (row(b, h, c), h)),
                  pl.BlockSpec((L, E), lambda b, h, c: (row(b, h, c), H + h)),
                  pl.BlockSpec((L, E), lambda b, h, c: (row(b, h, c), 2 * H + h)),
                  pl.BlockSpec((L, E), lambda b, h, c: (row(b, h, c), 3 * H + h)),
                  pl.BlockSpec((L, N_GATE_PAD), lambda b, h, c: (row(b, h, c), 0)),
                  pl.BlockSpec((8, L), lambda b, h, c: (0, row(b, h, c))),
                  pl.BlockSpec((1, N_GATE_PAD), lambda b, h, c: (0, 0)),
                  pl.BlockSpec((8, 1), lambda b, h, c: (0, 0)),
                  pl.BlockSpec((CONV_WIDTH, E), lambda b, h, c: (0, h)),
                  pl.BlockSpec((CONV_WIDTH, E), lambda b, h, c: (0, H + h)),
                  pl.BlockSpec((1, E), lambda b, h, c: (0, h)),
                  pl.BlockSpec((1, E), lambda b, h, c: (0, H + h)),
                  pl.BlockSpec((1, E), lambda b, h, c: (0, h))],
        out_specs=pl.BlockSpec((L, E), lambda b, h, c: (row(b, h, c), h)),
        out_shape=SDS((t, MLSTM_WIDTH), BF16),
        scratch_shapes=[pltpu.VMEM((E, E + 128), F32), pltpu.VMEM((1, 1), F32),
                        pltpu.VMEM((8, E), F32), pltpu.VMEM((8, E), F32)],
        compiler_params=_params(("parallel", "parallel", "arbitrary")),
        name="mlstm",
    )(proj, proj, proj, proj, gif, gif_t, bcol, brow, conv_w, conv_w, conv_b, conv_b, ng)


def _bias_body(table_ref, o_ref):
    h = pl.program_id(0)
    dil = jnp.where(h < ATTN_HEADS, ATTN_GROUPS[0][1],
                    jnp.where(h < 2 * ATTN_HEADS, ATTN_GROUPS[1][1], ATTN_GROUPS[2][1]))
    shape = (ATTN_BLOCK, 2 * ATTN_BLOCK)
    i = lax.broadcasted_iota(jnp.int32, shape, 0)
    j = lax.broadcasted_iota(jnp.int32, shape, 1)
    rel = ATTN_BLOCK + i - j
    dist = jnp.maximum(rel, 0) * dil
    max_exact = REL_BUCKETS // 2
    nf = jnp.maximum(dist, max_exact).astype(F32)
    large = max_exact + (jnp.log(nf / max_exact) / math.log(REL_MAX_DISTANCE / max_exact)
                         * (REL_BUCKETS - max_exact)).astype(jnp.int32)
    large = jnp.minimum(large, REL_BUCKETS - 1)
    bucket = jnp.where(dist < max_exact, dist, large)
    acc = jnp.zeros(shape, F32)
    for b in range(REL_BUCKETS):
        acc = jnp.where(bucket == b, table_ref[b, h], acc)
    o_ref[0] = jnp.where(rel >= 0, jnp.where(rel <= ATTN_BLOCK, acc, NEG), NEG)


def attn_bias(rel_bias):
    return pl.pallas_call(
        _bias_body,
        grid=(ATTN_HEADS_TOTAL,),
        in_specs=[pl.BlockSpec(memory_space=pltpu.SMEM)],
        out_specs=pl.BlockSpec((1, ATTN_BLOCK, 2 * ATTN_BLOCK), lambda h: (h, 0, 0)),
        out_shape=SDS((ATTN_HEADS_TOTAL, ATTN_BLOCK, 2 * ATTN_BLOCK), F32),
        compiler_params=_params(("arbitrary",)),
        name="attn_bias",
    )(rel_bias)


def _attn_body(q_ref, kp_ref, kc_ref, vp_ref, vc_ref, bias_ref, o_ref, lse_ref):
    n = pl.program_id(2)
    B = ATTN_BLOCK
    E = ATTN_HEAD_DIM
    scale = E ** -0.5
    no_prev = jnp.where(n > 0, 0.0, NEG)
    lane = lax.broadcasted_iota(jnp.int32, (B, LANES), 1)
    lse = jnp.zeros((B, LANES), F32)
    for h in range(ATTN_HEADS):
        sl = slice(h * E, (h + 1) * E)
        q = q_ref[0, 0, :, sl]
        sc = _dot_nt(q, kc_ref[0, 0, :, sl]) * scale + bias_ref[h, :, B:]
        sp = _dot_nt(q, kp_ref[0, 0, :, sl]) * scale + bias_ref[h, :, :B] + no_prev
        m = jnp.maximum(jnp.max(sc, axis=1, keepdims=True), jnp.max(sp, axis=1, keepdims=True))
        pc = jnp.exp(sc - m)
        pp = jnp.exp(sp - m)
        den = jnp.sum(pc, axis=1, keepdims=True) + jnp.sum(pp, axis=1, keepdims=True)
        o = (_dot(pc.astype(BF16), vc_ref[0, 0, :, sl]) + _dot(pp.astype(BF16), vp_ref[0, 0, :, sl])) / den
        o_ref[0, 0, :, sl] = o.astype(o_ref.dtype)
        lse = jnp.where(lane == h, m + jnp.log(den), lse)
    lse_ref[0, 0] = lse


def attn_group(qkv, bias, g):
    batch, r, l, _ = qkv.shape
    W = ATTN_WIDTH
    nb = l // ATTN_BLOCK
    blk = (1, 1, ATTN_BLOCK, W)
    prev = lambda n: jnp.maximum(n - 1, 0)
    return pl.pallas_call(
        _attn_body,
        grid=(batch, r, nb),
        in_specs=[pl.BlockSpec(blk, lambda b, c, n: (b, c, n, 0)),
                  pl.BlockSpec(blk, lambda b, c, n: (b, c, prev(n), 1)),
                  pl.BlockSpec(blk, lambda b, c, n: (b, c, n, 1)),
                  pl.BlockSpec(blk, lambda b, c, n: (b, c, prev(n), 2)),
                  pl.BlockSpec(blk, lambda b, c, n: (b, c, n, 2)),
                  pl.BlockSpec((ATTN_HEADS, ATTN_BLOCK, 2 * ATTN_BLOCK), lambda b, c, n: (g, 0, 0))],
        out_specs=[pl.BlockSpec(blk, lambda b, c, n: (b, c, n, 0)),
                   pl.BlockSpec((1, 1, ATTN_BLOCK, LANES), lambda b, c, n: (b, c, n, 0))],
        out_shape=[SDS((batch, r, l, W), BF16), SDS((batch, r, l, LANES), F32)],
        compiler_params=_params(("parallel", "parallel", "arbitrary")),
        name=f"attn_g{g}",
    )(qkv, qkv, qkv, qkv, qkv, bias)


def _s5_prep_body(lam_ref, lamc_ref, dt_ref, bt_r_ref, bt_i_ref, ce_r_ref, ce_i_ref, d_ref,
                  t_ref, w_ref, v_ref, coef_ref):
    P = S5_STATE
    lr = lam_ref[0, 0:1, :]
    li = lam_ref[0, 1:2, :]
    dt = jnp.exp(dt_ref[0])

    def apow(e, lr_, li_):
        mag = jnp.exp(lr_ * dt * e)
        ang = li_ * dt * e
        return mag * jnp.cos(ang), mag * jnp.sin(ang)

    one = jnp.ones((1, 1), F32)
    ar, ai = apow(one, lr, li)
    nr = ar - 1.0
    den = lr * lr + li * li
    f_re = (nr * lr + ai * li) / den
    f_im = (ai * lr - nr * li) / den
    bt_r = bt_r_ref[0]
    bt_i = bt_i_ref[0]
    bb_r = f_re * bt_r - f_im * bt_i
    bb_i = f_re * bt_i + f_im * bt_r

    lrc = lamc_ref[0, :, 0:1]
    lic = lamc_ref[0, :, 1:2]
    lag = (lax.broadcasted_iota(jnp.int32, (P, 256), 1) // S5_GROUP).astype(F32)
    adr, adi = apow(lag, lrc, lic)
    ce_r = ce_r_ref[0]
    ce_i = ce_i_ref[0]
    ca_r = ce_r * adr - ce_i * adi
    ca_i = ce_r * adi + ce_i * adr
    hp = lax.Precision.HIGHEST
    ks = (jnp.dot(bb_r[:, :P], ca_r, precision=hp, preferred_element_type=F32)
          - jnp.dot(bb_i[:, :P], ca_i, precision=hp, preferred_element_type=F32))
    si = lax.broadcasted_iota(jnp.int32, (S5_GROUP, 256), 0)
    lj = lax.broadcasted_iota(jnp.int32, (S5_GROUP, 256), 1)
    ks = ks + jnp.where(si == lj, d_ref[0], 0.0)
    for s in range(S5_TOK):
        sh = s * S5_GROUP
        blk = ks if s == 0 else jnp.where(lj >= sh, pltpu.roll(ks, sh, 1), 0.0)
        t_ref[0, sh:sh + S5_GROUP, :] = blk.astype(t_ref.dtype)

    lane128 = lax.broadcasted_iota(jnp.int32, (1, 128), 1)
    for s in range(S5_TOK):
        pr, pi = apow(float(S5_TOK - 1 - s) * one, lr, li)
        p1 = jnp.where(lane128 < P, pr, pi)
        p2 = jnp.where(lane128 < P, -pi, pr)
        w_ref[0, s * S5_GROUP:(s + 1) * S5_GROUP, :] = (bb_r * p1 + bb_i * p2).astype(w_ref.dtype)

    adr1, adi1 = apow(lag + 1.0, lrc, lic)
    v_ref[0, 0:P, :] = (ce_r * adr1 - ce_i * adi1).astype(v_ref.dtype)
    v_ref[0, P:2 * P, :] = (-(ce_r * adi1 + ce_i * adr1)).astype(v_ref.dtype)

    ek = jnp.left_shift(S5_TOK, lax.broadcasted_iota(jnp.int32, (16, 1), 0)).astype(F32)
    cr, ci = apow(ek, lr, li)
    coef_ref[0, 0:16, :] = cr
    coef_ref[0, 16:32, :] = jnp.where(lane128 < P, -ci, ci)


def s5_prep(lam_re, lam_im, log_dt, b_re, b_im, c_re, c_im, d_skip):
    G, P, I = b_re.shape
    lam = jnp.stack([jnp.tile(lam_re, (1, 2)), jnp.tile(lam_im, (1, 2))], axis=1)
    lam = jnp.pad(lam, ((0, 0), (0, 6), (0, 0)))
    lamc = jnp.stack([lam_re, lam_im], axis=2)
    dt = log_dt.reshape(G, 1, 1)
    bt_r = jnp.tile(jnp.swapaxes(b_re, 1, 2), (1, 1, 2))
    bt_i = jnp.tile(jnp.swapaxes(b_im, 1, 2), (1, 1, 2))
    ce_r = jnp.tile(jnp.swapaxes(c_re, 1, 2), (1, 1, S5_TOK))
    ce_i = jnp.tile(jnp.swapaxes(c_im, 1, 2), (1, 1, S5_TOK))
    d = jnp.pad(d_skip, ((0, 0), (0, 256 - I))).reshape(G, 1, 256)
    blk = lambda *s: pl.BlockSpec((1,) + s, lambda g: (g, 0, 0))
    return pl.pallas_call(
        _s5_prep_body,
        grid=(G,),
        in_specs=[blk(8, 128), blk(P, 2), blk(1, 1), blk(16, 128), blk(16, 128), blk(P, 256), blk(P, 256),
                  blk(1, 256)],
        out_specs=[blk(256, 256), blk(256, 128), blk(128, 256), blk(32, 128)],
        out_shape=[SDS((G, 256, 256), BF16), SDS((G, 256, 128), BF16), SDS((G, 128, 256), BF16),
                   SDS((G, 32, 128), F32)],
        compiler_params=_params(("arbitrary",)),
        name="s5_prep",
    )(lam, lamc, dt, bt_r, bt_i, ce_r, ce_i, d)


def _gelu_tanh(x):
    return 0.5 * x * (1.0 + jnp.tanh(math.sqrt(2.0 / math.pi) * (x + 0.044715 * (x * x * x))))


def _chunk_transpose(arrs, chunk):
    a = list(arrs)
    n = len(a)
    d = n // 2
    while d >= 1:
        bit = (chunk & d) != 0
        nxt = list(a)
        for i in range(n):
            if i & d == 0:
                lo, hi = a[i], a[i + d]
                nxt[i] = jnp.where(bit, pltpu.roll(hi, d * S5_GROUP, 1), lo)
                nxt[i + d] = jnp.where(bit, hi, pltpu.roll(lo, LANES - d * S5_GROUP, 1))
        a = nxt
        d //= 2
    return a


def _s5_body(u_ref, t_ref, w_ref, v_ref, coef_ref, z_ref, nat_ref, uf_ref, zs_ref):
    S = u_ref.shape[0]
    R = S // S5_TOK
    P = S5_STATE
    GL = LANES // S5_GROUP
    RC = min(R, 64)
    chunk = lax.broadcasted_iota(jnp.int32, (RC, LANES), 1) // S5_GROUP

    nat_ref[...] = u_ref[...].astype(F32)

    def to_flat(rc, carry):
        r0 = pl.multiple_of(rc * RC, RC)
        for hf in range(S5_TOK // GL):
            arrs = [nat_ref[pl.ds(r0 * S5_TOK + hf * GL + k, RC, stride=S5_TOK), :] for k in range(GL)]
            for gl, a in enumerate(_chunk_transpose(arrs, chunk)):
                uf_ref[gl, pl.ds(r0, RC), hf * LANES:(hf + 1) * LANES] = a.astype(uf_ref.dtype)
        return carry

    lax.fori_loop(0, R // RC, to_flat, 0)

    row = lax.broadcasted_iota(jnp.int32, (R, LANES), 0)

    def shift_down(a, sh):
        if sh % 8 == 0:
            return jnp.concatenate([jnp.zeros((sh, LANES), F32), a[:R - sh]], axis=0)
        return jnp.where(row < sh, 0.0, pltpu.roll(a, sh, 0))

    def group(gl, carry):
        u = uf_ref[gl]
        y = _dot(u, t_ref[gl])
        x = _dot(u, w_ref[gl])
        k = 0
        while (1 << k) < R:
            xs = shift_down(x, 1 << k)
            x = (x + coef_ref[gl, k:k + 1, :] * xs
                 + coef_ref[gl, 16 + k:17 + k, :] * pltpu.roll(xs, P, 1))
            k += 1
        y = y + _dot(shift_down(x, 1).astype(BF16), v_ref[gl])
        zs_ref[gl] = _gelu_tanh(y)
        return carry

    lax.fori_loop(0, GL, group, 0)

    def to_nat(rc, carry):
        r0 = pl.multiple_of(rc * RC, RC)
        for hf in range(S5_TOK // GL):
            arrs = [zs_ref[gl, pl.ds(r0, RC), hf * LANES:(hf + 1) * LANES] for gl in range(GL)]
            for k, a in enumerate(_chunk_transpose(arrs, chunk)):
                nat_ref[pl.ds(r0 * S5_TOK + hf * GL + k, RC, stride=S5_TOK), :] = a
        return carry

    lax.fori_loop(0, R // RC, to_nat, 0)
    z_ref[...] = nat_ref[...].astype(z_ref.dtype)


def s5_scan(proj, tm, wm, vm, coef, batch, seq):
    t = proj.shape[0]
    GL = LANES // S5_GROUP
    r = seq // S5_TOK
    ub = COL_U // LANES
    per_g = lambda *s: pl.BlockSpec((GL,) + s, lambda bi, si: (si, 0, 0))
    return pl.pallas_call(
        _s5_body,
        grid=(batch, S5_WIDTH // LANES),
        in_specs=[pl.BlockSpec((seq, LANES), lambda bi, si: (bi, ub + si)),
                  per_g(256, 256), per_g(256, 128), per_g(128, 256), per_g(32, 128)],
        out_specs=pl.BlockSpec((seq, LANES), lambda bi, si: (bi, si)),
        out_shape=SDS((t, S5_WIDTH), BF16),
        scratch_shapes=[pltpu.VMEM((seq, LANES), F32), pltpu.VMEM((GL, r, 2 * LANES), BF16),
                        pltpu.VMEM((GL, r, 2 * LANES), F32)],
        compiler_params=_params(("parallel", "parallel")),
        name="s5_scan",
    )(proj, tm, wm, vm, coef)


def _glu_body(z_ref, wl_ref, wg_ref, o_ref):
    z = z_ref[...]
    o_ref[...] = (_dot(z, wl_ref[...]) * _sigmoid(_dot(z, wg_ref[...]))).astype(o_ref.dtype)


def glu(z, w, tm, tn):
    t, k = z.shape
    n = w.shape[1] // 2
    nj = n // tn
    return pl.pallas_call(
        _glu_body,
        grid=(t // tm, nj),
        in_specs=[pl.BlockSpec((tm, k), lambda i, j: (i, 0)),
                  pl.BlockSpec((k, tn), lambda i, j: (0, j)),
                  pl.BlockSpec((k, tn), lambda i, j: (0, nj + j))],
        out_specs=pl.BlockSpec((tm, tn), lambda i, j: (i, j)),
        out_shape=SDS((t, n), BF16),
        compiler_params=_params(("parallel", "arbitrary")),
        name="glu",
    )(z, w, w)


def _mix_body(ya_ref, o0_ref, o1_ref, o2_ref, l0_ref, l1_ref, l2_ref, yc_ref,
              wa_ref, wb_ref, wc_ref, ga_ref, gb_ref, gc_ref, out_ref, yb_ref, wt_ref, acc_ref):
    tm = ya_ref.shape[0]
    E = ATTN_HEAD_DIM

    @pl.when(pl.program_id(1) == 0)
    def _():
        o_refs = (o0_ref, o1_ref, o2_ref)
        l_refs = (l0_ref, l1_ref, l2_ref)
        for g, (_, r) in enumerate(ATTN_GROUPS):
            for c in range(r):
                wt_ref[g, pl.ds(c, tm // r, stride=r), :] = l_refs[g][0, c]
        l0, l1, l2 = wt_ref[0], wt_ref[1], wt_ref[2]
        m = jnp.maximum(jnp.maximum(l0, l1), l2)
        e0, e1, e2 = jnp.exp(l0 - m), jnp.exp(l1 - m), jnp.exp(l2 - m)
        inv = 1.0 / (e0 + e1 + e2)
        wt_ref[0] = e0 * inv
        wt_ref[1] = e1 * inv
        wt_ref[2] = e2 * inv
        for g, (_, r) in enumerate(ATTN_GROUPS):
            for c in range(r):
                rows = pl.ds(c, tm // r, stride=r)
                w = wt_ref[g, rows, :]
                for h in range(ATTN_HEADS):
                    part = w[:, h:h + 1] * o_refs[g][0, c, :, h * E:(h + 1) * E].astype(F32)
                    if g == 0:
                        acc_ref[h, rows, :] = part
                    else:
                        acc_ref[h, rows, :] += part
        for h in range(ATTN_HEADS):
            yb_ref[:, h * E:(h + 1) * E] = acc_ref[h].astype(yb_ref.dtype)

    mix = (_sigmoid(ga_ref[...].astype(F32)) * _dot(ya_ref[...], wa_ref[...])
           + _sigmoid(gb_ref[...].astype(F32)) * _dot(yb_ref[...], wb_ref[...])
           + _sigmoid(gc_ref[...].astype(F32)) * _dot(yc_ref[...], wc_ref[...]))
    out_ref[...] = mix.astype(out_ref.dtype)


def gated_mix(ya, outs, lses, yc, wa, wb, wc, proj, seq, tm, tn):
    t, kw = ya.shape
    d = wa.shape[1]
    go = COL_G // tn
    nbt = seq // tm
    row = lambda w: pl.BlockSpec((tm, w), lambda i, j: (i, 0))
    grp = lambda r, w: pl.BlockSpec((1, r, tm // r, w), lambda i, j: (i // nbt, 0, i % nbt, 0))
    wsp = pl.BlockSpec((kw, tn), lambda i, j: (0, j))
    gate = lambda o: pl.BlockSpec((tm, tn), lambda i, j: (i, go + o * (d // tn) + j))
    dils = [r for _, r in ATTN_GROUPS]
    return pl.pallas_call(
        _mix_body,
        grid=(t // tm, d // tn),
        in_specs=[row(kw)] + [grp(r, kw) for r in dils] + [grp(r, LANES) for r in dils] + [row(kw)]
        + [wsp, wsp, wsp, gate(0), gate(1), gate(2)],
        out_specs=pl.BlockSpec((tm, tn), lambda i, j: (i, j)),
        out_shape=SDS((t, d), BF16),
        scratch_shapes=[pltpu.VMEM((tm, kw), BF16), pltpu.VMEM((len(dils), tm, LANES), F32),
                        pltpu.VMEM((ATTN_HEADS, tm, LANES), F32)],
        compiler_params=_params(("parallel", "arbitrary")),
        name="gated_mix",
    )(ya, *outs, *lses, yc, wa, wb, wc, proj, proj, proj)


def _outproj_body(x_ref, m_ref, w_ref, o_ref):
    o_ref[...] = x_ref[...] + _dot(m_ref[...], w_ref[...])


def out_proj(x2d, mix, w, tm, tn):
    t, d = x2d.shape
    return pl.pallas_call(
        _outproj_body,
        grid=(t // tm, d // tn),
        in_specs=[pl.BlockSpec((tm, tn), lambda i, j: (i, j)),
                  pl.BlockSpec((tm, d), lambda i, j: (i, 0)),
                  pl.BlockSpec((d, tn), lambda i, j: (0, j))],
        out_specs=pl.BlockSpec((tm, tn), lambda i, j: (i, j)),
        out_shape=SDS((t, d), F32),
        compiler_params=_params(("parallel", "arbitrary")),
        name="out_proj",
    )(x2d, mix, w)


def _ffn_body(x_ref, g_ref, w1_ref, w2_ref, o_ref, xn_ref):
    @pl.when(pl.program_id(1) == 0)
    def _():
        x = x_ref[...]
        xn_ref[...] = _rms(x, g_ref[...]).astype(BF16)
        o_ref[...] = x

    hid = jnp.maximum(_dot(xn_ref[...], w1_ref[...]), 0.0)
    o_ref[...] += _dot((hid * hid).astype(BF16), w2_ref[...])


def ffn(x2d, g, w1, w2, tm, th):
    t, d = x2d.shape
    hdim = w1.shape[1]
    return pl.pallas_call(
        _ffn_body,
        grid=(t // tm, hdim // th),
        in_specs=[pl.BlockSpec((tm, d), lambda i, j: (i, 0)),
                  pl.BlockSpec((1, d), lambda i, j: (0, 0)),
                  pl.BlockSpec((d, th), lambda i, j: (0, j)),
                  pl.BlockSpec((th, d), lambda i, j: (j, 0))],
        out_specs=pl.BlockSpec((tm, d), lambda i, j: (i, 0)),
        out_shape=SDS((t, d), F32),
        scratch_shapes=[pltpu.VMEM((tm, d), BF16)],
        compiler_params=_params(("parallel", "arbitrary")),
        name="ffn",
    )(x2d, g, w1, w2)


def _norm_body(x_ref, g_ref, o_ref):
    o_ref[...] = _rms(x_ref[...], g_ref[...])


def final_norm(x2d, g, tm):
    t, d = x2d.shape
    return pl.pallas_call(
        _norm_body,
        grid=(t // tm,),
        in_specs=[pl.BlockSpec((tm, d), lambda i: (i, 0)), pl.BlockSpec((1, d), lambda i: (0, 0))],
        out_specs=pl.BlockSpec((tm, d), lambda i: (i, 0)),
        out_shape=SDS((t, d), F32),
        compiler_params=_params(("parallel",)),
        name="final_norm",
    )(x2d, g)


def _tile(n, want):
    t = min(n, want)
    assert n % t == 0, (n, want)
    return t


def _layer(x2d, bias, batch, seq, p):
    t, d = x2d.shape
    tm = _tile(seq, 1024)
    n_gate0 = 4 * MLSTM_WIDTH
    n_att0 = n_gate0 + 2 * MLSTM_HEADS
    n_att = 3 * ATTN_HEADS_TOTAL * ATTN_HEAD_DIM
    n_groups = len(ATTN_GROUPS)
    w_in = p["w_in"]
    w_att = w_in[:, n_att0:n_att0 + n_att].reshape(d, 3, n_groups, ATTN_WIDTH)
    w_att = w_att.transpose(0, 2, 1, 3).reshape(d, n_att)
    w_main = jnp.concatenate([w_in[:, :n_gate0], w_in[:, n_att0 + n_att:], w_att], axis=1).astype(BF16)
    w_if = jnp.pad(w_in[:, n_gate0:n_att0], ((0, 0), (0, N_GATE_PAD - 2 * MLSTM_HEADS))).astype(BF16)
    proj, gif, *qkvs = in_proj(x2d, p["norm1_g"].reshape(1, d), w_main, w_if, batch, seq, tm)

    bif = jnp.concatenate([p["b_igate"], p["b_fgate"]]).astype(F32)
    bcol = jnp.pad(bif, (0, N_GATE_PAD - bif.shape[0])).reshape(1, N_GATE_PAD)
    ya = mlstm(proj, gif, gif[:, :8].T, bcol, bif.reshape(8, 1), p["conv_w"], p["conv_b"].reshape(1, -1),
               p["mh_norm_g"].reshape(1, -1), batch, seq)

    outs, lses = zip(*[attn_group(qkvs[g], bias, g) for g in range(n_groups)])

    tmat, wmat, vmat, coef = s5_prep(p["lam_re"], p["lam_im"], p["log_dt"], p["b_re"], p["b_im"],
                                     p["c_re"], p["c_im"], p["d_skip"])
    z = s5_scan(proj, tmat, wmat, vmat, coef, batch, seq)
    yc = glu(z, p["w_glu"].astype(BF16), tm, 512)

    mix = gated_mix(ya, outs, lses, yc, p["w_br_a"].astype(BF16), p["w_br_b"].astype(BF16),
                    p["w_br_c"].astype(BF16), proj, seq, tm, 512)
    x2d = out_proj(x2d, mix, p["w_out"].astype(BF16), tm, 512)
    return ffn(x2d, p["norm2_g"].reshape(1, d), p["w_ff1"].astype(BF16), p["w_ff2"].astype(BF16),
               _tile(t, 512), 1024)


_PER_LAYER = ("norm1_g", "w_in", "conv_w", "conv_b", "b_igate", "b_fgate", "mh_norm_g", "lam_re", "lam_im",
              "log_dt", "b_re", "b_im", "c_re", "c_im", "d_skip", "w_glu", "w_br_a", "w_br_b", "w_br_c",
              "w_out", "norm2_g", "w_ff1", "w_ff2")


def kernel(x, norm1_g, w_in, conv_w, conv_b, b_igate, b_fgate, mh_norm_g, rel_bias, lam_re, lam_im, log_dt,
           b_re, b_im, c_re, c_im, d_skip, w_glu, w_br_a, w_br_b, w_br_c, w_out, norm2_g, w_ff1, w_ff2,
           final_g):
    stacked = dict(norm1_g=norm1_g, w_in=w_in, conv_w=conv_w, conv_b=conv_b, b_igate=b_igate,
                   b_fgate=b_fgate, mh_norm_g=mh_norm_g, lam_re=lam_re, lam_im=lam_im, log_dt=log_dt,
                   b_re=b_re, b_im=b_im, c_re=c_re, c_im=c_im, d_skip=d_skip, w_glu=w_glu, w_br_a=w_br_a,
                   w_br_b=w_br_b, w_br_c=w_br_c, w_out=w_out, norm2_g=norm2_g, w_ff1=w_ff1, w_ff2=w_ff2)
    batch, seq, d = x.shape
    x2d = x.astype(F32).reshape(batch * seq, d)
    bias = attn_bias(rel_bias.astype(F32))
    for l in range(w_in.shape[0]):
        x2d = _layer(x2d, bias, batch, seq, {k: stacked[k][l] for k in _PER_LAYER})
    out = final_norm(x2d, final_g.reshape(1, d), _tile(batch * seq, 1024))
    return out.reshape(batch, seq, d).astype(x.dtype)
```

```python
import functools
import math

import jax
import jax.numpy as jnp
from jax import lax
from jax.experimental import pallas as pl
from jax.experimental.pallas import tpu as pltpu

F32 = jnp.float32
BF16 = jnp.bfloat16
SDS = jax.ShapeDtypeStruct

NORM_EPS = 1e-6
NEG = -1e30

MLSTM_HEADS = 4
MLSTM_HEAD_DIM = 256
MLSTM_WIDTH = MLSTM_HEADS * MLSTM_HEAD_DIM
CONV_WIDTH = 4
ATTN_GROUPS = ((128, 1), (512, 4), (2048, 16))
ATTN_HEADS = 8
ATTN_HEAD_DIM = 128
ATTN_WIDTH = ATTN_HEADS * ATTN_HEAD_DIM
ATTN_HEADS_TOTAL = len(ATTN_GROUPS) * ATTN_HEADS
ATTN_BLOCK = 128
ATTN_Q_BLOCKS = 4
REL_BUCKETS = 32
REL_MAX_DISTANCE = 2048
S5_WIDTH = 1024
S5_GROUP = 16
S5_GROUPS = S5_WIDTH // S5_GROUP
S5_STATE = 64
S5_TOK = 16
N_GATE_PAD = 128

COL_U = 4 * MLSTM_WIDTH
COL_G = COL_U + S5_WIDTH
D_MODEL = 2048
N_MAIN = COL_G + 3 * D_MODEL
LANES = 128

MLSTM_CHUNK = 256
VMEM_LIMIT = 56 * 2**20


def _params(sem):
    return pltpu.CompilerParams(dimension_semantics=sem, vmem_limit_bytes=VMEM_LIMIT)


def _sigmoid(x):
    return 1.0 / (1.0 + jnp.exp(-x))


def _log_sigmoid(x):
    return jnp.minimum(x, 0.0) - jnp.log(1.0 + jnp.exp(-jnp.abs(x)))


def _rms(x, g):
    ms = jnp.mean(x * x, axis=-1, keepdims=True)
    return x * lax.rsqrt(ms + NORM_EPS) * g


def _dot(a, b):
    return jnp.dot(a, b, preferred_element_type=F32)


def _dot_nt(a, b):
    return lax.dot_general(a, b, (((1,), (1,)), ((), ())), preferred_element_type=F32)


def _dot_tn(a, b):
    return lax.dot_general(a, b, (((0,), (0,)), ((), ())), preferred_element_type=F32)


def _inproj_body(nm, x_ref, g_ref, w_ref, wif_ref, o_ref, oif_ref, a0_ref, a1_ref, a2_ref, xn_ref, acc_ref):
    j = pl.program_id(1)
    tm = x_ref.shape[0]
    tn = w_ref.shape[1]

    @pl.when(j == 0)
    def _():
        xn = _rms(x_ref[...], g_ref[...]).astype(BF16)
        xn_ref[...] = xn
        oif_ref[...] = _dot(xn, wif_ref[...])

    @pl.when(j < nm)
    def _():
        o_ref[...] = _dot(xn_ref[...], w_ref[...]).astype(o_ref.dtype)

    for g, a_ref in enumerate((a0_ref, a1_ref, a2_ref)):
        r = ATTN_GROUPS[g][1]
        lo = nm + 3 * g

        @pl.when(jnp.logical_and(j >= lo, j < lo + 3))
        def _(a_ref=a_ref, r=r):
            res = _dot(xn_ref[...], w_ref[...])
            if r == 1:
                a_ref[0, 0] = res.astype(a_ref.dtype)
            else:
                for s in range(tn // LANES):
                    acc_ref[s] = res[:, s * LANES:(s + 1) * LANES]
                for c in range(r):
                    for s in range(tn // LANES):
                        a_ref[0, c, :, s * LANES:(s + 1) * LANES] = (
                            acc_ref[s, pl.ds(c, tm // r, stride=r), :].astype(a_ref.dtype))


def in_proj(x2d, g, w, wif, batch, seq, tm):
    t, d = x2d.shape
    tn = ATTN_WIDTH
    nm = N_MAIN // tn
    nbt = seq // tm
    n_groups = len(ATTN_GROUPS)

    def a_spec(gi):
        r = ATTN_GROUPS[gi][1]
        return pl.BlockSpec((1, r, tm // r, tn),
                            lambda i, j: (i // nbt, 0, i % nbt, jnp.clip(j - nm - 3 * gi, 0, 2)))

    return pl.pallas_call(
        functools.partial(_inproj_body, nm),
        grid=(t // tm, nm + 3 * n_groups),
        in_specs=[pl.BlockSpec((tm, d), lambda i, j: (i, 0)),
                  pl.BlockSpec((1, d), lambda i, j: (0, 0)),
                  pl.BlockSpec((d, tn), lambda i, j: (0, j)),
                  pl.BlockSpec((d, N_GATE_PAD), lambda i, j: (0, 0))],
        out_specs=[pl.BlockSpec((tm, tn), lambda i, j: (i, jnp.minimum(j, nm - 1))),
                   pl.BlockSpec((tm, N_GATE_PAD), lambda i, j: (i, 0))] + [a_spec(gi) for gi in range(n_groups)],
        out_shape=[SDS((t, N_MAIN), BF16), SDS((t, N_GATE_PAD), F32)]
        + [SDS((batch, r, seq // r, 3 * tn), BF16) for _, r in ATTN_GROUPS],
        scratch_shapes=[pltpu.VMEM((tm, d), BF16), pltpu.VMEM((tn // LANES, tm, LANES), F32)],
        compiler_params=_params(("parallel", "arbitrary")),
        name="in_proj",
    )(x2d, g, w, wif)


def _mlstm_body(q_ref, k_ref, v_ref, og_ref, gcol_ref, grow_ref, bcol_ref, brow_ref,
                cwq_ref, cwk_ref, cbq_ref, cbk_ref, ng_ref, y_ref,
                ct_ref, m_ref, tq_ref, tk_ref):
    h = pl.program_id(1)
    c = pl.program_id(2)
    L, E = q_ref.shape

    @pl.when(c == 0)
    def _():
        ct_ref[...] = jnp.zeros_like(ct_ref)
        m_ref[...] = jnp.zeros_like(m_ref)
        tq_ref[...] = jnp.zeros_like(tq_ref)
        tk_ref[...] = jnp.zeros_like(tk_ref)

    row8 = lax.broadcasted_iota(jnp.int32, (8, E), 0)

    def conv_silu(x_ref, tail_ref, w_ref, b_ref):
        x = x_ref[...].astype(F32)
        tail = tail_ref[...]
        w = w_ref[...]
        acc = b_ref[...] + w[CONV_WIDTH - 1:CONV_WIDTH] * x
        for d in range(1, CONV_WIDTH):
            r = pltpu.roll(x, d, 0)
            head = jnp.where(row8 < d, pltpu.roll(tail, d, 0), r[:8])
            xs = jnp.concatenate([head, r[8:]], axis=0)
            acc = acc + w[CONV_WIDTH - 1 - d:CONV_WIDTH - d] * xs
        tail_ref[...] = x[L - 8:]
        return acc * _sigmoid(acc)

    q = conv_silu(q_ref, tq_ref, cwq_ref, cbq_ref)
    k = conv_silu(k_ref, tk_ref, cwk_ref, cbk_ref) * (E ** -0.5)

    gc = gcol_ref[...] + bcol_ref[...]
    lane = lax.broadcasted_iota(jnp.int32, gc.shape, 1)
    ig_col = jnp.sum(jnp.where(lane == h, gc, 0.0), axis=1, keepdims=True)
    lf_col = _log_sigmoid(jnp.sum(jnp.where(lane == h + MLSTM_HEADS, gc, 0.0), axis=1, keepdims=True))
    gr = grow_ref[...] + brow_ref[...]
    sub = lax.broadcasted_iota(jnp.int32, gr.shape, 0)
    ig_row = jnp.sum(jnp.where(sub == h, gr, 0.0), axis=0, keepdims=True)
    lf_row = _log_sigmoid(jnp.sum(jnp.where(sub == h + MLSTM_HEADS, gr, 0.0), axis=0, keepdims=True))

    tt = lax.broadcasted_iota(jnp.int32, (L, L), 0)
    ss = lax.broadcasted_iota(jnp.int32, (L, L), 1)
    causal = ss <= tt
    bcum_col = jnp.sum(jnp.where(causal, lf_row, 0.0), axis=1, keepdims=True)
    bcum_row = jnp.sum(jnp.where(tt <= ss, lf_col, 0.0), axis=0, keepdims=True)
    a_row = ig_row - bcum_row
    a_col = ig_col - bcum_col

    m_prev = m_ref[...]
    amat = jnp.where(causal, a_row, NEG)
    mrow = jnp.maximum(m_prev, jnp.max(amat, axis=1, keepdims=True))
    w_intra = jnp.exp(amat - mrow)
    w_inter = jnp.exp(m_prev - mrow)

    qb = q.astype(BF16)
    kb = k.astype(BF16)
    vaug = jnp.concatenate([v_ref[...], jnp.ones((L, 128), BF16)], axis=1)
    s = _dot_nt(qb, kb) * w_intra
    ct = ct_ref[...]
    num_aug = _dot(s.astype(BF16), vaug) + w_inter * _dot(qb, ct.astype(BF16))
    num = num_aug[:, :E]
    den = num_aug[:, E:E + 1]
    m_t = bcum_col + mrow
    hout = num / jnp.maximum(jnp.abs(den), jnp.exp(-m_t))
    hn = _rms(hout, ng_ref[...])
    y_ref[...] = (_sigmoid(og_ref[...].astype(F32)) * hn).astype(y_ref.dtype)

    b_last = jnp.sum(lf_col, axis=0, keepdims=True)
    g_col = b_last + a_col
    m_new = jnp.maximum(b_last + m_prev, jnp.max(g_col, axis=0, keepdims=True))
    w_s = jnp.exp(g_col - m_new)
    decay = jnp.exp(b_last + m_prev - m_new)
    kw = (k * w_s).astype(BF16)
    ct_ref[...] = decay * ct + _dot_tn(kw, vaug)
    m_ref[...] = m_new


def mlstm(proj, gif, gif_t, bcol, brow, conv_w, conv_b, ng, batch, seq):
    L = min(MLSTM_CHUNK, seq)
    E = MLSTM_HEAD_DIM
    nc = seq // L
    H = MLSTM_HEADS
    t = batch * seq
    row = lambda b, h, c: b * nc + c
    return pl.pallas_call(
        _mlstm_body,
        grid=(batch, H, nc),
        in_specs=[pl.BlockSpec((L, E), lambda b, h, c: (row(b, h, c), h)),
                  pl.BlockSpec((L, E), lambda b, h, c: (row(b, h, c), H + h)),
                  pl.BlockSpec((L, E), lambda b, h, c: (row(b, h, c), 2 * H + h)),
                  pl.BlockSpec((L, E), lambda b, h, c: (row(b, h, c), 3 * H + h)),
                  pl.BlockSpec((L, N_GATE_PAD), lambda b, h, c: (row(b, h, c), 0)),
                  pl.BlockSpec((8, L), lambda b, h, c: (0, row(b, h, c))),
                  pl.BlockSpec((1, N_GATE_PAD), lambda b, h, c: (0, 0)),
                  pl.BlockSpec((8, 1), lambda b, h, c: (0, 0)),
                  pl.BlockSpec((CONV_WIDTH, E), lambda b, h, c: (0, h)),
                  pl.BlockSpec((CONV_WIDTH, E), lambda b, h, c: (0, H + h)),
                  pl.BlockSpec((1, E), lambda b, h, c: (0, h)),
                  pl.BlockSpec((1, E), lambda b, h, c: (0, H + h)),
                  pl.BlockSpec((1, E), lambda b, h, c: (0, h))],
        out_specs=pl.BlockSpec((L, E), lambda b, h, c: (row(b, h, c), h)),
        out_shape=SDS((t, MLSTM_WIDTH), BF16),
        scratch_shapes=[pltpu.VMEM((E, E + 128), F32), pltpu.VMEM((1, 1), F32),
                        pltpu.VMEM((8, E), F32), pltpu.VMEM((8, E), F32)],
        compiler_params=_params(("parallel", "parallel", "arbitrary")),
        name="mlstm",
    )(proj, proj, proj, proj, gif, gif_t, bcol, brow, conv_w, conv_w, conv_b, conv_b, ng)


def _bias_body(table_ref, o_ref):
    h = pl.program_id(0)
    dil = jnp.where(h < ATTN_HEADS, ATTN_GROUPS[0][1],
                    jnp.where(h < 2 * ATTN_HEADS, ATTN_GROUPS[1][1], ATTN_GROUPS[2][1]))
    shape = (ATTN_BLOCK, 2 * ATTN_BLOCK)
    i = lax.broadcasted_iota(jnp.int32, shape, 0)
    j = lax.broadcasted_iota(jnp.int32, shape, 1)
    rel = ATTN_BLOCK + i - j
    dist = jnp.maximum(rel, 0) * dil
    max_exact = REL_BUCKETS // 2
    nf = jnp.maximum(dist, max_exact).astype(F32)
    large = max_exact + (jnp.log(nf / max_exact) / math.log(REL_MAX_DISTANCE / max_exact)
                         * (REL_BUCKETS - max_exact)).astype(jnp.int32)
    large = jnp.minimum(large, REL_BUCKETS - 1)
    bucket = jnp.where(dist < max_exact, dist, large)
    acc = jnp.zeros(shape, F32)
    for b in range(REL_BUCKETS):
        acc = jnp.where(bucket == b, table_ref[b, h], acc)
    o_ref[0] = jnp.where(rel >= 0, jnp.where(rel <= ATTN_BLOCK, acc, NEG), NEG)


def attn_bias(rel_bias):
    return pl.pallas_call(
        _bias_body,
        grid=(ATTN_HEADS_TOTAL,),
        in_specs=[pl.BlockSpec(memory_space=pltpu.SMEM)],
        out_specs=pl.BlockSpec((1, ATTN_BLOCK, 2 * ATTN_BLOCK), lambda h: (h, 0, 0)),
        out_shape=SDS((ATTN_HEADS_TOTAL, ATTN_BLOCK, 2 * ATTN_BLOCK), F32),
        compiler_params=_params(("arbitrary",)),
        name="attn_bias",
    )(rel_bias)


def _attn_body(q_ref, kp_ref, kc_ref, vp_ref, vc_ref, bias_ref, o_ref, lse_ref):
    n = pl.program_id(2)
    B = ATTN_BLOCK
    E = ATTN_HEAD_DIM
    H = ATTN_HEADS
    NQ = q_ref.shape[2] // B
    scale = E ** -0.5
    hs = [slice(h * E, (h + 1) * E) for h in range(H)]

    def keys(cur_ref, prev_ref, i, sl):
        if i == 0:
            return jnp.concatenate([prev_ref[0, 0, :, sl], cur_ref[0, 0, :B, sl]], axis=0)
        return cur_ref[0, 0, (i - 1) * B:(i + 1) * B, sl]

    s = jnp.concatenate([_dot_nt(q_ref[0, 0, i * B:(i + 1) * B, sl], keys(kc_ref, kp_ref, i, sl))
                         for i in range(NQ) for sl in hs], axis=0)
    bias = bias_ref[...].reshape(H * B, 2 * B)
    key = lax.broadcasted_iota(jnp.int32, (1, 2 * B), 1)
    first = bias + jnp.where(key < B, jnp.where(n > 0, 0.0, NEG), 0.0)
    s = s * scale + jnp.concatenate([first] + [bias] * (NQ - 1), axis=0)
    m = jnp.max(s, axis=1, keepdims=True)
    p = jnp.exp(s - m)
    den = jnp.sum(p, axis=1, keepdims=True)
    inv = 1.0 / den
    pb = p.astype(BF16)
    lse_all = m + jnp.log(den)
    lane = lax.broadcasted_iota(jnp.int32, (B, LANES), 1)
    for i in range(NQ):
        lse = jnp.zeros((B, LANES), F32)
        for h, sl in enumerate(hs):
            rows = slice((i * H + h) * B, (i * H + h + 1) * B)
            o = _dot(pb[rows], keys(vc_ref, vp_ref, i, sl)) * inv[rows]
            o_ref[0, 0, i * B:(i + 1) * B, sl] = o.astype(o_ref.dtype)
            lse = jnp.where(lane == h, lse_all[rows], lse)
        lse_ref[0, 0, i * B:(i + 1) * B, :] = lse


def attn_group(qkv, bias, g):
    batch, r, l, _ = qkv.shape
    W = ATTN_WIDTH
    nq = min(ATTN_Q_BLOCKS, l // ATTN_BLOCK)
    rows = nq * ATTN_BLOCK
    cur = (1, 1, rows, W)
    one = (1, 1, ATTN_BLOCK, W)
    prev = lambda n: jnp.maximum(n * nq - 1, 0)
    return pl.pallas_call(
        _attn_body,
        grid=(batch, r, l // rows),
        in_specs=[pl.BlockSpec(cur, lambda b, c, n: (b, c, n, 0)),
                  pl.BlockSpec(one, lambda b, c, n: (b, c, prev(n), 1)),
                  pl.BlockSpec(cur, lambda b, c, n: (b, c, n, 1)),
                  pl.BlockSpec(one, lambda b, c, n: (b, c, prev(n), 2)),
                  pl.BlockSpec(cur, lambda b, c, n: (b, c, n, 2)),
                  pl.BlockSpec((ATTN_HEADS, ATTN_BLOCK, 2 * ATTN_BLOCK), lambda b, c, n: (g, 0, 0))],
        out_specs=[pl.BlockSpec(cur, lambda b, c, n: (b, c, n, 0)),
                   pl.BlockSpec((1, 1, rows, LANES), lambda b, c, n: (b, c, n, 0))],
        out_shape=[SDS((batch, r, l, W), BF16), SDS((batch, r, l, LANES), F32)],
        compiler_params=_params(("parallel", "parallel", "arbitrary")),
        name=f"attn_g{g}",
    )(qkv, qkv, qkv, qkv, qkv, bias)


def _s5_prep_body(lam_ref, lamc_ref, dt_ref, bt_r_ref, bt_i_ref, ce_r_ref, ce_i_ref, d_ref,
                  t_ref, w_ref, v_ref, coef_ref):
    P = S5_STATE
    lr = lam_ref[0, 0:1, :]
    li = lam_ref[0, 1:2, :]
    dt = jnp.exp(dt_ref[0])

    def apow(e, lr_, li_):
        mag = jnp.exp(lr_ * dt * e)
        ang = li_ * dt * e
        return mag * jnp.cos(ang), mag * jnp.sin(ang)

    one = jnp.ones((1, 1), F32)
    ar, ai = apow(one, lr, li)
    nr = ar - 1.0
    den = lr * lr + li * li
    f_re = (nr * lr + ai * li) / den
    f_im = (ai * lr - nr * li) / den
    bt_r = bt_r_ref[0]
    bt_i = bt_i_ref[0]
    bb_r = f_re * bt_r - f_im * bt_i
    bb_i = f_re * bt_i + f_im * bt_r

    lrc = lamc_ref[0, :, 0:1]
    lic = lamc_ref[0, :, 1:2]
    lag = (lax.broadcasted_iota(jnp.int32, (P, 256), 1) // S5_GROUP).astype(F32)
    adr, adi = apow(lag, lrc, lic)
    ce_r = ce_r_ref[0]
    ce_i = ce_i_ref[0]
    ca_r = ce_r * adr - ce_i * adi
    ca_i = ce_r * adi + ce_i * adr
    hp = lax.Precision.HIGHEST
    ks = (jnp.dot(bb_r[:, :P], ca_r, precision=hp, preferred_element_type=F32)
          - jnp.dot(bb_i[:, :P], ca_i, precision=hp, preferred_element_type=F32))
    si = lax.broadcasted_iota(jnp.int32, (S5_GROUP, 256), 0)
    lj = lax.broadcasted_iota(jnp.int32, (S5_GROUP, 256), 1)
    ks = ks + jnp.where(si == lj, d_ref[0], 0.0)
    for s in range(S5_TOK):
        sh = s * S5_GROUP
        blk = ks if s == 0 else jnp.where(lj >= sh, pltpu.roll(ks, sh, 1), 0.0)
        t_ref[0, sh:sh + S5_GROUP, :] = blk.astype(t_ref.dtype)

    lane128 = lax.broadcasted_iota(jnp.int32, (1, 128), 1)
    for s in range(S5_TOK):
        pr, pi = apow(float(S5_TOK - 1 - s) * one, lr, li)
        p1 = jnp.where(lane128 < P, pr, pi)
        p2 = jnp.where(lane128 < P, -pi, pr)
        w_ref[0, s * S5_GROUP:(s + 1) * S5_GROUP, :] = (bb_r * p1 + bb_i * p2).astype(w_ref.dtype)

    adr1, adi1 = apow(lag + 1.0, lrc, lic)
    v_ref[0, 0:P, :] = (ce_r * adr1 - ce_i * adi1).astype(v_ref.dtype)
    v_ref[0, P:2 * P, :] = (-(ce_r * adi1 + ce_i * adr1)).astype(v_ref.dtype)

    ek = jnp.left_shift(S5_TOK, lax.broadcasted_iota(jnp.int32, (16, 1), 0)).astype(F32)
    cr, ci = apow(ek, lr, li)
    coef_ref[0, 0:16, :] = cr
    coef_ref[0, 16:32, :] = jnp.where(lane128 < P, -ci, ci)


def s5_prep(lam_re, lam_im, log_dt, b_re, b_im, c_re, c_im, d_skip):
    G, P, I = b_re.shape
    lam = jnp.stack([jnp.tile(lam_re, (1, 2)), jnp.tile(lam_im, (1, 2))], axis=1)
    lam = jnp.pad(lam, ((0, 0), (0, 6), (0, 0)))
    lamc = jnp.stack([lam_re, lam_im], axis=2)
    dt = log_dt.reshape(G, 1, 1)
    bt_r = jnp.tile(jnp.swapaxes(b_re, 1, 2), (1, 1, 2))
    bt_i = jnp.tile(jnp.swapaxes(b_im, 1, 2), (1, 1, 2))
    ce_r = jnp.tile(jnp.swapaxes(c_re, 1, 2), (1, 1, S5_TOK))
    ce_i = jnp.tile(jnp.swapaxes(c_im, 1, 2), (1, 1, S5_TOK))
    d = jnp.pad(d_skip, ((0, 0), (0, 256 - I))).reshape(G, 1, 256)
    blk = lambda *s: pl.BlockSpec((1,) + s, lambda g: (g, 0, 0))
    return pl.pallas_call(
        _s5_prep_body,
        grid=(G,),
        in_specs=[blk(8, 128), blk(P, 2), blk(1, 1), blk(16, 128), blk(16, 128), blk(P, 256), blk(P, 256),
                  blk(1, 256)],
        out_specs=[blk(256, 256), blk(256, 128), blk(128, 256), blk(32, 128)],
        out_shape=[SDS((G, 256, 256), BF16), SDS((G, 256, 128), BF16), SDS((G, 128, 256), BF16),
                   SDS((G, 32, 128), F32)],
        compiler_params=_params(("arbitrary",)),
        name="s5_prep",
    )(lam, lamc, dt, bt_r, bt_i, ce_r, ce_i, d)


def _gelu_tanh(x):
    return 0.5 * x * (1.0 + jnp.tanh(math.sqrt(2.0 / math.pi) * (x + 0.044715 * (x * x * x))))


def _chunk_transpose(arrs, chunk):
    a = list(arrs)
    n = len(a)
    d = n // 2
    while d >= 1:
        bit = (chunk & d) != 0
        nxt = list(a)
        for i in range(n):
            if i & d == 0:
                lo, hi = a[i], a[i + d]
                nxt[i] = jnp.where(bit, pltpu.roll(hi, d * S5_GROUP, 1), lo)
                nxt[i + d] = jnp.where(bit, hi, pltpu.roll(lo, LANES - d * S5_GROUP, 1))
        a = nxt
        d //= 2
    return a


def _s5_body(u_ref, t_ref, w_ref, v_ref, coef_ref, z_ref, nat_ref, uf_ref, zs_ref):
    S = u_ref.shape[0]
    R = S // S5_TOK
    P = S5_STATE
    GL = LANES // S5_GROUP
    RC = min(R, 64)
    chunk = lax.broadcasted_iota(jnp.int32, (RC, LANES), 1) // S5_GROUP

    nat_ref[...] = u_ref[...].astype(F32)

    def to_flat(rc, carry):
        r0 = pl.multiple_of(rc * RC, RC)
        for hf in range(S5_TOK // GL):
            arrs = [nat_ref[pl.ds(r0 * S5_TOK + hf * GL + k, RC, stride=S5_TOK), :] for k in range(GL)]
            for gl, a in enumerate(_chunk_transpose(arrs, chunk)):
                uf_ref[gl, pl.ds(r0, RC), hf * LANES:(hf + 1) * LANES] = a.astype(uf_ref.dtype)
        return carry

    lax.fori_loop(0, R // RC, to_flat, 0)

    row = lax.broadcasted_iota(jnp.int32, (R, LANES), 0)

    def shift_down(a, sh):
        if sh % 8 == 0:
            return jnp.concatenate([jnp.zeros((sh, LANES), F32), a[:R - sh]], axis=0)
        return jnp.where(row < sh, 0.0, pltpu.roll(a, sh, 0))

    def group(gl, carry):
        u = uf_ref[gl]
        y = _dot(u, t_ref[gl])
        x = _dot(u, w_ref[gl])
        k = 0
        while (1 << k) < R:
            xs = shift_down(x, 1 << k)
            x = (x + coef_ref[gl, k:k + 1, :] * xs
                 + coef_ref[gl, 16 + k:17 + k, :] * pltpu.roll(xs, P, 1))
            k += 1
        y = y + _dot(shift_down(x, 1).astype(BF16), v_ref[gl])
        zs_ref[gl] = _gelu_tanh(y)
        return carry

    lax.fori_loop(0, GL, group, 0)

    def to_nat(rc, carry):
        r0 = pl.multiple_of(rc * RC, RC)
        for hf in range(S5_TOK // GL):
            arrs = [zs_ref[gl, pl.ds(r0, RC), hf * LANES:(hf + 1) * LANES] for gl in range(GL)]
            for k, a in enumerate(_chunk_transpose(arrs, chunk)):
                nat_ref[pl.ds(r0 * S5_TOK + hf * GL + k, RC, stride=S5_TOK), :] = a
        return carry

    lax.fori_loop(0, R // RC, to_nat, 0)
    z_ref[...] = nat_ref[...].astype(z_ref.dtype)


def s5_scan(proj, tm, wm, vm, coef, batch, seq):
    t = proj.shape[0]
    GL = LANES // S5_GROUP
    r = seq // S5_TOK
    ub = COL_U // LANES
    per_g = lambda *s: pl.BlockSpec((GL,) + s, lambda bi, si: (si, 0, 0))
    return pl.pallas_call(
        _s5_body,
        grid=(batch, S5_WIDTH // LANES),
        in_specs=[pl.BlockSpec((seq, LANES), lambda bi, si: (bi, ub + si)),
                  per_g(256, 256), per_g(256, 128), per_g(128, 256), per_g(32, 128)],
        out_specs=pl.BlockSpec((seq, LANES), lambda bi, si: (bi, si)),
        out_shape=SDS((t, S5_WIDTH), BF16),
        scratch_shapes=[pltpu.VMEM((seq, LANES), F32), pltpu.VMEM((GL, r, 2 * LANES), BF16),
                        pltpu.VMEM((GL, r, 2 * LANES), F32)],
        compiler_params=_params(("parallel", "parallel")),
        name="s5_scan",
    )(proj, tm, wm, vm, coef)


def _glu_body(z_ref, wl_ref, wg_ref, o_ref):
    z = z_ref[...]
    o_ref[...] = (_dot(z, wl_ref[...]) * _sigmoid(_dot(z, wg_ref[...]))).astype(o_ref.dtype)


def glu(z, w, tm, tn):
    t, k = z.shape
    n = w.shape[1] // 2
    nj = n // tn
    return pl.pallas_call(
        _glu_body,
        grid=(t // tm, nj),
        in_specs=[pl.BlockSpec((tm, k), lambda i, j: (i, 0)),
                  pl.BlockSpec((k, tn), lambda i, j: (0, j)),
                  pl.BlockSpec((k, tn), lambda i, j: (0, nj + j))],
        out_specs=pl.BlockSpec((tm, tn), lambda i, j: (i, j)),
        out_shape=SDS((t, n), BF16),
        compiler_params=_params(("parallel", "arbitrary")),
        name="glu",
    )(z, w, w)


def _mix_body(ya_ref, o0_ref, o1_ref, o2_ref, l0_ref, l1_ref, l2_ref, yc_ref,
              wa_ref, wb_ref, wc_ref, ga_ref, gb_ref, gc_ref, out_ref, yb_ref, wt_ref, acc_ref):
    tm = ya_ref.shape[0]
    E = ATTN_HEAD_DIM

    @pl.when(pl.program_id(1) == 0)
    def _():
        o_refs = (o0_ref, o1_ref, o2_ref)
        l_refs = (l0_ref, l1_ref, l2_ref)
        for g, (_, r) in enumerate(ATTN_GROUPS):
            for c in range(r):
                wt_ref[g, pl.ds(c, tm // r, stride=r), :] = l_refs[g][0, c]
        l0, l1, l2 = wt_ref[0], wt_ref[1], wt_ref[2]
        m = jnp.maximum(jnp.maximum(l0, l1), l2)
        e0, e1, e2 = jnp.exp(l0 - m), jnp.exp(l1 - m), jnp.exp(l2 - m)
        inv = 1.0 / (e0 + e1 + e2)
        wt_ref[0] = e0 * inv
        wt_ref[1] = e1 * inv
        wt_ref[2] = e2 * inv
        for g, (_, r) in enumerate(ATTN_GROUPS):
            for c in range(r):
                rows = pl.ds(c, tm // r, stride=r)
                w = wt_ref[g, rows, :]
                for h in range(ATTN_HEADS):
                    part = w[:, h:h + 1] * o_refs[g][0, c, :, h * E:(h + 1) * E].astype(F32)
                    if g == 0:
                        acc_ref[h, rows, :] = part
                    else:
                        acc_ref[h, rows, :] += part
        for h in range(ATTN_HEADS):
            yb_ref[:, h * E:(h + 1) * E] = acc_ref[h].astype(yb_ref.dtype)

    mix = (_sigmoid(ga_ref[...].astype(F32)) * _dot(ya_ref[...], wa_ref[...])
           + _sigmoid(gb_ref[...].astype(F32)) * _dot(yb_ref[...], wb_ref[...])
           + _sigmoid(gc_ref[...].astype(F32)) * _dot(yc_ref[...], wc_ref[...]))
    out_ref[...] = mix.astype(out_ref.dtype)


def gated_mix(ya, outs, lses, yc, wa, wb, wc, proj, seq, tm, tn):
    t, kw = ya.shape
    d = wa.shape[1]
    go = COL_G // tn
    nbt = seq // tm
    row = lambda w: pl.BlockSpec((tm, w), lambda i, j: (i, 0))
    grp = lambda r, w: pl.BlockSpec((1, r, tm // r, w), lambda i, j: (i // nbt, 0, i % nbt, 0))
    wsp = pl.BlockSpec((kw, tn), lambda i, j: (0, j))
    gate = lambda o: pl.BlockSpec((tm, tn), lambda i, j: (i, go + o * (d // tn) + j))
    dils = [r for _, r in ATTN_GROUPS]
    return pl.pallas_call(
        _mix_body,
        grid=(t // tm, d // tn),
        in_specs=[row(kw)] + [grp(r, kw) for r in dils] + [grp(r, LANES) for r in dils] + [row(kw)]
        + [wsp, wsp, wsp, gate(0), gate(1), gate(2)],
        out_specs=pl.BlockSpec((tm, tn), lambda i, j: (i, j)),
        out_shape=SDS((t, d), BF16),
        scratch_shapes=[pltpu.VMEM((tm, kw), BF16), pltpu.VMEM((len(dils), tm, LANES), F32),
                        pltpu.VMEM((ATTN_HEADS, tm, LANES), F32)],
        compiler_params=_params(("parallel", "arbitrary")),
        name="gated_mix",
    )(ya, *outs, *lses, yc, wa, wb, wc, proj, proj, proj)


def _outproj_body(x_ref, m_ref, w_ref, o_ref):
    o_ref[...] = x_ref[...] + _dot(m_ref[...], w_ref[...])


def out_proj(x2d, mix, w, tm, tn):
    t, d = x2d.shape
    return pl.pallas_call(
        _outproj_body,
        grid=(t // tm, d // tn),
        in_specs=[pl.BlockSpec((tm, tn), lambda i, j: (i, j)),
                  pl.BlockSpec((tm, d), lambda i, j: (i, 0)),
                  pl.BlockSpec((d, tn), lambda i, j: (0, j))],
        out_specs=pl.BlockSpec((tm, tn), lambda i, j: (i, j)),
        out_shape=SDS((t, d), F32),
        compiler_params=_params(("parallel", "arbitrary")),
        name="out_proj",
    )(x2d, mix, w)


def _ffn_body(x_ref, g_ref, w1_ref, w2_ref, o_ref, xn_ref):
    @pl.when(pl.program_id(1) == 0)
    def _():
        x = x_ref[...]
        xn_ref[...] = _rms(x, g_ref[...]).astype(BF16)
        o_ref[...] = x

    hid = jnp.maximum(_dot(xn_ref[...], w1_ref[...]), 0.0)
    o_ref[...] += _dot((hid * hid).astype(BF16), w2_ref[...])


def ffn(x2d, g, w1, w2, tm, th):
    t, d = x2d.shape
    hdim = w1.shape[1]
    return pl.pallas_call(
        _ffn_body,
        grid=(t // tm, hdim // th),
        in_specs=[pl.BlockSpec((tm, d), lambda i, j: (i, 0)),
                  pl.BlockSpec((1, d), lambda i, j: (0, 0)),
                  pl.BlockSpec((d, th), lambda i, j: (0, j)),
                  pl.BlockSpec((th, d), lambda i, j: (j, 0))],
        out_specs=pl.BlockSpec((tm, d), lambda i, j: (i, 0)),
        out_shape=SDS((t, d), F32),
        scratch_shapes=[pltpu.VMEM((tm, d), BF16)],
        compiler_params=_params(("parallel", "arbitrary")),
        name="ffn",
    )(x2d, g, w1, w2)


def _norm_body(x_ref, g_ref, o_ref):
    o_ref[...] = _rms(x_ref[...], g_ref[...])


def final_norm(x2d, g, tm):
    t, d = x2d.shape
    return pl.pallas_call(
        _norm_body,
        grid=(t // tm,),
        in_specs=[pl.BlockSpec((tm, d), lambda i: (i, 0)), pl.BlockSpec((1, d), lambda i: (0, 0))],
        out_specs=pl.BlockSpec((tm, d), lambda i: (i, 0)),
        out_shape=SDS((t, d), F32),
        compiler_params=_params(("parallel",)),
        name="final_norm",
    )(x2d, g)


def _tile(n, want):
    t = min(n, want)
    assert n % t == 0, (n, want)
    return t


def _layer(x2d, bias, batch, seq, p):
    t, d = x2d.shape
    tm = _tile(seq, 1024)
    n_gate0 = 4 * MLSTM_WIDTH
    n_att0 = n_gate0 + 2 * MLSTM_HEADS
    n_att = 3 * ATTN_HEADS_TOTAL * ATTN_HEAD_DIM
    n_groups = len(ATTN_GROUPS)
    w_in = p["w_in"]
    w_att = w_in[:, n_att0:n_att0 + n_att].reshape(d, 3, n_groups, ATTN_WIDTH)
    w_att = w_att.transpose(0, 2, 1, 3).reshape(d, n_att)
    w_main = jnp.concatenate([w_in[:, :n_gate0], w_in[:, n_att0 + n_att:], w_att], axis=1).astype(BF16)
    w_if = jnp.pad(w_in[:, n_gate0:n_att0], ((0, 0), (0, N_GATE_PAD - 2 * MLSTM_HEADS))).astype(BF16)
    proj, gif, *qkvs = in_proj(x2d, p["norm1_g"].reshape(1, d), w_main, w_if, batch, seq, tm)

    bif = jnp.concatenate([p["b_igate"], p["b_fgate"]]).astype(F32)
    bcol = jnp.pad(bif, (0, N_GATE_PAD - bif.shape[0])).reshape(1, N_GATE_PAD)
    ya = mlstm(proj, gif, gif[:, :8].T, bcol, bif.reshape(8, 1), p["conv_w"], p["conv_b"].reshape(1, -1),
               p["mh_norm_g"].reshape(1, -1), batch, seq)

    outs, lses = zip(*[attn_group(qkvs[g], bias, g) for g in range(n_groups)])

    tmat, wmat, vmat, coef = s5_prep(p["lam_re"], p["lam_im"], p["log_dt"], p["b_re"], p["b_im"],
                                     p["c_re"], p["c_im"], p["d_skip"])
    z = s5_scan(proj, tmat, wmat, vmat, coef, batch, seq)
    yc = glu(z, p["w_glu"].astype(BF16), tm, 512)

    mix = gated_mix(ya, outs, lses, yc, p["w_br_a"].astype(BF16), p["w_br_b"].astype(BF16),
                    p["w_br_c"].astype(BF16), proj, seq, tm, 512)
    x2d = out_proj(x2d, mix, p["w_out"].astype(BF16), tm, 512)
    return ffn(x2d, p["norm2_g"].reshape(1, d), p["w_ff1"].astype(BF16), p["w_ff2"].astype(BF16),
               _tile(t, 512), 1024)


_PER_LAYER = ("norm1_g", "w_in", "conv_w", "conv_b", "b_igate", "b_fgate", "mh_norm_g", "lam_re", "lam_im",
              "log_dt", "b_re", "b_im", "c_re", "c_im", "d_skip", "w_glu", "w_br_a", "w_br_b", "w_br_c",
              "w_out", "norm2_g", "w_ff1", "w_ff2")


def kernel(x, norm1_g, w_in, conv_w, conv_b, b_igate, b_fgate, mh_norm_g, rel_bias, lam_re, lam_im, log_dt,
           b_re, b_im, c_re, c_im, d_skip, w_glu, w_br_a, w_br_b, w_br_c, w_out, norm2_g, w_ff1, w_ff2,
           final_g):
    stacked = dict(norm1_g=norm1_g, w_in=w_in, conv_w=conv_w, conv_b=conv_b, b_igate=b_igate,
                   b_fgate=b_fgate, mh_norm_g=mh_norm_g, lam_re=lam_re, lam_im=lam_im, log_dt=log_dt,
                   b_re=b_re, b_im=b_im, c_re=c_re, c_im=c_im, d_skip=d_skip, w_glu=w_glu, w_br_a=w_br_a,
                   w_br_b=w_br_b, w_br_c=w_br_c, w_out=w_out, norm2_g=norm2_g, w_ff1=w_ff1, w_ff2=w_ff2)
    batch, seq, d = x.shape
    x2d = x.astype(F32).reshape(batch * seq, d)
    bias = attn_bias(rel_bias.astype(F32))
    for l in range(w_in.shape[0]):
        x2d = _layer(x2d, bias, batch, seq, {k: stacked[k][l] for k in _PER_LAYER})
    out = final_norm(x2d, final_g.reshape(1, d), _tile(batch * seq, 1024))
    return out.reshape(batch, seq, d).astype(x.dtype)
```

```python
import functools
import math

import jax
import jax.numpy as jnp
from jax import lax
from jax.experimental import pallas as pl
from jax.experimental.pallas import tpu as pltpu

F32 = jnp.float32
BF16 = jnp.bfloat16
SDS = jax.ShapeDtypeStruct

NORM_EPS = 1e-6
NEG = -1e30

MLSTM_HEADS = 4
MLSTM_HEAD_DIM = 256
MLSTM_WIDTH = MLSTM_HEADS * MLSTM_HEAD_DIM
CONV_WIDTH = 4
ATTN_GROUPS = ((128, 1), (512, 4), (2048, 16))
ATTN_HEADS = 8
ATTN_HEAD_DIM = 128
ATTN_WIDTH = ATTN_HEADS * ATTN_HEAD_DIM
ATTN_HEADS_TOTAL = len(ATTN_GROUPS) * ATTN_HEADS
ATTN_BLOCK = 128
ATTN_Q_BLOCKS = 4
REL_BUCKETS = 32
REL_MAX_DISTANCE = 2048
S5_WIDTH = 1024
S5_GROUP = 16
S5_GROUPS = S5_WIDTH // S5_GROUP
S5_STATE = 64
S5_TOK = 16
N_GATE_PAD = 128

COL_U = 3 * MLSTM_WIDTH
COL_O = COL_U + S5_WIDTH
COL_G = COL_O + MLSTM_WIDTH
D_MODEL = 2048
N_MAIN = COL_G + 3 * D_MODEL
LANES = 128

MLSTM_CHUNK = 256
VMEM_LIMIT = 56 * 2**20


def _params(sem):
    return pltpu.CompilerParams(dimension_semantics=sem, vmem_limit_bytes=VMEM_LIMIT)


def _sigmoid(x):
    return 1.0 / (1.0 + jnp.exp(-x))


def _log_sigmoid(x):
    return jnp.minimum(x, 0.0) - jnp.log(1.0 + jnp.exp(-jnp.abs(x)))


def _rms(x, g):
    ms = jnp.mean(x * x, axis=-1, keepdims=True)
    return x * lax.rsqrt(ms + NORM_EPS) * g


def _dot(a, b):
    return jnp.dot(a, b, preferred_element_type=F32)


def _dot_nt(a, b):
    return lax.dot_general(a, b, (((1,), (1,)), ((), ())), preferred_element_type=F32)


def _dot_tn(a, b):
    return lax.dot_general(a, b, (((0,), (0,)), ((), ())), preferred_element_type=F32)


def _inproj_body(ns, nm, x_ref, g_ref, w_ref, wif_ref, o_ref, oif_ref, a0_ref, a1_ref, a2_ref, xn_ref,
                 acc_ref):
    j = pl.program_id(1)
    tm = x_ref.shape[0]
    tn = w_ref.shape[1]

    @pl.when(j == 0)
    def _():
        xn = _rms(x_ref[...], g_ref[...]).astype(BF16)
        xn_ref[...] = xn
        oif_ref[...] = _dot(xn, wif_ref[...])

    @pl.when(j < ns)
    def _():
        o_ref[...] = _dot(xn_ref[...], w_ref[...]).astype(o_ref.dtype)

    @pl.when(jnp.logical_and(j >= ns, j < nm))
    def _():
        o_ref[...] = _sigmoid(_dot(xn_ref[...], w_ref[...])).astype(o_ref.dtype)

    for g, a_ref in enumerate((a0_ref, a1_ref, a2_ref)):
        r = ATTN_GROUPS[g][1]
        lo = nm + 3 * g

        @pl.when(jnp.logical_and(j >= lo, j < lo + 3))
        def _(a_ref=a_ref, r=r):
            res = _dot(xn_ref[...], w_ref[...])
            if r == 1:
                a_ref[0, 0] = res.astype(a_ref.dtype)
            else:
                for s in range(tn // LANES):
                    acc_ref[s] = res[:, s * LANES:(s + 1) * LANES]
                for c in range(r):
                    for s in range(tn // LANES):
                        a_ref[0, c, :, s * LANES:(s + 1) * LANES] = (
                            acc_ref[s, pl.ds(c, tm // r, stride=r), :].astype(a_ref.dtype))


def in_proj(x2d, g, w, wif, batch, seq, tm):
    t, d = x2d.shape
    tn = ATTN_WIDTH
    nm = N_MAIN // tn
    nbt = seq // tm
    n_groups = len(ATTN_GROUPS)

    def a_spec(gi):
        r = ATTN_GROUPS[gi][1]
        return pl.BlockSpec((1, r, tm // r, tn),
                            lambda i, j: (i // nbt, 0, i % nbt, jnp.clip(j - nm - 3 * gi, 0, 2)))

    return pl.pallas_call(
        functools.partial(_inproj_body, COL_O // tn, nm),
        grid=(t // tm, nm + 3 * n_groups),
        in_specs=[pl.BlockSpec((tm, d), lambda i, j: (i, 0)),
                  pl.BlockSpec((1, d), lambda i, j: (0, 0)),
                  pl.BlockSpec((d, tn), lambda i, j: (0, j)),
                  pl.BlockSpec((d, N_GATE_PAD), lambda i, j: (0, 0))],
        out_specs=[pl.BlockSpec((tm, tn), lambda i, j: (i, jnp.minimum(j, nm - 1))),
                   pl.BlockSpec((tm, N_GATE_PAD), lambda i, j: (i, 0))] + [a_spec(gi) for gi in range(n_groups)],
        out_shape=[SDS((t, N_MAIN), BF16), SDS((t, N_GATE_PAD), F32)]
        + [SDS((batch, r, seq // r, 3 * tn), BF16) for _, r in ATTN_GROUPS],
        scratch_shapes=[pltpu.VMEM((tm, d), BF16), pltpu.VMEM((tn // LANES, tm, LANES), F32)],
        compiler_params=_params(("parallel", "arbitrary")),
        name="in_proj",
    )(x2d, g, w, wif)


def _mlstm_body(q_ref, k_ref, v_ref, og_ref, gcol_ref, grow_ref, bcol_ref, brow_ref,
                cw_ref, cb_ref, ng_ref, y_ref, ct_ref, m_ref, tail_ref):
    c = pl.program_id(1)
    L = q_ref.shape[0]
    E = MLSTM_HEAD_DIM
    H = MLSTM_HEADS

    @pl.when(c == 0)
    def _():
        ct_ref[...] = jnp.zeros_like(ct_ref)
        m_ref[...] = jnp.zeros_like(m_ref)
        tail_ref[...] = jnp.zeros_like(tail_ref)

    row8 = lax.broadcasted_iota(jnp.int32, (8, E), 0)
    tt = lax.broadcasted_iota(jnp.int32, (L, L), 0)
    ss = lax.broadcasted_iota(jnp.int32, (L, L), 1)
    causal = ss <= tt
    gc = gcol_ref[...] + bcol_ref[...]
    gr = grow_ref[...] + brow_ref[...]

    def conv_silu(x_ref, h, slot):
        x = x_ref[:, h * E:(h + 1) * E].astype(F32)
        tail = tail_ref[slot]
        w = cw_ref[:, slot * E:(slot + 1) * E]
        acc = cb_ref[:, slot * E:(slot + 1) * E] + w[CONV_WIDTH - 1:CONV_WIDTH] * x
        for d in range(1, CONV_WIDTH):
            r = pltpu.roll(x, d, 0)
            head = jnp.where(row8 < d, pltpu.roll(tail, d, 0), r[:8])
            xs = jnp.concatenate([head, r[8:]], axis=0)
            acc = acc + w[CONV_WIDTH - 1 - d:CONV_WIDTH - d] * xs
        tail_ref[slot] = x[L - 8:]
        return acc * _sigmoid(acc)

    for h in range(H):
        hs = slice(h * E, (h + 1) * E)
        q = conv_silu(q_ref, h, h)
        k = conv_silu(k_ref, h, H + h) * (E ** -0.5)

        ig_col = gc[:, h:h + 1]
        lf_col = _log_sigmoid(gc[:, H + h:H + h + 1])
        ig_row = gr[h:h + 1, :]
        lf_row = _log_sigmoid(gr[H + h:H + h + 1, :])
        bcum_col = jnp.sum(jnp.where(causal, lf_row, 0.0), axis=1, keepdims=True)
        bcum_row = jnp.sum(jnp.where(tt <= ss, lf_col, 0.0), axis=0, keepdims=True)
        a_row = ig_row - bcum_row
        a_col = ig_col - bcum_col

        m_prev = m_ref[h]
        amat = jnp.where(causal, a_row, NEG)
        mrow = jnp.maximum(m_prev, jnp.max(amat, axis=1, keepdims=True))
        w_intra = jnp.exp(amat - mrow)
        w_inter = jnp.exp(m_prev - mrow)

        qb = q.astype(BF16)
        kb = k.astype(BF16)
        vaug = jnp.concatenate([v_ref[:, hs], jnp.ones((L, LANES), BF16)], axis=1)
        s = _dot_nt(qb, kb) * w_intra
        ct = ct_ref[h]
        num_aug = _dot(s.astype(BF16), vaug) + w_inter * _dot(qb, ct.astype(BF16))
        num = num_aug[:, :E]
        den = num_aug[:, E:E + 1]
        m_t = bcum_col + mrow
        hout = num / jnp.maximum(jnp.abs(den), jnp.exp(-m_t))
        hn = _rms(hout, ng_ref[:, hs])
        y_ref[:, hs] = (og_ref[:, hs].astype(F32) * hn).astype(y_ref.dtype)

        b_last = jnp.sum(lf_col, axis=0, keepdims=True)
        g_col = b_last + a_col
        m_new = jnp.maximum(b_last + m_prev, jnp.max(g_col, axis=0, keepdims=True))
        w_s = jnp.exp(g_col - m_new)
        decay = jnp.exp(b_last + m_prev - m_new)
        kw = (k * w_s).astype(BF16)
        ct_ref[h] = decay * ct + _dot_tn(kw, vaug)
        m_ref[h] = m_new


def mlstm(proj, gif, gif_t, bcol, brow, conv_w, conv_b, ng, batch, seq):
    L = min(MLSTM_CHUNK, seq)
    E = MLSTM_HEAD_DIM
    W = MLSTM_WIDTH
    nc = seq // L
    H = MLSTM_HEADS
    t = batch * seq
    row = lambda b, c: b * nc + c
    return pl.pallas_call(
        _mlstm_body,
        grid=(batch, nc),
        in_specs=[pl.BlockSpec((L, W), lambda b, c: (row(b, c), 0)),
                  pl.BlockSpec((L, W), lambda b, c: (row(b, c), 1)),
                  pl.BlockSpec((L, W), lambda b, c: (row(b, c), 2)),
                  pl.BlockSpec((L, W), lambda b, c: (row(b, c), COL_O // W)),
                  pl.BlockSpec((L, N_GATE_PAD), lambda b, c: (row(b, c), 0)),
                  pl.BlockSpec((8, L), lambda b, c: (0, row(b, c))),
                  pl.BlockSpec((1, N_GATE_PAD), lambda b, c: (0, 0)),
                  pl.BlockSpec((8, 1), lambda b, c: (0, 0)),
                  pl.BlockSpec((CONV_WIDTH, 2 * W), lambda b, c: (0, 0)),
                  pl.BlockSpec((1, 2 * W), lambda b, c: (0, 0)),
                  pl.BlockSpec((1, W), lambda b, c: (0, 0))],
        out_specs=pl.BlockSpec((L, W), lambda b, c: (row(b, c), 0)),
        out_shape=SDS((t, W), BF16),
        scratch_shapes=[pltpu.VMEM((H, E, E + LANES), F32), pltpu.VMEM((H, 1, 1), F32),
                        pltpu.VMEM((2 * H, 8, E), F32)],
        compiler_params=_params(("parallel", "arbitrary")),
        name="mlstm",
    )(proj, proj, proj, proj, gif, gif_t, bcol, brow, conv_w, conv_b, ng)


def _bias_body(table_ref, o_ref):
    h = pl.program_id(0)
    dil = jnp.where(h < ATTN_HEADS, ATTN_GROUPS[0][1],
                    jnp.where(h < 2 * ATTN_HEADS, ATTN_GROUPS[1][1], ATTN_GROUPS[2][1]))
    shape = (ATTN_BLOCK, 2 * ATTN_BLOCK)
    i = lax.broadcasted_iota(jnp.int32, shape, 0)
    j = lax.broadcasted_iota(jnp.int32, shape, 1)
    rel = ATTN_BLOCK + i - j
    dist = jnp.maximum(rel, 0) * dil
    max_exact = REL_BUCKETS // 2
    nf = jnp.maximum(dist, max_exact).astype(F32)
    large = max_exact + (jnp.log(nf / max_exact) / math.log(REL_MAX_DISTANCE / max_exact)
                         * (REL_BUCKETS - max_exact)).astype(jnp.int32)
    large = jnp.minimum(large, REL_BUCKETS - 1)
    bucket = jnp.where(dist < max_exact, dist, large)
    acc = jnp.zeros(shape, F32)
    for b in range(REL_BUCKETS):
        acc = jnp.where(bucket == b, table_ref[b, h], acc)
    o_ref[0] = jnp.where(rel >= 0, jnp.where(rel <= ATTN_BLOCK, acc, NEG), NEG)


def attn_bias(rel_bias):
    return pl.pallas_call(
        _bias_body,
        grid=(ATTN_HEADS_TOTAL,),
        in_specs=[pl.BlockSpec(memory_space=pltpu.SMEM)],
        out_specs=pl.BlockSpec((1, ATTN_BLOCK, 2 * ATTN_BLOCK), lambda h: (h, 0, 0)),
        out_shape=SDS((ATTN_HEADS_TOTAL, ATTN_BLOCK, 2 * ATTN_BLOCK), F32),
        compiler_params=_params(("arbitrary",)),
        name="attn_bias",
    )(rel_bias)


def _attn_body(q_ref, kp_ref, kc_ref, vp_ref, vc_ref, bias_ref, o_ref, lse_ref):
    n = pl.program_id(2)
    B = ATTN_BLOCK
    E = ATTN_HEAD_DIM
    H = ATTN_HEADS
    NQ = q_ref.shape[2] // B
    scale = E ** -0.5
    hs = [slice(h * E, (h + 1) * E) for h in range(H)]

    def keys(cur_ref, prev_ref, i, sl):
        if i == 0:
            return jnp.concatenate([prev_ref[0, 0, :, sl], cur_ref[0, 0, :B, sl]], axis=0)
        return cur_ref[0, 0, (i - 1) * B:(i + 1) * B, sl]

    s = jnp.concatenate([_dot_nt(q_ref[0, 0, i * B:(i + 1) * B, sl], keys(kc_ref, kp_ref, i, sl))
                         for i in range(NQ) for sl in hs], axis=0)
    bias = bias_ref[...].reshape(H * B, 2 * B)
    key = lax.broadcasted_iota(jnp.int32, (1, 2 * B), 1)
    first = bias + jnp.where(key < B, jnp.where(n > 0, 0.0, NEG), 0.0)
    s = s * scale + jnp.concatenate([first] + [bias] * (NQ - 1), axis=0)
    m = jnp.max(s, axis=1, keepdims=True)
    p = jnp.exp(s - m)
    den = jnp.sum(p, axis=1, keepdims=True)
    inv = 1.0 / den
    pb = p.astype(BF16)
    lse_all = m + jnp.log(den)
    lane = lax.broadcasted_iota(jnp.int32, (B, LANES), 1)
    for i in range(NQ):
        lse = jnp.zeros((B, LANES), F32)
        for h, sl in enumerate(hs):
            rows = slice((i * H + h) * B, (i * H + h + 1) * B)
            o = _dot(pb[rows], keys(vc_ref, vp_ref, i, sl)) * inv[rows]
            o_ref[0, 0, i * B:(i + 1) * B, sl] = o.astype(o_ref.dtype)
            lse = jnp.where(lane == h, lse_all[rows], lse)
        lse_ref[0, 0, i * B:(i + 1) * B, :] = lse


def attn_group(qkv, bias, g):
    batch, r, l, _ = qkv.shape
    W = ATTN_WIDTH
    nq = min(ATTN_Q_BLOCKS, l // ATTN_BLOCK)
    rows = nq * ATTN_BLOCK
    cur = (1, 1, rows, W)
    one = (1, 1, ATTN_BLOCK, W)
    prev = lambda n: jnp.maximum(n * nq - 1, 0)
    return pl.pallas_call(
        _attn_body,
        grid=(batch, r, l // rows),
        in_specs=[pl.BlockSpec(cur, lambda b, c, n: (b, c, n, 0)),
                  pl.BlockSpec(one, lambda b, c, n: (b, c, prev(n), 1)),
                  pl.BlockSpec(cur, lambda b, c, n: (b, c, n, 1)),
                  pl.BlockSpec(one, lambda b, c, n: (b, c, prev(n), 2)),
                  pl.BlockSpec(cur, lambda b, c, n: (b, c, n, 2)),
                  pl.BlockSpec((ATTN_HEADS, ATTN_BLOCK, 2 * ATTN_BLOCK), lambda b, c, n: (g, 0, 0))],
        out_specs=[pl.BlockSpec(cur, lambda b, c, n: (b, c, n, 0)),
                   pl.BlockSpec((1, 1, rows, LANES), lambda b, c, n: (b, c, n, 0))],
        out_shape=[SDS((batch, r, l, W), BF16), SDS((batch, r, l, LANES), F32)],
        compiler_params=_params(("parallel", "parallel", "arbitrary")),
        name=f"attn_g{g}",
    )(qkv, qkv, qkv, qkv, qkv, bias)


def _s5_prep_body(lam_ref, lamc_ref, dt_ref, bt_r_ref, bt_i_ref, ce_r_ref, ce_i_ref, d_ref,
                  t_ref, w_ref, v_ref, coef_ref):
    P = S5_STATE
    lr = lam_ref[0, 0:1, :]
    li = lam_ref[0, 1:2, :]
    dt = jnp.exp(dt_ref[0])

    def apow(e, lr_, li_):
        mag = jnp.exp(lr_ * dt * e)
        ang = li_ * dt * e
        return mag * jnp.cos(ang), mag * jnp.sin(ang)

    one = jnp.ones((1, 1), F32)
    ar, ai = apow(one, lr, li)
    nr = ar - 1.0
    den = lr * lr + li * li
    f_re = (nr * lr + ai * li) / den
    f_im = (ai * lr - nr * li) / den
    bt_r = bt_r_ref[0]
    bt_i = bt_i_ref[0]
    bb_r = f_re * bt_r - f_im * bt_i
    bb_i = f_re * bt_i + f_im * bt_r

    lrc = lamc_ref[0, :, 0:1]
    lic = lamc_ref[0, :, 1:2]
    lag = (lax.broadcasted_iota(jnp.int32, (P, 256), 1) // S5_GROUP).astype(F32)
    adr, adi = apow(lag, lrc, lic)
    ce_r = ce_r_ref[0]
    ce_i = ce_i_ref[0]
    ca_r = ce_r * adr - ce_i * adi
    ca_i = ce_r * adi + ce_i * adr
    hp = lax.Precision.HIGHEST
    ks = (jnp.dot(bb_r[:, :P], ca_r, precision=hp, preferred_element_type=F32)
          - jnp.dot(bb_i[:, :P], ca_i, precision=hp, preferred_element_type=F32))
    si = lax.broadcasted_iota(jnp.int32, (S5_GROUP, 256), 0)
    lj = lax.broadcasted_iota(jnp.int32, (S5_GROUP, 256), 1)
    ks = ks + jnp.where(si == lj, d_ref[0], 0.0)
    for s in range(S5_TOK):
        sh = s * S5_GROUP
        blk = ks if s == 0 else jnp.where(lj >= sh, pltpu.roll(ks, sh, 1), 0.0)
        t_ref[0, sh:sh + S5_GROUP, :] = blk.astype(t_ref.dtype)

    lane128 = lax.broadcasted_iota(jnp.int32, (1, 128), 1)
    for s in range(S5_TOK):
        pr, pi = apow(float(S5_TOK - 1 - s) * one, lr, li)
        p1 = jnp.where(lane128 < P, pr, pi)
        p2 = jnp.where(lane128 < P, -pi, pr)
        w_ref[0, s * S5_GROUP:(s + 1) * S5_GROUP, :] = (bb_r * p1 + bb_i * p2).astype(w_ref.dtype)

    adr1, adi1 = apow(lag + 1.0, lrc, lic)
    v_ref[0, 0:P, :] = (ce_r * adr1 - ce_i * adi1).astype(v_ref.dtype)
    v_ref[0, P:2 * P, :] = (-(ce_r * adi1 + ce_i * adr1)).astype(v_ref.dtype)

    ek = jnp.left_shift(S5_TOK, lax.broadcasted_iota(jnp.int32, (16, 1), 0)).astype(F32)
    cr, ci = apow(ek, lr, li)
    coef_ref[0, 0:16, :] = cr
    coef_ref[0, 16:32, :] = jnp.where(lane128 < P, -ci, ci)


def s5_prep(lam_re, lam_im, log_dt, b_re, b_im, c_re, c_im, d_skip):
    G, P, I = b_re.shape
    lam = jnp.stack([jnp.tile(lam_re, (1, 2)), jnp.tile(lam_im, (1, 2))], axis=1)
    lam = jnp.pad(lam, ((0, 0), (0, 6), (0, 0)))
    lamc = jnp.stack([lam_re, lam_im], axis=2)
    dt = log_dt.reshape(G, 1, 1)
    bt_r = jnp.tile(jnp.swapaxes(b_re, 1, 2), (1, 1, 2))
    bt_i = jnp.tile(jnp.swapaxes(b_im, 1, 2), (1, 1, 2))
    ce_r = jnp.tile(jnp.swapaxes(c_re, 1, 2), (1, 1, S5_TOK))
    ce_i = jnp.tile(jnp.swapaxes(c_im, 1, 2), (1, 1, S5_TOK))
    d = jnp.pad(d_skip, ((0, 0), (0, 256 - I))).reshape(G, 1, 256)
    blk = lambda *s: pl.BlockSpec((1,) + s, lambda g: (g, 0, 0))
    return pl.pallas_call(
        _s5_prep_body,
        grid=(G,),
        in_specs=[blk(8, 128), blk(P, 2), blk(1, 1), blk(16, 128), blk(16, 128), blk(P, 256), blk(P, 256),
                  blk(1, 256)],
        out_specs=[blk(256, 256), blk(256, 128), blk(128, 256), blk(32, 128)],
        out_shape=[SDS((G, 256, 256), BF16), SDS((G, 256, 128), BF16), SDS((G, 128, 256), BF16),
                   SDS((G, 32, 128), F32)],
        compiler_params=_params(("arbitrary",)),
        name="s5_prep",
    )(lam, lamc, dt, bt_r, bt_i, ce_r, ce_i, d)


def _gelu_tanh(x):
    return 0.5 * x * (1.0 + jnp.tanh(math.sqrt(2.0 / math.pi) * (x + 0.044715 * (x * x * x))))


def _chunk_transpose(arrs, chunk):
    a = list(arrs)
    n = len(a)
    d = n // 2
    while d >= 1:
        bit = (chunk & d) != 0
        nxt = list(a)
        for i in range(n):
            if i & d == 0:
                lo, hi = a[i], a[i + d]
                nxt[i] = jnp.where(bit, pltpu.roll(hi, d * S5_GROUP, 1), lo)
                nxt[i + d] = jnp.where(bit, hi, pltpu.roll(lo, LANES - d * S5_GROUP, 1))
        a = nxt
        d //= 2
    return a


def _s5_body(u_ref, t_ref, w_ref, v_ref, coef_ref, z_ref, nat_ref, uf_ref, zs_ref):
    S = u_ref.shape[0]
    R = S // S5_TOK
    P = S5_STATE
    GL = LANES // S5_GROUP
    RC = min(R, 64)
    chunk = lax.broadcasted_iota(jnp.int32, (RC, LANES), 1) // S5_GROUP

    nat_ref[...] = u_ref[...].astype(F32)

    def to_flat(rc, carry):
        r0 = pl.multiple_of(rc * RC, RC)
        for hf in range(S5_TOK // GL):
            arrs = [nat_ref[pl.ds(r0 * S5_TOK + hf * GL + k, RC, stride=S5_TOK), :] for k in range(GL)]
            for gl, a in enumerate(_chunk_transpose(arrs, chunk)):
                uf_ref[gl, pl.ds(r0, RC), hf * LANES:(hf + 1) * LANES] = a.astype(uf_ref.dtype)
        return carry

    lax.fori_loop(0, R // RC, to_flat, 0)

    row = lax.broadcasted_iota(jnp.int32, (R, LANES), 0)

    def shift_down(a, sh):
        if sh % 8 == 0:
            return jnp.concatenate([jnp.zeros((sh, LANES), F32), a[:R - sh]], axis=0)
        return jnp.where(row < sh, 0.0, pltpu.roll(a, sh, 0))

    def group(gl, carry):
        u = uf_ref[gl]
        y = _dot(u, t_ref[gl])
        x = _dot(u, w_ref[gl])
        k = 0
        while (1 << k) < R:
            xs = shift_down(x, 1 << k)
            x = (x + coef_ref[gl, k:k + 1, :] * xs
                 + coef_ref[gl, 16 + k:17 + k, :] * pltpu.roll(xs, P, 1))
            k += 1
        y = y + _dot(shift_down(x, 1).astype(BF16), v_ref[gl])
        zs_ref[gl] = _gelu_tanh(y)
        return carry

    lax.fori_loop(0, GL, group, 0)

    def to_nat(rc, carry):
        r0 = pl.multiple_of(rc * RC, RC)
        for hf in range(S5_TOK // GL):
            arrs = [zs_ref[gl, pl.ds(r0, RC), hf * LANES:(hf + 1) * LANES] for gl in range(GL)]
            for k, a in enumerate(_chunk_transpose(arrs, chunk)):
                nat_ref[pl.ds(r0 * S5_TOK + hf * GL + k, RC, stride=S5_TOK), :] = a
        return carry

    lax.fori_loop(0, R // RC, to_nat, 0)
    z_ref[...] = nat_ref[...].astype(z_ref.dtype)


def s5_scan(proj, tm, wm, vm, coef, batch, seq):
    t = proj.shape[0]
    GL = LANES // S5_GROUP
    r = seq // S5_TOK
    ub = COL_U // LANES
    per_g = lambda *s: pl.BlockSpec((GL,) + s, lambda bi, si: (si, 0, 0))
    return pl.pallas_call(
        _s5_body,
        grid=(batch, S5_WIDTH // LANES),
        in_specs=[pl.BlockSpec((seq, LANES), lambda bi, si: (bi, ub + si)),
                  per_g(256, 256), per_g(256, 128), per_g(128, 256), per_g(32, 128)],
        out_specs=pl.BlockSpec((seq, LANES), lambda bi, si: (bi, si)),
        out_shape=SDS((t, S5_WIDTH), BF16),
        scratch_shapes=[pltpu.VMEM((seq, LANES), F32), pltpu.VMEM((GL, r, 2 * LANES), BF16),
                        pltpu.VMEM((GL, r, 2 * LANES), F32)],
        compiler_params=_params(("parallel", "parallel")),
        name="s5_scan",
    )(proj, tm, wm, vm, coef)


def _glu_body(z_ref, wl_ref, wg_ref, o_ref):
    z = z_ref[...]
    o_ref[...] = (_dot(z, wl_ref[...]) * _sigmoid(_dot(z, wg_ref[...]))).astype(o_ref.dtype)


def glu(z, w, tm, tn):
    t, k = z.shape
    n = w.shape[1] // 2
    nj = n // tn
    return pl.pallas_call(
        _glu_body,
        grid=(t // tm, nj),
        in_specs=[pl.BlockSpec((tm, k), lambda i, j: (i, 0)),
                  pl.BlockSpec((k, tn), lambda i, j: (0, j)),
                  pl.BlockSpec((k, tn), lambda i, j: (0, nj + j))],
        out_specs=pl.BlockSpec((tm, tn), lambda i, j: (i, j)),
        out_shape=SDS((t, n), BF16),
        compiler_params=_params(("parallel", "arbitrary")),
        name="glu",
    )(z, w, w)


def _mix_body(ya_ref, o0_ref, o1_ref, o2_ref, l0_ref, l1_ref, l2_ref, yc_ref,
              wa_ref, wb_ref, wc_ref, ga_ref, gb_ref, gc_ref, out_ref, yb_ref, wt_ref, acc_ref):
    tm = ya_ref.shape[0]
    E = ATTN_HEAD_DIM

    @pl.when(pl.program_id(1) == 0)
    def _():
        o_refs = (o0_ref, o1_ref, o2_ref)
        l_refs = (l0_ref, l1_ref, l2_ref)
        for g, (_, r) in enumerate(ATTN_GROUPS):
            for c in range(r):
                wt_ref[g, pl.ds(c, tm // r, stride=r), :] = l_refs[g][0, c]
        l0, l1, l2 = wt_ref[0], wt_ref[1], wt_ref[2]
        m = jnp.maximum(jnp.maximum(l0, l1), l2)
        e0, e1, e2 = jnp.exp(l0 - m), jnp.exp(l1 - m), jnp.exp(l2 - m)
        inv = 1.0 / (e0 + e1 + e2)
        wt_ref[0] = e0 * inv
        wt_ref[1] = e1 * inv
        wt_ref[2] = e2 * inv
        order = sorted(range(len(ATTN_GROUPS)), key=lambda g: -ATTN_GROUPS[g][1])
        for pos, g in enumerate(order[:-1]):
            r = ATTN_GROUPS[g][1]
            for c in range(r):
                rows = pl.ds(c, tm // r, stride=r)
                w = wt_ref[g, rows, :]
                for h in range(ATTN_HEADS):
                    part = w[:, h:h + 1] * o_refs[g][0, c, :, h * E:(h + 1) * E].astype(F32)
                    if pos == 0:
                        acc_ref[h, rows, :] = part
                    else:
                        acc_ref[h, rows, :] += part
        g = order[-1]
        assert ATTN_GROUPS[g][1] == 1
        w = wt_ref[g]
        for h in range(ATTN_HEADS):
            sl = slice(h * E, (h + 1) * E)
            yb = acc_ref[h] + w[:, h:h + 1] * o_refs[g][0, 0, :, sl].astype(F32)
            yb_ref[:, sl] = yb.astype(yb_ref.dtype)

    mix = (ga_ref[...].astype(F32) * _dot(ya_ref[...], wa_ref[...])
           + gb_ref[...].astype(F32) * _dot(yb_ref[...], wb_ref[...])
           + gc_ref[...].astype(F32) * _dot(yc_ref[...], wc_ref[...]))
    out_ref[...] = mix.astype(out_ref.dtype)


def gated_mix(ya, outs, lses, yc, wa, wb, wc, proj, seq, tm, tn):
    t, kw = ya.shape
    d = wa.shape[1]
    go = COL_G // tn
    nbt = seq // tm
    row = lambda w: pl.BlockSpec((tm, w), lambda i, j: (i, 0))
    grp = lambda r, w: pl.BlockSpec((1, r, tm // r, w), lambda i, j: (i // nbt, 0, i % nbt, 0))
    wsp = pl.BlockSpec((kw, tn), lambda i, j: (0, j))
    gate = lambda o: pl.BlockSpec((tm, tn), lambda i, j: (i, go + o * (d // tn) + j))
    dils = [r for _, r in ATTN_GROUPS]
    return pl.pallas_call(
        _mix_body,
        grid=(t // tm, d // tn),
        in_specs=[row(kw)] + [grp(r, kw) for r in dils] + [grp(r, LANES) for r in dils] + [row(kw)]
        + [wsp, wsp, wsp, gate(0), gate(1), gate(2)],
        out_specs=pl.BlockSpec((tm, tn), lambda i, j: (i, j)),
        out_shape=SDS((t, d), BF16),
        scratch_shapes=[pltpu.VMEM((tm, kw), BF16), pltpu.VMEM((len(dils), tm, LANES), F32),
                        pltpu.VMEM((ATTN_HEADS, tm, LANES), F32)],
        compiler_params=_params(("parallel", "arbitrary")),
        name="gated_mix",
    )(ya, *outs, *lses, yc, wa, wb, wc, proj, proj, proj)


def _outproj_body(x_ref, m_ref, w_ref, o_ref):
    o_ref[...] = x_ref[...] + _dot(m_ref[...], w_ref[...])


def out_proj(x2d, mix, w, tm, tn):
    t, d = x2d.shape
    return pl.pallas_call(
        _outproj_body,
        grid=(t // tm, d // tn),
        in_specs=[pl.BlockSpec((tm, tn), lambda i, j: (i, j)),
                  pl.BlockSpec((tm, d), lambda i, j: (i, 0)),
                  pl.BlockSpec((d, tn), lambda i, j: (0, j))],
        out_specs=pl.BlockSpec((tm, tn), lambda i, j: (i, j)),
        out_shape=SDS((t, d), F32),
        compiler_params=_params(("parallel", "arbitrary")),
        name="out_proj",
    )(x2d, mix, w)


def _ffn_body(x_ref, g_ref, w1_ref, w2_ref, o_ref, xn_ref):
    @pl.when(pl.program_id(1) == 0)
    def _():
        x = x_ref[...]
        xn_ref[...] = _rms(x, g_ref[...]).astype(BF16)
        o_ref[...] = x

    hid = jnp.maximum(_dot(xn_ref[...], w1_ref[...]), 0.0)
    o_ref[...] += _dot((hid * hid).astype(BF16), w2_ref[...])


def ffn(x2d, g, w1, w2, tm, th):
    t, d = x2d.shape
    hdim = w1.shape[1]
    return pl.pallas_call(
        _ffn_body,
        grid=(t // tm, hdim // th),
        in_specs=[pl.BlockSpec((tm, d), lambda i, j: (i, 0)),
                  pl.BlockSpec((1, d), lambda i, j: (0, 0)),
                  pl.BlockSpec((d, th), lambda i, j: (0, j)),
                  pl.BlockSpec((th, d), lambda i, j: (j, 0))],
        out_specs=pl.BlockSpec((tm, d), lambda i, j: (i, 0)),
        out_shape=SDS((t, d), F32),
        scratch_shapes=[pltpu.VMEM((tm, d), BF16)],
        compiler_params=_params(("parallel", "arbitrary")),
        name="ffn",
    )(x2d, g, w1, w2)


def _norm_body(x_ref, g_ref, o_ref):
    o_ref[...] = _rms(x_ref[...], g_ref[...])


def final_norm(x2d, g, tm):
    t, d = x2d.shape
    return pl.pallas_call(
        _norm_body,
        grid=(t // tm,),
        in_specs=[pl.BlockSpec((tm, d), lambda i: (i, 0)), pl.BlockSpec((1, d), lambda i: (0, 0))],
        out_specs=pl.BlockSpec((tm, d), lambda i: (i, 0)),
        out_shape=SDS((t, d), F32),
        compiler_params=_params(("parallel",)),
        name="final_norm",
    )(x2d, g)


def _tile(n, want):
    t = min(n, want)
    assert n % t == 0, (n, want)
    return t


def _layer(x2d, bias, batch, seq, p):
    t, d = x2d.shape
    tm = _tile(seq, 1024)
    n_gate0 = 4 * MLSTM_WIDTH
    n_att0 = n_gate0 + 2 * MLSTM_HEADS
    n_att = 3 * ATTN_HEADS_TOTAL * ATTN_HEAD_DIM
    n_groups = len(ATTN_GROUPS)
    w_in = p["w_in"]
    w_att = w_in[:, n_att0:n_att0 + n_att].reshape(d, 3, n_groups, ATTN_WIDTH)
    w_att = w_att.transpose(0, 2, 1, 3).reshape(d, n_att)
    n_u0 = n_att0 + n_att
    w_main = jnp.concatenate([w_in[:, :COL_U], w_in[:, n_u0:n_u0 + S5_WIDTH], w_in[:, COL_U:n_gate0],
                              w_in[:, n_u0 + S5_WIDTH:], w_att], axis=1).astype(BF16)
    w_if = jnp.pad(w_in[:, n_gate0:n_att0], ((0, 0), (0, N_GATE_PAD - 2 * MLSTM_HEADS))).astype(BF16)
    proj, gif, *qkvs = in_proj(x2d, p["norm1_g"].reshape(1, d), w_main, w_if, batch, seq, tm)

    bif = jnp.concatenate([p["b_igate"], p["b_fgate"]]).astype(F32)
    bcol = jnp.pad(bif, (0, N_GATE_PAD - bif.shape[0])).reshape(1, N_GATE_PAD)
    ya = mlstm(proj, gif, gif[:, :8].T, bcol, bif.reshape(8, 1), p["conv_w"], p["conv_b"].reshape(1, -1),
               p["mh_norm_g"].reshape(1, -1), batch, seq)

    outs, lses = zip(*[attn_group(qkvs[g], bias, g) for g in range(n_groups)])

    tmat, wmat, vmat, coef = s5_prep(p["lam_re"], p["lam_im"], p["log_dt"], p["b_re"], p["b_im"],
                                     p["c_re"], p["c_im"], p["d_skip"])
    z = s5_scan(proj, tmat, wmat, vmat, coef, batch, seq)
    yc = glu(z, p["w_glu"].astype(BF16), tm, 512)

    mix = gated_mix(ya, outs, lses, yc, p["w_br_a"].astype(BF16), p["w_br_b"].astype(BF16),
                    p["w_br_c"].astype(BF16), proj, seq, tm, 512)
    x2d = out_proj(x2d, mix, p["w_out"].astype(BF16), tm, 512)
    return ffn(x2d, p["norm2_g"].reshape(1, d), p["w_ff1"].astype(BF16), p["w_ff2"].astype(BF16),
               _tile(t, 512), 1024)


_PER_LAYER = ("norm1_g", "w_in", "conv_w", "conv_b", "b_igate", "b_fgate", "mh_norm_g", "lam_re", "lam_im",
              "log_dt", "b_re", "b_im", "c_re", "c_im", "d_skip", "w_glu", "w_br_a", "w_br_b", "w_br_c",
              "w_out", "norm2_g", "w_ff1", "w_ff2")


def kernel(x, norm1_g, w_in, conv_w, conv_b, b_igate, b_fgate, mh_norm_g, rel_bias, lam_re, lam_im, log_dt,
           b_re, b_im, c_re, c_im, d_skip, w_glu, w_br_a, w_br_b, w_br_c, w_out, norm2_g, w_ff1, w_ff2,
           final_g):
    stacked = dict(norm1_g=norm1_g, w_in=w_in, conv_w=conv_w, conv_b=conv_b, b_igate=b_igate,
                   b_fgate=b_fgate, mh_norm_g=mh_norm_g, lam_re=lam_re, lam_im=lam_im, log_dt=log_dt,
                   b_re=b_re, b_im=b_im, c_re=c_re, c_im=c_im, d_skip=d_skip, w_glu=w_glu, w_br_a=w_br_a,
                   w_br_b=w_br_b, w_br_c=w_br_c, w_out=w_out, norm2_g=norm2_g, w_ff1=w_ff1, w_ff2=w_ff2)
    batch, seq, d = x.shape
    x2d = x.astype(F32).reshape(batch * seq, d)
    bias = attn_bias(rel_bias.astype(F32))
    for l in range(w_in.shape[0]):
        x2d = _layer(x2d, bias, batch, seq, {k: stacked[k][l] for k in _PER_LAYER})
    out = final_norm(x2d, final_g.reshape(1, d), _tile(batch * seq, 1024))
    return out.reshape(batch, seq, d).astype(x.dtype)
```

```python
import functools
import math

import jax
import jax.numpy as jnp
from jax import lax
from jax.experimental import pallas as pl
from jax.experimental.pallas import tpu as pltpu

F32 = jnp.float32
BF16 = jnp.bfloat16
SDS = jax.ShapeDtypeStruct

NORM_EPS = 1e-6
NEG = -1e30

MLSTM_HEADS = 4
MLSTM_HEAD_DIM = 256
MLSTM_WIDTH = MLSTM_HEADS * MLSTM_HEAD_DIM
CONV_WIDTH = 4
ATTN_GROUPS = ((128, 1), (512, 4), (2048, 16))
ATTN_HEADS = 8
ATTN_HEAD_DIM = 128
ATTN_WIDTH = ATTN_HEADS * ATTN_HEAD_DIM
ATTN_HEADS_TOTAL = len(ATTN_GROUPS) * ATTN_HEADS
ATTN_BLOCK = 128
ATTN_Q_BLOCKS = 4
REL_BUCKETS = 32
REL_MAX_DISTANCE = 2048
S5_WIDTH = 1024
S5_GROUP = 16
S5_GROUPS = S5_WIDTH // S5_GROUP
S5_STATE = 64
S5_TOK = 16
N_GATE_PAD = 128

COL_U = 3 * MLSTM_WIDTH
COL_O = COL_U + S5_WIDTH
COL_G = COL_O + MLSTM_WIDTH
D_MODEL = 2048
N_MAIN = COL_G + 3 * D_MODEL
LANES = 128

MLSTM_CHUNK = 256
VMEM_LIMIT = 56 * 2**20


def _params(sem):
    return pltpu.CompilerParams(dimension_semantics=sem, vmem_limit_bytes=VMEM_LIMIT)


def _sigmoid(x):
    return 1.0 / (1.0 + jnp.exp(-x))


def _log_sigmoid(x):
    return jnp.minimum(x, 0.0) - jnp.log(1.0 + jnp.exp(-jnp.abs(x)))


def _rms(x, g):
    ms = jnp.mean(x * x, axis=-1, keepdims=True)
    return x * lax.rsqrt(ms + NORM_EPS) * g


def _dot(a, b):
    return jnp.dot(a, b, preferred_element_type=F32)


def _dot_nt(a, b):
    return lax.dot_general(a, b, (((1,), (1,)), ((), ())), preferred_element_type=F32)


def _dot_tn(a, b):
    return lax.dot_general(a, b, (((0,), (0,)), ((), ())), preferred_element_type=F32)


def _inproj_body(ns, nm, x_ref, g_ref, w_ref, wif_ref, wift_ref, o_ref, oif_ref, oift_ref, a0_ref, a1_ref,
                 a2_ref, xn_ref, acc_ref):
    j = pl.program_id(1)
    tm = x_ref.shape[0]
    tn = w_ref.shape[1]

    @pl.when(j == 0)
    def _():
        xn = _rms(x_ref[...], g_ref[...]).astype(BF16)
        xn_ref[...] = xn
        oif_ref[...] = _dot(xn, wif_ref[...])
        oift_ref[...] = _dot_nt(wift_ref[...], xn)[:8]

    @pl.when(jnp.logical_and(j < nm, j != ns))
    def _():
        o_ref[...] = _dot(xn_ref[...], w_ref[...]).astype(o_ref.dtype)

    @pl.when(j == ns)
    def _():
        o_ref[...] = _sigmoid(_dot(xn_ref[...], w_ref[...])).astype(o_ref.dtype)

    for g, a_ref in enumerate((a0_ref, a1_ref, a2_ref)):
        r = ATTN_GROUPS[g][1]
        lo = nm + 3 * g

        @pl.when(jnp.logical_and(j >= lo, j < lo + 3))
        def _(a_ref=a_ref, r=r):
            res = _dot(xn_ref[...], w_ref[...])
            if r == 1:
                a_ref[0, 0] = res.astype(a_ref.dtype)
            else:
                for s in range(tn // LANES):
                    acc_ref[s] = res[:, s * LANES:(s + 1) * LANES]
                for c in range(r):
                    for s in range(tn // LANES):
                        a_ref[0, c, :, s * LANES:(s + 1) * LANES] = (
                            acc_ref[s, pl.ds(c, tm // r, stride=r), :].astype(a_ref.dtype))


def in_proj(x2d, g, w, wif, wift, batch, seq, tm):
    t, d = x2d.shape
    tn = ATTN_WIDTH
    nm = N_MAIN // tn
    nbt = seq // tm
    n_groups = len(ATTN_GROUPS)

    def a_spec(gi):
        r = ATTN_GROUPS[gi][1]
        return pl.BlockSpec((1, r, tm // r, tn),
                            lambda i, j: (i // nbt, 0, i % nbt, jnp.clip(j - nm - 3 * gi, 0, 2)))

    return pl.pallas_call(
        functools.partial(_inproj_body, COL_O // tn, nm),
        grid=(t // tm, nm + 3 * n_groups),
        in_specs=[pl.BlockSpec((tm, d), lambda i, j: (i, 0)),
                  pl.BlockSpec((1, d), lambda i, j: (0, 0)),
                  pl.BlockSpec((d, tn), lambda i, j: (0, j)),
                  pl.BlockSpec((d, N_GATE_PAD), lambda i, j: (0, 0)),
                  pl.BlockSpec((16, d), lambda i, j: (0, 0))],
        out_specs=[pl.BlockSpec((tm, tn), lambda i, j: (i, jnp.minimum(j, nm - 1))),
                   pl.BlockSpec((tm, N_GATE_PAD), lambda i, j: (i, 0)),
                   pl.BlockSpec((8, tm), lambda i, j: (0, i))] + [a_spec(gi) for gi in range(n_groups)],
        out_shape=[SDS((t, N_MAIN), BF16), SDS((t, N_GATE_PAD), F32), SDS((8, t), F32)]
        + [SDS((batch, r, seq // r, 3 * tn), BF16) for _, r in ATTN_GROUPS],
        scratch_shapes=[pltpu.VMEM((tm, d), BF16), pltpu.VMEM((tn // LANES, tm, LANES), F32)],
        compiler_params=_params(("parallel", "arbitrary")),
        name="in_proj",
    )(x2d, g, w, wif, wift)


def _mlstm_body(q_ref, k_ref, v_ref, og_ref, gcol_ref, grow_ref, bcol_ref, brow_ref,
                cw_ref, cb_ref, ng_ref, y_ref, ct_ref, m_ref, tail_ref):
    c = pl.program_id(1)
    L = q_ref.shape[0]
    E = MLSTM_HEAD_DIM
    H = MLSTM_HEADS

    @pl.when(c == 0)
    def _():
        ct_ref[...] = jnp.zeros_like(ct_ref)
        m_ref[...] = jnp.zeros_like(m_ref)
        tail_ref[...] = jnp.zeros_like(tail_ref)

    row8 = lax.broadcasted_iota(jnp.int32, (8, E), 0)
    tt = lax.broadcasted_iota(jnp.int32, (L, L), 0)
    ss = lax.broadcasted_iota(jnp.int32, (L, L), 1)
    causal = ss <= tt
    gc = gcol_ref[...] + bcol_ref[...]
    gr = grow_ref[...] + brow_ref[...]

    shifts = [jnp.where(tt - ss == d, 1.0, 0.0).astype(BF16) for d in range(1, CONV_WIDTH)]

    def conv_silu(x_ref, h, slot):
        xb = x_ref[:, h * E:(h + 1) * E]
        x = xb.astype(F32)
        tail = tail_ref[slot]
        w = cw_ref[:, slot * E:(slot + 1) * E]
        acc = cb_ref[:, slot * E:(slot + 1) * E] + w[CONV_WIDTH - 1:CONV_WIDTH] * x
        head = jnp.zeros((8, E), F32)
        for d in range(1, CONV_WIDTH):
            wd = w[CONV_WIDTH - 1 - d:CONV_WIDTH - d]
            acc = acc + wd * _dot(shifts[d - 1], xb)
            head = head + wd * jnp.where(row8 < d, pltpu.roll(tail, d, 0), 0.0)
        acc = jnp.concatenate([acc[:8] + head, acc[8:]], axis=0)
        tail_ref[slot] = x[L - 8:]
        return acc * _sigmoid(acc)

    for h in range(H):
        hs = slice(h * E, (h + 1) * E)
        q = conv_silu(q_ref, h, h)
        k = conv_silu(k_ref, h, H + h) * (E ** -0.5)

        ig_col = gc[:, h:h + 1]
        lf_col = _log_sigmoid(gc[:, H + h:H + h + 1])
        ig_row = gr[h:h + 1, :]
        lf_row = _log_sigmoid(gr[H + h:H + h + 1, :])
        bcum_col = jnp.sum(jnp.where(causal, lf_row, 0.0), axis=1, keepdims=True)
        bcum_row = jnp.sum(jnp.where(tt <= ss, lf_col, 0.0), axis=0, keepdims=True)
        a_row = ig_row - bcum_row
        a_col = ig_col - bcum_col

        m_prev = m_ref[h]
        amat = jnp.where(causal, a_row, NEG)
        mrow = jnp.maximum(m_prev, jnp.max(amat, axis=1, keepdims=True))
        w_intra = jnp.exp(amat - mrow)
        w_inter = jnp.exp(m_prev - mrow)

        qb = q.astype(BF16)
        kb = k.astype(BF16)
        vaug = jnp.concatenate([v_ref[:, hs], jnp.ones((L, LANES), BF16)], axis=1)
        s = _dot_nt(qb, kb) * w_intra
        ct = ct_ref[h]
        num_aug = _dot(s.astype(BF16), vaug) + w_inter * _dot(qb, ct.astype(BF16))
        num = num_aug[:, :E]
        den = num_aug[:, E:E + 1]
        m_t = bcum_col + mrow
        hout = num / jnp.maximum(jnp.abs(den), jnp.exp(-m_t))
        hn = _rms(hout, ng_ref[:, hs])
        y_ref[:, hs] = (og_ref[:, hs].astype(F32) * hn).astype(y_ref.dtype)

        b_last = jnp.sum(lf_col, axis=0, keepdims=True)
        g_col = b_last + a_col
        m_new = jnp.maximum(b_last + m_prev, jnp.max(g_col, axis=0, keepdims=True))
        w_s = jnp.exp(g_col - m_new)
        decay = jnp.exp(b_last + m_prev - m_new)
        kw = (k * w_s).astype(BF16)
        ct_ref[h] = decay * ct + _dot_tn(kw, vaug)
        m_ref[h] = m_new


def mlstm(proj, gif, gif_t, bcol, brow, conv_w, conv_b, ng, batch, seq):
    L = min(MLSTM_CHUNK, seq)
    E = MLSTM_HEAD_DIM
    W = MLSTM_WIDTH
    nc = seq // L
    H = MLSTM_HEADS
    t = batch * seq
    row = lambda b, c: b * nc + c
    return pl.pallas_call(
        _mlstm_body,
        grid=(batch, nc),
        in_specs=[pl.BlockSpec((L, W), lambda b, c: (row(b, c), 0)),
                  pl.BlockSpec((L, W), lambda b, c: (row(b, c), 1)),
                  pl.BlockSpec((L, W), lambda b, c: (row(b, c), 2)),
                  pl.BlockSpec((L, W), lambda b, c: (row(b, c), COL_O // W)),
                  pl.BlockSpec((L, N_GATE_PAD), lambda b, c: (row(b, c), 0)),
                  pl.BlockSpec((8, L), lambda b, c: (0, row(b, c))),
                  pl.BlockSpec((1, N_GATE_PAD), lambda b, c: (0, 0)),
                  pl.BlockSpec((8, 1), lambda b, c: (0, 0)),
                  pl.BlockSpec((CONV_WIDTH, 2 * W), lambda b, c: (0, 0)),
                  pl.BlockSpec((1, 2 * W), lambda b, c: (0, 0)),
                  pl.BlockSpec((1, W), lambda b, c: (0, 0))],
        out_specs=pl.BlockSpec((L, W), lambda b, c: (row(b, c), 0)),
        out_shape=SDS((t, W), BF16),
        scratch_shapes=[pltpu.VMEM((H, E, E + LANES), F32), pltpu.VMEM((H, 1, 1), F32),
                        pltpu.VMEM((2 * H, 8, E), F32)],
        compiler_params=_params(("parallel", "arbitrary")),
        name="mlstm",
    )(proj, proj, proj, proj, gif, gif_t, bcol, brow, conv_w, conv_b, ng)


def _bias_body(table_ref, o_ref):
    h = pl.program_id(0)
    dil = jnp.where(h < ATTN_HEADS, ATTN_GROUPS[0][1],
                    jnp.where(h < 2 * ATTN_HEADS, ATTN_GROUPS[1][1], ATTN_GROUPS[2][1]))
    shape = (ATTN_BLOCK, 2 * ATTN_BLOCK)
    i = lax.broadcasted_iota(jnp.int32, shape, 0)
    j = lax.broadcasted_iota(jnp.int32, shape, 1)
    rel = ATTN_BLOCK + i - j
    dist = jnp.maximum(rel, 0) * dil
    max_exact = REL_BUCKETS // 2
    nf = jnp.maximum(dist, max_exact).astype(F32)
    large = max_exact + (jnp.log(nf / max_exact) / math.log(REL_MAX_DISTANCE / max_exact)
                         * (REL_BUCKETS - max_exact)).astype(jnp.int32)
    large = jnp.minimum(large, REL_BUCKETS - 1)
    bucket = jnp.where(dist < max_exact, dist, large)
    acc = jnp.zeros(shape, F32)
    for b in range(REL_BUCKETS):
        acc = jnp.where(bucket == b, table_ref[b, h], acc)
    o_ref[0] = jnp.where(rel >= 0, jnp.where(rel <= ATTN_BLOCK, acc, NEG), NEG)


def attn_bias(rel_bias):
    return pl.pallas_call(
        _bias_body,
        grid=(ATTN_HEADS_TOTAL,),
        in_specs=[pl.BlockSpec(memory_space=pltpu.SMEM)],
        out_specs=pl.BlockSpec((1, ATTN_BLOCK, 2 * ATTN_BLOCK), lambda h: (h, 0, 0)),
        out_shape=SDS((ATTN_HEADS_TOTAL, ATTN_BLOCK, 2 * ATTN_BLOCK), F32),
        compiler_params=_params(("arbitrary",)),
        name="attn_bias",
    )(rel_bias)


def _attn_body(q_ref, kp_ref, kc_ref, vp_ref, vc_ref, bias_ref, o_ref, lse_ref):
    n = pl.program_id(2)
    B = ATTN_BLOCK
    E = ATTN_HEAD_DIM
    H = ATTN_HEADS
    NQ = q_ref.shape[2] // B
    scale = E ** -0.5
    hs = [slice(h * E, (h + 1) * E) for h in range(H)]

    def keys(cur_ref, prev_ref, i, sl):
        if i == 0:
            return jnp.concatenate([prev_ref[0, 0, :, sl], cur_ref[0, 0, :B, sl]], axis=0)
        return cur_ref[0, 0, (i - 1) * B:(i + 1) * B, sl]

    s = jnp.concatenate([_dot_nt(q_ref[0, 0, i * B:(i + 1) * B, sl], keys(kc_ref, kp_ref, i, sl))
                         for i in range(NQ) for sl in hs], axis=0)
    bias = bias_ref[...].reshape(H * B, 2 * B)
    key = lax.broadcasted_iota(jnp.int32, (1, 2 * B), 1)
    first = bias + jnp.where(key < B, jnp.where(n > 0, 0.0, NEG), 0.0)
    s = s * scale + jnp.concatenate([first] + [bias] * (NQ - 1), axis=0)
    m = jnp.max(s, axis=1, keepdims=True)
    p = jnp.exp(s - m)
    den = jnp.sum(p, axis=1, keepdims=True)
    inv = 1.0 / den
    pb = p.astype(BF16)
    lse_all = m + jnp.log(den)
    lane = lax.broadcasted_iota(jnp.int32, (B, LANES), 1)
    for i in range(NQ):
        lse = jnp.zeros((B, LANES), F32)
        for h, sl in enumerate(hs):
            rows = slice((i * H + h) * B, (i * H + h + 1) * B)
            o = _dot(pb[rows], keys(vc_ref, vp_ref, i, sl)) * inv[rows]
            o_ref[0, 0, i * B:(i + 1) * B, sl] = o.astype(o_ref.dtype)
            lse = jnp.where(lane == h, lse_all[rows], lse)
        lse_ref[0, 0, i * B:(i + 1) * B, :] = lse


def attn_group(qkv, bias, g):
    batch, r, l, _ = qkv.shape
    W = ATTN_WIDTH
    nq = min(ATTN_Q_BLOCKS, l // ATTN_BLOCK)
    rows = nq * ATTN_BLOCK
    cur = (1, 1, rows, W)
    one = (1, 1, ATTN_BLOCK, W)
    prev = lambda n: jnp.maximum(n * nq - 1, 0)
    return pl.pallas_call(
        _attn_body,
        grid=(batch, r, l // rows),
        in_specs=[pl.BlockSpec(cur, lambda b, c, n: (b, c, n, 0)),
                  pl.BlockSpec(one, lambda b, c, n: (b, c, prev(n), 1)),
                  pl.BlockSpec(cur, lambda b, c, n: (b, c, n, 1)),
                  pl.BlockSpec(one, lambda b, c, n: (b, c, prev(n), 2)),
                  pl.BlockSpec(cur, lambda b, c, n: (b, c, n, 2)),
                  pl.BlockSpec((ATTN_HEADS, ATTN_BLOCK, 2 * ATTN_BLOCK), lambda b, c, n: (g, 0, 0))],
        out_specs=[pl.BlockSpec(cur, lambda b, c, n: (b, c, n, 0)),
                   pl.BlockSpec((1, 1, rows, LANES), lambda b, c, n: (b, c, n, 0))],
        out_shape=[SDS((batch, r, l, W), BF16), SDS((batch, r, l, LANES), F32)],
        compiler_params=_params(("parallel", "parallel", "arbitrary")),
        name=f"attn_g{g}",
    )(qkv, qkv, qkv, qkv, qkv, bias)


def _s5_prep_body(lam_ref, lamc_ref, dt_ref, bt_r_ref, bt_i_ref, ce_r_ref, ce_i_ref, d_ref,
                  t_ref, w_ref, v_ref, coef_ref):
    P = S5_STATE
    lr = lam_ref[0, 0:1, :]
    li = lam_ref[0, 1:2, :]
    dt = jnp.exp(dt_ref[0])

    def apow(e, lr_, li_):
        mag = jnp.exp(lr_ * dt * e)
        ang = li_ * dt * e
        return mag * jnp.cos(ang), mag * jnp.sin(ang)

    one = jnp.ones((1, 1), F32)
    ar, ai = apow(one, lr, li)
    nr = ar - 1.0
    den = lr * lr + li * li
    f_re = (nr * lr + ai * li) / den
    f_im = (ai * lr - nr * li) / den
    bt_r = bt_r_ref[0]
    bt_i = bt_i_ref[0]
    bb_r = f_re * bt_r - f_im * bt_i
    bb_i = f_re * bt_i + f_im * bt_r

    lrc = lamc_ref[0, :, 0:1]
    lic = lamc_ref[0, :, 1:2]
    lag = (lax.broadcasted_iota(jnp.int32, (P, 256), 1) // S5_GROUP).astype(F32)
    adr, adi = apow(lag, lrc, lic)
    ce_r = ce_r_ref[0]
    ce_i = ce_i_ref[0]
    ca_r = ce_r * adr - ce_i * adi
    ca_i = ce_r * adi + ce_i * adr
    hp = lax.Precision.HIGHEST
    ks = (jnp.dot(bb_r[:, :P], ca_r, precision=hp, preferred_element_type=F32)
          - jnp.dot(bb_i[:, :P], ca_i, precision=hp, preferred_element_type=F32))
    si = lax.broadcasted_iota(jnp.int32, (S5_GROUP, 256), 0)
    lj = lax.broadcasted_iota(jnp.int32, (S5_GROUP, 256), 1)
    ks = ks + jnp.where(si == lj, d_ref[0], 0.0)
    for s in range(S5_TOK):
        sh = s * S5_GROUP
        blk = ks if s == 0 else jnp.where(lj >= sh, pltpu.roll(ks, sh, 1), 0.0)
        t_ref[0, sh:sh + S5_GROUP, :] = blk.astype(t_ref.dtype)

    lane128 = lax.broadcasted_iota(jnp.int32, (1, 128), 1)
    for s in range(S5_TOK):
        pr, pi = apow(float(S5_TOK - 1 - s) * one, lr, li)
        p1 = jnp.where(lane128 < P, pr, pi)
        p2 = jnp.where(lane128 < P, -pi, pr)
        w_ref[0, s * S5_GROUP:(s + 1) * S5_GROUP, :] = (bb_r * p1 + bb_i * p2).astype(w_ref.dtype)

    adr1, adi1 = apow(lag + 1.0, lrc, lic)
    v_ref[0, 0:P, :] = (ce_r * adr1 - ce_i * adi1).astype(v_ref.dtype)
    v_ref[0, P:2 * P, :] = (-(ce_r * adi1 + ce_i * adr1)).astype(v_ref.dtype)

    ek = jnp.left_shift(S5_TOK, lax.broadcasted_iota(jnp.int32, (16, 1), 0)).astype(F32)
    cr, ci = apow(ek, lr, li)
    coef_ref[0, 0:16, :] = cr
    coef_ref[0, 16:32, :] = jnp.where(lane128 < P, -ci, ci)


def s5_prep(lam_re, lam_im, log_dt, b_re, b_im, c_re, c_im, d_skip):
    G, P, I = b_re.shape
    lam = jnp.stack([jnp.tile(lam_re, (1, 2)), jnp.tile(lam_im, (1, 2))], axis=1)
    lam = jnp.pad(lam, ((0, 0), (0, 6), (0, 0)))
    lamc = jnp.stack([lam_re, lam_im], axis=2)
    dt = log_dt.reshape(G, 1, 1)
    bt_r = jnp.tile(jnp.swapaxes(b_re, 1, 2), (1, 1, 2))
    bt_i = jnp.tile(jnp.swapaxes(b_im, 1, 2), (1, 1, 2))
    ce_r = jnp.tile(jnp.swapaxes(c_re, 1, 2), (1, 1, S5_TOK))
    ce_i = jnp.tile(jnp.swapaxes(c_im, 1, 2), (1, 1, S5_TOK))
    d = jnp.pad(d_skip, ((0, 0), (0, 256 - I))).reshape(G, 1, 256)
    blk = lambda *s: pl.BlockSpec((1,) + s, lambda g: (g, 0, 0))
    return pl.pallas_call(
        _s5_prep_body,
        grid=(G,),
        in_specs=[blk(8, 128), blk(P, 2), blk(1, 1), blk(16, 128), blk(16, 128), blk(P, 256), blk(P, 256),
                  blk(1, 256)],
        out_specs=[blk(256, 256), blk(256, 128), blk(128, 256), blk(32, 128)],
        out_shape=[SDS((G, 256, 256), BF16), SDS((G, 256, 128), BF16), SDS((G, 128, 256), BF16),
                   SDS((G, 32, 128), F32)],
        compiler_params=_params(("arbitrary",)),
        name="s5_prep",
    )(lam, lamc, dt, bt_r, bt_i, ce_r, ce_i, d)


def _gelu_tanh(x):
    return 0.5 * x * (1.0 + jnp.tanh(math.sqrt(2.0 / math.pi) * (x + 0.044715 * (x * x * x))))


def _chunk_transpose(arrs, chunk):
    a = list(arrs)
    n = len(a)
    d = n // 2
    while d >= 1:
        bit = (chunk & d) != 0
        nxt = list(a)
        for i in range(n):
            if i & d == 0:
                lo, hi = a[i], a[i + d]
                nxt[i] = jnp.where(bit, pltpu.roll(hi, d * S5_GROUP, 1), lo)
                nxt[i + d] = jnp.where(bit, hi, pltpu.roll(lo, LANES - d * S5_GROUP, 1))
        a = nxt
        d //= 2
    return a


def _s5_body(u_ref, t_ref, w_ref, v_ref, coef_ref, z_ref, nat_ref, uf_ref, zs_ref):
    S = u_ref.shape[0]
    R = S // S5_TOK
    P = S5_STATE
    GL = LANES // S5_GROUP
    RC = min(R, 256)
    chunk = lax.broadcasted_iota(jnp.int32, (RC, LANES), 1) // S5_GROUP

    nat_ref[...] = u_ref[...].astype(F32)

    def to_flat(rc, carry):
        r0 = pl.multiple_of(rc * RC, RC)
        for hf in range(S5_TOK // GL):
            arrs = [nat_ref[pl.ds(r0 * S5_TOK + hf * GL + k, RC, stride=S5_TOK), :] for k in range(GL)]
            for gl, a in enumerate(_chunk_transpose(arrs, chunk)):
                uf_ref[gl, pl.ds(r0, RC), hf * LANES:(hf + 1) * LANES] = a.astype(uf_ref.dtype)
        return carry

    lax.fori_loop(0, R // RC, to_flat, 0)

    row = lax.broadcasted_iota(jnp.int32, (R, LANES), 0)

    def shift_down(a, sh):
        if sh % 8 == 0:
            return jnp.concatenate([jnp.zeros((sh, LANES), F32), a[:R - sh]], axis=0)
        return jnp.where(row < sh, 0.0, pltpu.roll(a, sh, 0))

    def group(gl, carry):
        u = uf_ref[gl]
        y = _dot(u, t_ref[gl])
        x = _dot(u, w_ref[gl])
        k = 0
        while (1 << k) < R:
            xs = shift_down(x, 1 << k)
            x = (x + coef_ref[gl, k:k + 1, :] * xs
                 + coef_ref[gl, 16 + k:17 + k, :] * pltpu.roll(xs, P, 1))
            k += 1
        y = y + _dot(shift_down(x, 1).astype(BF16), v_ref[gl])
        zs_ref[gl] = _gelu_tanh(y)
        return carry

    lax.fori_loop(0, GL, group, 0)

    def to_nat(rc, carry):
        r0 = pl.multiple_of(rc * RC, RC)
        for hf in range(S5_TOK // GL):
            arrs = [zs_ref[gl, pl.ds(r0, RC), hf * LANES:(hf + 1) * LANES] for gl in range(GL)]
            for k, a in enumerate(_chunk_transpose(arrs, chunk)):
                nat_ref[pl.ds(r0 * S5_TOK + hf * GL + k, RC, stride=S5_TOK), :] = a
        return carry

    lax.fori_loop(0, R // RC, to_nat, 0)
    z_ref[...] = nat_ref[...].astype(z_ref.dtype)


def s5_scan(proj, tm, wm, vm, coef, batch, seq):
    t = proj.shape[0]
    GL = LANES // S5_GROUP
    r = seq // S5_TOK
    ub = COL_U // LANES
    per_g = lambda *s: pl.BlockSpec((GL,) + s, lambda bi, si: (si, 0, 0))
    return pl.pallas_call(
        _s5_body,
        grid=(batch, S5_WIDTH // LANES),
        in_specs=[pl.BlockSpec((seq, LANES), lambda bi, si: (bi, ub + si)),
                  per_g(256, 256), per_g(256, 128), per_g(128, 256), per_g(32, 128)],
        out_specs=pl.BlockSpec((seq, LANES), lambda bi, si: (bi, si)),
        out_shape=SDS((t, S5_WIDTH), BF16),
        scratch_shapes=[pltpu.VMEM((seq, LANES), F32), pltpu.VMEM((GL, r, 2 * LANES), BF16),
                        pltpu.VMEM((GL, r, 2 * LANES), F32)],
        compiler_params=_params(("parallel", "parallel")),
        name="s5_scan",
    )(proj, tm, wm, vm, coef)


def _glu_body(z_ref, wl_ref, wg_ref, o_ref):
    z = z_ref[...]
    o_ref[...] = (_dot(z, wl_ref[...]) * _sigmoid(_dot(z, wg_ref[...]))).astype(o_ref.dtype)


def glu(z, w, tm, tn):
    t, k = z.shape
    n = w.shape[1] // 2
    nj = n // tn
    return pl.pallas_call(
        _glu_body,
        grid=(t // tm, nj),
        in_specs=[pl.BlockSpec((tm, k), lambda i, j: (i, 0)),
                  pl.BlockSpec((k, tn), lambda i, j: (0, j)),
                  pl.BlockSpec((k, tn), lambda i, j: (0, nj + j))],
        out_specs=pl.BlockSpec((tm, tn), lambda i, j: (i, j)),
        out_shape=SDS((t, n), BF16),
        compiler_params=_params(("parallel", "arbitrary")),
        name="glu",
    )(z, w, w)


def _mix_body(ya_ref, o0_ref, o1_ref, o2_ref, l0_ref, l1_ref, l2_ref, yc_ref,
              wa_ref, wb_ref, wc_ref, ga_ref, gb_ref, gc_ref, out_ref, yb_ref, wt_ref, acc_ref):
    tm = ya_ref.shape[0]
    E = ATTN_HEAD_DIM

    @pl.when(pl.program_id(1) == 0)
    def _():
        o_refs = (o0_ref, o1_ref, o2_ref)
        l_refs = (l0_ref, l1_ref, l2_ref)
        for g, (_, r) in enumerate(ATTN_GROUPS):
            for c in range(r):
                wt_ref[g, pl.ds(c, tm // r, stride=r), :] = l_refs[g][0, c]
        l0, l1, l2 = wt_ref[0], wt_ref[1], wt_ref[2]
        m = jnp.maximum(jnp.maximum(l0, l1), l2)
        e0, e1, e2 = jnp.exp(l0 - m), jnp.exp(l1 - m), jnp.exp(l2 - m)
        inv = 1.0 / (e0 + e1 + e2)
        wt_ref[0] = e0 * inv
        wt_ref[1] = e1 * inv
        wt_ref[2] = e2 * inv
        order = sorted(range(len(ATTN_GROUPS)), key=lambda g: -ATTN_GROUPS[g][1])
        for pos, g in enumerate(order[:-1]):
            r = ATTN_GROUPS[g][1]
            for c in range(r):
                rows = pl.ds(c, tm // r, stride=r)
                w = wt_ref[g, rows, :]
                for h in range(ATTN_HEADS):
                    part = w[:, h:h + 1] * o_refs[g][0, c, :, h * E:(h + 1) * E].astype(F32)
                    if pos == 0:
                        acc_ref[h, rows, :] = part
                    else:
                        acc_ref[h, rows, :] += part
        g = order[-1]
        assert ATTN_GROUPS[g][1] == 1
        w = wt_ref[g]
        for h in range(ATTN_HEADS):
            sl = slice(h * E, (h + 1) * E)
            yb = acc_ref[h] + w[:, h:h + 1] * o_refs[g][0, 0, :, sl].astype(F32)
            yb_ref[:, sl] = yb.astype(yb_ref.dtype)

    mix = (_sigmoid(ga_ref[...].astype(F32)) * _dot(ya_ref[...], wa_ref[...])
           + _sigmoid(gb_ref[...].astype(F32)) * _dot(yb_ref[...], wb_ref[...])
           + _sigmoid(gc_ref[...].astype(F32)) * _dot(yc_ref[...], wc_ref[...]))
    out_ref[...] = mix.astype(out_ref.dtype)


def gated_mix(ya, outs, lses, yc, wa, wb, wc, proj, seq, tm, tn):
    t, kw = ya.shape
    d = wa.shape[1]
    go = COL_G // tn
    nbt = seq // tm
    row = lambda w: pl.BlockSpec((tm, w), lambda i, j: (i, 0))
    grp = lambda r, w: pl.BlockSpec((1, r, tm // r, w), lambda i, j: (i // nbt, 0, i % nbt, 0))
    wsp = pl.BlockSpec((kw, tn), lambda i, j: (0, j))
    gate = lambda o: pl.BlockSpec((tm, tn), lambda i, j: (i, go + o * (d // tn) + j))
    dils = [r for _, r in ATTN_GROUPS]
    return pl.pallas_call(
        _mix_body,
        grid=(t // tm, d // tn),
        in_specs=[row(kw)] + [grp(r, kw) for r in dils] + [grp(r, LANES) for r in dils] + [row(kw)]
        + [wsp, wsp, wsp, gate(0), gate(1), gate(2)],
        out_specs=pl.BlockSpec((tm, tn), lambda i, j: (i, j)),
        out_shape=SDS((t, d), BF16),
        scratch_shapes=[pltpu.VMEM((tm, kw), BF16), pltpu.VMEM((len(dils), tm, LANES), F32),
                        pltpu.VMEM((ATTN_HEADS, tm, LANES), F32)],
        compiler_params=_params(("parallel", "arbitrary")),
        name="gated_mix",
    )(ya, *outs, *lses, yc, wa, wb, wc, proj, proj, proj)


def _outproj_body(x_ref, m_ref, w_ref, o_ref):
    o_ref[...] = x_ref[...] + _dot(m_ref[...], w_ref[...])


def out_proj(x2d, mix, w, tm, tn):
    t, d = x2d.shape
    return pl.pallas_call(
        _outproj_body,
        grid=(t // tm, d // tn),
        in_specs=[pl.BlockSpec((tm, tn), lambda i, j: (i, j)),
                  pl.BlockSpec((tm, d), lambda i, j: (i, 0)),
                  pl.BlockSpec((d, tn), lambda i, j: (0, j))],
        out_specs=pl.BlockSpec((tm, tn), lambda i, j: (i, j)),
        out_shape=SDS((t, d), F32),
        compiler_params=_params(("parallel", "arbitrary")),
        name="out_proj",
    )(x2d, mix, w)


def _ffn_body(x_ref, g_ref, w1_ref, w2_ref, o_ref, xn_ref):
    @pl.when(pl.program_id(1) == 0)
    def _():
        x = x_ref[...]
        xn_ref[...] = _rms(x, g_ref[...]).astype(BF16)
        o_ref[...] = x

    hid = jnp.maximum(_dot(xn_ref[...], w1_ref[...]), 0.0)
    o_ref[...] += _dot((hid * hid).astype(BF16), w2_ref[...])


def ffn(x2d, g, w1, w2, tm, th):
    t, d = x2d.shape
    hdim = w1.shape[1]
    return pl.pallas_call(
        _ffn_body,
        grid=(t // tm, hdim // th),
        in_specs=[pl.BlockSpec((tm, d), lambda i, j: (i, 0)),
                  pl.BlockSpec((1, d), lambda i, j: (0, 0)),
                  pl.BlockSpec((d, th), lambda i, j: (0, j)),
                  pl.BlockSpec((th, d), lambda i, j: (j, 0))],
        out_specs=pl.BlockSpec((tm, d), lambda i, j: (i, 0)),
        out_shape=SDS((t, d), F32),
        scratch_shapes=[pltpu.VMEM((tm, d), BF16)],
        compiler_params=_params(("parallel", "arbitrary")),
        name="ffn",
    )(x2d, g, w1, w2)


def _norm_body(x_ref, g_ref, o_ref):
    o_ref[...] = _rms(x_ref[...], g_ref[...])


def final_norm(x2d, g, tm):
    t, d = x2d.shape
    return pl.pallas_call(
        _norm_body,
        grid=(t // tm,),
        in_specs=[pl.BlockSpec((tm, d), lambda i: (i, 0)), pl.BlockSpec((1, d), lambda i: (0, 0))],
        out_specs=pl.BlockSpec((tm, d), lambda i: (i, 0)),
        out_shape=SDS((t, d), F32),
        compiler_params=_params(("parallel",)),
        name="final_norm",
    )(x2d, g)


def _tile(n, want):
    t = min(n, want)
    assert n % t == 0, (n, want)
    return t


def _layer(x2d, bias, batch, seq, p):
    t, d = x2d.shape
    tm = _tile(seq, 1024)
    n_gate0 = 4 * MLSTM_WIDTH
    n_att0 = n_gate0 + 2 * MLSTM_HEADS
    n_att = 3 * ATTN_HEADS_TOTAL * ATTN_HEAD_DIM
    n_groups = len(ATTN_GROUPS)
    w_in = p["w_in"]
    w_att = w_in[:, n_att0:n_att0 + n_att].reshape(d, 3, n_groups, ATTN_WIDTH)
    w_att = w_att.transpose(0, 2, 1, 3).reshape(d, n_att)
    n_u0 = n_att0 + n_att
    w_main = jnp.concatenate([w_in[:, :COL_U], w_in[:, n_u0:n_u0 + S5_WIDTH], w_in[:, COL_U:n_gate0],
                              w_in[:, n_u0 + S5_WIDTH:], w_att], axis=1).astype(BF16)
    w_if = jnp.pad(w_in[:, n_gate0:n_att0], ((0, 0), (0, N_GATE_PAD - 2 * MLSTM_HEADS))).astype(BF16)
    w_ift = w_if[:, :16].T
    proj, gif, gif_t, *qkvs = in_proj(x2d, p["norm1_g"].reshape(1, d), w_main, w_if, w_ift, batch, seq, tm)

    bif = jnp.concatenate([p["b_igate"], p["b_fgate"]]).astype(F32)
    bcol = jnp.pad(bif, (0, N_GATE_PAD - bif.shape[0])).reshape(1, N_GATE_PAD)
    ya = mlstm(proj, gif, gif_t, bcol, bif.reshape(8, 1), p["conv_w"], p["conv_b"].reshape(1, -1),
               p["mh_norm_g"].reshape(1, -1), batch, seq)

    outs, lses = zip(*[attn_group(qkvs[g], bias, g) for g in range(n_groups)])

    tmat, wmat, vmat, coef = s5_prep(p["lam_re"], p["lam_im"], p["log_dt"], p["b_re"], p["b_im"],
                                     p["c_re"], p["c_im"], p["d_skip"])
    z = s5_scan(proj, tmat, wmat, vmat, coef, batch, seq)
    yc = glu(z, p["w_glu"].astype(BF16), tm, 512)

    mix = gated_mix(ya, outs, lses, yc, p["w_br_a"].astype(BF16), p["w_br_b"].astype(BF16),
                    p["w_br_c"].astype(BF16), proj, seq, tm, 512)
    x2d = out_proj(x2d, mix, p["w_out"].astype(BF16), tm, 512)
    return ffn(x2d, p["norm2_g"].reshape(1, d), p["w_ff1"].astype(BF16), p["w_ff2"].astype(BF16),
               _tile(t, 512), 1024)


_PER_LAYER = ("norm1_g", "w_in", "conv_w", "conv_b", "b_igate", "b_fgate", "mh_norm_g", "lam_re", "lam_im",
              "log_dt", "b_re", "b_im", "c_re", "c_im", "d_skip", "w_glu", "w_br_a", "w_br_b", "w_br_c",
              "w_out", "norm2_g", "w_ff1", "w_ff2")


def kernel(x, norm1_g, w_in, conv_w, conv_b, b_igate, b_fgate, mh_norm_g, rel_bias, lam_re, lam_im, log_dt,
           b_re, b_im, c_re, c_im, d_skip, w_glu, w_br_a, w_br_b, w_br_c, w_out, norm2_g, w_ff1, w_ff2,
           final_g):
    stacked = dict(norm1_g=norm1_g, w_in=w_in, conv_w=conv_w, conv_b=conv_b, b_igate=b_igate,
                   b_fgate=b_fgate, mh_norm_g=mh_norm_g, lam_re=lam_re, lam_im=lam_im, log_dt=log_dt,
                   b_re=b_re, b_im=b_im, c_re=c_re, c_im=c_im, d_skip=d_skip, w_glu=w_glu, w_br_a=w_br_a,
                   w_br_b=w_br_b, w_br_c=w_br_c, w_out=w_out, norm2_g=norm2_g, w_ff1=w_ff1, w_ff2=w_ff2)
    batch, seq, d = x.shape
    x2d = x.astype(F32).reshape(batch * seq, d)
    bias = attn_bias(rel_bias.astype(F32))
    for l in range(w_in.shape[0]):
        x2d = _layer(x2d, bias, batch, seq, {k: stacked[k][l] for k in _PER_LAYER})
    out = final_norm(x2d, final_g.reshape(1, d), _tile(batch * seq, 1024))
    return out.reshape(batch, seq, d).astype(x.dtype)
```

```python
import functools
import math

import jax
import jax.numpy as jnp
from jax import lax
from jax.experimental import pallas as pl
from jax.experimental.pallas import tpu as pltpu

F32 = jnp.float32
BF16 = jnp.bfloat16
SDS = jax.ShapeDtypeStruct

NORM_EPS = 1e-6
NEG = -1e30

MLSTM_HEADS = 4
MLSTM_HEAD_DIM = 256
MLSTM_WIDTH = MLSTM_HEADS * MLSTM_HEAD_DIM
CONV_WIDTH = 4
ATTN_GROUPS = ((128, 1), (512, 4), (2048, 16))
ATTN_HEADS = 8
ATTN_HEAD_DIM = 128
ATTN_WIDTH = ATTN_HEADS * ATTN_HEAD_DIM
ATTN_HEADS_TOTAL = len(ATTN_GROUPS) * ATTN_HEADS
ATTN_BLOCK = 128
ATTN_Q_BLOCKS = 4
REL_BUCKETS = 32
REL_MAX_DISTANCE = 2048
S5_WIDTH = 1024
S5_GROUP = 16
S5_GROUPS = S5_WIDTH // S5_GROUP
S5_STATE = 64
S5_TOK = 16
N_GATE_PAD = 128

COL_U = 3 * MLSTM_WIDTH
COL_O = COL_U + S5_WIDTH
COL_G = COL_O + MLSTM_WIDTH
D_MODEL = 2048
N_MAIN = COL_G + 3 * D_MODEL
LANES = 128

MLSTM_CHUNK = 256
VMEM_LIMIT = 56 * 2**20


def _params(sem):
    return pltpu.CompilerParams(dimension_semantics=sem, vmem_limit_bytes=VMEM_LIMIT)


def _sigmoid(x):
    return 1.0 / (1.0 + jnp.exp(-x))


def _log_sigmoid(x):
    return jnp.minimum(x, 0.0) - jnp.log(1.0 + jnp.exp(-jnp.abs(x)))


def _rms(x, g):
    ms = jnp.mean(x * x, axis=-1, keepdims=True)
    return x * lax.rsqrt(ms + NORM_EPS) * g


def _dot(a, b):
    return jnp.dot(a, b, preferred_element_type=F32)


def _dot_nt(a, b):
    return lax.dot_general(a, b, (((1,), (1,)), ((), ())), preferred_element_type=F32)


def _dot_tn(a, b):
    return lax.dot_general(a, b, (((0,), (0,)), ((), ())), preferred_element_type=F32)


def _inproj_body(ns, nm, x_ref, g_ref, w_ref, wif_ref, o_ref, oif_ref, oift_ref, a0_ref, a1_ref, a2_ref,
                 xn_ref, acc_ref, acc2_ref):
    j = pl.program_id(1)
    tm = x_ref.shape[0]
    tn = w_ref.shape[1]

    @pl.when(j == 0)
    def _():
        xn = _rms(x_ref[...], g_ref[...]).astype(BF16)
        xn_ref[...] = xn
        gates = _dot(xn, wif_ref[...])
        oif_ref[...] = gates
        oift_ref[...] = jnp.transpose(gates)[:8]

    @pl.when(jnp.logical_and(j < nm, j != ns))
    def _():
        o_ref[...] = _dot(xn_ref[...], w_ref[...]).astype(o_ref.dtype)

    @pl.when(j == ns)
    def _():
        o_ref[...] = _sigmoid(_dot(xn_ref[...], w_ref[...])).astype(o_ref.dtype)

    for g, a_ref in enumerate((a0_ref, a1_ref, a2_ref)):
        r = ATTN_GROUPS[g][1]
        lo = nm + 3 * g

        @pl.when(jnp.logical_and(j >= lo, j < lo + 3))
        def _(a_ref=a_ref, r=r):
            res = _dot(xn_ref[...], w_ref[...])
            if r == 1:
                a_ref[0, 0] = res.astype(a_ref.dtype)
                return
            ns_ = acc_ref.shape[0]
            for base in range(0, tn // LANES, ns_):
                lanes = [slice((base + s) * LANES, (base + s + 1) * LANES) for s in range(ns_)]
                for s in range(ns_):
                    acc_ref[s] = res[:, lanes[s]]
                if r <= 4:
                    for c in range(r):
                        for s in range(ns_):
                            a_ref[0, c, :, lanes[s]] = acc_ref[s, pl.ds(c, tm // r, stride=r), :].astype(a_ref.dtype)
                    continue
                q, r2 = 4, r // 4
                for s in range(ns_):
                    for c0 in range(q):
                        acc2_ref[s, c0 * (tm // q):(c0 + 1) * (tm // q), :] = (
                            acc_ref[s, pl.ds(c0, tm // q, stride=q), :])
                for c0 in range(q):
                    for c1 in range(r2):
                        for s in range(ns_):
                            rows = pl.ds(c0 * (tm // q) + c1, tm // r, stride=r2)
                            a_ref[0, c1 * q + c0, :, lanes[s]] = acc2_ref[s, rows, :].astype(a_ref.dtype)


def in_proj(x2d, g, w, wif, batch, seq, tm):
    t, d = x2d.shape
    tn = ATTN_WIDTH
    nm = N_MAIN // tn
    nbt = seq // tm
    n_groups = len(ATTN_GROUPS)

    def a_spec(gi):
        r = ATTN_GROUPS[gi][1]
        return pl.BlockSpec((1, r, tm // r, tn),
                            lambda i, j: (i // nbt, 0, i % nbt, jnp.clip(j - nm - 3 * gi, 0, 2)))

    return pl.pallas_call(
        functools.partial(_inproj_body, COL_O // tn, nm),
        grid=(t // tm, nm + 3 * n_groups),
        in_specs=[pl.BlockSpec((tm, d), lambda i, j: (i, 0)),
                  pl.BlockSpec((1, d), lambda i, j: (0, 0)),
                  pl.BlockSpec((d, tn), lambda i, j: (0, j)),
                  pl.BlockSpec((d, N_GATE_PAD), lambda i, j: (0, 0))],
        out_specs=[pl.BlockSpec((tm, tn), lambda i, j: (i, jnp.minimum(j, nm - 1))),
                   pl.BlockSpec((tm, N_GATE_PAD), lambda i, j: (i, 0)),
                   pl.BlockSpec((8, tm), lambda i, j: (0, i))] + [a_spec(gi) for gi in range(n_groups)],
        out_shape=[SDS((t, N_MAIN), BF16), SDS((t, N_GATE_PAD), F32), SDS((8, t), F32)]
        + [SDS((batch, r, seq // r, 3 * tn), BF16) for _, r in ATTN_GROUPS],
        scratch_shapes=[pltpu.VMEM((tm, d), BF16), pltpu.VMEM((tn // LANES // 2, tm, LANES), F32),
                        pltpu.VMEM((tn // LANES // 2, tm, LANES), F32)],
        compiler_params=_params(("parallel", "arbitrary")),
        name="in_proj",
    )(x2d, g, w, wif)


def _mlstm_body(q_ref, k_ref, v_ref, og_ref, gcol_ref, grow_ref, bcol_ref, brow_ref,
                cw_ref, cb_ref, ng_ref, y_ref, ct_ref, m_ref, tail_ref):
    c = pl.program_id(1)
    L = q_ref.shape[0]
    E = MLSTM_HEAD_DIM
    H = MLSTM_HEADS

    @pl.when(c == 0)
    def _():
        ct_ref[...] = jnp.zeros_like(ct_ref)
        m_ref[...] = jnp.zeros_like(m_ref)
        tail_ref[...] = jnp.zeros_like(tail_ref)

    row8 = lax.broadcasted_iota(jnp.int32, (8, E), 0)
    tt = lax.broadcasted_iota(jnp.int32, (L, L), 0)
    ss = lax.broadcasted_iota(jnp.int32, (L, L), 1)
    causal = ss <= tt
    gc = gcol_ref[...] + bcol_ref[...]
    gr = grow_ref[...] + brow_ref[...]

    shifts = [jnp.where(tt - ss == d, 1.0, 0.0).astype(BF16) for d in range(1, CONV_WIDTH)]

    def conv_silu(x_ref, h, slot):
        xb = x_ref[:, h * E:(h + 1) * E]
        x = xb.astype(F32)
        tail = tail_ref[slot]
        w = cw_ref[:, slot * E:(slot + 1) * E]
        acc = cb_ref[:, slot * E:(slot + 1) * E] + w[CONV_WIDTH - 1:CONV_WIDTH] * x
        head = jnp.zeros((8, E), F32)
        for d in range(1, CONV_WIDTH):
            wd = w[CONV_WIDTH - 1 - d:CONV_WIDTH - d]
            acc = acc + wd * _dot(shifts[d - 1], xb)
            head = head + wd * jnp.where(row8 < d, pltpu.roll(tail, d, 0), 0.0)
        acc = jnp.concatenate([acc[:8] + head, acc[8:]], axis=0)
        tail_ref[slot] = x[L - 8:]
        return acc * _sigmoid(acc)

    for h in range(H):
        hs = slice(h * E, (h + 1) * E)
        q = conv_silu(q_ref, h, h)
        k = conv_silu(k_ref, h, H + h) * (E ** -0.5)

        ig_col = gc[:, h:h + 1]
        lf_col = _log_sigmoid(gc[:, H + h:H + h + 1])
        ig_row = gr[h:h + 1, :]
        lf_row = _log_sigmoid(gr[H + h:H + h + 1, :])
        bcum_col = jnp.sum(jnp.where(causal, lf_row, 0.0), axis=1, keepdims=True)
        bcum_row = jnp.sum(jnp.where(tt <= ss, lf_col, 0.0), axis=0, keepdims=True)
        a_row = ig_row - bcum_row
        a_col = ig_col - bcum_col

        m_prev = m_ref[h]
        amat = jnp.where(causal, a_row, NEG)
        mrow = jnp.maximum(m_prev, jnp.max(amat, axis=1, keepdims=True))
        w_intra = jnp.exp(amat - mrow)
        w_inter = jnp.exp(m_prev - mrow)

        qb = q.astype(BF16)
        kb = k.astype(BF16)
        vaug = jnp.concatenate([v_ref[:, hs], jnp.ones((L, LANES), BF16)], axis=1)
        s = _dot_nt(qb, kb) * w_intra
        ct = ct_ref[h]
        num_aug = _dot(s.astype(BF16), vaug) + w_inter * _dot(qb, ct.astype(BF16))
        num = num_aug[:, :E]
        den = num_aug[:, E:E + 1]
        m_t = bcum_col + mrow
        hout = num / jnp.maximum(jnp.abs(den), jnp.exp(-m_t))
        hn = _rms(hout, ng_ref[:, hs])
        y_ref[:, hs] = (og_ref[:, hs].astype(F32) * hn).astype(y_ref.dtype)

        b_last = jnp.sum(lf_col, axis=0, keepdims=True)
        g_col = b_last + a_col
        m_new = jnp.maximum(b_last + m_prev, jnp.max(g_col, axis=0, keepdims=True))
        w_s = jnp.exp(g_col - m_new)
        decay = jnp.exp(b_last + m_prev - m_new)
        kw = (k * w_s).astype(BF16)
        ct_ref[h] = decay * ct + _dot_tn(kw, vaug)
        m_ref[h] = m_new


def mlstm(proj, gif, gif_t, bcol, brow, conv_w, conv_b, ng, batch, seq):
    L = min(MLSTM_CHUNK, seq)
    E = MLSTM_HEAD_DIM
    W = MLSTM_WIDTH
    nc = seq // L
    H = MLSTM_HEADS
    t = batch * seq
    row = lambda b, c: b * nc + c
    return pl.pallas_call(
        _mlstm_body,
        grid=(batch, nc),
        in_specs=[pl.BlockSpec((L, W), lambda b, c: (row(b, c), 0)),
                  pl.BlockSpec((L, W), lambda b, c: (row(b, c), 1)),
                  pl.BlockSpec((L, W), lambda b, c: (row(b, c), 2)),
                  pl.BlockSpec((L, W), lambda b, c: (row(b, c), COL_O // W)),
                  pl.BlockSpec((L, N_GATE_PAD), lambda b, c: (row(b, c), 0)),
                  pl.BlockSpec((8, L), lambda b, c: (0, row(b, c))),
                  pl.BlockSpec((1, N_GATE_PAD), lambda b, c: (0, 0)),
                  pl.BlockSpec((8, 1), lambda b, c: (0, 0)),
                  pl.BlockSpec((CONV_WIDTH, 2 * W), lambda b, c: (0, 0)),
                  pl.BlockSpec((1, 2 * W), lambda b, c: (0, 0)),
                  pl.BlockSpec((1, W), lambda b, c: (0, 0))],
        out_specs=pl.BlockSpec((L, W), lambda b, c: (row(b, c), 0)),
        out_shape=SDS((t, W), BF16),
        scratch_shapes=[pltpu.VMEM((H, E, E + LANES), F32), pltpu.VMEM((H, 1, 1), F32),
                        pltpu.VMEM((2 * H, 8, E), F32)],
        compiler_params=_params(("parallel", "arbitrary")),
        name="mlstm",
    )(proj, proj, proj, proj, gif, gif_t, bcol, brow, conv_w, conv_b, ng)


def _bias_body(table_ref, o_ref):
    h = pl.program_id(0)
    dil = jnp.where(h < ATTN_HEADS, ATTN_GROUPS[0][1],
                    jnp.where(h < 2 * ATTN_HEADS, ATTN_GROUPS[1][1], ATTN_GROUPS[2][1]))
    shape = (ATTN_BLOCK, 2 * ATTN_BLOCK)
    i = lax.broadcasted_iota(jnp.int32, shape, 0)
    j = lax.broadcasted_iota(jnp.int32, shape, 1)
    rel = ATTN_BLOCK + i - j
    dist = jnp.maximum(rel, 0) * dil
    max_exact = REL_BUCKETS // 2
    nf = jnp.maximum(dist, max_exact).astype(F32)
    large = max_exact + (jnp.log(nf / max_exact) / math.log(REL_MAX_DISTANCE / max_exact)
                         * (REL_BUCKETS - max_exact)).astype(jnp.int32)
    large = jnp.minimum(large, REL_BUCKETS - 1)
    bucket = jnp.where(dist < max_exact, dist, large)
    acc = jnp.zeros(shape, F32)
    for b in range(REL_BUCKETS):
        acc = jnp.where(bucket == b, table_ref[b, h], acc)
    o_ref[0] = jnp.where(rel >= 0, jnp.where(rel <= ATTN_BLOCK, acc, NEG), NEG)


def attn_bias(rel_bias):
    return pl.pallas_call(
        _bias_body,
        grid=(ATTN_HEADS_TOTAL,),
        in_specs=[pl.BlockSpec(memory_space=pltpu.SMEM)],
        out_specs=pl.BlockSpec((1, ATTN_BLOCK, 2 * ATTN_BLOCK), lambda h: (h, 0, 0)),
        out_shape=SDS((ATTN_HEADS_TOTAL, ATTN_BLOCK, 2 * ATTN_BLOCK), F32),
        compiler_params=_params(("arbitrary",)),
        name="attn_bias",
    )(rel_bias)


def _attn_body(q_ref, kp_ref, kc_ref, vp_ref, vc_ref, bias_ref, o_ref, lse_ref):
    n = pl.program_id(2)
    B = ATTN_BLOCK
    E = ATTN_HEAD_DIM
    H = ATTN_HEADS
    NQ = q_ref.shape[2] // B
    scale = E ** -0.5
    hs = [slice(h * E, (h + 1) * E) for h in range(H)]

    def keys(cur_ref, prev_ref, i, sl):
        if i == 0:
            return jnp.concatenate([prev_ref[0, 0, :, sl], cur_ref[0, 0, :B, sl]], axis=0)
        return cur_ref[0, 0, (i - 1) * B:(i + 1) * B, sl]

    s = jnp.concatenate([_dot_nt(q_ref[0, 0, i * B:(i + 1) * B, sl], keys(kc_ref, kp_ref, i, sl))
                         for i in range(NQ) for sl in hs], axis=0)
    bias = bias_ref[...].reshape(H * B, 2 * B)
    key = lax.broadcasted_iota(jnp.int32, (1, 2 * B), 1)
    first = bias + jnp.where(key < B, jnp.where(n > 0, 0.0, NEG), 0.0)
    s = s * scale + jnp.concatenate([first] + [bias] * (NQ - 1), axis=0)
    m = jnp.max(s, axis=1, keepdims=True)
    p = jnp.exp(s - m)
    den = jnp.sum(p, axis=1, keepdims=True)
    inv = 1.0 / den
    pb = p.astype(BF16)
    lse_all = m + jnp.log(den)
    lane = lax.broadcasted_iota(jnp.int32, (B, LANES), 1)
    for i in range(NQ):
        lse = jnp.zeros((B, LANES), F32)
        for h, sl in enumerate(hs):
            rows = slice((i * H + h) * B, (i * H + h + 1) * B)
            o = _dot(pb[rows], keys(vc_ref, vp_ref, i, sl)) * inv[rows]
            o_ref[0, 0, i * B:(i + 1) * B, sl] = o.astype(o_ref.dtype)
            lse = jnp.where(lane == h, lse_all[rows], lse)
        lse_ref[0, 0, i * B:(i + 1) * B, :] = lse


def attn_group(qkv, bias, g):
    batch, r, l, _ = qkv.shape
    W = ATTN_WIDTH
    nq = min(ATTN_Q_BLOCKS, l // ATTN_BLOCK)
    rows = nq * ATTN_BLOCK
    cur = (1, 1, rows, W)
    one = (1, 1, ATTN_BLOCK, W)
    prev = lambda n: jnp.maximum(n * nq - 1, 0)
    return pl.pallas_call(
        _attn_body,
        grid=(batch, r, l // rows),
        in_specs=[pl.BlockSpec(cur, lambda b, c, n: (b, c, n, 0)),
                  pl.BlockSpec(one, lambda b, c, n: (b, c, prev(n), 1)),
                  pl.BlockSpec(cur, lambda b, c, n: (b, c, n, 1)),
                  pl.BlockSpec(one, lambda b, c, n: (b, c, prev(n), 2)),
                  pl.BlockSpec(cur, lambda b, c, n: (b, c, n, 2)),
                  pl.BlockSpec((ATTN_HEADS, ATTN_BLOCK, 2 * ATTN_BLOCK), lambda b, c, n: (g, 0, 0))],
        out_specs=[pl.BlockSpec(cur, lambda b, c, n: (b, c, n, 0)),
                   pl.BlockSpec((1, 1, rows, LANES), lambda b, c, n: (b, c, n, 0))],
        out_shape=[SDS((batch, r, l, W), BF16), SDS((batch, r, l, LANES), F32)],
        compiler_params=_params(("parallel", "parallel", "arbitrary")),
        name=f"attn_g{g}",
    )(qkv, qkv, qkv, qkv, qkv, bias)


def _s5_prep_body(lam_ref, lamc_ref, dt_ref, bt_r_ref, bt_i_ref, ce_r_ref, ce_i_ref, d_ref,
                  t_ref, w_ref, v_ref, coef_ref):
    P = S5_STATE
    lr = lam_ref[0, 0:1, :]
    li = lam_ref[0, 1:2, :]
    dt = jnp.exp(dt_ref[0])

    def apow(e, lr_, li_):
        mag = jnp.exp(lr_ * dt * e)
        ang = li_ * dt * e
        return mag * jnp.cos(ang), mag * jnp.sin(ang)

    one = jnp.ones((1, 1), F32)
    ar, ai = apow(one, lr, li)
    nr = ar - 1.0
    den = lr * lr + li * li
    f_re = (nr * lr + ai * li) / den
    f_im = (ai * lr - nr * li) / den
    bt_r = bt_r_ref[0]
    bt_i = bt_i_ref[0]
    bb_r = f_re * bt_r - f_im * bt_i
    bb_i = f_re * bt_i + f_im * bt_r

    lrc = lamc_ref[0, :, 0:1]
    lic = lamc_ref[0, :, 1:2]
    lag = (lax.broadcasted_iota(jnp.int32, (P, 256), 1) // S5_GROUP).astype(F32)
    adr, adi = apow(lag, lrc, lic)
    ce_r = ce_r_ref[0]
    ce_i = ce_i_ref[0]
    ca_r = ce_r * adr - ce_i * adi
    ca_i = ce_r * adi + ce_i * adr
    hp = lax.Precision.HIGHEST
    ks = (jnp.dot(bb_r[:, :P], ca_r, precision=hp, preferred_element_type=F32)
          - jnp.dot(bb_i[:, :P], ca_i, precision=hp, preferred_element_type=F32))
    si = lax.broadcasted_iota(jnp.int32, (S5_GROUP, 256), 0)
    lj = lax.broadcasted_iota(jnp.int32, (S5_GROUP, 256), 1)
    ks = ks + jnp.where(si == lj, d_ref[0], 0.0)
    for s in range(S5_TOK):
        sh = s * S5_GROUP
        blk = ks if s == 0 else jnp.where(lj >= sh, pltpu.roll(ks, sh, 1), 0.0)
        t_ref[0, sh:sh + S5_GROUP, :] = blk.astype(t_ref.dtype)

    lane128 = lax.broadcasted_iota(jnp.int32, (1, 128), 1)
    for s in range(S5_TOK):
        pr, pi = apow(float(S5_TOK - 1 - s) * one, lr, li)
        p1 = jnp.where(lane128 < P, pr, pi)
        p2 = jnp.where(lane128 < P, -pi, pr)
        w_ref[0, s * S5_GROUP:(s + 1) * S5_GROUP, :] = (bb_r * p1 + bb_i * p2).astype(w_ref.dtype)

    adr1, adi1 = apow(lag + 1.0, lrc, lic)
    v_ref[0, 0:P, :] = (ce_r * adr1 - ce_i * adi1).astype(v_ref.dtype)
    v_ref[0, P:2 * P, :] = (-(ce_r * adi1 + ce_i * adr1)).astype(v_ref.dtype)

    ek = jnp.left_shift(S5_TOK, lax.broadcasted_iota(jnp.int32, (16, 1), 0)).astype(F32)
    cr, ci = apow(ek, lr, li)
    coef_ref[0, 0:16, :] = cr
    coef_ref[0, 16:32, :] = jnp.where(lane128 < P, -ci, ci)


def s5_prep(lam_re, lam_im, log_dt, b_re, b_im, c_re, c_im, d_skip):
    G, P, I = b_re.shape
    lam = jnp.stack([jnp.tile(lam_re, (1, 2)), jnp.tile(lam_im, (1, 2))], axis=1)
    lam = jnp.pad(lam, ((0, 0), (0, 6), (0, 0)))
    lamc = jnp.stack([lam_re, lam_im], axis=2)
    dt = log_dt.reshape(G, 1, 1)
    bt_r = jnp.tile(jnp.swapaxes(b_re, 1, 2), (1, 1, 2))
    bt_i = jnp.tile(jnp.swapaxes(b_im, 1, 2), (1, 1, 2))
    ce_r = jnp.tile(jnp.swapaxes(c_re, 1, 2), (1, 1, S5_TOK))
    ce_i = jnp.tile(jnp.swapaxes(c_im, 1, 2), (1, 1, S5_TOK))
    d = jnp.pad(d_skip, ((0, 0), (0, 256 - I))).reshape(G, 1, 256)
    blk = lambda *s: pl.BlockSpec((1,) + s, lambda g: (g, 0, 0))
    return pl.pallas_call(
        _s5_prep_body,
        grid=(G,),
        in_specs=[blk(8, 128), blk(P, 2), blk(1, 1), blk(16, 128), blk(16, 128), blk(P, 256), blk(P, 256),
                  blk(1, 256)],
        out_specs=[blk(256, 256), blk(256, 128), blk(128, 256), blk(32, 128)],
        out_shape=[SDS((G, 256, 256), BF16), SDS((G, 256, 128), BF16), SDS((G, 128, 256), BF16),
                   SDS((G, 32, 128), F32)],
        compiler_params=_params(("arbitrary",)),
        name="s5_prep",
    )(lam, lamc, dt, bt_r, bt_i, ce_r, ce_i, d)


def _gelu_tanh(x):
    return 0.5 * x * (1.0 + jnp.tanh(math.sqrt(2.0 / math.pi) * (x + 0.044715 * (x * x * x))))


def _chunk_transpose(arrs, chunk):
    a = list(arrs)
    n = len(a)
    d = n // 2
    while d >= 1:
        bit = (chunk & d) != 0
        nxt = list(a)
        for i in range(n):
            if i & d == 0:
                lo, hi = a[i], a[i + d]
                nxt[i] = jnp.where(bit, pltpu.roll(hi, d * S5_GROUP, 1), lo)
                nxt[i + d] = jnp.where(bit, hi, pltpu.roll(lo, LANES - d * S5_GROUP, 1))
        a = nxt
        d //= 2
    return a


def _s5_body(u_ref, t_ref, w_ref, v_ref, coef_ref, z_ref, nat_ref, uf_ref, zs_ref):
    S = u_ref.shape[0]
    R = S // S5_TOK
    P = S5_STATE
    GL = LANES // S5_GROUP
    RC = min(R, 256)
    chunk = lax.broadcasted_iota(jnp.int32, (RC, LANES), 1) // S5_GROUP

    nat_ref[...] = u_ref[...].astype(F32)

    def to_flat(rc, carry):
        r0 = pl.multiple_of(rc * RC, RC)
        for hf in range(S5_TOK // GL):
            arrs = [nat_ref[pl.ds(r0 * S5_TOK + hf * GL + k, RC, stride=S5_TOK), :] for k in range(GL)]
            for gl, a in enumerate(_chunk_transpose(arrs, chunk)):
                uf_ref[gl, pl.ds(r0, RC), hf * LANES:(hf + 1) * LANES] = a.astype(uf_ref.dtype)
        return carry

    lax.fori_loop(0, R // RC, to_flat, 0)

    row = lax.broadcasted_iota(jnp.int32, (R, LANES), 0)

    def shift_down(a, sh):
        if sh % 8 == 0:
            return jnp.concatenate([jnp.zeros((sh, LANES), F32), a[:R - sh]], axis=0)
        return jnp.where(row < sh, 0.0, pltpu.roll(a, sh, 0))

    def group(gl, carry):
        u = uf_ref[gl]
        y = _dot(u, t_ref[gl])
        x = _dot(u, w_ref[gl])
        k = 0
        while (1 << k) < R:
            xs = shift_down(x, 1 << k)
            x = (x + coef_ref[gl, k:k + 1, :] * xs
                 + coef_ref[gl, 16 + k:17 + k, :] * pltpu.roll(xs, P, 1))
            k += 1
        y = y + _dot(shift_down(x, 1).astype(BF16), v_ref[gl])
        zs_ref[gl] = _gelu_tanh(y)
        return carry

    lax.fori_loop(0, GL, group, 0)

    def to_nat(rc, carry):
        r0 = pl.multiple_of(rc * RC, RC)
        for hf in range(S5_TOK // GL):
            arrs = [zs_ref[gl, pl.ds(r0, RC), hf * LANES:(hf + 1) * LANES] for gl in range(GL)]
            for k, a in enumerate(_chunk_transpose(arrs, chunk)):
                nat_ref[pl.ds(r0 * S5_TOK + hf * GL + k, RC, stride=S5_TOK), :] = a
        return carry

    lax.fori_loop(0, R // RC, to_nat, 0)
    z_ref[...] = nat_ref[...].astype(z_ref.dtype)


def s5_scan(proj, tm, wm, vm, coef, batch, seq):
    t = proj.shape[0]
    GL = LANES // S5_GROUP
    r = seq // S5_TOK
    ub = COL_U // LANES
    per_g = lambda *s: pl.BlockSpec((GL,) + s, lambda bi, si: (si, 0, 0))
    return pl.pallas_call(
        _s5_body,
        grid=(batch, S5_WIDTH // LANES),
        in_specs=[pl.BlockSpec((seq, LANES), lambda bi, si: (bi, ub + si)),
                  per_g(256, 256), per_g(256, 128), per_g(128, 256), per_g(32, 128)],
        out_specs=pl.BlockSpec((seq, LANES), lambda bi, si: (bi, si)),
        out_shape=SDS((t, S5_WIDTH), BF16),
        scratch_shapes=[pltpu.VMEM((seq, LANES), F32), pltpu.VMEM((GL, r, 2 * LANES), BF16),
                        pltpu.VMEM((GL, r, 2 * LANES), F32)],
        compiler_params=_params(("parallel", "parallel")),
        name="s5_scan",
    )(proj, tm, wm, vm, coef)


def _glu_body(z_ref, wl_ref, wg_ref, o_ref):
    z = z_ref[...]
    o_ref[...] = (_dot(z, wl_ref[...]) * _sigmoid(_dot(z, wg_ref[...]))).astype(o_ref.dtype)


def glu(z, w, tm, tn):
    t, k = z.shape
    n = w.shape[1] // 2
    nj = n // tn
    return pl.pallas_call(
        _glu_body,
        grid=(t // tm, nj),
        in_specs=[pl.BlockSpec((tm, k), lambda i, j: (i, 0)),
                  pl.BlockSpec((k, tn), lambda i, j: (0, j)),
                  pl.BlockSpec((k, tn), lambda i, j: (0, nj + j))],
        out_specs=pl.BlockSpec((tm, tn), lambda i, j: (i, j)),
        out_shape=SDS((t, n), BF16),
        compiler_params=_params(("parallel", "arbitrary")),
        name="glu",
    )(z, w, w)


def _mix_body(ya_ref, o0_ref, o1_ref, o2_ref, l0_ref, l1_ref, l2_ref, yc_ref,
              wa_ref, wb_ref, wc_ref, ga_ref, gb_ref, gc_ref, out_ref, yb_ref, wt_ref, acc_ref):
    tm = ya_ref.shape[0]
    E = ATTN_HEAD_DIM

    @pl.when(pl.program_id(1) == 0)
    def _():
        o_refs = (o0_ref, o1_ref, o2_ref)
        l_refs = (l0_ref, l1_ref, l2_ref)
        for g, (_, r) in enumerate(ATTN_GROUPS):
            for c in range(r):
                wt_ref[g, pl.ds(c, tm // r, stride=r), :] = l_refs[g][0, c]
        l0, l1, l2 = wt_ref[0], wt_ref[1], wt_ref[2]
        m = jnp.maximum(jnp.maximum(l0, l1), l2)
        e0, e1, e2 = jnp.exp(l0 - m), jnp.exp(l1 - m), jnp.exp(l2 - m)
        inv = 1.0 / (e0 + e1 + e2)
        wt_ref[0] = e0 * inv
        wt_ref[1] = e1 * inv
        wt_ref[2] = e2 * inv
        order = sorted(range(len(ATTN_GROUPS)), key=lambda g: -ATTN_GROUPS[g][1])
        for pos, g in enumerate(order[:-1]):
            r = ATTN_GROUPS[g][1]
            for c in range(r):
                rows = pl.ds(c, tm // r, stride=r)
                w = wt_ref[g, rows, :]
                for h in range(ATTN_HEADS):
                    part = w[:, h:h + 1] * o_refs[g][0, c, :, h * E:(h + 1) * E].astype(F32)
                    if pos == 0:
                        acc_ref[h, rows, :] = part
                    else:
                        acc_ref[h, rows, :] += part
        g = order[-1]
        assert ATTN_GROUPS[g][1] == 1
        w = wt_ref[g]
        for h in range(ATTN_HEADS):
            sl = slice(h * E, (h + 1) * E)
            yb = acc_ref[h] + w[:, h:h + 1] * o_refs[g][0, 0, :, sl].astype(F32)
            yb_ref[:, sl] = yb.astype(yb_ref.dtype)

    mix = (_sigmoid(ga_ref[...].astype(F32)) * _dot(ya_ref[...], wa_ref[...])
           + _sigmoid(gb_ref[...].astype(F32)) * _dot(yb_ref[...], wb_ref[...])
           + _sigmoid(gc_ref[...].astype(F32)) * _dot(yc_ref[...], wc_ref[...]))
    out_ref[...] = mix.astype(out_ref.dtype)


def gated_mix(ya, outs, lses, yc, wa, wb, wc, proj, seq, tm, tn):
    t, kw = ya.shape
    d = wa.shape[1]
    go = COL_G // tn
    nbt = seq // tm
    row = lambda w: pl.BlockSpec((tm, w), lambda i, j: (i, 0))
    grp = lambda r, w: pl.BlockSpec((1, r, tm // r, w), lambda i, j: (i // nbt, 0, i % nbt, 0))
    wsp = pl.BlockSpec((kw, tn), lambda i, j: (0, j))
    gate = lambda o: pl.BlockSpec((tm, tn), lambda i, j: (i, go + o * (d // tn) + j))
    dils = [r for _, r in ATTN_GROUPS]
    return pl.pallas_call(
        _mix_body,
        grid=(t // tm, d // tn),
        in_specs=[row(kw)] + [grp(r, kw) for r in dils] + [grp(r, LANES) for r in dils] + [row(kw)]
        + [wsp, wsp, wsp, gate(0), gate(1), gate(2)],
        out_specs=pl.BlockSpec((tm, tn), lambda i, j: (i, j)),
        out_shape=SDS((t, d), BF16),
        scratch_shapes=[pltpu.VMEM((tm, kw), BF16), pltpu.VMEM((len(dils), tm, LANES), F32),
                        pltpu.VMEM((ATTN_HEADS, tm, LANES), F32)],
        compiler_params=_params(("parallel", "arbitrary")),
        name="gated_mix",
    )(ya, *outs, *lses, yc, wa, wb, wc, proj, proj, proj)


def _out_ffn_body(final, x_ref, m_ref, wo_ref, g_ref, w1_ref, w2_ref, fg_ref, o_ref, xn_ref):
    j = pl.program_id(1)

    @pl.when(j == 0)
    def _():
        x1 = x_ref[...] + _dot(m_ref[...], wo_ref[...])
        xn_ref[...] = _rms(x1, g_ref[...]).astype(BF16)
        o_ref[...] = x1

    hid = jnp.maximum(_dot(xn_ref[...], w1_ref[...]), 0.0)
    o_ref[...] += _dot((hid * hid).astype(BF16), w2_ref[...])

    if final:
        @pl.when(j == pl.num_programs(1) - 1)
        def _():
            o_ref[...] = _rms(o_ref[...], fg_ref[...])


def out_ffn(x2d, mix, w_out, g, w1, w2, final_g, final, tm, th):
    t, d = x2d.shape
    hdim = w1.shape[1]
    const = lambda shape: pl.BlockSpec(shape, lambda i, j: (0, 0))
    return pl.pallas_call(
        functools.partial(_out_ffn_body, final),
        grid=(t // tm, hdim // th),
        in_specs=[pl.BlockSpec((tm, d), lambda i, j: (i, 0)),
                  pl.BlockSpec((tm, d), lambda i, j: (i, 0)),
                  pl.BlockSpec((d, d), lambda i, j: (0, 0), pipeline_mode=pl.Buffered(1)),
                  const((1, d)),
                  pl.BlockSpec((d, th), lambda i, j: (0, j)),
                  pl.BlockSpec((th, d), lambda i, j: (j, 0)),
                  const((1, d))],
        out_specs=pl.BlockSpec((tm, d), lambda i, j: (i, 0)),
        out_shape=SDS((t, d), F32),
        scratch_shapes=[pltpu.VMEM((tm, d), BF16)],
        compiler_params=_params(("parallel", "arbitrary")),
        name="out_ffn",
    )(x2d, mix, w_out, g, w1, w2, final_g)


def _tile(n, want):
    t = min(n, want)
    assert n % t == 0, (n, want)
    return t


def _layer(x2d, bias, batch, seq, p, final_g, final):
    t, d = x2d.shape
    tm = _tile(seq, 1024)
    n_gate0 = 4 * MLSTM_WIDTH
    n_att0 = n_gate0 + 2 * MLSTM_HEADS
    n_att = 3 * ATTN_HEADS_TOTAL * ATTN_HEAD_DIM
    n_groups = len(ATTN_GROUPS)
    w_in = p["w_in"]
    w_att = w_in[:, n_att0:n_att0 + n_att].reshape(d, 3, n_groups, ATTN_WIDTH)
    w_att = w_att.transpose(0, 2, 1, 3).reshape(d, n_att)
    n_u0 = n_att0 + n_att
    w_main = jnp.concatenate([w_in[:, :COL_U], w_in[:, n_u0:n_u0 + S5_WIDTH], w_in[:, COL_U:n_gate0],
                              w_in[:, n_u0 + S5_WIDTH:], w_att], axis=1).astype(BF16)
    w_if = jnp.pad(w_in[:, n_gate0:n_att0], ((0, 0), (0, N_GATE_PAD - 2 * MLSTM_HEADS))).astype(BF16)
    proj, gif, gif_t, *qkvs = in_proj(x2d, p["norm1_g"].reshape(1, d), w_main, w_if, batch, seq, tm)

    bif = jnp.concatenate([p["b_igate"], p["b_fgate"]]).astype(F32)
    bcol = jnp.pad(bif, (0, N_GATE_PAD - bif.shape[0])).reshape(1, N_GATE_PAD)
    ya = mlstm(proj, gif, gif_t, bcol, bif.reshape(8, 1), p["conv_w"], p["conv_b"].reshape(1, -1),
               p["mh_norm_g"].reshape(1, -1), batch, seq)

    outs, lses = zip(*[attn_group(qkvs[g], bias, g) for g in range(n_groups)])

    tmat, wmat, vmat, coef = s5_prep(p["lam_re"], p["lam_im"], p["log_dt"], p["b_re"], p["b_im"],
                                     p["c_re"], p["c_im"], p["d_skip"])
    z = s5_scan(proj, tmat, wmat, vmat, coef, batch, seq)
    yc = glu(z, p["w_glu"].astype(BF16), tm, 512)

    mix = gated_mix(ya, outs, lses, yc, p["w_br_a"].astype(BF16), p["w_br_b"].astype(BF16),
                    p["w_br_c"].astype(BF16), proj, seq, tm, 512)
    return out_ffn(x2d, mix, p["w_out"].astype(BF16), p["norm2_g"].reshape(1, d), p["w_ff1"].astype(BF16),
                   p["w_ff2"].astype(BF16), final_g.reshape(1, d), final, _tile(t, 512), 1024)


_PER_LAYER = ("norm1_g", "w_in", "conv_w", "conv_b", "b_igate", "b_fgate", "mh_norm_g", "lam_re", "lam_im",
              "log_dt", "b_re", "b_im", "c_re", "c_im", "d_skip", "w_glu", "w_br_a", "w_br_b", "w_br_c",
              "w_out", "norm2_g", "w_ff1", "w_ff2")


def kernel(x, norm1_g, w_in, conv_w, conv_b, b_igate, b_fgate, mh_norm_g, rel_bias, lam_re, lam_im, log_dt,
           b_re, b_im, c_re, c_im, d_skip, w_glu, w_br_a, w_br_b, w_br_c, w_out, norm2_g, w_ff1, w_ff2,
           final_g):
    stacked = dict(norm1_g=norm1_g, w_in=w_in, conv_w=conv_w, conv_b=conv_b, b_igate=b_igate,
                   b_fgate=b_fgate, mh_norm_g=mh_norm_g, lam_re=lam_re, lam_im=lam_im, log_dt=log_dt,
                   b_re=b_re, b_im=b_im, c_re=c_re, c_im=c_im, d_skip=d_skip, w_glu=w_glu, w_br_a=w_br_a,
                   w_br_b=w_br_b, w_br_c=w_br_c, w_out=w_out, norm2_g=norm2_g, w_ff1=w_ff1, w_ff2=w_ff2)
    batch, seq, d = x.shape
    x2d = x.astype(F32).reshape(batch * seq, d)
    bias = attn_bias(rel_bias.astype(F32))
    depth = w_in.shape[0]
    for l in range(depth):
        x2d = _layer(x2d, bias, batch, seq, {k: stacked[k][l] for k in _PER_LAYER}, final_g, l == depth - 1)
    return x2d.reshape(batch, seq, d).astype(x.dtype)
```

```python
import functools
import math

import jax
import jax.numpy as jnp
from jax import lax
from jax.experimental import pallas as pl
from jax.experimental.pallas import tpu as pltpu

F32 = jnp.float32
BF16 = jnp.bfloat16
SDS = jax.ShapeDtypeStruct

NORM_EPS = 1e-6
NEG = -1e30

MLSTM_HEADS = 4
MLSTM_HEAD_DIM = 256
MLSTM_WIDTH = MLSTM_HEADS * MLSTM_HEAD_DIM
CONV_WIDTH = 4
ATTN_GROUPS = ((128, 1), (512, 4), (2048, 16))
ATTN_HEADS = 8
ATTN_HEAD_DIM = 128
ATTN_WIDTH = ATTN_HEADS * ATTN_HEAD_DIM
ATTN_HEADS_TOTAL = len(ATTN_GROUPS) * ATTN_HEADS
ATTN_BLOCK = 128
ATTN_Q_BLOCKS = 4
REL_BUCKETS = 32
REL_MAX_DISTANCE = 2048
S5_WIDTH = 1024
S5_GROUP = 16
S5_GROUPS = S5_WIDTH // S5_GROUP
S5_STATE = 64
S5_TOK = 16
N_GATE_PAD = 128

COL_U = 3 * MLSTM_WIDTH
COL_O = COL_U + S5_WIDTH
COL_G = COL_O + MLSTM_WIDTH
D_MODEL = 2048
N_MAIN = COL_G + 3 * D_MODEL
LANES = 128

MLSTM_CHUNK = 256
VMEM_LIMIT = 56 * 2**20


def _params(sem):
    return pltpu.CompilerParams(dimension_semantics=sem, vmem_limit_bytes=VMEM_LIMIT)


def _sigmoid(x):
    return 1.0 / (1.0 + jnp.exp(-x))


def _log_sigmoid(x):
    return jnp.minimum(x, 0.0) - jnp.log(1.0 + jnp.exp(-jnp.abs(x)))


def _rms(x, g):
    ms = jnp.mean(x * x, axis=-1, keepdims=True)
    return x * lax.rsqrt(ms + NORM_EPS) * g


def _dot(a, b):
    return jnp.dot(a, b, preferred_element_type=F32)


def _dot_nt(a, b):
    return lax.dot_general(a, b, (((1,), (1,)), ((), ())), preferred_element_type=F32)


def _dot_tn(a, b):
    return lax.dot_general(a, b, (((0,), (0,)), ((), ())), preferred_element_type=F32)


def _inproj_body(ns, nm, x_ref, g_ref, w_ref, wif_ref, o_ref, oif_ref, oift_ref, a0_ref, a1_ref, a2_ref,
                 xn_ref, acc_ref, acc2_ref):
    j = pl.program_id(1)
    tm = x_ref.shape[0]
    tn = w_ref.shape[2]

    @pl.when(j == 0)
    def _():
        xn = _rms(x_ref[...], g_ref[...]).astype(BF16)
        xn_ref[...] = xn
        gates = _dot(xn, wif_ref[...])
        oif_ref[...] = gates
        oift_ref[...] = jnp.transpose(gates)[:8]

    @pl.when(jnp.logical_and(j < nm, j != ns))
    def _():
        o_ref[...] = _dot(xn_ref[...], w_ref[0]).astype(o_ref.dtype)

    @pl.when(j == ns)
    def _():
        o_ref[...] = _sigmoid(_dot(xn_ref[...], w_ref[0])).astype(o_ref.dtype)

    for g, a_ref in enumerate((a0_ref, a1_ref, a2_ref)):
        r = ATTN_GROUPS[g][1]
        lo = nm + 3 * g

        @pl.when(jnp.logical_and(j >= lo, j < lo + 3))
        def _(a_ref=a_ref, r=r):
            res = _dot(xn_ref[...], w_ref[0])
            if r == 1:
                a_ref[0, 0] = res.astype(a_ref.dtype)
                return
            ns_ = acc_ref.shape[0]
            for base in range(0, tn // LANES, ns_):
                lanes = [slice((base + s) * LANES, (base + s + 1) * LANES) for s in range(ns_)]
                for s in range(ns_):
                    acc_ref[s] = res[:, lanes[s]]
                if r <= 4:
                    for c in range(r):
                        for s in range(ns_):
                            a_ref[0, c, :, lanes[s]] = acc_ref[s, pl.ds(c, tm // r, stride=r), :].astype(a_ref.dtype)
                    continue
                q, r2 = 4, r // 4
                for s in range(ns_):
                    for c0 in range(q):
                        acc2_ref[s, c0 * (tm // q):(c0 + 1) * (tm // q), :] = (
                            acc_ref[s, pl.ds(c0, tm // q, stride=q), :])
                for c0 in range(q):
                    for c1 in range(r2):
                        for s in range(ns_):
                            rows = pl.ds(c0 * (tm // q) + c1, tm // r, stride=r2)
                            a_ref[0, c1 * q + c0, :, lanes[s]] = acc2_ref[s, rows, :].astype(a_ref.dtype)


def in_proj(x2d, g, w, wif, batch, seq, tm):
    t, d = x2d.shape
    tn = ATTN_WIDTH
    nm = N_MAIN // tn
    nbt = seq // tm
    n_groups = len(ATTN_GROUPS)

    def a_spec(gi):
        r = ATTN_GROUPS[gi][1]
        return pl.BlockSpec((1, r, tm // r, tn),
                            lambda i, j: (i // nbt, 0, i % nbt, jnp.clip(j - nm - 3 * gi, 0, 2)))

    return pl.pallas_call(
        functools.partial(_inproj_body, COL_O // tn, nm),
        grid=(t // tm, nm + 3 * n_groups),
        in_specs=[pl.BlockSpec((tm, d), lambda i, j: (i, 0)),
                  pl.BlockSpec((1, d), lambda i, j: (0, 0)),
                  pl.BlockSpec((1, d, tn), lambda i, j: (j, 0, 0)),
                  pl.BlockSpec((d, N_GATE_PAD), lambda i, j: (0, 0))],
        out_specs=[pl.BlockSpec((tm, tn), lambda i, j: (i, jnp.minimum(j, nm - 1))),
                   pl.BlockSpec((tm, N_GATE_PAD), lambda i, j: (i, 0)),
                   pl.BlockSpec((8, tm), lambda i, j: (0, i))] + [a_spec(gi) for gi in range(n_groups)],
        out_shape=[SDS((t, N_MAIN), BF16), SDS((t, N_GATE_PAD), F32), SDS((8, t), F32)]
        + [SDS((batch, r, seq // r, 3 * tn), BF16) for _, r in ATTN_GROUPS],
        scratch_shapes=[pltpu.VMEM((tm, d), BF16), pltpu.VMEM((tn // LANES // 2, tm, LANES), F32),
                        pltpu.VMEM((tn // LANES // 2, tm, LANES), F32)],
        compiler_params=_params(("parallel", "arbitrary")),
        name="in_proj",
    )(x2d, g, w, wif)


def _mlstm_body(q_ref, k_ref, v_ref, og_ref, gcol_ref, grow_ref, bcol_ref, brow_ref,
                cw_ref, cb_ref, ng_ref, y_ref, ct_ref, m_ref, tail_ref):
    c = pl.program_id(1)
    L = q_ref.shape[0]
    E = MLSTM_HEAD_DIM
    H = MLSTM_HEADS

    @pl.when(c == 0)
    def _():
        ct_ref[...] = jnp.zeros_like(ct_ref)
        m_ref[...] = jnp.zeros_like(m_ref)
        tail_ref[...] = jnp.zeros_like(tail_ref)

    row8 = lax.broadcasted_iota(jnp.int32, (8, E), 0)
    tt = lax.broadcasted_iota(jnp.int32, (L, L), 0)
    ss = lax.broadcasted_iota(jnp.int32, (L, L), 1)
    causal = ss <= tt
    gc = gcol_ref[...] + bcol_ref[...]
    gr = grow_ref[...] + brow_ref[...]

    shifts = [jnp.where(tt - ss == d, 1.0, 0.0).astype(BF16) for d in range(1, CONV_WIDTH)]

    def conv_silu(x_ref, h, slot):
        xb = x_ref[:, h * E:(h + 1) * E]
        x = xb.astype(F32)
        tail = tail_ref[slot]
        w = cw_ref[:, slot * E:(slot + 1) * E]
        acc = cb_ref[:, slot * E:(slot + 1) * E] + w[CONV_WIDTH - 1:CONV_WIDTH] * x
        head = jnp.zeros((8, E), F32)
        for d in range(1, CONV_WIDTH):
            wd = w[CONV_WIDTH - 1 - d:CONV_WIDTH - d]
            acc = acc + wd * _dot(shifts[d - 1], xb)
            head = head + wd * jnp.where(row8 < d, pltpu.roll(tail, d, 0), 0.0)
        acc = jnp.concatenate([acc[:8] + head, acc[8:]], axis=0)
        tail_ref[slot] = x[L - 8:]
        return acc * _sigmoid(acc)

    for h in range(H):
        hs = slice(h * E, (h + 1) * E)
        q = conv_silu(q_ref, h, h)
        k = conv_silu(k_ref, h, H + h) * (E ** -0.5)

        ig_col = gc[:, h:h + 1]
        lf_col = _log_sigmoid(gc[:, H + h:H + h + 1])
        ig_row = gr[h:h + 1, :]
        lf_row = _log_sigmoid(gr[H + h:H + h + 1, :])
        bcum_col = jnp.sum(jnp.where(causal, lf_row, 0.0), axis=1, keepdims=True)
        bcum_row = jnp.sum(jnp.where(tt <= ss, lf_col, 0.0), axis=0, keepdims=True)
        a_row = ig_row - bcum_row
        a_col = ig_col - bcum_col

        m_prev = m_ref[h]
        amat = jnp.where(causal, a_row, NEG)
        mrow = jnp.maximum(m_prev, jnp.max(amat, axis=1, keepdims=True))
        w_intra = jnp.exp(amat - mrow)
        w_inter = jnp.exp(m_prev - mrow)

        qb = q.astype(BF16)
        kb = k.astype(BF16)
        vaug = jnp.concatenate([v_ref[:, hs], jnp.ones((L, LANES), BF16)], axis=1)
        s = _dot_nt(qb, kb) * w_intra
        ct = ct_ref[h]
        num_aug = _dot(s.astype(BF16), vaug) + w_inter * _dot(qb, ct.astype(BF16))
        num = num_aug[:, :E]
        den = num_aug[:, E:E + 1]
        m_t = bcum_col + mrow
        hout = num / jnp.maximum(jnp.abs(den), jnp.exp(-m_t))
        hn = _rms(hout, ng_ref[:, hs])
        y_ref[:, hs] = (og_ref[:, hs].astype(F32) * hn).astype(y_ref.dtype)

        b_last = jnp.sum(lf_col, axis=0, keepdims=True)
        g_col = b_last + a_col
        m_new = jnp.maximum(b_last + m_prev, jnp.max(g_col, axis=0, keepdims=True))
        w_s = jnp.exp(g_col - m_new)
        decay = jnp.exp(b_last + m_prev - m_new)
        kw = (k * w_s).astype(BF16)
        ct_ref[h] = decay * ct + _dot_tn(kw, vaug)
        m_ref[h] = m_new


def mlstm(proj, gif, gif_t, bcol, brow, conv_w, conv_b, ng, batch, seq):
    L = min(MLSTM_CHUNK, seq)
    E = MLSTM_HEAD_DIM
    W = MLSTM_WIDTH
    nc = seq // L
    H = MLSTM_HEADS
    t = batch * seq
    row = lambda b, c: b * nc + c
    return pl.pallas_call(
        _mlstm_body,
        grid=(batch, nc),
        in_specs=[pl.BlockSpec((L, W), lambda b, c: (row(b, c), 0)),
                  pl.BlockSpec((L, W), lambda b, c: (row(b, c), 1)),
                  pl.BlockSpec((L, W), lambda b, c: (row(b, c), 2)),
                  pl.BlockSpec((L, W), lambda b, c: (row(b, c), COL_O // W)),
                  pl.BlockSpec((L, N_GATE_PAD), lambda b, c: (row(b, c), 0)),
                  pl.BlockSpec((8, L), lambda b, c: (0, row(b, c))),
                  pl.BlockSpec((1, N_GATE_PAD), lambda b, c: (0, 0)),
                  pl.BlockSpec((8, 1), lambda b, c: (0, 0)),
                  pl.BlockSpec((CONV_WIDTH, 2 * W), lambda b, c: (0, 0)),
                  pl.BlockSpec((1, 2 * W), lambda b, c: (0, 0)),
                  pl.BlockSpec((1, W), lambda b, c: (0, 0))],
        out_specs=pl.BlockSpec((L, W), lambda b, c: (row(b, c), 0)),
        out_shape=SDS((t, W), BF16),
        scratch_shapes=[pltpu.VMEM((H, E, E + LANES), F32), pltpu.VMEM((H, 1, 1), F32),
                        pltpu.VMEM((2 * H, 8, E), F32)],
        compiler_params=_params(("parallel", "arbitrary")),
        name="mlstm",
    )(proj, proj, proj, proj, gif, gif_t, bcol, brow, conv_w, conv_b, ng)


def _bias_body(table_ref, o_ref):
    h = pl.program_id(0)
    dil = jnp.where(h < ATTN_HEADS, ATTN_GROUPS[0][1],
                    jnp.where(h < 2 * ATTN_HEADS, ATTN_GROUPS[1][1], ATTN_GROUPS[2][1]))
    shape = (ATTN_BLOCK, 2 * ATTN_BLOCK)
    i = lax.broadcasted_iota(jnp.int32, shape, 0)
    j = lax.broadcasted_iota(jnp.int32, shape, 1)
    rel = ATTN_BLOCK + i - j
    dist = jnp.maximum(rel, 0) * dil
    max_exact = REL_BUCKETS // 2
    nf = jnp.maximum(dist, max_exact).astype(F32)
    large = max_exact + (jnp.log(nf / max_exact) / math.log(REL_MAX_DISTANCE / max_exact)
                         * (REL_BUCKETS - max_exact)).astype(jnp.int32)
    large = jnp.minimum(large, REL_BUCKETS - 1)
    bucket = jnp.where(dist < max_exact, dist, large)
    acc = jnp.zeros(shape, F32)
    for b in range(REL_BUCKETS):
        acc = jnp.where(bucket == b, table_ref[b, h], acc)
    o_ref[0] = jnp.where(rel >= 0, jnp.where(rel <= ATTN_BLOCK, acc, NEG), NEG)


def attn_bias(rel_bias):
    return pl.pallas_call(
        _bias_body,
        grid=(ATTN_HEADS_TOTAL,),
        in_specs=[pl.BlockSpec(memory_space=pltpu.SMEM)],
        out_specs=pl.BlockSpec((1, ATTN_BLOCK, 2 * ATTN_BLOCK), lambda h: (h, 0, 0)),
        out_shape=SDS((ATTN_HEADS_TOTAL, ATTN_BLOCK, 2 * ATTN_BLOCK), F32),
        compiler_params=_params(("arbitrary",)),
        name="attn_bias",
    )(rel_bias)


def _attn_body(q_ref, kp_ref, kc_ref, vp_ref, vc_ref, bias_ref, o_ref, lse_ref):
    n = pl.program_id(2)
    B = ATTN_BLOCK
    E = ATTN_HEAD_DIM
    H = ATTN_HEADS
    NQ = q_ref.shape[2] // B
    scale = E ** -0.5
    hs = [slice(h * E, (h + 1) * E) for h in range(H)]

    def keys(cur_ref, prev_ref, i, sl):
        if i == 0:
            return jnp.concatenate([prev_ref[0, 0, :, sl], cur_ref[0, 0, :B, sl]], axis=0)
        return cur_ref[0, 0, (i - 1) * B:(i + 1) * B, sl]

    s = jnp.concatenate([_dot_nt(q_ref[0, 0, i * B:(i + 1) * B, sl], keys(kc_ref, kp_ref, i, sl))
                         for i in range(NQ) for sl in hs], axis=0)
    bias = bias_ref[...].reshape(H * B, 2 * B)
    key = lax.broadcasted_iota(jnp.int32, (1, 2 * B), 1)
    first = bias + jnp.where(key < B, jnp.where(n > 0, 0.0, NEG), 0.0)
    s = s * scale + jnp.concatenate([first] + [bias] * (NQ - 1), axis=0)
    m = jnp.max(s, axis=1, keepdims=True)
    p = jnp.exp(s - m)
    den = jnp.sum(p, axis=1, keepdims=True)
    inv = 1.0 / den
    pb = p.astype(BF16)
    lse_all = m + jnp.log(den)
    lane = lax.broadcasted_iota(jnp.int32, (B, LANES), 1)
    for i in range(NQ):
        lse = jnp.zeros((B, LANES), F32)
        for h, sl in enumerate(hs):
            rows = slice((i * H + h) * B, (i * H + h + 1) * B)
            o = _dot(pb[rows], keys(vc_ref, vp_ref, i, sl)) * inv[rows]
            o_ref[0, 0, i * B:(i + 1) * B, sl] = o.astype(o_ref.dtype)
            lse = jnp.where(lane == h, lse_all[rows], lse)
        lse_ref[0, 0, i * B:(i + 1) * B, :] = lse


def attn_group(qkv, bias, g):
    batch, r, l, _ = qkv.shape
    W = ATTN_WIDTH
    nq = min(ATTN_Q_BLOCKS, l // ATTN_BLOCK)
    rows = nq * ATTN_BLOCK
    cur = (1, 1, rows, W)
    one = (1, 1, ATTN_BLOCK, W)
    prev = lambda n: jnp.maximum(n * nq - 1, 0)
    return pl.pallas_call(
        _attn_body,
        grid=(batch, r, l // rows),
        in_specs=[pl.BlockSpec(cur, lambda b, c, n: (b, c, n, 0)),
                  pl.BlockSpec(one, lambda b, c, n: (b, c, prev(n), 1)),
                  pl.BlockSpec(cur, lambda b, c, n: (b, c, n, 1)),
                  pl.BlockSpec(one, lambda b, c, n: (b, c, prev(n), 2)),
                  pl.BlockSpec(cur, lambda b, c, n: (b, c, n, 2)),
                  pl.BlockSpec((ATTN_HEADS, ATTN_BLOCK, 2 * ATTN_BLOCK), lambda b, c, n: (g, 0, 0))],
        out_specs=[pl.BlockSpec(cur, lambda b, c, n: (b, c, n, 0)),
                   pl.BlockSpec((1, 1, rows, LANES), lambda b, c, n: (b, c, n, 0))],
        out_shape=[SDS((batch, r, l, W), BF16), SDS((batch, r, l, LANES), F32)],
        compiler_params=_params(("parallel", "parallel", "arbitrary")),
        name=f"attn_g{g}",
    )(qkv, qkv, qkv, qkv, qkv, bias)


def _s5_prep_body(lam_ref, lamc_ref, dt_ref, bt_r_ref, bt_i_ref, ce_r_ref, ce_i_ref, d_ref,
                  t_ref, w_ref, v_ref, coef_ref):
    P = S5_STATE
    lr = lam_ref[0, 0:1, :]
    li = lam_ref[0, 1:2, :]
    dt = jnp.exp(dt_ref[0])

    def apow(e, lr_, li_):
        mag = jnp.exp(lr_ * dt * e)
        ang = li_ * dt * e
        return mag * jnp.cos(ang), mag * jnp.sin(ang)

    one = jnp.ones((1, 1), F32)
    ar, ai = apow(one, lr, li)
    nr = ar - 1.0
    den = lr * lr + li * li
    f_re = (nr * lr + ai * li) / den
    f_im = (ai * lr - nr * li) / den
    bt_r = bt_r_ref[0]
    bt_i = bt_i_ref[0]
    bb_r = f_re * bt_r - f_im * bt_i
    bb_i = f_re * bt_i + f_im * bt_r

    lrc = lamc_ref[0, :, 0:1]
    lic = lamc_ref[0, :, 1:2]
    lag = (lax.broadcasted_iota(jnp.int32, (P, 256), 1) // S5_GROUP).astype(F32)
    adr, adi = apow(lag, lrc, lic)
    ce_r = ce_r_ref[0]
    ce_i = ce_i_ref[0]
    ca_r = ce_r * adr - ce_i * adi
    ca_i = ce_r * adi + ce_i * adr
    hp = lax.Precision.HIGHEST
    ks = (jnp.dot(bb_r[:, :P], ca_r, precision=hp, preferred_element_type=F32)
          - jnp.dot(bb_i[:, :P], ca_i, precision=hp, preferred_element_type=F32))
    si = lax.broadcasted_iota(jnp.int32, (S5_GROUP, 256), 0)
    lj = lax.broadcasted_iota(jnp.int32, (S5_GROUP, 256), 1)
    ks = ks + jnp.where(si == lj, d_ref[0], 0.0)
    for s in range(S5_TOK):
        sh = s * S5_GROUP
        blk = ks if s == 0 else jnp.where(lj >= sh, pltpu.roll(ks, sh, 1), 0.0)
        t_ref[0, sh:sh + S5_GROUP, :] = blk.astype(t_ref.dtype)

    lane128 = lax.broadcasted_iota(jnp.int32, (1, 128), 1)
    for s in range(S5_TOK):
        pr, pi = apow(float(S5_TOK - 1 - s) * one, lr, li)
        p1 = jnp.where(lane128 < P, pr, pi)
        p2 = jnp.where(lane128 < P, -pi, pr)
        w_ref[0, s * S5_GROUP:(s + 1) * S5_GROUP, :] = (bb_r * p1 + bb_i * p2).astype(w_ref.dtype)

    adr1, adi1 = apow(lag + 1.0, lrc, lic)
    v_ref[0, 0:P, :] = (ce_r * adr1 - ce_i * adi1).astype(v_ref.dtype)
    v_ref[0, P:2 * P, :] = (-(ce_r * adi1 + ce_i * adr1)).astype(v_ref.dtype)

    ek = jnp.left_shift(S5_TOK, lax.broadcasted_iota(jnp.int32, (16, 1), 0)).astype(F32)
    cr, ci = apow(ek, lr, li)
    coef_ref[0, 0:16, :] = cr
    coef_ref[0, 16:32, :] = jnp.where(lane128 < P, -ci, ci)


def s5_prep(lam_re, lam_im, log_dt, b_re, b_im, c_re, c_im, d_skip):
    G, P, I = b_re.shape
    lam = jnp.stack([jnp.tile(lam_re, (1, 2)), jnp.tile(lam_im, (1, 2))], axis=1)
    lam = jnp.pad(lam, ((0, 0), (0, 6), (0, 0)))
    lamc = jnp.stack([lam_re, lam_im], axis=2)
    dt = log_dt.reshape(G, 1, 1)
    bt_r = jnp.tile(jnp.swapaxes(b_re, 1, 2), (1, 1, 2))
    bt_i = jnp.tile(jnp.swapaxes(b_im, 1, 2), (1, 1, 2))
    ce_r = jnp.tile(jnp.swapaxes(c_re, 1, 2), (1, 1, S5_TOK))
    ce_i = jnp.tile(jnp.swapaxes(c_im, 1, 2), (1, 1, S5_TOK))
    d = jnp.pad(d_skip, ((0, 0), (0, 256 - I))).reshape(G, 1, 256)
    blk = lambda *s: pl.BlockSpec((1,) + s, lambda g: (g, 0, 0))
    return pl.pallas_call(
        _s5_prep_body,
        grid=(G,),
        in_specs=[blk(8, 128), blk(P, 2), blk(1, 1), blk(16, 128), blk(16, 128), blk(P, 256), blk(P, 256),
                  blk(1, 256)],
        out_specs=[blk(256, 256), blk(256, 128), blk(128, 256), blk(32, 128)],
        out_shape=[SDS((G, 256, 256), BF16), SDS((G, 256, 128), BF16), SDS((G, 128, 256), BF16),
                   SDS((G, 32, 128), F32)],
        compiler_params=_params(("arbitrary",)),
        name="s5_prep",
    )(lam, lamc, dt, bt_r, bt_i, ce_r, ce_i, d)


def _gelu_tanh(x):
    return 0.5 * x * (1.0 + jnp.tanh(math.sqrt(2.0 / math.pi) * (x + 0.044715 * (x * x * x))))


def _chunk_transpose(arrs, chunk):
    a = list(arrs)
    n = len(a)
    d = n // 2
    while d >= 1:
        bit = (chunk & d) != 0
        nxt = list(a)
        for i in range(n):
            if i & d == 0:
                lo, hi = a[i], a[i + d]
                nxt[i] = jnp.where(bit, pltpu.roll(hi, d * S5_GROUP, 1), lo)
                nxt[i + d] = jnp.where(bit, hi, pltpu.roll(lo, LANES - d * S5_GROUP, 1))
        a = nxt
        d //= 2
    return a


def _s5_body(u_ref, t_ref, w_ref, v_ref, coef_ref, z_ref, nat_ref, uf_ref, zs_ref):
    S = u_ref.shape[0]
    R = S // S5_TOK
    P = S5_STATE
    GL = LANES // S5_GROUP
    RC = min(R, 256)
    chunk = lax.broadcasted_iota(jnp.int32, (RC, LANES), 1) // S5_GROUP

    nat_ref[...] = u_ref[...].astype(F32)

    def to_flat(rc, carry):
        r0 = pl.multiple_of(rc * RC, RC)
        for hf in range(S5_TOK // GL):
            arrs = [nat_ref[pl.ds(r0 * S5_TOK + hf * GL + k, RC, stride=S5_TOK), :] for k in range(GL)]
            for gl, a in enumerate(_chunk_transpose(arrs, chunk)):
                uf_ref[gl, pl.ds(r0, RC), hf * LANES:(hf + 1) * LANES] = a.astype(uf_ref.dtype)
        return carry

    lax.fori_loop(0, R // RC, to_flat, 0)

    row = lax.broadcasted_iota(jnp.int32, (R, LANES), 0)

    def shift_down(a, sh):
        if sh % 8 == 0:
            return jnp.concatenate([jnp.zeros((sh, LANES), F32), a[:R - sh]], axis=0)
        return jnp.where(row < sh, 0.0, pltpu.roll(a, sh, 0))

    def group(gl, carry):
        u = uf_ref[gl]
        y = _dot(u, t_ref[gl])
        x = _dot(u, w_ref[gl])
        k = 0
        while (1 << k) < R:
            xs = shift_down(x, 1 << k)
            x = (x + coef_ref[gl, k:k + 1, :] * xs
                 + coef_ref[gl, 16 + k:17 + k, :] * pltpu.roll(xs, P, 1))
            k += 1
        y = y + _dot(shift_down(x, 1).astype(BF16), v_ref[gl])
        zs_ref[gl] = _gelu_tanh(y)
        return carry

    lax.fori_loop(0, GL, group, 0)

    def to_nat(rc, carry):
        r0 = pl.multiple_of(rc * RC, RC)
        for hf in range(S5_TOK // GL):
            arrs = [zs_ref[gl, pl.ds(r0, RC), hf * LANES:(hf + 1) * LANES] for gl in range(GL)]
            for k, a in enumerate(_chunk_transpose(arrs, chunk)):
                nat_ref[pl.ds(r0 * S5_TOK + hf * GL + k, RC, stride=S5_TOK), :] = a
        return carry

    lax.fori_loop(0, R // RC, to_nat, 0)
    z_ref[...] = nat_ref[...].astype(z_ref.dtype)


def s5_scan(proj, tm, wm, vm, coef, batch, seq):
    t = proj.shape[0]
    GL = LANES // S5_GROUP
    r = seq // S5_TOK
    ub = COL_U // LANES
    per_g = lambda *s: pl.BlockSpec((GL,) + s, lambda bi, si: (si, 0, 0))
    return pl.pallas_call(
        _s5_body,
        grid=(batch, S5_WIDTH // LANES),
        in_specs=[pl.BlockSpec((seq, LANES), lambda bi, si: (bi, ub + si)),
                  per_g(256, 256), per_g(256, 128), per_g(128, 256), per_g(32, 128)],
        out_specs=pl.BlockSpec((seq, LANES), lambda bi, si: (bi, si)),
        out_shape=SDS((t, S5_WIDTH), BF16),
        scratch_shapes=[pltpu.VMEM((seq, LANES), F32), pltpu.VMEM((GL, r, 2 * LANES), BF16),
                        pltpu.VMEM((GL, r, 2 * LANES), F32)],
        compiler_params=_params(("parallel", "parallel")),
        name="s5_scan",
    )(proj, tm, wm, vm, coef)


def _glu_body(z_ref, wl_ref, wg_ref, o_ref):
    z = z_ref[...]
    o_ref[...] = (_dot(z, wl_ref[...]) * _sigmoid(_dot(z, wg_ref[...]))).astype(o_ref.dtype)


def glu(z, w, tm, tn):
    t, k = z.shape
    n = w.shape[1] // 2
    nj = n // tn
    return pl.pallas_call(
        _glu_body,
        grid=(t // tm, nj),
        in_specs=[pl.BlockSpec((tm, k), lambda i, j: (i, 0)),
                  pl.BlockSpec((k, tn), lambda i, j: (0, j)),
                  pl.BlockSpec((k, tn), lambda i, j: (0, nj + j))],
        out_specs=pl.BlockSpec((tm, tn), lambda i, j: (i, j)),
        out_shape=SDS((t, n), BF16),
        compiler_params=_params(("parallel", "arbitrary")),
        name="glu",
    )(z, w, w)


def _mix_body(ya_ref, o0_ref, o1_ref, o2_ref, l0_ref, l1_ref, l2_ref, yc_ref,
              wa_ref, wb_ref, wc_ref, ga_ref, gb_ref, gc_ref, out_ref, yb_ref, wt_ref, acc_ref):
    tm = ya_ref.shape[0]
    E = ATTN_HEAD_DIM

    @pl.when(pl.program_id(1) == 0)
    def _():
        o_refs = (o0_ref, o1_ref, o2_ref)
        l_refs = (l0_ref, l1_ref, l2_ref)
        for g, (_, r) in enumerate(ATTN_GROUPS):
            for c in range(r):
                wt_ref[g, pl.ds(c, tm // r, stride=r), :] = l_refs[g][0, c]
        l0, l1, l2 = wt_ref[0], wt_ref[1], wt_ref[2]
        m = jnp.maximum(jnp.maximum(l0, l1), l2)
        e0, e1, e2 = jnp.exp(l0 - m), jnp.exp(l1 - m), jnp.exp(l2 - m)
        inv = 1.0 / (e0 + e1 + e2)
        wt_ref[0] = e0 * inv
        wt_ref[1] = e1 * inv
        wt_ref[2] = e2 * inv
        order = sorted(range(len(ATTN_GROUPS)), key=lambda g: -ATTN_GROUPS[g][1])
        for pos, g in enumerate(order[:-1]):
            r = ATTN_GROUPS[g][1]
            for c in range(r):
                rows = pl.ds(c, tm // r, stride=r)
                w = wt_ref[g, rows, :]
                for h in range(ATTN_HEADS):
                    part = w[:, h:h + 1] * o_refs[g][0, c, :, h * E:(h + 1) * E].astype(F32)
                    if pos == 0:
                        acc_ref[h, rows, :] = part
                    else:
                        acc_ref[h, rows, :] += part
        g = order[-1]
        assert ATTN_GROUPS[g][1] == 1
        w = wt_ref[g]
        for h in range(ATTN_HEADS):
            sl = slice(h * E, (h + 1) * E)
            yb = acc_ref[h] + w[:, h:h + 1] * o_refs[g][0, 0, :, sl].astype(F32)
            yb_ref[:, sl] = yb.astype(yb_ref.dtype)

    tn = out_ref.shape[1]
    cols = pl.ds(pl.multiple_of(pl.program_id(1) * tn, tn), tn)
    mix = (_sigmoid(ga_ref[...].astype(F32)) * _dot(ya_ref[...], wa_ref[:, cols])
           + _sigmoid(gb_ref[...].astype(F32)) * _dot(yb_ref[...], wb_ref[:, cols])
           + _sigmoid(gc_ref[...].astype(F32)) * _dot(yc_ref[...], wc_ref[:, cols]))
    out_ref[...] = mix.astype(out_ref.dtype)


def gated_mix(ya, outs, lses, yc, wa, wb, wc, proj, seq, tm, tn):
    t, kw = ya.shape
    d = wa.shape[1]
    go = COL_G // tn
    nbt = seq // tm
    row = lambda w: pl.BlockSpec((tm, w), lambda i, j: (i, 0))
    grp = lambda r, w: pl.BlockSpec((1, r, tm // r, w), lambda i, j: (i // nbt, 0, i % nbt, 0))
    wsp = pl.BlockSpec((kw, d), lambda i, j: (0, 0), pipeline_mode=pl.Buffered(1))
    gate = lambda o: pl.BlockSpec((tm, tn), lambda i, j: (i, go + o * (d // tn) + j))
    dils = [r for _, r in ATTN_GROUPS]
    return pl.pallas_call(
        _mix_body,
        grid=(t // tm, d // tn),
        in_specs=[row(kw)] + [grp(r, kw) for r in dils] + [grp(r, LANES) for r in dils] + [row(kw)]
        + [wsp, wsp, wsp, gate(0), gate(1), gate(2)],
        out_specs=pl.BlockSpec((tm, tn), lambda i, j: (i, j)),
        out_shape=SDS((t, d), BF16),
        scratch_shapes=[pltpu.VMEM((tm, kw), BF16), pltpu.VMEM((len(dils), tm, LANES), F32),
                        pltpu.VMEM((ATTN_HEADS, tm, LANES), F32)],
        compiler_params=_params(("parallel", "arbitrary")),
        name="gated_mix",
    )(ya, *outs, *lses, yc, wa, wb, wc, proj, proj, proj)


def _out_ffn_body(final, x_ref, m_ref, wo_ref, g_ref, w1_ref, w2_ref, fg_ref, o_ref, xn_ref):
    j = pl.program_id(1)

    @pl.when(j == 0)
    def _():
        x1 = x_ref[...] + _dot(m_ref[...], wo_ref[...])
        xn_ref[...] = _rms(x1, g_ref[...]).astype(BF16)
        o_ref[...] = x1

    hid = jnp.maximum(_dot(xn_ref[...], w1_ref[0]), 0.0)
    o_ref[...] += _dot((hid * hid).astype(BF16), w2_ref[...])

    if final:
        @pl.when(j == pl.num_programs(1) - 1)
        def _():
            o_ref[...] = _rms(o_ref[...], fg_ref[...])


def out_ffn(x2d, mix, w_out, g, w1, w2, final_g, final, tm):
    t, d = x2d.shape
    hdim, th = w1.shape[0] * w1.shape[2], w1.shape[2]
    const = lambda shape: pl.BlockSpec(shape, lambda i, j: (0, 0))
    return pl.pallas_call(
        functools.partial(_out_ffn_body, final),
        grid=(t // tm, hdim // th),
        in_specs=[pl.BlockSpec((tm, d), lambda i, j: (i, 0)),
                  pl.BlockSpec((tm, d), lambda i, j: (i, 0)),
                  pl.BlockSpec((d, d), lambda i, j: (0, 0), pipeline_mode=pl.Buffered(1)),
                  const((1, d)),
                  pl.BlockSpec((1, d, th), lambda i, j: (j, 0, 0)),
                  pl.BlockSpec((th, d), lambda i, j: (j, 0)),
                  const((1, d))],
        out_specs=pl.BlockSpec((tm, d), lambda i, j: (i, 0)),
        out_shape=SDS((t, d), F32),
        scratch_shapes=[pltpu.VMEM((tm, d), BF16)],
        compiler_params=_params(("parallel", "arbitrary")),
        name="out_ffn",
    )(x2d, mix, w_out, g, w1, w2, final_g)


def _tile(n, want):
    t = min(n, want)
    assert n % t == 0, (n, want)
    return t


def _col_tiles(w, tn):
    k, n = w.shape
    return w.reshape(k, n // tn, tn).transpose(1, 0, 2)


def _layer(x2d, bias, batch, seq, p, final_g, final):
    t, d = x2d.shape
    tm = _tile(seq, 1024)
    n_gate0 = 4 * MLSTM_WIDTH
    n_att0 = n_gate0 + 2 * MLSTM_HEADS
    n_att = 3 * ATTN_HEADS_TOTAL * ATTN_HEAD_DIM
    n_groups = len(ATTN_GROUPS)
    w_in = p["w_in"]
    w_att = w_in[:, n_att0:n_att0 + n_att].reshape(d, 3, n_groups, ATTN_WIDTH)
    w_att = w_att.transpose(0, 2, 1, 3).reshape(d, n_att)
    n_u0 = n_att0 + n_att
    w_main = jnp.concatenate([w_in[:, :COL_U], w_in[:, n_u0:n_u0 + S5_WIDTH], w_in[:, COL_U:n_gate0],
                              w_in[:, n_u0 + S5_WIDTH:], w_att], axis=1).astype(BF16)
    w_if = jnp.pad(w_in[:, n_gate0:n_att0], ((0, 0), (0, N_GATE_PAD - 2 * MLSTM_HEADS))).astype(BF16)
    proj, gif, gif_t, *qkvs = in_proj(x2d, p["norm1_g"].reshape(1, d), _col_tiles(w_main, ATTN_WIDTH), w_if,
                                      batch, seq, tm)

    bif = jnp.concatenate([p["b_igate"], p["b_fgate"]]).astype(F32)
    bcol = jnp.pad(bif, (0, N_GATE_PAD - bif.shape[0])).reshape(1, N_GATE_PAD)
    ya = mlstm(proj, gif, gif_t, bcol, bif.reshape(8, 1), p["conv_w"], p["conv_b"].reshape(1, -1),
               p["mh_norm_g"].reshape(1, -1), batch, seq)

    outs, lses = zip(*[attn_group(qkvs[g], bias, g) for g in range(n_groups)])

    tmat, wmat, vmat, coef = s5_prep(p["lam_re"], p["lam_im"], p["log_dt"], p["b_re"], p["b_im"],
                                     p["c_re"], p["c_im"], p["d_skip"])
    z = s5_scan(proj, tmat, wmat, vmat, coef, batch, seq)
    yc = glu(z, p["w_glu"].astype(BF16), tm, 512)

    mix = gated_mix(ya, outs, lses, yc, p["w_br_a"].astype(BF16), p["w_br_b"].astype(BF16),
                    p["w_br_c"].astype(BF16), proj, seq, _tile(seq, 512), 1024)
    return out_ffn(x2d, mix, p["w_out"].astype(BF16), p["norm2_g"].reshape(1, d),
                   _col_tiles(p["w_ff1"].astype(BF16), 1024), p["w_ff2"].astype(BF16), final_g.reshape(1, d),
                   final, _tile(t, 512))


_PER_LAYER = ("norm1_g", "w_in", "conv_w", "conv_b", "b_igate", "b_fgate", "mh_norm_g", "lam_re", "lam_im",
              "log_dt", "b_re", "b_im", "c_re", "c_im", "d_skip", "w_glu", "w_br_a", "w_br_b", "w_br_c",
              "w_out", "norm2_g", "w_ff1", "w_ff2")


def kernel(x, norm1_g, w_in, conv_w, conv_b, b_igate, b_fgate, mh_norm_g, rel_bias, lam_re, lam_im, log_dt,
           b_re, b_im, c_re, c_im, d_skip, w_glu, w_br_a, w_br_b, w_br_c, w_out, norm2_g, w_ff1, w_ff2,
           final_g):
    stacked = dict(norm1_g=norm1_g, w_in=w_in, conv_w=conv_w, conv_b=conv_b, b_igate=b_igate,
                   b_fgate=b_fgate, mh_norm_g=mh_norm_g, lam_re=lam_re, lam_im=lam_im, log_dt=log_dt,
                   b_re=b_re, b_im=b_im, c_re=c_re, c_im=c_im, d_skip=d_skip, w_glu=w_glu, w_br_a=w_br_a,
                   w_br_b=w_br_b, w_br_c=w_br_c, w_out=w_out, norm2_g=norm2_g, w_ff1=w_ff1, w_ff2=w_ff2)
    batch, seq, d = x.shape
    x2d = x.astype(F32).reshape(batch * seq, d)
    bias = attn_bias(rel_bias.astype(F32))
    depth = w_in.shape[0]
    for l in range(depth):
        x2d = _layer(x2d, bias, batch, seq, {k: stacked[k][l] for k in _PER_LAYER}, final_g, l == depth - 1)
    return x2d.reshape(batch, seq, d).astype(x.dtype)
```

```python
import functools
import math

import jax
import jax.numpy as jnp
from jax import lax
from jax.experimental import pallas as pl
from jax.experimental.pallas import tpu as pltpu

F32 = jnp.float32
BF16 = jnp.bfloat16
SDS = jax.ShapeDtypeStruct

NORM_EPS = 1e-6
NEG = -1e30

MLSTM_HEADS = 4
MLSTM_HEAD_DIM = 256
MLSTM_WIDTH = MLSTM_HEADS * MLSTM_HEAD_DIM
CONV_WIDTH = 4
ATTN_GROUPS = ((128, 1), (512, 4), (2048, 16))
ATTN_HEADS = 8
ATTN_HEAD_DIM = 128
ATTN_WIDTH = ATTN_HEADS * ATTN_HEAD_DIM
ATTN_HEADS_TOTAL = len(ATTN_GROUPS) * ATTN_HEADS
ATTN_BLOCK = 128
ATTN_Q_BLOCKS = 4
REL_BUCKETS = 32
REL_MAX_DISTANCE = 2048
S5_WIDTH = 1024
S5_GROUP = 16
S5_GROUPS = S5_WIDTH // S5_GROUP
S5_STATE = 64
S5_TOK = 16
N_GATE_PAD = 128

COL_U = 3 * MLSTM_WIDTH
COL_O = COL_U + S5_WIDTH
COL_G = COL_O + MLSTM_WIDTH
D_MODEL = 2048
N_MAIN = COL_G + 3 * D_MODEL
LANES = 128

MLSTM_CHUNK = 256
VMEM_LIMIT = 56 * 2**20


def _params(sem):
    return pltpu.CompilerParams(dimension_semantics=sem, vmem_limit_bytes=VMEM_LIMIT)


def _sigmoid(x):
    return 1.0 / (1.0 + jnp.exp(-x))


def _log_sigmoid(x):
    return jnp.minimum(x, 0.0) - jnp.log(1.0 + jnp.exp(-jnp.abs(x)))


def _rms(x, g):
    ms = jnp.mean(x * x, axis=-1, keepdims=True)
    return x * lax.rsqrt(ms + NORM_EPS) * g


def _dot(a, b):
    return jnp.dot(a, b, preferred_element_type=F32)


def _dot_nt(a, b):
    return lax.dot_general(a, b, (((1,), (1,)), ((), ())), preferred_element_type=F32)


def _dot_tn(a, b):
    return lax.dot_general(a, b, (((0,), (0,)), ((), ())), preferred_element_type=F32)


def _inproj_body(ns, nm, x_ref, g_ref, w_ref, wif_ref, o_ref, oif_ref, oift_ref, a0_ref, a1_ref, a2_ref,
                 xn_ref, acc_ref, acc2_ref):
    j = pl.program_id(1)
    tm = x_ref.shape[0]
    tn = w_ref.shape[1]

    @pl.when(j == 0)
    def _():
        xn = _rms(x_ref[...], g_ref[...]).astype(BF16)
        xn_ref[...] = xn
        gates = _dot(xn, wif_ref[...])
        oif_ref[...] = gates
        oift_ref[...] = jnp.transpose(gates)[:8]

    @pl.when(jnp.logical_and(j < nm, j != ns))
    def _():
        o_ref[...] = _dot(xn_ref[...], w_ref[...]).astype(o_ref.dtype)

    @pl.when(j == ns)
    def _():
        o_ref[...] = _sigmoid(_dot(xn_ref[...], w_ref[...])).astype(o_ref.dtype)

    for g, a_ref in enumerate((a0_ref, a1_ref, a2_ref)):
        r = ATTN_GROUPS[g][1]
        lo = nm + 3 * g

        @pl.when(jnp.logical_and(j >= lo, j < lo + 3))
        def _(a_ref=a_ref, r=r):
            res = _dot(xn_ref[...], w_ref[...])
            if r == 1:
                a_ref[0, 0] = res.astype(a_ref.dtype)
                return
            ns_ = acc_ref.shape[0]
            for base in range(0, tn // LANES, ns_):
                lanes = [slice((base + s) * LANES, (base + s + 1) * LANES) for s in range(ns_)]
                for s in range(ns_):
                    acc_ref[s] = res[:, lanes[s]]
                if r <= 4:
                    for c in range(r):
                        for s in range(ns_):
                            a_ref[0, c, :, lanes[s]] = acc_ref[s, pl.ds(c, tm // r, stride=r), :].astype(a_ref.dtype)
                    continue
                q, r2 = 4, r // 4
                for s in range(ns_):
                    for c0 in range(q):
                        acc2_ref[s, c0 * (tm // q):(c0 + 1) * (tm // q), :] = (
                            acc_ref[s, pl.ds(c0, tm // q, stride=q), :])
                for c0 in range(q):
                    for c1 in range(r2):
                        for s in range(ns_):
                            rows = pl.ds(c0 * (tm // q) + c1, tm // r, stride=r2)
                            a_ref[0, c1 * q + c0, :, lanes[s]] = acc2_ref[s, rows, :].astype(a_ref.dtype)


def in_proj(x2d, g, w, wif, batch, seq, tm):
    t, d = x2d.shape
    tn = ATTN_WIDTH
    nm = N_MAIN // tn
    nbt = seq // tm
    n_groups = len(ATTN_GROUPS)

    def a_spec(gi):
        r = ATTN_GROUPS[gi][1]
        return pl.BlockSpec((1, r, tm // r, tn),
                            lambda i, j: (i // nbt, 0, i % nbt, jnp.clip(j - nm - 3 * gi, 0, 2)))

    return pl.pallas_call(
        functools.partial(_inproj_body, COL_O // tn, nm),
        grid=(t // tm, nm + 3 * n_groups),
        in_specs=[pl.BlockSpec((tm, d), lambda i, j: (i, 0)),
                  pl.BlockSpec((1, d), lambda i, j: (0, 0)),
                  pl.BlockSpec((d, tn), lambda i, j: (0, j)),
                  pl.BlockSpec((d, N_GATE_PAD), lambda i, j: (0, 0))],
        out_specs=[pl.BlockSpec((tm, tn), lambda i, j: (i, jnp.minimum(j, nm - 1))),
                   pl.BlockSpec((tm, N_GATE_PAD), lambda i, j: (i, 0)),
                   pl.BlockSpec((8, tm), lambda i, j: (0, i))] + [a_spec(gi) for gi in range(n_groups)],
        out_shape=[SDS((t, N_MAIN), BF16), SDS((t, N_GATE_PAD), F32), SDS((8, t), F32)]
        + [SDS((batch, r, seq // r, 3 * tn), BF16) for _, r in ATTN_GROUPS],
        scratch_shapes=[pltpu.VMEM((tm, d), BF16), pltpu.VMEM((tn // LANES // 2, tm, LANES), F32),
                        pltpu.VMEM((tn // LANES // 2, tm, LANES), F32)],
        compiler_params=_params(("parallel", "arbitrary")),
        name="in_proj",
    )(x2d, g, w, wif)


def _mlstm_body(q_ref, k_ref, v_ref, og_ref, gcol_ref, grow_ref, bcol_ref, brow_ref,
                cw_ref, cb_ref, ng_ref, y_ref, ct_ref, m_ref, tail_ref):
    c = pl.program_id(1)
    L = q_ref.shape[0]
    E = MLSTM_HEAD_DIM
    H = MLSTM_HEADS

    @pl.when(c == 0)
    def _():
        ct_ref[...] = jnp.zeros_like(ct_ref)
        m_ref[...] = jnp.zeros_like(m_ref)
        tail_ref[...] = jnp.zeros_like(tail_ref)

    row8 = lax.broadcasted_iota(jnp.int32, (8, E), 0)
    tt = lax.broadcasted_iota(jnp.int32, (L, L), 0)
    ss = lax.broadcasted_iota(jnp.int32, (L, L), 1)
    causal = ss <= tt
    gc = gcol_ref[...] + bcol_ref[...]
    gr = grow_ref[...] + brow_ref[...]

    shifts = [jnp.where(tt - ss == d, 1.0, 0.0).astype(BF16) for d in range(1, CONV_WIDTH)]

    def conv_silu(x_ref, h, slot):
        xb = x_ref[:, h * E:(h + 1) * E]
        x = xb.astype(F32)
        tail = tail_ref[slot]
        w = cw_ref[:, slot * E:(slot + 1) * E]
        acc = cb_ref[:, slot * E:(slot + 1) * E] + w[CONV_WIDTH - 1:CONV_WIDTH] * x
        head = jnp.zeros((8, E), F32)
        for d in range(1, CONV_WIDTH):
            wd = w[CONV_WIDTH - 1 - d:CONV_WIDTH - d]
            acc = acc + wd * _dot(shifts[d - 1], xb)
            head = head + wd * jnp.where(row8 < d, pltpu.roll(tail, d, 0), 0.0)
        acc = jnp.concatenate([acc[:8] + head, acc[8:]], axis=0)
        tail_ref[slot] = x[L - 8:]
        return acc * _sigmoid(acc)

    for h in range(H):
        hs = slice(h * E, (h + 1) * E)
        q = conv_silu(q_ref, h, h)
        k = conv_silu(k_ref, h, H + h) * (E ** -0.5)

        ig_col = gc[:, h:h + 1]
        lf_col = _log_sigmoid(gc[:, H + h:H + h + 1])
        ig_row = gr[h:h + 1, :]
        lf_row = _log_sigmoid(gr[H + h:H + h + 1, :])
        bcum_col = jnp.sum(jnp.where(causal, lf_row, 0.0), axis=1, keepdims=True)
        bcum_row = jnp.sum(jnp.where(tt <= ss, lf_col, 0.0), axis=0, keepdims=True)
        a_row = ig_row - bcum_row
        a_col = ig_col - bcum_col

        m_prev = m_ref[h]
        amat = jnp.where(causal, a_row, NEG)
        mrow = jnp.maximum(m_prev, jnp.max(amat, axis=1, keepdims=True))
        w_intra = jnp.exp(amat - mrow)
        w_inter = jnp.exp(m_prev - mrow)

        qb = q.astype(BF16)
        kb = k.astype(BF16)
        vaug = jnp.concatenate([v_ref[:, hs], jnp.ones((L, LANES), BF16)], axis=1)
        s = _dot_nt(qb, kb) * w_intra
        ct = ct_ref[h]
        num_aug = _dot(s.astype(BF16), vaug) + w_inter * _dot(qb, ct.astype(BF16))
        num = num_aug[:, :E]
        den = num_aug[:, E:E + 1]
        m_t = bcum_col + mrow
        hout = num / jnp.maximum(jnp.abs(den), jnp.exp(-m_t))
        hn = _rms(hout, ng_ref[:, hs])
        y_ref[:, hs] = (og_ref[:, hs].astype(F32) * hn).astype(y_ref.dtype)

        b_last = jnp.sum(lf_col, axis=0, keepdims=True)
        g_col = b_last + a_col
        m_new = jnp.maximum(b_last + m_prev, jnp.max(g_col, axis=0, keepdims=True))
        w_s = jnp.exp(g_col - m_new)
        decay = jnp.exp(b_last + m_prev - m_new)
        kw = (k * w_s).astype(BF16)
        ct_ref[h] = decay * ct + _dot_tn(kw, vaug)
        m_ref[h] = m_new


def mlstm(proj, gif, gif_t, bcol, brow, conv_w, conv_b, ng, batch, seq):
    L = min(MLSTM_CHUNK, seq)
    E = MLSTM_HEAD_DIM
    W = MLSTM_WIDTH
    nc = seq // L
    H = MLSTM_HEADS
    t = batch * seq
    row = lambda b, c: b * nc + c
    return pl.pallas_call(
        _mlstm_body,
        grid=(batch, nc),
        in_specs=[pl.BlockSpec((L, W), lambda b, c: (row(b, c), 0)),
                  pl.BlockSpec((L, W), lambda b, c: (row(b, c), 1)),
                  pl.BlockSpec((L, W), lambda b, c: (row(b, c), 2)),
                  pl.BlockSpec((L, W), lambda b, c: (row(b, c), COL_O // W)),
                  pl.BlockSpec((L, N_GATE_PAD), lambda b, c: (row(b, c), 0)),
                  pl.BlockSpec((8, L), lambda b, c: (0, row(b, c))),
                  pl.BlockSpec((1, N_GATE_PAD), lambda b, c: (0, 0)),
                  pl.BlockSpec((8, 1), lambda b, c: (0, 0)),
                  pl.BlockSpec((CONV_WIDTH, 2 * W), lambda b, c: (0, 0)),
                  pl.BlockSpec((1, 2 * W), lambda b, c: (0, 0)),
                  pl.BlockSpec((1, W), lambda b, c: (0, 0))],
        out_specs=pl.BlockSpec((L, W), lambda b, c: (row(b, c), 0)),
        out_shape=SDS((t, W), BF16),
        scratch_shapes=[pltpu.VMEM((H, E, E + LANES), F32), pltpu.VMEM((H, 1, 1), F32),
                        pltpu.VMEM((2 * H, 8, E), F32)],
        compiler_params=_params(("parallel", "arbitrary")),
        name="mlstm",
    )(proj, proj, proj, proj, gif, gif_t, bcol, brow, conv_w, conv_b, ng)


def _bias_body(table_ref, o_ref):
    h = pl.program_id(0)
    dil = jnp.where(h < ATTN_HEADS, ATTN_GROUPS[0][1],
                    jnp.where(h < 2 * ATTN_HEADS, ATTN_GROUPS[1][1], ATTN_GROUPS[2][1]))
    shape = (ATTN_BLOCK, 2 * ATTN_BLOCK)
    i = lax.broadcasted_iota(jnp.int32, shape, 0)
    j = lax.broadcasted_iota(jnp.int32, shape, 1)
    rel = ATTN_BLOCK + i - j
    dist = jnp.maximum(rel, 0) * dil
    max_exact = REL_BUCKETS // 2
    nf = jnp.maximum(dist, max_exact).astype(F32)
    large = max_exact + (jnp.log(nf / max_exact) / math.log(REL_MAX_DISTANCE / max_exact)
                         * (REL_BUCKETS - max_exact)).astype(jnp.int32)
    large = jnp.minimum(large, REL_BUCKETS - 1)
    bucket = jnp.where(dist < max_exact, dist, large)
    acc = jnp.zeros(shape, F32)
    for b in range(REL_BUCKETS):
        acc = jnp.where(bucket == b, table_ref[b, h], acc)
    o_ref[0] = jnp.where(rel >= 0, jnp.where(rel <= ATTN_BLOCK, acc, NEG), NEG)


def attn_bias(rel_bias):
    return pl.pallas_call(
        _bias_body,
        grid=(ATTN_HEADS_TOTAL,),
        in_specs=[pl.BlockSpec(memory_space=pltpu.SMEM)],
        out_specs=pl.BlockSpec((1, ATTN_BLOCK, 2 * ATTN_BLOCK), lambda h: (h, 0, 0)),
        out_shape=SDS((ATTN_HEADS_TOTAL, ATTN_BLOCK, 2 * ATTN_BLOCK), F32),
        compiler_params=_params(("arbitrary",)),
        name="attn_bias",
    )(rel_bias)


def _attn_body(q_ref, kp_ref, kc_ref, vp_ref, vc_ref, bias_ref, o_ref, lse_ref):
    n = pl.program_id(2)
    B = ATTN_BLOCK
    E = ATTN_HEAD_DIM
    H = ATTN_HEADS
    NQ = q_ref.shape[2] // B
    scale = E ** -0.5
    hs = [slice(h * E, (h + 1) * E) for h in range(H)]

    def keys(cur_ref, prev_ref, i, sl):
        if i == 0:
            return jnp.concatenate([prev_ref[0, 0, :, sl], cur_ref[0, 0, :B, sl]], axis=0)
        return cur_ref[0, 0, (i - 1) * B:(i + 1) * B, sl]

    s = jnp.concatenate([_dot_nt(q_ref[0, 0, i * B:(i + 1) * B, sl], keys(kc_ref, kp_ref, i, sl))
                         for i in range(NQ) for sl in hs], axis=0)
    bias = bias_ref[...].reshape(H * B, 2 * B)
    key = lax.broadcasted_iota(jnp.int32, (1, 2 * B), 1)
    first = bias + jnp.where(key < B, jnp.where(n > 0, 0.0, NEG), 0.0)
    s = s * scale + jnp.concatenate([first] + [bias] * (NQ - 1), axis=0)
    m = jnp.max(s, axis=1, keepdims=True)
    p = jnp.exp(s - m)
    den = jnp.sum(p, axis=1, keepdims=True)
    inv = 1.0 / den
    pb = p.astype(BF16)
    lse_all = m + jnp.log(den)
    lane = lax.broadcasted_iota(jnp.int32, (B, LANES), 1)
    for i in range(NQ):
        lse = jnp.zeros((B, LANES), F32)
        for h, sl in enumerate(hs):
            rows = slice((i * H + h) * B, (i * H + h + 1) * B)
            o = _dot(pb[rows], keys(vc_ref, vp_ref, i, sl)) * inv[rows]
            o_ref[0, 0, i * B:(i + 1) * B, sl] = o.astype(o_ref.dtype)
            lse = jnp.where(lane == h, lse_all[rows], lse)
        lse_ref[0, 0, i * B:(i + 1) * B, :] = lse


def attn_group(qkv, bias, g):
    batch, r, l, _ = qkv.shape
    W = ATTN_WIDTH
    nq = min(ATTN_Q_BLOCKS, l // ATTN_BLOCK)
    rows = nq * ATTN_BLOCK
    cur = (1, 1, rows, W)
    one = (1, 1, ATTN_BLOCK, W)
    prev = lambda n: jnp.maximum(n * nq - 1, 0)
    return pl.pallas_call(
        _attn_body,
        grid=(batch, r, l // rows),
        in_specs=[pl.BlockSpec(cur, lambda b, c, n: (b, c, n, 0)),
                  pl.BlockSpec(one, lambda b, c, n: (b, c, prev(n), 1)),
                  pl.BlockSpec(cur, lambda b, c, n: (b, c, n, 1)),
                  pl.BlockSpec(one, lambda b, c, n: (b, c, prev(n), 2)),
                  pl.BlockSpec(cur, lambda b, c, n: (b, c, n, 2)),
                  pl.BlockSpec((ATTN_HEADS, ATTN_BLOCK, 2 * ATTN_BLOCK), lambda b, c, n: (g, 0, 0))],
        out_specs=[pl.BlockSpec(cur, lambda b, c, n: (b, c, n, 0)),
                   pl.BlockSpec((1, 1, rows, LANES), lambda b, c, n: (b, c, n, 0))],
        out_shape=[SDS((batch, r, l, W), BF16), SDS((batch, r, l, LANES), F32)],
        compiler_params=_params(("parallel", "parallel", "arbitrary")),
        name=f"attn_g{g}",
    )(qkv, qkv, qkv, qkv, qkv, bias)


def _s5_prep_body(lam_ref, lamc_ref, dt_ref, bt_r_ref, bt_i_ref, ce_r_ref, ce_i_ref, d_ref,
                  t_ref, w_ref, v_ref, coef_ref):
    P = S5_STATE
    lr = lam_ref[0, 0:1, :]
    li = lam_ref[0, 1:2, :]
    dt = jnp.exp(dt_ref[0])

    def apow(e, lr_, li_):
        mag = jnp.exp(lr_ * dt * e)
        ang = li_ * dt * e
        return mag * jnp.cos(ang), mag * jnp.sin(ang)

    one = jnp.ones((1, 1), F32)
    ar, ai = apow(one, lr, li)
    nr = ar - 1.0
    den = lr * lr + li * li
    f_re = (nr * lr + ai * li) / den
    f_im = (ai * lr - nr * li) / den
    bt_r = bt_r_ref[0]
    bt_i = bt_i_ref[0]
    bb_r = f_re * bt_r - f_im * bt_i
    bb_i = f_re * bt_i + f_im * bt_r

    lrc = lamc_ref[0, :, 0:1]
    lic = lamc_ref[0, :, 1:2]
    lag = (lax.broadcasted_iota(jnp.int32, (P, 256), 1) // S5_GROUP).astype(F32)
    adr, adi = apow(lag, lrc, lic)
    ce_r = ce_r_ref[0]
    ce_i = ce_i_ref[0]
    ca_r = ce_r * adr - ce_i * adi
    ca_i = ce_r * adi + ce_i * adr
    hp = lax.Precision.HIGHEST
    ks = (jnp.dot(bb_r[:, :P], ca_r, precision=hp, preferred_element_type=F32)
          - jnp.dot(bb_i[:, :P], ca_i, precision=hp, preferred_element_type=F32))
    si = lax.broadcasted_iota(jnp.int32, (S5_GROUP, 256), 0)
    lj = lax.broadcasted_iota(jnp.int32, (S5_GROUP, 256), 1)
    ks = ks + jnp.where(si == lj, d_ref[0], 0.0)
    for s in range(S5_TOK):
        sh = s * S5_GROUP
        blk = ks if s == 0 else jnp.where(lj >= sh, pltpu.roll(ks, sh, 1), 0.0)
        t_ref[0, sh:sh + S5_GROUP, :] = blk.astype(t_ref.dtype)

    half = pl.program_id(0) % 2
    lane128 = lax.broadcasted_iota(jnp.int32, (1, 128), 1)
    mine = (lane128 // P) == half

    for s in range(S5_TOK):
        pr, pi = apow(float(S5_TOK - 1 - s) * one, lr, li)
        rows = slice(s * S5_GROUP, (s + 1) * S5_GROUP)
        w_ref[0, rows, 0:128] = jnp.where(mine, bb_r * pr - bb_i * pi, 0.0).astype(w_ref.dtype)
        w_ref[0, rows, 128:256] = jnp.where(mine, bb_r * pi + bb_i * pr, 0.0).astype(w_ref.dtype)

    adr1, adi1 = apow(lag + 1.0, lrc, lic)
    v_ref[0] = jnp.zeros(v_ref.shape[1:], v_ref.dtype)
    row0 = pl.multiple_of(half * P, P)
    v_ref[0, pl.ds(row0, P), :] = (ce_r * adr1 - ce_i * adi1).astype(v_ref.dtype)
    row1 = pl.multiple_of(2 * P + half * P, P)
    v_ref[0, pl.ds(row1, P), :] = (-(ce_r * adi1 + ce_i * adr1)).astype(v_ref.dtype)

    ek = jnp.left_shift(S5_TOK, lax.broadcasted_iota(jnp.int32, (16, 1), 0)).astype(F32)
    cr, ci = apow(ek, lr, li)
    coef_ref[0, 0:16, :] = jnp.where(mine, cr, 0.0)
    coef_ref[0, 16:32, :] = jnp.where(mine, ci, 0.0)


def s5_prep(lam_re, lam_im, log_dt, b_re, b_im, c_re, c_im, d_skip):
    G, P, I = b_re.shape
    lam = jnp.stack([jnp.tile(lam_re, (1, 2)), jnp.tile(lam_im, (1, 2))], axis=1)
    lam = jnp.pad(lam, ((0, 0), (0, 6), (0, 0)))
    lamc = jnp.stack([lam_re, lam_im], axis=2)
    dt = log_dt.reshape(G, 1, 1)
    bt_r = jnp.tile(jnp.swapaxes(b_re, 1, 2), (1, 1, 2))
    bt_i = jnp.tile(jnp.swapaxes(b_im, 1, 2), (1, 1, 2))
    ce_r = jnp.tile(jnp.swapaxes(c_re, 1, 2), (1, 1, S5_TOK))
    ce_i = jnp.tile(jnp.swapaxes(c_im, 1, 2), (1, 1, S5_TOK))
    d = jnp.pad(d_skip, ((0, 0), (0, 256 - I))).reshape(G, 1, 256)
    blk = lambda *s: pl.BlockSpec((1,) + s, lambda g: (g, 0, 0))
    return pl.pallas_call(
        _s5_prep_body,
        grid=(G,),
        in_specs=[blk(8, 128), blk(P, 2), blk(1, 1), blk(16, 128), blk(16, 128), blk(P, 256), blk(P, 256),
                  blk(1, 256)],
        out_specs=[blk(256, 256), blk(256, 256), blk(256, 256), blk(32, 128)],
        out_shape=[SDS((G, 256, 256), BF16), SDS((G, 256, 256), BF16), SDS((G, 256, 256), BF16),
                   SDS((G, 32, 128), F32)],
        compiler_params=_params(("arbitrary",)),
        name="s5_prep",
    )(lam, lamc, dt, bt_r, bt_i, ce_r, ce_i, d)


def _gelu_tanh(x):
    return 0.5 * x * (1.0 + jnp.tanh(math.sqrt(2.0 / math.pi) * (x + 0.044715 * (x * x * x))))


def _chunk_transpose(arrs, chunk):
    a = list(arrs)
    n = len(a)
    d = n // 2
    while d >= 1:
        bit = (chunk & d) != 0
        nxt = list(a)
        for i in range(n):
            if i & d == 0:
                lo, hi = a[i], a[i + d]
                nxt[i] = jnp.where(bit, pltpu.roll(hi, d * S5_GROUP, 1), lo)
                nxt[i + d] = jnp.where(bit, hi, pltpu.roll(lo, LANES - d * S5_GROUP, 1))
        a = nxt
        d //= 2
    return a


def _s5_body(u_ref, t_ref, w_ref, v_ref, coef_ref, z_ref, nat_ref, uf_ref, zs_ref):
    S = u_ref.shape[0]
    R = S // S5_TOK
    GL = LANES // S5_GROUP
    RC = min(R, 256)
    chunk = lax.broadcasted_iota(jnp.int32, (RC, LANES), 1) // S5_GROUP

    nat_ref[...] = u_ref[...].astype(F32)

    def to_flat(rc, carry):
        r0 = pl.multiple_of(rc * RC, RC)
        for hf in range(S5_TOK // GL):
            arrs = [nat_ref[pl.ds(r0 * S5_TOK + hf * GL + k, RC, stride=S5_TOK), :] for k in range(GL)]
            for gl, a in enumerate(_chunk_transpose(arrs, chunk)):
                uf_ref[gl, pl.ds(r0, RC), hf * LANES:(hf + 1) * LANES] = a.astype(uf_ref.dtype)
        return carry

    lax.fori_loop(0, R // RC, to_flat, 0)

    row = lax.broadcasted_iota(jnp.int32, (R, LANES), 0)

    def shift_down(a, sh):
        if sh % 8 == 0:
            return jnp.concatenate([jnp.zeros((sh, LANES), F32), a[:R - sh]], axis=0)
        return jnp.where(row < sh, 0.0, pltpu.roll(a, sh, 0))

    def pair(p, carry):
        g0 = 2 * p
        g1 = g0 + 1
        u0 = uf_ref[g0]
        u1 = uf_ref[g1]
        s2 = _dot(u0, w_ref[g0]) + _dot(u1, w_ref[g1])
        xr = s2[:, :LANES]
        xi = s2[:, LANES:]
        k = 0
        while (1 << k) < R:
            cr = coef_ref[g0, k:k + 1, :] + coef_ref[g1, k:k + 1, :]
            ci = coef_ref[g0, 16 + k:17 + k, :] + coef_ref[g1, 16 + k:17 + k, :]
            xrs = shift_down(xr, 1 << k)
            xis = shift_down(xi, 1 << k)
            xr, xi = xr + cr * xrs - ci * xis, xi + cr * xis + ci * xrs
            k += 1
        xp = jnp.concatenate([shift_down(xr, 1), shift_down(xi, 1)], axis=1).astype(BF16)
        zs_ref[g0] = _gelu_tanh(_dot(u0, t_ref[g0]) + _dot(xp, v_ref[g0]))
        zs_ref[g1] = _gelu_tanh(_dot(u1, t_ref[g1]) + _dot(xp, v_ref[g1]))
        return carry

    lax.fori_loop(0, GL // 2, pair, 0)

    def to_nat(rc, carry):
        r0 = pl.multiple_of(rc * RC, RC)
        for hf in range(S5_TOK // GL):
            arrs = [zs_ref[gl, pl.ds(r0, RC), hf * LANES:(hf + 1) * LANES] for gl in range(GL)]
            for k, a in enumerate(_chunk_transpose(arrs, chunk)):
                nat_ref[pl.ds(r0 * S5_TOK + hf * GL + k, RC, stride=S5_TOK), :] = a
        return carry

    lax.fori_loop(0, R // RC, to_nat, 0)
    z_ref[...] = nat_ref[...].astype(z_ref.dtype)


def s5_scan(proj, tm, wm, vm, coef, batch, seq):
    t = proj.shape[0]
    GL = LANES // S5_GROUP
    r = seq // S5_TOK
    ub = COL_U // LANES
    per_g = lambda *s: pl.BlockSpec((GL,) + s, lambda bi, si: (si, 0, 0))
    return pl.pallas_call(
        _s5_body,
        grid=(batch, S5_WIDTH // LANES),
        in_specs=[pl.BlockSpec((seq, LANES), lambda bi, si: (bi, ub + si)),
                  per_g(256, 256), per_g(256, 256), per_g(256, 256), per_g(32, 128)],
        out_specs=pl.BlockSpec((seq, LANES), lambda bi, si: (bi, si)),
        out_shape=SDS((t, S5_WIDTH), BF16),
        scratch_shapes=[pltpu.VMEM((seq, LANES), F32), pltpu.VMEM((GL, r, 2 * LANES), BF16),
                        pltpu.VMEM((GL, r, 2 * LANES), F32)],
        compiler_params=_params(("parallel", "parallel")),
        name="s5_scan",
    )(proj, tm, wm, vm, coef)


def _glu_body(z_ref, wl_ref, wg_ref, o_ref):
    z = z_ref[...]
    o_ref[...] = (_dot(z, wl_ref[...]) * _sigmoid(_dot(z, wg_ref[...]))).astype(o_ref.dtype)


def glu(z, w, tm, tn):
    t, k = z.shape
    n = w.shape[1] // 2
    nj = n // tn
    return pl.pallas_call(
        _glu_body,
        grid=(t // tm, nj),
        in_specs=[pl.BlockSpec((tm, k), lambda i, j: (i, 0)),
                  pl.BlockSpec((k, tn), lambda i, j: (0, j)),
                  pl.BlockSpec((k, tn), lambda i, j: (0, nj + j))],
        out_specs=pl.BlockSpec((tm, tn), lambda i, j: (i, j)),
        out_shape=SDS((t, n), BF16),
        compiler_params=_params(("parallel", "arbitrary")),
        name="glu",
    )(z, w, w)


def _mix_body(ya_ref, o0_ref, o1_ref, o2_ref, l0_ref, l1_ref, l2_ref, yc_ref,
              wa_ref, wb_ref, wc_ref, ga_ref, gb_ref, gc_ref, out_ref, yb_ref, wt_ref, acc_ref):
    tm = ya_ref.shape[0]
    E = ATTN_HEAD_DIM

    @pl.when(pl.program_id(1) == 0)
    def _():
        o_refs = (o0_ref, o1_ref, o2_ref)
        l_refs = (l0_ref, l1_ref, l2_ref)
        for g, (_, r) in enumerate(ATTN_GROUPS):
            for c in range(r):
                wt_ref[g, pl.ds(c, tm // r, stride=r), :] = l_refs[g][0, c]
        l0, l1, l2 = wt_ref[0], wt_ref[1], wt_ref[2]
        m = jnp.maximum(jnp.maximum(l0, l1), l2)
        e0, e1, e2 = jnp.exp(l0 - m), jnp.exp(l1 - m), jnp.exp(l2 - m)
        inv = 1.0 / (e0 + e1 + e2)
        wt_ref[0] = e0 * inv
        wt_ref[1] = e1 * inv
        wt_ref[2] = e2 * inv
        order = sorted(range(len(ATTN_GROUPS)), key=lambda g: -ATTN_GROUPS[g][1])
        for pos, g in enumerate(order[:-1]):
            r = ATTN_GROUPS[g][1]
            for c in range(r):
                rows = pl.ds(c, tm // r, stride=r)
                w = wt_ref[g, rows, :]
                for h in range(ATTN_HEADS):
                    part = w[:, h:h + 1] * o_refs[g][0, c, :, h * E:(h + 1) * E].astype(F32)
                    if pos == 0:
                        acc_ref[h, rows, :] = part
                    else:
                        acc_ref[h, rows, :] += part
        g = order[-1]
        assert ATTN_GROUPS[g][1] == 1
        w = wt_ref[g]
        for h in range(ATTN_HEADS):
            sl = slice(h * E, (h + 1) * E)
            yb = acc_ref[h] + w[:, h:h + 1] * o_refs[g][0, 0, :, sl].astype(F32)
            yb_ref[:, sl] = yb.astype(yb_ref.dtype)

    tn = out_ref.shape[1]
    cols = pl.ds(pl.multiple_of(pl.program_id(1) * tn, tn), tn)
    mix = (_sigmoid(ga_ref[...].astype(F32)) * _dot(ya_ref[...], wa_ref[:, cols])
           + _sigmoid(gb_ref[...].astype(F32)) * _dot(yb_ref[...], wb_ref[:, cols])
           + _sigmoid(gc_ref[...].astype(F32)) * _dot(yc_ref[...], wc_ref[:, cols]))
    out_ref[...] = mix.astype(out_ref.dtype)


def gated_mix(ya, outs, lses, yc, wa, wb, wc, proj, seq, tm, tn):
    t, kw = ya.shape
    d = wa.shape[1]
    go = COL_G // tn
    nbt = seq // tm
    row = lambda w: pl.BlockSpec((tm, w), lambda i, j: (i, 0))
    grp = lambda r, w: pl.BlockSpec((1, r, tm // r, w), lambda i, j: (i // nbt, 0, i % nbt, 0))
    wsp = pl.BlockSpec((kw, d), lambda i, j: (0, 0), pipeline_mode=pl.Buffered(1))
    gate = lambda o: pl.BlockSpec((tm, tn), lambda i, j: (i, go + o * (d // tn) + j))
    dils = [r for _, r in ATTN_GROUPS]
    return pl.pallas_call(
        _mix_body,
        grid=(t // tm, d // tn),
        in_specs=[row(kw)] + [grp(r, kw) for r in dils] + [grp(r, LANES) for r in dils] + [row(kw)]
        + [wsp, wsp, wsp, gate(0), gate(1), gate(2)],
        out_specs=pl.BlockSpec((tm, tn), lambda i, j: (i, j)),
        out_shape=SDS((t, d), BF16),
        scratch_shapes=[pltpu.VMEM((tm, kw), BF16), pltpu.VMEM((len(dils), tm, LANES), F32),
                        pltpu.VMEM((ATTN_HEADS, tm, LANES), F32)],
        compiler_params=_params(("parallel", "arbitrary")),
        name="gated_mix",
    )(ya, *outs, *lses, yc, wa, wb, wc, proj, proj, proj)


def _out_ffn_body(final, x_ref, m_ref, wo_ref, g_ref, w1_ref, w2_ref, fg_ref, o_ref, xn_ref):
    j = pl.program_id(1)

    @pl.when(j == 0)
    def _():
        x1 = x_ref[...] + _dot(m_ref[...], wo_ref[...])
        xn_ref[...] = _rms(x1, g_ref[...]).astype(BF16)
        o_ref[...] = x1

    hid = jnp.maximum(_dot(xn_ref[...], w1_ref[...]), 0.0)
    o_ref[...] += _dot((hid * hid).astype(BF16), w2_ref[...])

    if final:
        @pl.when(j == pl.num_programs(1) - 1)
        def _():
            o_ref[...] = _rms(o_ref[...], fg_ref[...])


def out_ffn(x2d, mix, w_out, g, w1, w2, final_g, final, tm, th):
    t, d = x2d.shape
    hdim = w1.shape[1]
    const = lambda shape: pl.BlockSpec(shape, lambda i, j: (0, 0))
    return pl.pallas_call(
        functools.partial(_out_ffn_body, final),
        grid=(t // tm, hdim // th),
        in_specs=[pl.BlockSpec((tm, d), lambda i, j: (i, 0)),
                  pl.BlockSpec((tm, d), lambda i, j: (i, 0)),
                  pl.BlockSpec((d, d), lambda i, j: (0, 0), pipeline_mode=pl.Buffered(1)),
                  const((1, d)),
                  pl.BlockSpec((d, th), lambda i, j: (0, j)),
                  pl.BlockSpec((th, d), lambda i, j: (j, 0)),
                  const((1, d))],
        out_specs=pl.BlockSpec((tm, d), lambda i, j: (i, 0)),
        out_shape=SDS((t, d), F32),
        scratch_shapes=[pltpu.VMEM((tm, d), BF16)],
        compiler_params=_params(("parallel", "arbitrary")),
        name="out_ffn",
    )(x2d, mix, w_out, g, w1, w2, final_g)


def _tile(n, want):
    t = min(n, want)
    assert n % t == 0, (n, want)
    return t


def _layer(x2d, bias, batch, seq, p, final_g, final):
    t, d = x2d.shape
    tm = _tile(seq, 1024)
    n_gate0 = 4 * MLSTM_WIDTH
    n_att0 = n_gate0 + 2 * MLSTM_HEADS
    n_att = 3 * ATTN_HEADS_TOTAL * ATTN_HEAD_DIM
    n_groups = len(ATTN_GROUPS)
    w_in = p["w_in"]
    w_att = w_in[:, n_att0:n_att0 + n_att].reshape(d, 3, n_groups, ATTN_WIDTH)
    w_att = w_att.transpose(0, 2, 1, 3).reshape(d, n_att)
    n_u0 = n_att0 + n_att
    w_main = jnp.concatenate([w_in[:, :COL_U], w_in[:, n_u0:n_u0 + S5_WIDTH], w_in[:, COL_U:n_gate0],
                              w_in[:, n_u0 + S5_WIDTH:], w_att], axis=1).astype(BF16)
    w_if = jnp.pad(w_in[:, n_gate0:n_att0], ((0, 0), (0, N_GATE_PAD - 2 * MLSTM_HEADS))).astype(BF16)
    proj, gif, gif_t, *qkvs = in_proj(x2d, p["norm1_g"].reshape(1, d), w_main, w_if, batch, seq, tm)

    bif = jnp.concatenate([p["b_igate"], p["b_fgate"]]).astype(F32)
    bcol = jnp.pad(bif, (0, N_GATE_PAD - bif.shape[0])).reshape(1, N_GATE_PAD)
    ya = mlstm(proj, gif, gif_t, bcol, bif.reshape(8, 1), p["conv_w"], p["conv_b"].reshape(1, -1),
               p["mh_norm_g"].reshape(1, -1), batch, seq)

    outs, lses = zip(*[attn_group(qkvs[g], bias, g) for g in range(n_groups)])

    tmat, wmat, vmat, coef = s5_prep(p["lam_re"], p["lam_im"], p["log_dt"], p["b_re"], p["b_im"],
                                     p["c_re"], p["c_im"], p["d_skip"])
    z = s5_scan(proj, tmat, wmat, vmat, coef, batch, seq)
    yc = glu(z, p["w_glu"].astype(BF16), tm, 512)

    mix = gated_mix(ya, outs, lses, yc, p["w_br_a"].astype(BF16), p["w_br_b"].astype(BF16),
                    p["w_br_c"].astype(BF16), proj, seq, _tile(seq, 512), 1024)
    return out_ffn(x2d, mix, p["w_out"].astype(BF16), p["norm2_g"].reshape(1, d), p["w_ff1"].astype(BF16),
                   p["w_ff2"].astype(BF16), final_g.reshape(1, d), final, _tile(t, 512), 1024)


_PER_LAYER = ("norm1_g", "w_in", "conv_w", "conv_b", "b_igate", "b_fgate", "mh_norm_g", "lam_re", "lam_im",
              "log_dt", "b_re", "b_im", "c_re", "c_im", "d_skip", "w_glu", "w_br_a", "w_br_b", "w_br_c",
              "w_out", "norm2_g", "w_ff1", "w_ff2")


def kernel(x, norm1_g, w_in, conv_w, conv_b, b_igate, b_fgate, mh_norm_g, rel_bias, lam_re, lam_im, log_dt,
           b_re, b_im, c_re, c_im, d_skip, w_glu, w_br_a, w_br_b, w_br_c, w_out, norm2_g, w_ff1, w_ff2,
           final_g):
    stacked = dict(norm1_g=norm1_g, w_in=w_in, conv_w=conv_w, conv_b=conv_b, b_igate=b_igate,
                   b_fgate=b_fgate, mh_norm_g=mh_norm_g, lam_re=lam_re, lam_im=lam_im, log_dt=log_dt,
                   b_re=b_re, b_im=b_im, c_re=c_re, c_im=c_im, d_skip=d_skip, w_glu=w_glu, w_br_a=w_br_a,
                   w_br_b=w_br_b, w_br_c=w_br_c, w_out=w_out, norm2_g=norm2_g, w_ff1=w_ff1, w_ff2=w_ff2)
    batch, seq, d = x.shape
    x2d = x.astype(F32).reshape(batch * seq, d)
    bias = attn_bias(rel_bias.astype(F32))
    depth = w_in.shape[0]
    for l in range(depth):
        x2d = _layer(x2d, bias, batch, seq, {k: stacked[k][l] for k in _PER_LAYER}, final_g, l == depth - 1)
    return x2d.reshape(batch, seq, d).astype(x.dtype)
```

```python
import functools
import math

import jax
import jax.numpy as jnp
from jax import lax
from jax.experimental import pallas as pl
from jax.experimental.pallas import tpu as pltpu

F32 = jnp.float32
BF16 = jnp.bfloat16
SDS = jax.ShapeDtypeStruct

NORM_EPS = 1e-6
NEG = -1e30

MLSTM_HEADS = 4
MLSTM_HEAD_DIM = 256
MLSTM_WIDTH = MLSTM_HEADS * MLSTM_HEAD_DIM
CONV_WIDTH = 4
ATTN_GROUPS = ((128, 1), (512, 4), (2048, 16))
ATTN_HEADS = 8
ATTN_HEAD_DIM = 128
ATTN_WIDTH = ATTN_HEADS * ATTN_HEAD_DIM
ATTN_HEADS_TOTAL = len(ATTN_GROUPS) * ATTN_HEADS
ATTN_BLOCK = 128
ATTN_Q_BLOCKS = 4
REL_BUCKETS = 32
REL_MAX_DISTANCE = 2048
S5_WIDTH = 1024
S5_GROUP = 16
S5_GROUPS = S5_WIDTH // S5_GROUP
S5_STATE = 64
S5_TOK = 16
N_GATE_PAD = 128

COL_U = 3 * MLSTM_WIDTH
COL_O = COL_U + S5_WIDTH
COL_G = COL_O + MLSTM_WIDTH
D_MODEL = 2048
N_MAIN = COL_G + 3 * D_MODEL
LANES = 128

MLSTM_CHUNK = 256
VMEM_LIMIT = 56 * 2**20


def _params(sem):
    return pltpu.CompilerParams(dimension_semantics=sem, vmem_limit_bytes=VMEM_LIMIT)


def _sigmoid(x):
    return 1.0 / (1.0 + jnp.exp(-x))


def _log_sigmoid(x):
    return jnp.minimum(x, 0.0) - jnp.log(1.0 + jnp.exp(-jnp.abs(x)))


def _rms(x, g):
    ms = jnp.mean(x * x, axis=-1, keepdims=True)
    return x * lax.rsqrt(ms + NORM_EPS) * g


def _dot(a, b):
    return jnp.dot(a, b, preferred_element_type=F32)


def _dot_nt(a, b):
    return lax.dot_general(a, b, (((1,), (1,)), ((), ())), preferred_element_type=F32)


def _dot_tn(a, b):
    return lax.dot_general(a, b, (((0,), (0,)), ((), ())), preferred_element_type=F32)


def _inproj_body(ns, nm, x_ref, g_ref, w_ref, wif_ref, o_ref, oif_ref, oift_ref, a0_ref, a1_ref, a2_ref,
                 xn_ref, acc_ref, acc2_ref):
    j = pl.program_id(1)
    tm = x_ref.shape[0]
    tn = w_ref.shape[1]

    @pl.when(j == 0)
    def _():
        xn = _rms(x_ref[...], g_ref[...]).astype(BF16)
        xn_ref[...] = xn
        gates = _dot(xn, wif_ref[...])
        oif_ref[...] = gates
        oift_ref[...] = jnp.transpose(gates)[:8]

    @pl.when(jnp.logical_and(j < nm, j != ns))
    def _():
        o_ref[...] = _dot(xn_ref[...], w_ref[...]).astype(o_ref.dtype)

    @pl.when(j == ns)
    def _():
        o_ref[...] = _sigmoid(_dot(xn_ref[...], w_ref[...])).astype(o_ref.dtype)

    for g, a_ref in enumerate((a0_ref, a1_ref, a2_ref)):
        r = ATTN_GROUPS[g][1]
        lo = nm + 3 * g

        @pl.when(jnp.logical_and(j >= lo, j < lo + 3))
        def _(a_ref=a_ref, r=r):
            res = _dot(xn_ref[...], w_ref[...])
            if r == 1:
                a_ref[0, 0] = res.astype(a_ref.dtype)
                return
            ns_ = acc_ref.shape[0]
            for base in range(0, tn // LANES, ns_):
                lanes = [slice((base + s) * LANES, (base + s + 1) * LANES) for s in range(ns_)]
                for s in range(ns_):
                    acc_ref[s] = res[:, lanes[s]]
                if r <= 4:
                    for c in range(r):
                        for s in range(ns_):
                            a_ref[0, c, :, lanes[s]] = acc_ref[s, pl.ds(c, tm // r, stride=r), :].astype(a_ref.dtype)
                    continue
                q, r2 = 4, r // 4
                for s in range(ns_):
                    for c0 in range(q):
                        acc2_ref[s, c0 * (tm // q):(c0 + 1) * (tm // q), :] = (
                            acc_ref[s, pl.ds(c0, tm // q, stride=q), :])
                for c0 in range(q):
                    for c1 in range(r2):
                        for s in range(ns_):
                            rows = pl.ds(c0 * (tm // q) + c1, tm // r, stride=r2)
                            a_ref[0, c1 * q + c0, :, lanes[s]] = acc2_ref[s, rows, :].astype(a_ref.dtype)


def in_proj(x2d, g, w, wif, layer, batch, seq, tm):
    t, d = x2d.shape
    tn = ATTN_WIDTH
    nm = N_MAIN // tn
    nbt = seq // tm
    n_groups = len(ATTN_GROUPS)

    def a_spec(gi):
        r = ATTN_GROUPS[gi][1]
        return pl.BlockSpec((1, r, tm // r, tn),
                            lambda i, j: (i // nbt, 0, i % nbt, jnp.clip(j - nm - 3 * gi, 0, 2)))

    return pl.pallas_call(
        functools.partial(_inproj_body, COL_O // tn, nm),
        grid=(t // tm, nm + 3 * n_groups),
        in_specs=[pl.BlockSpec((tm, d), lambda i, j: (i, 0)),
                  pl.BlockSpec((1, d), lambda i, j: (0, 0)),
                  pl.BlockSpec((None, d, tn), lambda i, j: (layer, 0, j)),
                  pl.BlockSpec((None, d, N_GATE_PAD), lambda i, j: (layer, 0, 0))],
        out_specs=[pl.BlockSpec((tm, tn), lambda i, j: (i, jnp.minimum(j, nm - 1))),
                   pl.BlockSpec((tm, N_GATE_PAD), lambda i, j: (i, 0)),
                   pl.BlockSpec((8, tm), lambda i, j: (0, i))] + [a_spec(gi) for gi in range(n_groups)],
        out_shape=[SDS((t, N_MAIN), BF16), SDS((t, N_GATE_PAD), F32), SDS((8, t), F32)]
        + [SDS((batch, r, seq // r, 3 * tn), BF16) for _, r in ATTN_GROUPS],
        scratch_shapes=[pltpu.VMEM((tm, d), BF16), pltpu.VMEM((tn // LANES // 2, tm, LANES), F32),
                        pltpu.VMEM((tn // LANES // 2, tm, LANES), F32)],
        compiler_params=_params(("parallel", "arbitrary")),
        name="in_proj",
    )(x2d, g, w, wif)


def _mlstm_body(q_ref, k_ref, v_ref, og_ref, gcol_ref, grow_ref, bcol_ref, brow_ref,
                cw_ref, cb_ref, ng_ref, y_ref, ct_ref, m_ref, tail_ref):
    c = pl.program_id(1)
    L = q_ref.shape[0]
    E = MLSTM_HEAD_DIM
    H = MLSTM_HEADS

    @pl.when(c == 0)
    def _():
        ct_ref[...] = jnp.zeros_like(ct_ref)
        m_ref[...] = jnp.zeros_like(m_ref)
        tail_ref[...] = jnp.zeros_like(tail_ref)

    row8 = lax.broadcasted_iota(jnp.int32, (8, E), 0)
    tt = lax.broadcasted_iota(jnp.int32, (L, L), 0)
    ss = lax.broadcasted_iota(jnp.int32, (L, L), 1)
    causal = ss <= tt
    gc = gcol_ref[...] + bcol_ref[...]
    gr = grow_ref[...] + brow_ref[...]

    shifts = [jnp.where(tt - ss == d, 1.0, 0.0).astype(BF16) for d in range(1, CONV_WIDTH)]

    def conv_silu(x_ref, h, slot):
        xb = x_ref[:, h * E:(h + 1) * E]
        x = xb.astype(F32)
        tail = tail_ref[slot]
        w = cw_ref[:, slot * E:(slot + 1) * E]
        acc = cb_ref[:, slot * E:(slot + 1) * E] + w[CONV_WIDTH - 1:CONV_WIDTH] * x
        head = jnp.zeros((8, E), F32)
        for d in range(1, CONV_WIDTH):
            wd = w[CONV_WIDTH - 1 - d:CONV_WIDTH - d]
            acc = acc + wd * _dot(shifts[d - 1], xb)
            head = head + wd * jnp.where(row8 < d, pltpu.roll(tail, d, 0), 0.0)
        acc = jnp.concatenate([acc[:8] + head, acc[8:]], axis=0)
        tail_ref[slot] = x[L - 8:]
        return acc * _sigmoid(acc)

    for h in range(H):
        hs = slice(h * E, (h + 1) * E)
        q = conv_silu(q_ref, h, h)
        k = conv_silu(k_ref, h, H + h) * (E ** -0.5)

        ig_col = gc[:, h:h + 1]
        lf_col = _log_sigmoid(gc[:, H + h:H + h + 1])
        ig_row = gr[h:h + 1, :]
        lf_row = _log_sigmoid(gr[H + h:H + h + 1, :])
        bcum_col = jnp.sum(jnp.where(causal, lf_row, 0.0), axis=1, keepdims=True)
        bcum_row = jnp.sum(jnp.where(tt <= ss, lf_col, 0.0), axis=0, keepdims=True)
        a_row = ig_row - bcum_row
        a_col = ig_col - bcum_col

        m_prev = m_ref[h]
        amat = jnp.where(causal, a_row, NEG)
        mrow = jnp.maximum(m_prev, jnp.max(amat, axis=1, keepdims=True))
        w_intra = jnp.exp(amat - mrow)
        w_inter = jnp.exp(m_prev - mrow)

        qb = q.astype(BF16)
        kb = k.astype(BF16)
        vaug = jnp.concatenate([v_ref[:, hs], jnp.ones((L, LANES), BF16)], axis=1)
        s = _dot_nt(qb, kb) * w_intra
        ct = ct_ref[h]
        num_aug = _dot(s.astype(BF16), vaug) + w_inter * _dot(qb, ct.astype(BF16))
        num = num_aug[:, :E]
        den = num_aug[:, E:E + 1]
        m_t = bcum_col + mrow
        hout = num / jnp.maximum(jnp.abs(den), jnp.exp(-m_t))
        hn = _rms(hout, ng_ref[:, hs])
        y_ref[:, hs] = (og_ref[:, hs].astype(F32) * hn).astype(y_ref.dtype)

        b_last = jnp.sum(lf_col, axis=0, keepdims=True)
        g_col = b_last + a_col
        m_new = jnp.maximum(b_last + m_prev, jnp.max(g_col, axis=0, keepdims=True))
        w_s = jnp.exp(g_col - m_new)
        decay = jnp.exp(b_last + m_prev - m_new)
        kw = (k * w_s).astype(BF16)
        ct_ref[h] = decay * ct + _dot_tn(kw, vaug)
        m_ref[h] = m_new


def mlstm(proj, gif, gif_t, bcol, brow, conv_w, conv_b, ng, batch, seq):
    L = min(MLSTM_CHUNK, seq)
    E = MLSTM_HEAD_DIM
    W = MLSTM_WIDTH
    nc = seq // L
    H = MLSTM_HEADS
    t = batch * seq
    row = lambda b, c: b * nc + c
    return pl.pallas_call(
        _mlstm_body,
        grid=(batch, nc),
        in_specs=[pl.BlockSpec((L, W), lambda b, c: (row(b, c), 0)),
                  pl.BlockSpec((L, W), lambda b, c: (row(b, c), 1)),
                  pl.BlockSpec((L, W), lambda b, c: (row(b, c), 2)),
                  pl.BlockSpec((L, W), lambda b, c: (row(b, c), COL_O // W)),
                  pl.BlockSpec((L, N_GATE_PAD), lambda b, c: (row(b, c), 0)),
                  pl.BlockSpec((8, L), lambda b, c: (0, row(b, c))),
                  pl.BlockSpec((1, N_GATE_PAD), lambda b, c: (0, 0)),
                  pl.BlockSpec((8, 1), lambda b, c: (0, 0)),
                  pl.BlockSpec((CONV_WIDTH, 2 * W), lambda b, c: (0, 0)),
                  pl.BlockSpec((1, 2 * W), lambda b, c: (0, 0)),
                  pl.BlockSpec((1, W), lambda b, c: (0, 0))],
        out_specs=pl.BlockSpec((L, W), lambda b, c: (row(b, c), 0)),
        out_shape=SDS((t, W), BF16),
        scratch_shapes=[pltpu.VMEM((H, E, E + LANES), F32), pltpu.VMEM((H, 1, 1), F32),
                        pltpu.VMEM((2 * H, 8, E), F32)],
        compiler_params=_params(("parallel", "arbitrary")),
        name="mlstm",
    )(proj, proj, proj, proj, gif, gif_t, bcol, brow, conv_w, conv_b, ng)


def _bias_body(table_ref, o_ref):
    h = pl.program_id(0)
    dil = jnp.where(h < ATTN_HEADS, ATTN_GROUPS[0][1],
                    jnp.where(h < 2 * ATTN_HEADS, ATTN_GROUPS[1][1], ATTN_GROUPS[2][1]))
    shape = (ATTN_BLOCK, 2 * ATTN_BLOCK)
    i = lax.broadcasted_iota(jnp.int32, shape, 0)
    j = lax.broadcasted_iota(jnp.int32, shape, 1)
    rel = ATTN_BLOCK + i - j
    dist = jnp.maximum(rel, 0) * dil
    max_exact = REL_BUCKETS // 2
    nf = jnp.maximum(dist, max_exact).astype(F32)
    large = max_exact + (jnp.log(nf / max_exact) / math.log(REL_MAX_DISTANCE / max_exact)
                         * (REL_BUCKETS - max_exact)).astype(jnp.int32)
    large = jnp.minimum(large, REL_BUCKETS - 1)
    bucket = jnp.where(dist < max_exact, dist, large)
    acc = jnp.zeros(shape, F32)
    for b in range(REL_BUCKETS):
        acc = jnp.where(bucket == b, table_ref[b, h], acc)
    o_ref[0] = jnp.where(rel >= 0, jnp.where(rel <= ATTN_BLOCK, acc, NEG), NEG)


def attn_bias(rel_bias):
    return pl.pallas_call(
        _bias_body,
        grid=(ATTN_HEADS_TOTAL,),
        in_specs=[pl.BlockSpec(memory_space=pltpu.SMEM)],
        out_specs=pl.BlockSpec((1, ATTN_BLOCK, 2 * ATTN_BLOCK), lambda h: (h, 0, 0)),
        out_shape=SDS((ATTN_HEADS_TOTAL, ATTN_BLOCK, 2 * ATTN_BLOCK), F32),
        compiler_params=_params(("arbitrary",)),
        name="attn_bias",
    )(rel_bias)


def _attn_body(q_ref, kp_ref, kc_ref, vp_ref, vc_ref, bias_ref, o_ref, lse_ref):
    n = pl.program_id(2)
    B = ATTN_BLOCK
    E = ATTN_HEAD_DIM
    H = ATTN_HEADS
    NQ = q_ref.shape[2] // B
    scale = E ** -0.5
    hs = [slice(h * E, (h + 1) * E) for h in range(H)]

    def keys(cur_ref, prev_ref, i, sl):
        if i == 0:
            return jnp.concatenate([prev_ref[0, 0, :, sl], cur_ref[0, 0, :B, sl]], axis=0)
        return cur_ref[0, 0, (i - 1) * B:(i + 1) * B, sl]

    key = lax.broadcasted_iota(jnp.int32, (1, 2 * B), 1)
    no_prev = jnp.where(key < B, jnp.where(n > 0, 0.0, NEG), 0.0)
    logits, maxima = [], []
    for i in range(NQ):
        for h, sl in enumerate(hs):
            s = _dot_nt(q_ref[0, 0, i * B:(i + 1) * B, sl], keys(kc_ref, kp_ref, i, sl)) * scale + bias_ref[h]
            if i == 0:
                s = s + no_prev
            logits.append(s)
            maxima.append(jnp.max(s, axis=1, keepdims=True))
    lane = lax.broadcasted_iota(jnp.int32, (B, LANES), 1)
    for i in range(NQ):
        lse = jnp.zeros((B, LANES), F32)
        for h, sl in enumerate(hs):
            s, m = logits[i * H + h], maxima[i * H + h]
            p = jnp.exp(s - m)
            den = jnp.sum(p, axis=1, keepdims=True)
            o = _dot(p.astype(BF16), keys(vc_ref, vp_ref, i, sl)) * (1.0 / den)
            o_ref[0, 0, i * B:(i + 1) * B, sl] = o.astype(o_ref.dtype)
            lse = jnp.where(lane == h, m + jnp.log(den), lse)
        lse_ref[0, 0, i * B:(i + 1) * B, :] = lse


def attn_group(qkv, bias, g):
    batch, r, l, _ = qkv.shape
    W = ATTN_WIDTH
    nq = min(ATTN_Q_BLOCKS, l // ATTN_BLOCK)
    rows = nq * ATTN_BLOCK
    cur = (1, 1, rows, W)
    one = (1, 1, ATTN_BLOCK, W)
    prev = lambda n: jnp.maximum(n * nq - 1, 0)
    return pl.pallas_call(
        _attn_body,
        grid=(batch, r, l // rows),
        in_specs=[pl.BlockSpec(cur, lambda b, c, n: (b, c, n, 0)),
                  pl.BlockSpec(one, lambda b, c, n: (b, c, prev(n), 1)),
                  pl.BlockSpec(cur, lambda b, c, n: (b, c, n, 1)),
                  pl.BlockSpec(one, lambda b, c, n: (b, c, prev(n), 2)),
                  pl.BlockSpec(cur, lambda b, c, n: (b, c, n, 2)),
                  pl.BlockSpec((ATTN_HEADS, ATTN_BLOCK, 2 * ATTN_BLOCK), lambda b, c, n: (g, 0, 0))],
        out_specs=[pl.BlockSpec(cur, lambda b, c, n: (b, c, n, 0)),
                   pl.BlockSpec((1, 1, rows, LANES), lambda b, c, n: (b, c, n, 0))],
        out_shape=[SDS((batch, r, l, W), BF16), SDS((batch, r, l, LANES), F32)],
        compiler_params=_params(("parallel", "parallel", "arbitrary")),
        name=f"attn_g{g}",
    )(qkv, qkv, qkv, qkv, qkv, bias)


def _s5_prep_body(lam_ref, lamc_ref, dt_ref, bt_r_ref, bt_i_ref, ce_r_ref, ce_i_ref, d_ref,
                  t_ref, w_ref, v_ref, coef_ref):
    P = S5_STATE
    lr = lam_ref[0, 0:1, :]
    li = lam_ref[0, 1:2, :]
    dt = jnp.exp(dt_ref[0])

    def apow(e, lr_, li_):
        mag = jnp.exp(lr_ * dt * e)
        ang = li_ * dt * e
        return mag * jnp.cos(ang), mag * jnp.sin(ang)

    one = jnp.ones((1, 1), F32)
    ar, ai = apow(one, lr, li)
    nr = ar - 1.0
    den = lr * lr + li * li
    f_re = (nr * lr + ai * li) / den
    f_im = (ai * lr - nr * li) / den
    bt_r = bt_r_ref[0]
    bt_i = bt_i_ref[0]
    bb_r = f_re * bt_r - f_im * bt_i
    bb_i = f_re * bt_i + f_im * bt_r

    lrc = lamc_ref[0, :, 0:1]
    lic = lamc_ref[0, :, 1:2]
    lag = (lax.broadcasted_iota(jnp.int32, (P, 256), 1) // S5_GROUP).astype(F32)
    adr, adi = apow(lag, lrc, lic)
    ce_r = ce_r_ref[0]
    ce_i = ce_i_ref[0]
    ca_r = ce_r * adr - ce_i * adi
    ca_i = ce_r * adi + ce_i * adr
    hp = lax.Precision.HIGHEST
    ks = (jnp.dot(bb_r[:, :P], ca_r, precision=hp, preferred_element_type=F32)
          - jnp.dot(bb_i[:, :P], ca_i, precision=hp, preferred_element_type=F32))
    si = lax.broadcasted_iota(jnp.int32, (S5_GROUP, 256), 0)
    lj = lax.broadcasted_iota(jnp.int32, (S5_GROUP, 256), 1)
    ks = ks + jnp.where(si == lj, d_ref[0], 0.0)
    for s in range(S5_TOK):
        sh = s * S5_GROUP
        blk = ks if s == 0 else jnp.where(lj >= sh, pltpu.roll(ks, sh, 1), 0.0)
        t_ref[0, sh:sh + S5_GROUP, :] = blk.astype(t_ref.dtype)

    half = pl.program_id(0) % 2
    lane128 = lax.broadcasted_iota(jnp.int32, (1, 128), 1)
    mine = (lane128 // P) == half

    for s in range(S5_TOK):
        pr, pi = apow(float(S5_TOK - 1 - s) * one, lr, li)
        rows = slice(s * S5_GROUP, (s + 1) * S5_GROUP)
        w_ref[0, rows, 0:128] = jnp.where(mine, bb_r * pr - bb_i * pi, 0.0).astype(w_ref.dtype)
        w_ref[0, rows, 128:256] = jnp.where(mine, bb_r * pi + bb_i * pr, 0.0).astype(w_ref.dtype)

    adr1, adi1 = apow(lag + 1.0, lrc, lic)
    v_ref[0] = jnp.zeros(v_ref.shape[1:], v_ref.dtype)
    row0 = pl.multiple_of(half * P, P)
    v_ref[0, pl.ds(row0, P), :] = (ce_r * adr1 - ce_i * adi1).astype(v_ref.dtype)
    row1 = pl.multiple_of(2 * P + half * P, P)
    v_ref[0, pl.ds(row1, P), :] = (-(ce_r * adi1 + ce_i * adr1)).astype(v_ref.dtype)

    ek = jnp.left_shift(S5_TOK, lax.broadcasted_iota(jnp.int32, (16, 1), 0)).astype(F32)
    cr, ci = apow(ek, lr, li)
    coef_ref[0, 0:16, :] = jnp.where(mine, cr, 0.0)
    coef_ref[0, 16:32, :] = jnp.where(mine, ci, 0.0)


def s5_prep(lam_re, lam_im, log_dt, b_re, b_im, c_re, c_im, d_skip):
    G, P, I = b_re.shape
    lam = jnp.stack([jnp.tile(lam_re, (1, 2)), jnp.tile(lam_im, (1, 2))], axis=1)
    lam = jnp.pad(lam, ((0, 0), (0, 6), (0, 0)))
    lamc = jnp.stack([lam_re, lam_im], axis=2)
    dt = log_dt.reshape(G, 1, 1)
    bt_r = jnp.tile(jnp.swapaxes(b_re, 1, 2), (1, 1, 2))
    bt_i = jnp.tile(jnp.swapaxes(b_im, 1, 2), (1, 1, 2))
    ce_r = jnp.tile(jnp.swapaxes(c_re, 1, 2), (1, 1, S5_TOK))
    ce_i = jnp.tile(jnp.swapaxes(c_im, 1, 2), (1, 1, S5_TOK))
    d = jnp.pad(d_skip, ((0, 0), (0, 256 - I))).reshape(G, 1, 256)
    blk = lambda *s: pl.BlockSpec((1,) + s, lambda g: (g, 0, 0))
    return pl.pallas_call(
        _s5_prep_body,
        grid=(G,),
        in_specs=[blk(8, 128), blk(P, 2), blk(1, 1), blk(16, 128), blk(16, 128), blk(P, 256), blk(P, 256),
                  blk(1, 256)],
        out_specs=[blk(256, 256), blk(256, 256), blk(256, 256), blk(32, 128)],
        out_shape=[SDS((G, 256, 256), BF16), SDS((G, 256, 256), BF16), SDS((G, 256, 256), BF16),
                   SDS((G, 32, 128), F32)],
        compiler_params=_params(("arbitrary",)),
        name="s5_prep",
    )(lam, lamc, dt, bt_r, bt_i, ce_r, ce_i, d)


def _gelu_tanh(x):
    return 0.5 * x * (1.0 + jnp.tanh(math.sqrt(2.0 / math.pi) * (x + 0.044715 * (x * x * x))))


def _chunk_transpose(arrs, chunk):
    a = list(arrs)
    n = len(a)
    d = n // 2
    while d >= 1:
        bit = (chunk & d) != 0
        nxt = list(a)
        for i in range(n):
            if i & d == 0:
                lo, hi = a[i], a[i + d]
                nxt[i] = jnp.where(bit, pltpu.roll(hi, d * S5_GROUP, 1), lo)
                nxt[i + d] = jnp.where(bit, hi, pltpu.roll(lo, LANES - d * S5_GROUP, 1))
        a = nxt
        d //= 2
    return a


def _s5_body(u_ref, t_ref, w_ref, v_ref, coef_ref, z_ref, nat_ref, uf_ref, zs_ref):
    S = u_ref.shape[0]
    R = S // S5_TOK
    GL = LANES // S5_GROUP
    RC = min(R, 256)
    chunk = lax.broadcasted_iota(jnp.int32, (RC, LANES), 1) // S5_GROUP

    nat_ref[...] = u_ref[...].astype(F32)

    def to_flat(rc, carry):
        r0 = pl.multiple_of(rc * RC, RC)
        for hf in range(S5_TOK // GL):
            arrs = [nat_ref[pl.ds(r0 * S5_TOK + hf * GL + k, RC, stride=S5_TOK), :] for k in range(GL)]
            for gl, a in enumerate(_chunk_transpose(arrs, chunk)):
                uf_ref[gl, pl.ds(r0, RC), hf * LANES:(hf + 1) * LANES] = a.astype(uf_ref.dtype)
        return carry

    lax.fori_loop(0, R // RC, to_flat, 0)

    row = lax.broadcasted_iota(jnp.int32, (R, LANES), 0)

    def shift_down(a, sh):
        if sh % 8 == 0:
            return jnp.concatenate([jnp.zeros((sh, LANES), F32), a[:R - sh]], axis=0)
        return jnp.where(row < sh, 0.0, pltpu.roll(a, sh, 0))

    def pair(p, carry):
        g0 = 2 * p
        g1 = g0 + 1
        u0 = uf_ref[g0]
        u1 = uf_ref[g1]
        s2 = _dot(u0, w_ref[g0]) + _dot(u1, w_ref[g1])
        xr = s2[:, :LANES]
        xi = s2[:, LANES:]
        k = 0
        while (1 << k) < R:
            cr = coef_ref[g0, k:k + 1, :] + coef_ref[g1, k:k + 1, :]
            ci = coef_ref[g0, 16 + k:17 + k, :] + coef_ref[g1, 16 + k:17 + k, :]
            xrs = shift_down(xr, 1 << k)
            xis = shift_down(xi, 1 << k)
            xr, xi = xr + cr * xrs - ci * xis, xi + cr * xis + ci * xrs
            k += 1
        xp = jnp.concatenate([shift_down(xr, 1), shift_down(xi, 1)], axis=1).astype(BF16)
        zs_ref[g0] = _gelu_tanh(_dot(u0, t_ref[g0]) + _dot(xp, v_ref[g0]))
        zs_ref[g1] = _gelu_tanh(_dot(u1, t_ref[g1]) + _dot(xp, v_ref[g1]))
        return carry

    lax.fori_loop(0, GL // 2, pair, 0)

    def to_nat(rc, carry):
        r0 = pl.multiple_of(rc * RC, RC)
        for hf in range(S5_TOK // GL):
            arrs = [zs_ref[gl, pl.ds(r0, RC), hf * LANES:(hf + 1) * LANES] for gl in range(GL)]
            for k, a in enumerate(_chunk_transpose(arrs, chunk)):
                nat_ref[pl.ds(r0 * S5_TOK + hf * GL + k, RC, stride=S5_TOK), :] = a
        return carry

    lax.fori_loop(0, R // RC, to_nat, 0)
    z_ref[...] = nat_ref[...].astype(z_ref.dtype)


def s5_scan(proj, tm, wm, vm, coef, batch, seq):
    t = proj.shape[0]
    GL = LANES // S5_GROUP
    r = seq // S5_TOK
    ub = COL_U // LANES
    per_g = lambda *s: pl.BlockSpec((GL,) + s, lambda bi, si: (si, 0, 0))
    return pl.pallas_call(
        _s5_body,
        grid=(batch, S5_WIDTH // LANES),
        in_specs=[pl.BlockSpec((seq, LANES), lambda bi, si: (bi, ub + si)),
                  per_g(256, 256), per_g(256, 256), per_g(256, 256), per_g(32, 128)],
        out_specs=pl.BlockSpec((seq, LANES), lambda bi, si: (bi, si)),
        out_shape=SDS((t, S5_WIDTH), BF16),
        scratch_shapes=[pltpu.VMEM((seq, LANES), F32), pltpu.VMEM((GL, r, 2 * LANES), BF16),
                        pltpu.VMEM((GL, r, 2 * LANES), F32)],
        compiler_params=_params(("parallel", "parallel")),
        name="s5_scan",
    )(proj, tm, wm, vm, coef)


def _glu_body(z_ref, wl_ref, wg_ref, o_ref):
    z = z_ref[...]
    o_ref[...] = (_dot(z, wl_ref[...]) * _sigmoid(_dot(z, wg_ref[...]))).astype(o_ref.dtype)


def glu(z, w, layer, tm, tn):
    t, k = z.shape
    n = w.shape[2] // 2
    nj = n // tn
    return pl.pallas_call(
        _glu_body,
        grid=(t // tm, nj),
        in_specs=[pl.BlockSpec((tm, k), lambda i, j: (i, 0)),
                  pl.BlockSpec((None, k, tn), lambda i, j: (layer, 0, j)),
                  pl.BlockSpec((None, k, tn), lambda i, j: (layer, 0, nj + j))],
        out_specs=pl.BlockSpec((tm, tn), lambda i, j: (i, j)),
        out_shape=SDS((t, n), BF16),
        compiler_params=_params(("parallel", "arbitrary")),
        name="glu",
    )(z, w, w)


def _mix_body(ya_ref, o0_ref, o1_ref, o2_ref, l0_ref, l1_ref, l2_ref, yc_ref,
              wa_ref, wb_ref, wc_ref, ga_ref, gb_ref, gc_ref, out_ref, yb_ref, wt_ref, acc_ref):
    tm = ya_ref.shape[0]
    E = ATTN_HEAD_DIM

    @pl.when(pl.program_id(1) == 0)
    def _():
        o_refs = (o0_ref, o1_ref, o2_ref)
        l_refs = (l0_ref, l1_ref, l2_ref)
        for g, (_, r) in enumerate(ATTN_GROUPS):
            for c in range(r):
                wt_ref[g, pl.ds(c, tm // r, stride=r), :] = l_refs[g][0, c]
        l0, l1, l2 = wt_ref[0], wt_ref[1], wt_ref[2]
        m = jnp.maximum(jnp.maximum(l0, l1), l2)
        e0, e1, e2 = jnp.exp(l0 - m), jnp.exp(l1 - m), jnp.exp(l2 - m)
        inv = 1.0 / (e0 + e1 + e2)
        wt_ref[0] = e0 * inv
        wt_ref[1] = e1 * inv
        wt_ref[2] = e2 * inv
        order = sorted(range(len(ATTN_GROUPS)), key=lambda g: -ATTN_GROUPS[g][1])
        for pos, g in enumerate(order[:-1]):
            r = ATTN_GROUPS[g][1]
            for c in range(r):
                rows = pl.ds(c, tm // r, stride=r)
                w = wt_ref[g, rows, :]
                for h in range(ATTN_HEADS):
                    part = w[:, h:h + 1] * o_refs[g][0, c, :, h * E:(h + 1) * E].astype(F32)
                    if pos == 0:
                        acc_ref[h, rows, :] = part
                    else:
                        acc_ref[h, rows, :] += part
        g = order[-1]
        assert ATTN_GROUPS[g][1] == 1
        w = wt_ref[g]
        for h in range(ATTN_HEADS):
            sl = slice(h * E, (h + 1) * E)
            yb = acc_ref[h] + w[:, h:h + 1] * o_refs[g][0, 0, :, sl].astype(F32)
            yb_ref[:, sl] = yb.astype(yb_ref.dtype)

    tn = out_ref.shape[1]
    cols = pl.ds(pl.multiple_of(pl.program_id(1) * tn, tn), tn)
    mix = (_sigmoid(ga_ref[...].astype(F32)) * _dot(ya_ref[...], wa_ref[:, cols])
           + _sigmoid(gb_ref[...].astype(F32)) * _dot(yb_ref[...], wb_ref[:, cols])
           + _sigmoid(gc_ref[...].astype(F32)) * _dot(yc_ref[...], wc_ref[:, cols]))
    out_ref[...] = mix.astype(out_ref.dtype)


def gated_mix(ya, outs, lses, yc, wa, wb, wc, layer, proj, seq, tm, tn):
    t, kw = ya.shape
    d = wa.shape[2]
    go = COL_G // tn
    nbt = seq // tm
    row = lambda w: pl.BlockSpec((tm, w), lambda i, j: (i, 0))
    grp = lambda r, w: pl.BlockSpec((1, r, tm // r, w), lambda i, j: (i // nbt, 0, i % nbt, 0))
    wsp = pl.BlockSpec((None, kw, d), lambda i, j: (layer, 0, 0), pipeline_mode=pl.Buffered(1))
    gate = lambda o: pl.BlockSpec((tm, tn), lambda i, j: (i, go + o * (d // tn) + j))
    dils = [r for _, r in ATTN_GROUPS]
    return pl.pallas_call(
        _mix_body,
        grid=(t // tm, d // tn),
        in_specs=[row(kw)] + [grp(r, kw) for r in dils] + [grp(r, LANES) for r in dils] + [row(kw)]
        + [wsp, wsp, wsp, gate(0), gate(1), gate(2)],
        out_specs=pl.BlockSpec((tm, tn), lambda i, j: (i, j)),
        out_shape=SDS((t, d), BF16),
        scratch_shapes=[pltpu.VMEM((tm, kw), BF16), pltpu.VMEM((len(dils), tm, LANES), F32),
                        pltpu.VMEM((ATTN_HEADS, tm, LANES), F32)],
        compiler_params=_params(("parallel", "arbitrary")),
        name="gated_mix",
    )(ya, *outs, *lses, yc, wa, wb, wc, proj, proj, proj)


def _out_ffn_body(final, x_ref, m_ref, wo_ref, g_ref, w1_ref, w2_ref, fg_ref, o_ref, xn_ref):
    j = pl.program_id(1)

    @pl.when(j == 0)
    def _():
        x1 = x_ref[...] + _dot(m_ref[...], wo_ref[...])
        xn_ref[...] = _rms(x1, g_ref[...]).astype(BF16)
        o_ref[...] = x1

    hid = jnp.maximum(_dot(xn_ref[...], w1_ref[...]), 0.0)
    o_ref[...] += _dot((hid * hid).astype(BF16), w2_ref[...])

    if final:
        @pl.when(j == pl.num_programs(1) - 1)
        def _():
            o_ref[...] = _rms(o_ref[...], fg_ref[...])


def out_ffn(x2d, mix, w_out, g, w1, w2, layer, final_g, final, tm, th):
    t, d = x2d.shape
    hdim = w1.shape[2]
    const = lambda shape: pl.BlockSpec(shape, lambda i, j: (0, 0))
    return pl.pallas_call(
        functools.partial(_out_ffn_body, final),
        grid=(t // tm, hdim // th),
        in_specs=[pl.BlockSpec((tm, d), lambda i, j: (i, 0)),
                  pl.BlockSpec((tm, d), lambda i, j: (i, 0)),
                  pl.BlockSpec((None, d, d), lambda i, j: (layer, 0, 0), pipeline_mode=pl.Buffered(1)),
                  const((1, d)),
                  pl.BlockSpec((None, d, th), lambda i, j: (layer, 0, j)),
                  pl.BlockSpec((None, th, d), lambda i, j: (layer, j, 0)),
                  const((1, d))],
        out_specs=pl.BlockSpec((tm, d), lambda i, j: (i, 0)),
        out_shape=SDS((t, d), F32),
        scratch_shapes=[pltpu.VMEM((tm, d), BF16)],
        compiler_params=_params(("parallel", "arbitrary")),
        name="out_ffn",
    )(x2d, mix, w_out, g, w1, w2, final_g)


def _tile(n, want):
    t = min(n, want)
    assert n % t == 0, (n, want)
    return t


def _projection_weights(w_in):
    depth, d, _ = w_in.shape
    n_gate0 = 4 * MLSTM_WIDTH
    n_att0 = n_gate0 + 2 * MLSTM_HEADS
    n_att = 3 * ATTN_HEADS_TOTAL * ATTN_HEAD_DIM
    n_u0 = n_att0 + n_att
    w_att = w_in[:, :, n_att0:n_u0].reshape(depth, d, 3, len(ATTN_GROUPS), ATTN_WIDTH)
    w_att = w_att.transpose(0, 1, 3, 2, 4).reshape(depth, d, n_att)
    w_main = jnp.concatenate([w_in[:, :, :COL_U], w_in[:, :, n_u0:n_u0 + S5_WIDTH], w_in[:, :, COL_U:n_gate0],
                              w_in[:, :, n_u0 + S5_WIDTH:], w_att], axis=2).astype(BF16)
    w_if = jnp.pad(w_in[:, :, n_gate0:n_att0], ((0, 0), (0, 0), (0, N_GATE_PAD - 2 * MLSTM_HEADS))).astype(BF16)
    return w_main, w_if


def _layer(x2d, bias, batch, seq, layer, p, w, final_g, final):
    t, d = x2d.shape
    tm = _tile(seq, 1024)
    n_groups = len(ATTN_GROUPS)
    proj, gif, gif_t, *qkvs = in_proj(x2d, p["norm1_g"].reshape(1, d), w["w_main"], w["w_if"], layer,
                                      batch, seq, tm)

    bif = jnp.concatenate([p["b_igate"], p["b_fgate"]]).astype(F32)
    bcol = jnp.pad(bif, (0, N_GATE_PAD - bif.shape[0])).reshape(1, N_GATE_PAD)
    ya = mlstm(proj, gif, gif_t, bcol, bif.reshape(8, 1), p["conv_w"], p["conv_b"].reshape(1, -1),
               p["mh_norm_g"].reshape(1, -1), batch, seq)

    outs, lses = zip(*[attn_group(qkvs[g], bias, g) for g in range(n_groups)])

    tmat, wmat, vmat, coef = s5_prep(p["lam_re"], p["lam_im"], p["log_dt"], p["b_re"], p["b_im"],
                                     p["c_re"], p["c_im"], p["d_skip"])
    z = s5_scan(proj, tmat, wmat, vmat, coef, batch, seq)
    yc = glu(z, w["w_glu"], layer, tm, 512)

    mix = gated_mix(ya, outs, lses, yc, w["w_br_a"], w["w_br_b"], w["w_br_c"], layer, proj, seq,
                    _tile(seq, 512), 1024)
    return out_ffn(x2d, mix, w["w_out"], p["norm2_g"].reshape(1, d), w["w_ff1"], w["w_ff2"], layer,
                   final_g.reshape(1, d), final, _tile(t, 512), 1024)


_SMALL = ("norm1_g", "conv_w", "conv_b", "b_igate", "b_fgate", "mh_norm_g", "lam_re", "lam_im", "log_dt",
          "b_re", "b_im", "c_re", "c_im", "d_skip", "norm2_g")


def kernel(x, norm1_g, w_in, conv_w, conv_b, b_igate, b_fgate, mh_norm_g, rel_bias, lam_re, lam_im, log_dt,
           b_re, b_im, c_re, c_im, d_skip, w_glu, w_br_a, w_br_b, w_br_c, w_out, norm2_g, w_ff1, w_ff2,
           final_g):
    small = dict(norm1_g=norm1_g, conv_w=conv_w, conv_b=conv_b, b_igate=b_igate, b_fgate=b_fgate,
                 mh_norm_g=mh_norm_g, lam_re=lam_re, lam_im=lam_im, log_dt=log_dt, b_re=b_re, b_im=b_im,
                 c_re=c_re, c_im=c_im, d_skip=d_skip, norm2_g=norm2_g)
    w_main, w_if = _projection_weights(w_in)
    weights = dict(w_main=w_main, w_if=w_if, w_glu=w_glu.astype(BF16), w_br_a=w_br_a.astype(BF16),
                   w_br_b=w_br_b.astype(BF16), w_br_c=w_br_c.astype(BF16), w_out=w_out.astype(BF16),
                   w_ff1=w_ff1.astype(BF16), w_ff2=w_ff2.astype(BF16))
    batch, seq, d = x.shape
    x2d = x.astype(F32).reshape(batch * seq, d)
    bias = attn_bias(rel_bias.astype(F32))
    depth = w_in.shape[0]
    for l in range(depth):
        x2d = _layer(x2d, bias, batch, seq, l, {k: small[k][l] for k in _SMALL}, weights, final_g,
                     l == depth - 1)
    return x2d.reshape(batch, seq, d).astype(x.dtype)
```

```python
import functools
import math

import jax
import jax.numpy as jnp
from jax import lax
from jax.experimental import pallas as pl
from jax.experimental.pallas import tpu as pltpu

F32 = jnp.float32
BF16 = jnp.bfloat16
SDS = jax.ShapeDtypeStruct

NORM_EPS = 1e-6
NEG = -1e30

MLSTM_HEADS = 4
MLSTM_HEAD_DIM = 256
MLSTM_WIDTH = MLSTM_HEADS * MLSTM_HEAD_DIM
CONV_WIDTH = 4
ATTN_GROUPS = ((128, 1), (512, 4), (2048, 16))
ATTN_HEADS = 8
ATTN_HEAD_DIM = 128
ATTN_WIDTH = ATTN_HEADS * ATTN_HEAD_DIM
ATTN_HEADS_TOTAL = len(ATTN_GROUPS) * ATTN_HEADS
ATTN_BLOCK = 128
ATTN_Q_BLOCKS = 4
REL_BUCKETS = 32
REL_MAX_DISTANCE = 2048
S5_WIDTH = 1024
S5_GROUP = 16
S5_GROUPS = S5_WIDTH // S5_GROUP
S5_STATE = 64
S5_TOK = 16
N_GATE_PAD = 128

COL_U = 3 * MLSTM_WIDTH
COL_O = COL_U + S5_WIDTH
COL_G = COL_O + MLSTM_WIDTH
D_MODEL = 2048
N_MAIN = COL_G + 3 * D_MODEL
LANES = 128

MLSTM_CHUNK = 256
VMEM_LIMIT = 56 * 2**20


def _params(sem):
    return pltpu.CompilerParams(dimension_semantics=sem, vmem_limit_bytes=VMEM_LIMIT)


def _sigmoid(x):
    return 1.0 / (1.0 + jnp.exp(-x))


def _log_sigmoid(x):
    return jnp.minimum(x, 0.0) - jnp.log(1.0 + jnp.exp(-jnp.abs(x)))


def _rms(x, g):
    ms = jnp.mean(x * x, axis=-1, keepdims=True)
    return x * lax.rsqrt(ms + NORM_EPS) * g


def _dot(a, b):
    return jnp.dot(a, b, preferred_element_type=F32)


def _dot_nt(a, b):
    return lax.dot_general(a, b, (((1,), (1,)), ((), ())), preferred_element_type=F32)


def _dot_tn(a, b):
    return lax.dot_general(a, b, (((0,), (0,)), ((), ())), preferred_element_type=F32)


def _inproj_body(ns, nm, x_ref, g_ref, w_ref, wif_ref, o_ref, oif_ref, oift_ref, a0_ref, a1_ref, a2_ref,
                 xn_ref, acc_ref, acc2_ref):
    j = pl.program_id(1)
    tm = x_ref.shape[0]
    tn = w_ref.shape[1]

    @pl.when(j == 0)
    def _():
        xn = _rms(x_ref[...], g_ref[...]).astype(BF16)
        xn_ref[...] = xn
        gates = _dot(xn, wif_ref[...])
        oif_ref[...] = gates
        oift_ref[...] = jnp.transpose(gates)[:8]

    @pl.when(jnp.logical_and(j < nm, j != ns))
    def _():
        o_ref[...] = _dot(xn_ref[...], w_ref[...]).astype(o_ref.dtype)

    @pl.when(j == ns)
    def _():
        o_ref[...] = _sigmoid(_dot(xn_ref[...], w_ref[...])).astype(o_ref.dtype)

    for g, a_ref in enumerate((a0_ref, a1_ref, a2_ref)):
        r = ATTN_GROUPS[g][1]
        lo = nm + 3 * g

        @pl.when(jnp.logical_and(j >= lo, j < lo + 3))
        def _(a_ref=a_ref, r=r):
            res = _dot(xn_ref[...], w_ref[...])
            if r == 1:
                a_ref[0, 0] = res.astype(a_ref.dtype)
                return
            ns_ = acc_ref.shape[0]
            for base in range(0, tn // LANES, ns_):
                lanes = [slice((base + s) * LANES, (base + s + 1) * LANES) for s in range(ns_)]
                for s in range(ns_):
                    acc_ref[s] = res[:, lanes[s]]
                if r <= 4:
                    for c in range(r):
                        for s in range(ns_):
                            a_ref[0, c, :, lanes[s]] = acc_ref[s, pl.ds(c, tm // r, stride=r), :].astype(a_ref.dtype)
                    continue
                q, r2 = 4, r // 4
                for s in range(ns_):
                    for c0 in range(q):
                        acc2_ref[s, c0 * (tm // q):(c0 + 1) * (tm // q), :] = (
                            acc_ref[s, pl.ds(c0, tm // q, stride=q), :])
                for c0 in range(q):
                    for c1 in range(r2):
                        for s in range(ns_):
                            rows = pl.ds(c0 * (tm // q) + c1, tm // r, stride=r2)
                            a_ref[0, c1 * q + c0, :, lanes[s]] = acc2_ref[s, rows, :].astype(a_ref.dtype)


def in_proj(x2d, g, w, wif, layer, batch, seq, tm):
    t, d = x2d.shape
    tn = ATTN_WIDTH
    nm = N_MAIN // tn
    nbt = seq // tm
    n_groups = len(ATTN_GROUPS)

    def a_spec(gi):
        r = ATTN_GROUPS[gi][1]
        return pl.BlockSpec((1, r, tm // r, tn),
                            lambda i, j: (i // nbt, 0, i % nbt, jnp.clip(j - nm - 3 * gi, 0, 2)))

    def w_col(j):
        jj = jnp.maximum(j - nm, 0)
        return jnp.where(j < nm, j, nm + (jj % 3) * n_groups + jj // 3)

    return pl.pallas_call(
        functools.partial(_inproj_body, COL_O // tn, nm),
        grid=(t // tm, nm + 3 * n_groups),
        in_specs=[pl.BlockSpec((tm, d), lambda i, j: (i, 0)),
                  pl.BlockSpec((1, d), lambda i, j: (0, 0)),
                  pl.BlockSpec((None, d, tn), lambda i, j: (layer, 0, w_col(j))),
                  pl.BlockSpec((None, d, N_GATE_PAD), lambda i, j: (layer, 0, 0))],
        out_specs=[pl.BlockSpec((tm, tn), lambda i, j: (i, jnp.minimum(j, nm - 1))),
                   pl.BlockSpec((tm, N_GATE_PAD), lambda i, j: (i, 0)),
                   pl.BlockSpec((8, tm), lambda i, j: (0, i))] + [a_spec(gi) for gi in range(n_groups)],
        out_shape=[SDS((t, N_MAIN), BF16), SDS((t, N_GATE_PAD), F32), SDS((8, t), F32)]
        + [SDS((batch, r, seq // r, 3 * tn), BF16) for _, r in ATTN_GROUPS],
        scratch_shapes=[pltpu.VMEM((tm, d), BF16), pltpu.VMEM((tn // LANES // 2, tm, LANES), F32),
                        pltpu.VMEM((tn // LANES // 2, tm, LANES), F32)],
        compiler_params=_params(("parallel", "arbitrary")),
        name="in_proj",
    )(x2d, g, w, wif)


def _mlstm_body(q_ref, k_ref, v_ref, og_ref, gcol_ref, grow_ref, bcol_ref, brow_ref,
                cw_ref, cb_ref, ng_ref, y_ref, ct_ref, m_ref, tail_ref):
    c = pl.program_id(1)
    L = q_ref.shape[0]
    E = MLSTM_HEAD_DIM
    H = MLSTM_HEADS

    @pl.when(c == 0)
    def _():
        ct_ref[...] = jnp.zeros_like(ct_ref)
        m_ref[...] = jnp.zeros_like(m_ref)
        tail_ref[...] = jnp.zeros_like(tail_ref)

    row8 = lax.broadcasted_iota(jnp.int32, (8, E), 0)
    tt = lax.broadcasted_iota(jnp.int32, (L, L), 0)
    ss = lax.broadcasted_iota(jnp.int32, (L, L), 1)
    causal = ss <= tt
    gc = gcol_ref[...] + bcol_ref[...]
    gr = grow_ref[...] + brow_ref[...]

    shifts = [jnp.where(tt - ss == d, 1.0, 0.0).astype(BF16) for d in range(1, CONV_WIDTH)]

    def conv_silu(x_ref, h, slot):
        xb = x_ref[:, h * E:(h + 1) * E]
        x = xb.astype(F32)
        tail = tail_ref[slot]
        w = cw_ref[:, slot * E:(slot + 1) * E]
        acc = cb_ref[:, slot * E:(slot + 1) * E] + w[CONV_WIDTH - 1:CONV_WIDTH] * x
        head = jnp.zeros((8, E), F32)
        for d in range(1, CONV_WIDTH):
            wd = w[CONV_WIDTH - 1 - d:CONV_WIDTH - d]
            acc = acc + wd * _dot(shifts[d - 1], xb)
            head = head + wd * jnp.where(row8 < d, pltpu.roll(tail, d, 0), 0.0)
        acc = jnp.concatenate([acc[:8] + head, acc[8:]], axis=0)
        tail_ref[slot] = x[L - 8:]
        return acc * _sigmoid(acc)

    for h in range(H):
        hs = slice(h * E, (h + 1) * E)
        q = conv_silu(q_ref, h, h)
        k = conv_silu(k_ref, h, H + h) * (E ** -0.5)

        ig_col = gc[:, h:h + 1]
        lf_col = _log_sigmoid(gc[:, H + h:H + h + 1])
        ig_row = gr[h:h + 1, :]
        lf_row = _log_sigmoid(gr[H + h:H + h + 1, :])
        bcum_col = jnp.sum(jnp.where(causal, lf_row, 0.0), axis=1, keepdims=True)
        bcum_row = jnp.sum(jnp.where(tt <= ss, lf_col, 0.0), axis=0, keepdims=True)
        a_row = ig_row - bcum_row
        a_col = ig_col - bcum_col

        m_prev = m_ref[h]
        amat = jnp.where(causal, a_row, NEG)
        mrow = jnp.maximum(m_prev, jnp.max(amat, axis=1, keepdims=True))
        w_intra = jnp.exp(amat - mrow)
        w_inter = jnp.exp(m_prev - mrow)

        qb = q.astype(BF16)
        kb = k.astype(BF16)
        vaug = jnp.concatenate([v_ref[:, hs], jnp.ones((L, LANES), BF16)], axis=1)
        s = _dot_nt(qb, kb) * w_intra
        ct = ct_ref[h]
        num_aug = _dot(s.astype(BF16), vaug) + w_inter * _dot(qb, ct.astype(BF16))
        num = num_aug[:, :E]
        den = num_aug[:, E:E + 1]
        m_t = bcum_col + mrow
        hout = num / jnp.maximum(jnp.abs(den), jnp.exp(-m_t))
        hn = _rms(hout, ng_ref[:, hs])
        y_ref[:, hs] = (og_ref[:, hs].astype(F32) * hn).astype(y_ref.dtype)

        b_last = jnp.sum(lf_col, axis=0, keepdims=True)
        g_col = b_last + a_col
        m_new = jnp.maximum(b_last + m_prev, jnp.max(g_col, axis=0, keepdims=True))
        w_s = jnp.exp(g_col - m_new)
        decay = jnp.exp(b_last + m_prev - m_new)
        kw = (k * w_s).astype(BF16)
        ct_ref[h] = decay * ct + _dot_tn(kw, vaug)
        m_ref[h] = m_new


def mlstm(proj, gif, gif_t, bcol, brow, conv_w, conv_b, ng, batch, seq):
    L = min(MLSTM_CHUNK, seq)
    E = MLSTM_HEAD_DIM
    W = MLSTM_WIDTH
    nc = seq // L
    H = MLSTM_HEADS
    t = batch * seq
    row = lambda b, c: b * nc + c
    return pl.pallas_call(
        _mlstm_body,
        grid=(batch, nc),
        in_specs=[pl.BlockSpec((L, W), lambda b, c: (row(b, c), 0)),
                  pl.BlockSpec((L, W), lambda b, c: (row(b, c), 1)),
                  pl.BlockSpec((L, W), lambda b, c: (row(b, c), 2)),
                  pl.BlockSpec((L, W), lambda b, c: (row(b, c), COL_O // W)),
                  pl.BlockSpec((L, N_GATE_PAD), lambda b, c: (row(b, c), 0)),
                  pl.BlockSpec((8, L), lambda b, c: (0, row(b, c))),
                  pl.BlockSpec((1, N_GATE_PAD), lambda b, c: (0, 0)),
                  pl.BlockSpec((8, 1), lambda b, c: (0, 0)),
                  pl.BlockSpec((CONV_WIDTH, 2 * W), lambda b, c: (0, 0)),
                  pl.BlockSpec((1, 2 * W), lambda b, c: (0, 0)),
                  pl.BlockSpec((1, W), lambda b, c: (0, 0))],
        out_specs=pl.BlockSpec((L, W), lambda b, c: (row(b, c), 0)),
        out_shape=SDS((t, W), BF16),
        scratch_shapes=[pltpu.VMEM((H, E, E + LANES), F32), pltpu.VMEM((H, 1, 1), F32),
                        pltpu.VMEM((2 * H, 8, E), F32)],
        compiler_params=_params(("parallel", "arbitrary")),
        name="mlstm",
    )(proj, proj, proj, proj, gif, gif_t, bcol, brow, conv_w, conv_b, ng)


def _bias_body(table_ref, o_ref):
    h = pl.program_id(0)
    dil = jnp.where(h < ATTN_HEADS, ATTN_GROUPS[0][1],
                    jnp.where(h < 2 * ATTN_HEADS, ATTN_GROUPS[1][1], ATTN_GROUPS[2][1]))
    shape = (ATTN_BLOCK, 2 * ATTN_BLOCK)
    i = lax.broadcasted_iota(jnp.int32, shape, 0)
    j = lax.broadcasted_iota(jnp.int32, shape, 1)
    rel = ATTN_BLOCK + i - j
    dist = jnp.maximum(rel, 0) * dil
    max_exact = REL_BUCKETS // 2
    nf = jnp.maximum(dist, max_exact).astype(F32)
    large = max_exact + (jnp.log(nf / max_exact) / math.log(REL_MAX_DISTANCE / max_exact)
                         * (REL_BUCKETS - max_exact)).astype(jnp.int32)
    large = jnp.minimum(large, REL_BUCKETS - 1)
    bucket = jnp.where(dist < max_exact, dist, large)
    acc = jnp.zeros(shape, F32)
    for b in range(REL_BUCKETS):
        acc = jnp.where(bucket == b, table_ref[b, h], acc)
    o_ref[0] = jnp.where(rel >= 0, jnp.where(rel <= ATTN_BLOCK, acc, NEG), NEG)


def attn_bias(rel_bias):
    return pl.pallas_call(
        _bias_body,
        grid=(ATTN_HEADS_TOTAL,),
        in_specs=[pl.BlockSpec(memory_space=pltpu.SMEM)],
        out_specs=pl.BlockSpec((1, ATTN_BLOCK, 2 * ATTN_BLOCK), lambda h: (h, 0, 0)),
        out_shape=SDS((ATTN_HEADS_TOTAL, ATTN_BLOCK, 2 * ATTN_BLOCK), F32),
        compiler_params=_params(("arbitrary",)),
        name="attn_bias",
    )(rel_bias)


def _attn_body(q_ref, kp_ref, kc_ref, vp_ref, vc_ref, bias_ref, o_ref, lse_ref):
    n = pl.program_id(2)
    B = ATTN_BLOCK
    E = ATTN_HEAD_DIM
    H = ATTN_HEADS
    NQ = q_ref.shape[2] // B
    scale = E ** -0.5
    hs = [slice(h * E, (h + 1) * E) for h in range(H)]

    def keys(cur_ref, prev_ref, i, sl):
        if i == 0:
            return jnp.concatenate([prev_ref[0, 0, :, sl], cur_ref[0, 0, :B, sl]], axis=0)
        return cur_ref[0, 0, (i - 1) * B:(i + 1) * B, sl]

    key = lax.broadcasted_iota(jnp.int32, (1, 2 * B), 1)
    no_prev = jnp.where(key < B, jnp.where(n > 0, 0.0, NEG), 0.0)
    logits, maxima = [], []
    for i in range(NQ):
        for h, sl in enumerate(hs):
            s = _dot_nt(q_ref[0, 0, i * B:(i + 1) * B, sl], keys(kc_ref, kp_ref, i, sl)) * scale + bias_ref[h]
            if i == 0:
                s = s + no_prev
            logits.append(s)
            maxima.append(jnp.max(s, axis=1, keepdims=True))
    lane = lax.broadcasted_iota(jnp.int32, (B, LANES), 1)
    for i in range(NQ):
        lse = jnp.zeros((B, LANES), F32)
        for h, sl in enumerate(hs):
            s, m = logits[i * H + h], maxima[i * H + h]
            p = jnp.exp(s - m)
            den = jnp.sum(p, axis=1, keepdims=True)
            o = _dot(p.astype(BF16), keys(vc_ref, vp_ref, i, sl)) * (1.0 / den)
            o_ref[0, 0, i * B:(i + 1) * B, sl] = o.astype(o_ref.dtype)
            lse = jnp.where(lane == h, m + jnp.log(den), lse)
        lse_ref[0, 0, i * B:(i + 1) * B, :] = lse


def attn_group(qkv, bias, g):
    batch, r, l, _ = qkv.shape
    W = ATTN_WIDTH
    nq = min(ATTN_Q_BLOCKS, l // ATTN_BLOCK)
    rows = nq * ATTN_BLOCK
    cur = (1, 1, rows, W)
    one = (1, 1, ATTN_BLOCK, W)
    prev = lambda n: jnp.maximum(n * nq - 1, 0)
    return pl.pallas_call(
        _attn_body,
        grid=(batch, r, l // rows),
        in_specs=[pl.BlockSpec(cur, lambda b, c, n: (b, c, n, 0)),
                  pl.BlockSpec(one, lambda b, c, n: (b, c, prev(n), 1)),
                  pl.BlockSpec(cur, lambda b, c, n: (b, c, n, 1)),
                  pl.BlockSpec(one, lambda b, c, n: (b, c, prev(n), 2)),
                  pl.BlockSpec(cur, lambda b, c, n: (b, c, n, 2)),
                  pl.BlockSpec((ATTN_HEADS, ATTN_BLOCK, 2 * ATTN_BLOCK), lambda b, c, n: (g, 0, 0))],
        out_specs=[pl.BlockSpec(cur, lambda b, c, n: (b, c, n, 0)),
                   pl.BlockSpec((1, 1, rows, LANES), lambda b, c, n: (b, c, n, 0))],
        out_shape=[SDS((batch, r, l, W), BF16), SDS((batch, r, l, LANES), F32)],
        compiler_params=_params(("parallel", "parallel", "arbitrary")),
        name=f"attn_g{g}",
    )(qkv, qkv, qkv, qkv, qkv, bias)


def _s5_prep_body(lam_ref, lamc_ref, dt_ref, bt_r_ref, bt_i_ref, ce_r_ref, ce_i_ref, d_ref,
                  t_ref, w_ref, v_ref, coef_ref):
    P = S5_STATE
    lr = lam_ref[0, 0:1, :]
    li = lam_ref[0, 1:2, :]
    dt = jnp.exp(dt_ref[0])

    def apow(e, lr_, li_):
        mag = jnp.exp(lr_ * dt * e)
        ang = li_ * dt * e
        return mag * jnp.cos(ang), mag * jnp.sin(ang)

    one = jnp.ones((1, 1), F32)
    ar, ai = apow(one, lr, li)
    nr = ar - 1.0
    den = lr * lr + li * li
    f_re = (nr * lr + ai * li) / den
    f_im = (ai * lr - nr * li) / den
    bt_r = bt_r_ref[0]
    bt_i = bt_i_ref[0]
    bb_r = f_re * bt_r - f_im * bt_i
    bb_i = f_re * bt_i + f_im * bt_r

    lrc = lamc_ref[0, :, 0:1]
    lic = lamc_ref[0, :, 1:2]
    lag = (lax.broadcasted_iota(jnp.int32, (P, 256), 1) // S5_GROUP).astype(F32)
    adr, adi = apow(lag, lrc, lic)
    ce_r = ce_r_ref[0]
    ce_i = ce_i_ref[0]
    ca_r = ce_r * adr - ce_i * adi
    ca_i = ce_r * adi + ce_i * adr
    hp = lax.Precision.HIGHEST
    ks = (jnp.dot(bb_r[:, :P], ca_r, precision=hp, preferred_element_type=F32)
          - jnp.dot(bb_i[:, :P], ca_i, precision=hp, preferred_element_type=F32))
    si = lax.broadcasted_iota(jnp.int32, (S5_GROUP, 256), 0)
    lj = lax.broadcasted_iota(jnp.int32, (S5_GROUP, 256), 1)
    ks = ks + jnp.where(si == lj, d_ref[0], 0.0)
    for s in range(S5_TOK):
        sh = s * S5_GROUP
        blk = ks if s == 0 else jnp.where(lj >= sh, pltpu.roll(ks, sh, 1), 0.0)
        t_ref[0, sh:sh + S5_GROUP, :] = blk.astype(t_ref.dtype)

    half = pl.program_id(0) % 2
    lane128 = lax.broadcasted_iota(jnp.int32, (1, 128), 1)
    mine = (lane128 // P) == half

    pw_r, pw_i = apow(lax.broadcasted_iota(jnp.int32, (S5_TOK, 1), 0).astype(F32), lr, li)
    for s in range(S5_TOK):
        e = S5_TOK - 1 - s
        pr, pi = pw_r[e:e + 1], pw_i[e:e + 1]
        rows = slice(s * S5_GROUP, (s + 1) * S5_GROUP)
        w_ref[0, rows, 0:128] = jnp.where(mine, bb_r * pr - bb_i * pi, 0.0).astype(w_ref.dtype)
        w_ref[0, rows, 128:256] = jnp.where(mine, bb_r * pi + bb_i * pr, 0.0).astype(w_ref.dtype)

    a1r, a1i = adr[:, S5_GROUP:S5_GROUP + 1], adi[:, S5_GROUP:S5_GROUP + 1]
    adr1 = adr * a1r - adi * a1i
    adi1 = adr * a1i + adi * a1r
    v_ref[0] =jnp.zeros(v_ref.shape[1:], v_ref.dtype)
    row0 = pl.multiple_of(half * P, P)
    v_ref[0, pl.ds(row0, P), :] = (ce_r * adr1 - ce_i * adi1).astype(v_ref.dtype)
    row1 = pl.multiple_of(2 * P + half * P, P)
    v_ref[0, pl.ds(row1, P), :] = (-(ce_r * adi1 + ce_i * adr1)).astype(v_ref.dtype)

    ek = jnp.left_shift(S5_TOK, lax.broadcasted_iota(jnp.int32, (16, 1), 0)).astype(F32)
    cr, ci = apow(ek, lr, li)
    coef_ref[0, 0:16, :] = jnp.where(mine, cr, 0.0)
    coef_ref[0, 16:32, :] = jnp.where(mine, ci, 0.0)


def s5_prep(lam_re, lam_im, log_dt, b_re, b_im, c_re, c_im, d_skip):
    G, P, I = b_re.shape
    lam = jnp.stack([jnp.tile(lam_re, (1, 2)), jnp.tile(lam_im, (1, 2))], axis=1)
    lam = jnp.pad(lam, ((0, 0), (0, 6), (0, 0)))
    lamc = jnp.stack([lam_re, lam_im], axis=2)
    dt = log_dt.reshape(G, 1, 1)
    bt_r = jnp.tile(jnp.swapaxes(b_re, 1, 2), (1, 1, 2))
    bt_i = jnp.tile(jnp.swapaxes(b_im, 1, 2), (1, 1, 2))
    ce_r = jnp.tile(jnp.swapaxes(c_re, 1, 2), (1, 1, S5_TOK))
    ce_i = jnp.tile(jnp.swapaxes(c_im, 1, 2), (1, 1, S5_TOK))
    d = jnp.pad(d_skip, ((0, 0), (0, 256 - I))).reshape(G, 1, 256)
    blk = lambda *s: pl.BlockSpec((1,) + s, lambda g: (g, 0, 0))
    return pl.pallas_call(
        _s5_prep_body,
        grid=(G,),
        in_specs=[blk(8, 128), blk(P, 2), blk(1, 1), blk(16, 128), blk(16, 128), blk(P, 256), blk(P, 256),
                  blk(1, 256)],
        out_specs=[blk(256, 256), blk(256, 256), blk(256, 256), blk(32, 128)],
        out_shape=[SDS((G, 256, 256), BF16), SDS((G, 256, 256), BF16), SDS((G, 256, 256), BF16),
                   SDS((G, 32, 128), F32)],
        compiler_params=_params(("arbitrary",)),
        name="s5_prep",
    )(lam, lamc, dt, bt_r, bt_i, ce_r, ce_i, d)


def _gelu_tanh(x):
    return 0.5 * x * (1.0 + jnp.tanh(math.sqrt(2.0 / math.pi) * (x + 0.044715 * (x * x * x))))


def _chunk_transpose(arrs, chunk):
    a = list(arrs)
    n = len(a)
    d = n // 2
    while d >= 1:
        bit = (chunk & d) != 0
        nxt = list(a)
        for i in range(n):
            if i & d == 0:
                lo, hi = a[i], a[i + d]
                nxt[i] = jnp.where(bit, pltpu.roll(hi, d * S5_GROUP, 1), lo)
                nxt[i + d] = jnp.where(bit, hi, pltpu.roll(lo, LANES - d * S5_GROUP, 1))
        a = nxt
        d //= 2
    return a


def _s5_body(u_ref, t_ref, w_ref, v_ref, coef_ref, z_ref, nat_ref, uf_ref, zs_ref):
    S = u_ref.shape[0]
    R = S // S5_TOK
    GL = LANES // S5_GROUP
    RC = min(R, 256)
    chunk = lax.broadcasted_iota(jnp.int32, (RC, LANES), 1) // S5_GROUP

    nat_ref[...] = u_ref[...].astype(F32)

    def to_flat(rc, carry):
        r0 = pl.multiple_of(rc * RC, RC)
        for hf in range(S5_TOK // GL):
            arrs = [nat_ref[pl.ds(r0 * S5_TOK + hf * GL + k, RC, stride=S5_TOK), :] for k in range(GL)]
            for gl, a in enumerate(_chunk_transpose(arrs, chunk)):
                uf_ref[gl, pl.ds(r0, RC), hf * LANES:(hf + 1) * LANES] = a.astype(uf_ref.dtype)
        return carry

    lax.fori_loop(0, R // RC, to_flat, 0)

    row = lax.broadcasted_iota(jnp.int32, (R, LANES), 0)

    def shift_down(a, sh):
        if sh % 8 == 0:
            return jnp.concatenate([jnp.zeros((sh, LANES), F32), a[:R - sh]], axis=0)
        return jnp.where(row < sh, 0.0, pltpu.roll(a, sh, 0))

    def pair(p, carry):
        g0 = 2 * p
        g1 = g0 + 1
        u0 = uf_ref[g0]
        u1 = uf_ref[g1]
        s2 = _dot(u0, w_ref[g0]) + _dot(u1, w_ref[g1])
        xr = s2[:, :LANES]
        xi = s2[:, LANES:]
        k = 0
        while (1 << k) < R:
            cr = coef_ref[g0, k:k + 1, :] + coef_ref[g1, k:k + 1, :]
            ci = coef_ref[g0, 16 + k:17 + k, :] + coef_ref[g1, 16 + k:17 + k, :]
            xrs = shift_down(xr, 1 << k)
            xis = shift_down(xi, 1 << k)
            xr, xi = xr + cr * xrs - ci * xis, xi + cr * xis + ci * xrs
            k += 1
        xp = jnp.concatenate([shift_down(xr, 1), shift_down(xi, 1)], axis=1).astype(BF16)
        zs_ref[g0] = _gelu_tanh(_dot(u0, t_ref[g0]) + _dot(xp, v_ref[g0]))
        zs_ref[g1] = _gelu_tanh(_dot(u1, t_ref[g1]) + _dot(xp, v_ref[g1]))
        return carry

    lax.fori_loop(0, GL // 2, pair, 0)

    def to_nat(rc, carry):
        r0 = pl.multiple_of(rc * RC, RC)
        for hf in range(S5_TOK // GL):
            arrs = [zs_ref[gl, pl.ds(r0, RC), hf * LANES:(hf + 1) * LANES] for gl in range(GL)]
            for k, a in enumerate(_chunk_transpose(arrs, chunk)):
                nat_ref[pl.ds(r0 * S5_TOK + hf * GL + k, RC, stride=S5_TOK), :] = a
        return carry

    lax.fori_loop(0, R // RC, to_nat, 0)
    z_ref[...] = nat_ref[...].astype(z_ref.dtype)


def s5_scan(proj, tm, wm, vm, coef, batch, seq):
    t = proj.shape[0]
    GL = LANES // S5_GROUP
    r = seq // S5_TOK
    ub = COL_U // LANES
    per_g = lambda *s: pl.BlockSpec((GL,) + s, lambda bi, si: (si, 0, 0))
    return pl.pallas_call(
        _s5_body,
        grid=(batch, S5_WIDTH // LANES),
        in_specs=[pl.BlockSpec((seq, LANES), lambda bi, si: (bi, ub + si)),
                  per_g(256, 256), per_g(256, 256), per_g(256, 256), per_g(32, 128)],
        out_specs=pl.BlockSpec((seq, LANES), lambda bi, si: (bi, si)),
        out_shape=SDS((t, S5_WIDTH), BF16),
        scratch_shapes=[pltpu.VMEM((seq, LANES), F32), pltpu.VMEM((GL, r, 2 * LANES), BF16),
                        pltpu.VMEM((GL, r, 2 * LANES), F32)],
        compiler_params=_params(("parallel", "parallel")),
        name="s5_scan",
    )(proj, tm, wm, vm, coef)


def _glu_body(z_ref, wl_ref, wg_ref, o_ref):
    z = z_ref[...]
    o_ref[...] = (_dot(z, wl_ref[...]) * _sigmoid(_dot(z, wg_ref[...]))).astype(o_ref.dtype)


def glu(z, w, layer, tm, tn):
    t, k = z.shape
    n = w.shape[2] // 2
    nj = n // tn
    return pl.pallas_call(
        _glu_body,
        grid=(t // tm, nj),
        in_specs=[pl.BlockSpec((tm, k), lambda i, j: (i, 0)),
                  pl.BlockSpec((None, k, tn), lambda i, j: (layer, 0, j)),
                  pl.BlockSpec((None, k, tn), lambda i, j: (layer, 0, nj + j))],
        out_specs=pl.BlockSpec((tm, tn), lambda i, j: (i, j)),
        out_shape=SDS((t, n), BF16),
        compiler_params=_params(("parallel", "arbitrary")),
        name="glu",
    )(z, w, w)


def _mix_body(ya_ref, o0_ref, o1_ref, o2_ref, l0_ref, l1_ref, l2_ref, yc_ref,
              wa_ref, wb_ref, wc_ref, ga_ref, gb_ref, gc_ref, out_ref, yb_ref, wt_ref, acc_ref):
    tm = ya_ref.shape[0]
    E = ATTN_HEAD_DIM

    @pl.when(pl.program_id(1) == 0)
    def _():
        o_refs = (o0_ref, o1_ref, o2_ref)
        l_refs = (l0_ref, l1_ref, l2_ref)
        for g, (_, r) in enumerate(ATTN_GROUPS):
            for c in range(r):
                wt_ref[g, pl.ds(c, tm // r, stride=r), :] = l_refs[g][0, c]
        l0, l1, l2 = wt_ref[0], wt_ref[1], wt_ref[2]
        m = jnp.maximum(jnp.maximum(l0, l1), l2)
        e0, e1, e2 = jnp.exp(l0 - m), jnp.exp(l1 - m), jnp.exp(l2 - m)
        inv = 1.0 / (e0 + e1 + e2)
        wt_ref[0] = e0 * inv
        wt_ref[1] = e1 * inv
        wt_ref[2] = e2 * inv
        order = sorted(range(len(ATTN_GROUPS)), key=lambda g: -ATTN_GROUPS[g][1])
        for pos, g in enumerate(order[:-1]):
            r = ATTN_GROUPS[g][1]
            for c in range(r):
                rows = pl.ds(c, tm // r, stride=r)
                w = wt_ref[g, rows, :]
                for h in range(ATTN_HEADS):
                    part = w[:, h:h + 1] * o_refs[g][0, c, :, h * E:(h + 1) * E].astype(F32)
                    if pos == 0:
                        acc_ref[h, rows, :] = part
                    else:
                        acc_ref[h, rows, :] += part
        g = order[-1]
        assert ATTN_GROUPS[g][1] == 1
        w = wt_ref[g]
        for h in range(ATTN_HEADS):
            sl = slice(h * E, (h + 1) * E)
            yb = acc_ref[h] + w[:, h:h + 1] * o_refs[g][0, 0, :, sl].astype(F32)
            yb_ref[:, sl] = yb.astype(yb_ref.dtype)

    tn = out_ref.shape[1]
    cols = pl.ds(pl.multiple_of(pl.program_id(1) * tn, tn), tn)
    mix = (_sigmoid(ga_ref[...].astype(F32)) * _dot(ya_ref[...], wa_ref[:, cols])
           + _sigmoid(gb_ref[...].astype(F32)) * _dot(yb_ref[...], wb_ref[:, cols])
           + _sigmoid(gc_ref[...].astype(F32)) * _dot(yc_ref[...], wc_ref[:, cols]))
    out_ref[...] = mix.astype(out_ref.dtype)


def gated_mix(ya, outs, lses, yc, wa, wb, wc, layer, proj, seq, tm, tn):
    t, kw = ya.shape
    d = wa.shape[2]
    go = COL_G // tn
    nbt = seq // tm
    row = lambda w: pl.BlockSpec((tm, w), lambda i, j: (i, 0))
    grp = lambda r, w: pl.BlockSpec((1, r, tm // r, w), lambda i, j: (i // nbt, 0, i % nbt, 0))
    wsp = pl.BlockSpec((None, kw, d), lambda i, j: (layer, 0, 0), pipeline_mode=pl.Buffered(1))
    gate = lambda o: pl.BlockSpec((tm, tn), lambda i, j: (i, go + o * (d // tn) + j))
    dils = [r for _, r in ATTN_GROUPS]
    return pl.pallas_call(
        _mix_body,
        grid=(t // tm, d // tn),
        in_specs=[row(kw)] + [grp(r, kw) for r in dils] + [grp(r, LANES) for r in dils] + [row(kw)]
        + [wsp, wsp, wsp, gate(0), gate(1), gate(2)],
        out_specs=pl.BlockSpec((tm, tn), lambda i, j: (i, j)),
        out_shape=SDS((t, d), BF16),
        scratch_shapes=[pltpu.VMEM((tm, kw), BF16), pltpu.VMEM((len(dils), tm, LANES), F32),
                        pltpu.VMEM((ATTN_HEADS, tm, LANES), F32)],
        compiler_params=_params(("parallel", "arbitrary")),
        name="gated_mix",
    )(ya, *outs, *lses, yc, wa, wb, wc, proj, proj, proj)


def _out_ffn_body(final, x_ref, m_ref, wo_ref, g_ref, w1_ref, w2_ref, fg_ref, o_ref, xn_ref):
    j = pl.program_id(1)

    @pl.when(j == 0)
    def _():
        x1 = x_ref[...] + _dot(m_ref[...], wo_ref[...])
        xn_ref[...] = _rms(x1, g_ref[...]).astype(BF16)
        o_ref[...] = x1

    hid = jnp.maximum(_dot(xn_ref[...], w1_ref[...]), 0.0)
    o_ref[...] += _dot((hid * hid).astype(BF16), w2_ref[...])

    if final:
        @pl.when(j == pl.num_programs(1) - 1)
        def _():
            o_ref[...] = _rms(o_ref[...], fg_ref[...])


def out_ffn(x2d, mix, w_out, g, w1, w2, layer, final_g, final, tm, th):
    t, d = x2d.shape
    hdim = w1.shape[2]
    const = lambda shape: pl.BlockSpec(shape, lambda i, j: (0, 0))
    return pl.pallas_call(
        functools.partial(_out_ffn_body, final),
        grid=(t // tm, hdim // th),
        in_specs=[pl.BlockSpec((tm, d), lambda i, j: (i, 0)),
                  pl.BlockSpec((tm, d), lambda i, j: (i, 0)),
                  pl.BlockSpec((None, d, d), lambda i, j: (layer, 0, 0), pipeline_mode=pl.Buffered(1)),
                  const((1, d)),
                  pl.BlockSpec((None, d, th), lambda i, j: (layer, 0, j)),
                  pl.BlockSpec((None, th, d), lambda i, j: (layer, j, 0)),
                  const((1, d))],
        out_specs=pl.BlockSpec((tm, d), lambda i, j: (i, 0)),
        out_shape=SDS((t, d), F32),
        scratch_shapes=[pltpu.VMEM((tm, d), BF16)],
        compiler_params=_params(("parallel", "arbitrary")),
        name="out_ffn",
    )(x2d, mix, w_out, g, w1, w2, final_g)


def _tile(n, want):
    t = min(n, want)
    assert n % t == 0, (n, want)
    return t


def _projection_weights(w_in):
    n_gate0 = 4 * MLSTM_WIDTH
    n_att0 = n_gate0 + 2 * MLSTM_HEADS
    n_att = 3 * ATTN_HEADS_TOTAL * ATTN_HEAD_DIM
    n_u0 = n_att0 + n_att
    w_main =jnp.concatenate([w_in[:, :, :COL_U], w_in[:, :, n_u0:n_u0 + S5_WIDTH], w_in[:, :, COL_U:n_gate0],
                              w_in[:, :, n_u0 + S5_WIDTH:], w_in[:, :, n_att0:n_u0]], axis=2).astype(BF16)
    w_if = jnp.pad(w_in[:, :, n_gate0:n_att0], ((0, 0), (0, 0), (0, N_GATE_PAD - 2 * MLSTM_HEADS))).astype(BF16)
    return w_main, w_if


def _layer(x2d, bias, batch, seq, layer, p, w, final_g, final):
    t, d = x2d.shape
    tm = _tile(seq, 1024)
    n_groups = len(ATTN_GROUPS)
    proj, gif, gif_t, *qkvs = in_proj(x2d, p["norm1_g"].reshape(1, d), w["w_main"], w["w_if"], layer,
                                      batch, seq, tm)

    bif = jnp.concatenate([p["b_igate"], p["b_fgate"]]).astype(F32)
    bcol = jnp.pad(bif, (0, N_GATE_PAD - bif.shape[0])).reshape(1, N_GATE_PAD)
    ya = mlstm(proj, gif, gif_t, bcol, bif.reshape(8, 1), p["conv_w"], p["conv_b"].reshape(1, -1),
               p["mh_norm_g"].reshape(1, -1), batch, seq)

    outs, lses = zip(*[attn_group(qkvs[g], bias, g) for g in range(n_groups)])

    tmat, wmat, vmat, coef = s5_prep(p["lam_re"], p["lam_im"], p["log_dt"], p["b_re"], p["b_im"],
                                     p["c_re"], p["c_im"], p["d_skip"])
    z = s5_scan(proj, tmat, wmat, vmat, coef, batch, seq)
    yc = glu(z, w["w_glu"], layer, tm, 1024)

    mix = gated_mix(ya, outs, lses, yc, w["w_br_a"], w["w_br_b"], w["w_br_c"], layer, proj, seq,
                    _tile(seq, 512), 1024)
    return out_ffn(x2d, mix, w["w_out"], p["norm2_g"].reshape(1, d), w["w_ff1"], w["w_ff2"], layer,
                   final_g.reshape(1, d), final, _tile(t, 512), 1024)


_SMALL = ("norm1_g", "conv_w", "conv_b", "b_igate", "b_fgate", "mh_norm_g", "lam_re", "lam_im", "log_dt",
          "b_re", "b_im", "c_re", "c_im", "d_skip", "norm2_g")


def kernel(x, norm1_g, w_in, conv_w, conv_b, b_igate, b_fgate, mh_norm_g, rel_bias, lam_re, lam_im, log_dt,
           b_re, b_im, c_re, c_im, d_skip, w_glu, w_br_a, w_br_b, w_br_c, w_out, norm2_g, w_ff1, w_ff2,
           final_g):
    small = dict(norm1_g=norm1_g, conv_w=conv_w, conv_b=conv_b, b_igate=b_igate, b_fgate=b_fgate,
                 mh_norm_g=mh_norm_g, lam_re=lam_re, lam_im=lam_im, log_dt=log_dt, b_re=b_re, b_im=b_im,
                 c_re=c_re, c_im=c_im, d_skip=d_skip, norm2_g=norm2_g)
    w_main, w_if = _projection_weights(w_in)
    weights = dict(w_main=w_main, w_if=w_if, w_glu=w_glu.astype(BF16), w_br_a=w_br_a.astype(BF16),
                   w_br_b=w_br_b.astype(BF16), w_br_c=w_br_c.astype(BF16), w_out=w_out.astype(BF16),
                   w_ff1=w_ff1.astype(BF16), w_ff2=w_ff2.astype(BF16))
    batch, seq, d = x.shape
    x2d = x.astype(F32).reshape(batch * seq, d)
    bias = attn_bias(rel_bias.astype(F32))
    depth = w_in.shape[0]
    for l in range(depth):
        x2d = _layer(x2d, bias, batch, seq, l, {k: small[k][l] for k in _SMALL}, weights, final_g,
                     l == depth - 1)
    return x2d.reshape(batch, seq, d).astype(x.dtype)
```

```python
import functools
import math

import jax
import jax.numpy as jnp
from jax import lax
from jax.experimental import pallas as pl
from jax.experimental.pallas import tpu as pltpu

F32 = jnp.float32
BF16 = jnp.bfloat16
SDS = jax.ShapeDtypeStruct

NORM_EPS = 1e-6
NEG = -1e30

MLSTM_HEADS = 4
MLSTM_HEAD_DIM = 256
MLSTM_WIDTH = MLSTM_HEADS * MLSTM_HEAD_DIM
CONV_WIDTH = 4
ATTN_GROUPS = ((128, 1), (512, 4), (2048, 16))
ATTN_HEADS = 8
ATTN_HEAD_DIM = 128
ATTN_WIDTH = ATTN_HEADS * ATTN_HEAD_DIM
ATTN_HEADS_TOTAL = len(ATTN_GROUPS) * ATTN_HEADS
ATTN_BLOCK = 128
ATTN_Q_BLOCKS = 8
REL_BUCKETS = 32
REL_MAX_DISTANCE = 2048
S5_WIDTH = 1024
S5_GROUP = 16
S5_GROUPS = S5_WIDTH // S5_GROUP
S5_STATE = 64
S5_TOK = 16
N_GATE_PAD = 128

COL_U = 3 * MLSTM_WIDTH
COL_O = COL_U + S5_WIDTH
COL_G = COL_O + MLSTM_WIDTH
D_MODEL = 2048
N_MAIN = COL_G + 3 * D_MODEL
LANES = 128

MLSTM_CHUNK = 256
VMEM_LIMIT = 56 * 2**20


def _params(sem):
    return pltpu.CompilerParams(dimension_semantics=sem, vmem_limit_bytes=VMEM_LIMIT)


def _sigmoid(x):
    return 1.0 / (1.0 + jnp.exp(-x))


def _log_sigmoid(x):
    return jnp.minimum(x, 0.0) - jnp.log(1.0 + jnp.exp(-jnp.abs(x)))


def _rms(x, g):
    ms = jnp.mean(x * x, axis=-1, keepdims=True)
    return x * lax.rsqrt(ms + NORM_EPS) * g


def _dot(a, b):
    return jnp.dot(a, b, preferred_element_type=F32)


def _dot_nt(a, b):
    return lax.dot_general(a, b, (((1,), (1,)), ((), ())), preferred_element_type=F32)


def _dot_tn(a, b):
    return lax.dot_general(a, b, (((0,), (0,)), ((), ())), preferred_element_type=F32)


def _inproj_body(ns, nm, x_ref, g_ref, w_ref, wif_ref, o_ref, oif_ref, oift_ref, a0_ref, a1_ref, a2_ref,
                 xn_ref, acc_ref, acc2_ref):
    j = pl.program_id(1)
    tm = x_ref.shape[0]
    tn = w_ref.shape[1]

    @pl.when(j == 0)
    def _():
        xn = _rms(x_ref[...], g_ref[...]).astype(BF16)
        xn_ref[...] = xn
        gates = _dot(xn, wif_ref[...])
        oif_ref[...] = gates
        oift_ref[...] = jnp.transpose(gates)[:8]

    @pl.when(jnp.logical_and(j < nm, j != ns))
    def _():
        o_ref[...] = _dot(xn_ref[...], w_ref[...]).astype(o_ref.dtype)

    @pl.when(j == ns)
    def _():
        o_ref[...] = _sigmoid(_dot(xn_ref[...], w_ref[...])).astype(o_ref.dtype)

    for g, a_ref in enumerate((a0_ref, a1_ref, a2_ref)):
        r = ATTN_GROUPS[g][1]
        lo = nm + 3 * g

        @pl.when(jnp.logical_and(j >= lo, j < lo + 3))
        def _(a_ref=a_ref, r=r):
            res = _dot(xn_ref[...], w_ref[...])
            if r == 1:
                a_ref[0, 0] = res.astype(a_ref.dtype)
                return
            ns_ = acc_ref.shape[0]
            for base in range(0, tn // LANES, ns_):
                lanes = [slice((base + s) * LANES, (base + s + 1) * LANES) for s in range(ns_)]
                for s in range(ns_):
                    acc_ref[s] = res[:, lanes[s]]
                if r <= 4:
                    for c in range(r):
                        for s in range(ns_):
                            a_ref[0, c, :, lanes[s]] = acc_ref[s, pl.ds(c, tm // r, stride=r), :].astype(a_ref.dtype)
                    continue
                q, r2 = 4, r // 4
                for s in range(ns_):
                    for c0 in range(q):
                        acc2_ref[s, c0 * (tm // q):(c0 + 1) * (tm // q), :] = (
                            acc_ref[s, pl.ds(c0, tm // q, stride=q), :])
                for c0 in range(q):
                    for c1 in range(r2):
                        for s in range(ns_):
                            rows = pl.ds(c0 * (tm // q) + c1, tm // r, stride=r2)
                            a_ref[0, c1 * q + c0, :, lanes[s]] = acc2_ref[s, rows, :].astype(a_ref.dtype)


def in_proj(x2d, g, w, wif, layer, batch, seq, tm):
    t, d = x2d.shape
    tn = ATTN_WIDTH
    nm = N_MAIN // tn
    nbt = seq // tm
    n_groups = len(ATTN_GROUPS)

    def a_spec(gi):
        r = ATTN_GROUPS[gi][1]
        return pl.BlockSpec((1, r, tm // r, tn),
                            lambda i, j: (i // nbt, 0, i % nbt, jnp.clip(j - nm - 3 * gi, 0, 2)))

    def w_col(j):
        jj = jnp.maximum(j - nm, 0)
        return jnp.where(j < nm, j, nm + (jj % 3) * n_groups + jj // 3)

    return pl.pallas_call(
        functools.partial(_inproj_body, COL_O // tn, nm),
        grid=(t // tm, nm + 3 * n_groups),
        in_specs=[pl.BlockSpec((tm, d), lambda i, j: (i, 0)),
                  pl.BlockSpec((1, d), lambda i, j: (0, 0)),
                  pl.BlockSpec((None, d, tn), lambda i, j: (layer, 0, w_col(j))),
                  pl.BlockSpec((None, d, N_GATE_PAD), lambda i, j: (layer, 0, 0))],
        out_specs=[pl.BlockSpec((tm, tn), lambda i, j: (i, jnp.minimum(j, nm - 1))),
                   pl.BlockSpec((tm, N_GATE_PAD), lambda i, j: (i, 0)),
                   pl.BlockSpec((8, tm), lambda i, j: (0, i))] + [a_spec(gi) for gi in range(n_groups)],
        out_shape=[SDS((t, N_MAIN), BF16), SDS((t, N_GATE_PAD), F32), SDS((8, t), F32)]
        + [SDS((batch, r, seq // r, 3 * tn), BF16) for _, r in ATTN_GROUPS],
        scratch_shapes=[pltpu.VMEM((tm, d), BF16), pltpu.VMEM((tn // LANES // 2, tm, LANES), F32),
                        pltpu.VMEM((tn // LANES // 2, tm, LANES), F32)],
        compiler_params=_params(("parallel", "arbitrary")),
        name="in_proj",
    )(x2d, g, w, wif)


def _mlstm_body(q_ref, k_ref, v_ref, og_ref, gcol_ref, grow_ref, bcol_ref, brow_ref,
                cw_ref, cb_ref, ng_ref, y_ref, ct_ref, m_ref, tail_ref):
    c = pl.program_id(1)
    L = q_ref.shape[0]
    E = MLSTM_HEAD_DIM
    H = MLSTM_HEADS

    @pl.when(c == 0)
    def _():
        ct_ref[...] = jnp.zeros_like(ct_ref)
        m_ref[...] = jnp.zeros_like(m_ref)
        tail_ref[...] = jnp.zeros_like(tail_ref)

    row8 = lax.broadcasted_iota(jnp.int32, (8, E), 0)
    tt = lax.broadcasted_iota(jnp.int32, (L, L), 0)
    ss = lax.broadcasted_iota(jnp.int32, (L, L), 1)
    causal = ss <= tt
    gc = gcol_ref[...] + bcol_ref[...]
    gr = grow_ref[...] + brow_ref[...]

    shifts = [jnp.where(tt - ss == d, 1.0, 0.0).astype(BF16) for d in range(1, CONV_WIDTH)]

    def conv_silu(x_ref, h, slot):
        xb = x_ref[:, h * E:(h + 1) * E]
        x = xb.astype(F32)
        tail = tail_ref[slot]
        w = cw_ref[:, slot * E:(slot + 1) * E]
        acc = cb_ref[:, slot * E:(slot + 1) * E] + w[CONV_WIDTH - 1:CONV_WIDTH] * x
        head = jnp.zeros((8, E), F32)
        for d in range(1, CONV_WIDTH):
            wd = w[CONV_WIDTH - 1 - d:CONV_WIDTH - d]
            acc = acc + wd * _dot(shifts[d - 1], xb)
            head = head + wd * jnp.where(row8 < d, pltpu.roll(tail, d, 0), 0.0)
        acc = jnp.concatenate([acc[:8] + head, acc[8:]], axis=0)
        tail_ref[slot] = x[L - 8:]
        return acc * _sigmoid(acc)

    for h in range(H):
        hs = slice(h * E, (h + 1) * E)
        q = conv_silu(q_ref, h, h)
        k = conv_silu(k_ref, h, H + h) * (E ** -0.5)

        ig_col = gc[:, h:h + 1]
        lf_col = _log_sigmoid(gc[:, H + h:H + h + 1])
        ig_row = gr[h:h + 1, :]
        lf_row = _log_sigmoid(gr[H + h:H + h + 1, :])
        bcum_col = jnp.sum(jnp.where(causal, lf_row, 0.0), axis=1, keepdims=True)
        bcum_row = jnp.sum(jnp.where(tt <= ss, lf_col, 0.0), axis=0, keepdims=True)
        a_row = ig_row - bcum_row
        a_col = ig_col - bcum_col

        m_prev = m_ref[h]
        amat = jnp.where(causal, a_row, NEG)
        mrow = jnp.maximum(m_prev, jnp.max(amat, axis=1, keepdims=True))
        w_intra = jnp.exp(amat - mrow)
        w_inter = jnp.exp(m_prev - mrow)

        qb = q.astype(BF16)
        kb = k.astype(BF16)
        vaug = jnp.concatenate([v_ref[:, hs], jnp.ones((L, LANES), BF16)], axis=1)
        s = _dot_nt(qb, kb) * w_intra
        ct = ct_ref[h]
        num_aug = _dot(s.astype(BF16), vaug) + w_inter * _dot(qb, ct.astype(BF16))
        num = num_aug[:, :E]
        den = num_aug[:, E:E + 1]
        m_t = bcum_col + mrow
        hout = num / jnp.maximum(jnp.abs(den), jnp.exp(-m_t))
        hn = _rms(hout, ng_ref[:, hs])
        y_ref[:, hs] = (og_ref[:, hs].astype(F32) * hn).astype(y_ref.dtype)

        b_last = jnp.sum(lf_col, axis=0, keepdims=True)
        g_col = b_last + a_col
        m_new = jnp.maximum(b_last + m_prev, jnp.max(g_col, axis=0, keepdims=True))
        w_s = jnp.exp(g_col - m_new)
        decay = jnp.exp(b_last + m_prev - m_new)
        kw = (k * w_s).astype(BF16)
        ct_ref[h] = decay * ct + _dot_tn(kw, vaug)
        m_ref[h] = m_new


def mlstm(proj, gif, gif_t, bcol, brow, conv_w, conv_b, ng, batch, seq):
    L = min(MLSTM_CHUNK, seq)
    E = MLSTM_HEAD_DIM
    W = MLSTM_WIDTH
    nc = seq // L
    H = MLSTM_HEADS
    t = batch * seq
    row = lambda b, c: b * nc + c
    return pl.pallas_call(
        _mlstm_body,
        grid=(batch, nc),
        in_specs=[pl.BlockSpec((L, W), lambda b, c: (row(b, c), 0)),
                  pl.BlockSpec((L, W), lambda b, c: (row(b, c), 1)),
                  pl.BlockSpec((L, W), lambda b, c: (row(b, c), 2)),
                  pl.BlockSpec((L, W), lambda b, c: (row(b, c), COL_O // W)),
                  pl.BlockSpec((L, N_GATE_PAD), lambda b, c: (row(b, c), 0)),
                  pl.BlockSpec((8, L), lambda b, c: (0, row(b, c))),
                  pl.BlockSpec((1, N_GATE_PAD), lambda b, c: (0, 0)),
                  pl.BlockSpec((8, 1), lambda b, c: (0, 0)),
                  pl.BlockSpec((CONV_WIDTH, 2 * W), lambda b, c: (0, 0)),
                  pl.BlockSpec((1, 2 * W), lambda b, c: (0, 0)),
                  pl.BlockSpec((1, W), lambda b, c: (0, 0))],
        out_specs=pl.BlockSpec((L, W), lambda b, c: (row(b, c), 0)),
        out_shape=SDS((t, W), BF16),
        scratch_shapes=[pltpu.VMEM((H, E, E + LANES), F32), pltpu.VMEM((H, 1, 1), F32),
                        pltpu.VMEM((2 * H, 8, E), F32)],
        compiler_params=_params(("parallel", "arbitrary")),
        name="mlstm",
    )(proj, proj, proj, proj, gif, gif_t, bcol, brow, conv_w, conv_b, ng)


def _bias_body(table_ref, o_ref):
    h = pl.program_id(0)
    dil = jnp.where(h < ATTN_HEADS, ATTN_GROUPS[0][1],
                    jnp.where(h < 2 * ATTN_HEADS, ATTN_GROUPS[1][1], ATTN_GROUPS[2][1]))
    shape = (ATTN_BLOCK, 2 * ATTN_BLOCK)
    i = lax.broadcasted_iota(jnp.int32, shape, 0)
    j = lax.broadcasted_iota(jnp.int32, shape, 1)
    rel = ATTN_BLOCK + i - j
    dist = jnp.maximum(rel, 0) * dil
    max_exact = REL_BUCKETS // 2
    nf = jnp.maximum(dist, max_exact).astype(F32)
    large = max_exact + (jnp.log(nf / max_exact) / math.log(REL_MAX_DISTANCE / max_exact)
                         * (REL_BUCKETS - max_exact)).astype(jnp.int32)
    large = jnp.minimum(large, REL_BUCKETS - 1)
    bucket = jnp.where(dist < max_exact, dist, large)
    acc = jnp.zeros(shape, F32)
    for b in range(REL_BUCKETS):
        acc = jnp.where(bucket == b, table_ref[b, h], acc)
    o_ref[0] = jnp.where(rel >= 0, jnp.where(rel <= ATTN_BLOCK, acc, NEG), NEG)


def attn_bias(rel_bias):
    return pl.pallas_call(
        _bias_body,
        grid=(ATTN_HEADS_TOTAL,),
        in_specs=[pl.BlockSpec(memory_space=pltpu.SMEM)],
        out_specs=pl.BlockSpec((1, ATTN_BLOCK, 2 * ATTN_BLOCK), lambda h: (h, 0, 0)),
        out_shape=SDS((ATTN_HEADS_TOTAL, ATTN_BLOCK, 2 * ATTN_BLOCK), F32),
        compiler_params=_params(("arbitrary",)),
        name="attn_bias",
    )(rel_bias)


def _attn_body(q_ref, kp_ref, kc_ref, vp_ref, vc_ref, bias_ref, o_ref, lse_ref):
    n = pl.program_id(2)
    B = ATTN_BLOCK
    E = ATTN_HEAD_DIM
    H = ATTN_HEADS
    NQ = q_ref.shape[2] // B
    scale = E ** -0.5
    hs = [slice(h * E, (h + 1) * E) for h in range(H)]

    def keys(cur_ref, prev_ref, i, sl):
        if i == 0:
            return jnp.concatenate([prev_ref[0, 0, :, sl], cur_ref[0, 0, :B, sl]], axis=0)
        return cur_ref[0, 0, (i - 1) * B:(i + 1) * B, sl]

    key = lax.broadcasted_iota(jnp.int32, (1, 2 * B), 1)
    no_prev = jnp.where(key < B, jnp.where(n > 0, 0.0, NEG), 0.0)
    logits, maxima = [], []
    for i in range(NQ):
        for h, sl in enumerate(hs):
            s = _dot_nt(q_ref[0, 0, i * B:(i + 1) * B, sl], keys(kc_ref, kp_ref, i, sl)) * scale + bias_ref[h]
            if i == 0:
                s = s + no_prev
            logits.append(s)
            maxima.append(jnp.max(s, axis=1, keepdims=True))
    lane = lax.broadcasted_iota(jnp.int32, (B, LANES), 1)
    ones = jnp.ones((2 * B, LANES), BF16)
    for i in range(NQ):
        lse = jnp.zeros((B, LANES), F32)
        for h, sl in enumerate(hs):
            s, m = logits[i * H + h], maxima[i * H + h]
            p = jnp.exp(s - m).astype(BF16)
            v_aug = jnp.concatenate([keys(vc_ref, vp_ref, i, sl), ones], axis=1)
            o_aug = _dot(p, v_aug)
            den = o_aug[:, E:]
            o = o_aug[:, :E] * (1.0 / den)
            o_ref[0, 0, i * B:(i + 1) * B, sl] = o.astype(o_ref.dtype)
            lse = jnp.where(lane == h, m + jnp.log(den), lse)
        lse_ref[0, 0, i * B:(i + 1) * B, :] = lse


def attn_group(qkv, bias, g):
    batch, r, l, _ = qkv.shape
    W = ATTN_WIDTH
    nq = min(ATTN_Q_BLOCKS, l // ATTN_BLOCK)
    rows = nq * ATTN_BLOCK
    cur = (1, 1, rows, W)
    one = (1, 1, ATTN_BLOCK, W)
    prev = lambda n: jnp.maximum(n * nq - 1, 0)
    return pl.pallas_call(
        _attn_body,
        grid=(batch, r, l // rows),
        in_specs=[pl.BlockSpec(cur, lambda b, c, n: (b, c, n, 0)),
                  pl.BlockSpec(one, lambda b, c, n: (b, c, prev(n), 1)),
                  pl.BlockSpec(cur, lambda b, c, n: (b, c, n, 1)),
                  pl.BlockSpec(one, lambda b, c, n: (b, c, prev(n), 2)),
                  pl.BlockSpec(cur, lambda b, c, n: (b, c, n, 2)),
                  pl.BlockSpec((ATTN_HEADS, ATTN_BLOCK, 2 * ATTN_BLOCK), lambda b, c, n: (g, 0, 0))],
        out_specs=[pl.BlockSpec(cur, lambda b, c, n: (b, c, n, 0)),
                   pl.BlockSpec((1, 1, rows, LANES), lambda b, c, n: (b, c, n, 0))],
        out_shape=[SDS((batch, r, l, W), BF16), SDS((batch, r, l, LANES), F32)],
        compiler_params=_params(("parallel", "parallel", "arbitrary")),
        name=f"attn_g{g}",
    )(qkv, qkv, qkv, qkv, qkv, bias)


def _s5_prep_body(lam_ref, lamc_ref, dt_ref, bt_r_ref, bt_i_ref, ce_r_ref, ce_i_ref, d_ref,
                  t_ref, w_ref, v_ref, coef_ref):
    P = S5_STATE
    lr = lam_ref[0, 0:1, :]
    li = lam_ref[0, 1:2, :]
    dt = jnp.exp(dt_ref[0])

    def apow(e, lr_, li_):
        mag = jnp.exp(lr_ * dt * e)
        ang = li_ * dt * e
        return mag * jnp.cos(ang), mag * jnp.sin(ang)

    one = jnp.ones((1, 1), F32)
    ar, ai = apow(one, lr, li)
    nr = ar - 1.0
    den = lr * lr + li * li
    f_re = (nr * lr + ai * li) / den
    f_im = (ai * lr - nr * li) / den
    bt_r = bt_r_ref[0]
    bt_i = bt_i_ref[0]
    bb_r = f_re * bt_r - f_im * bt_i
    bb_i = f_re * bt_i + f_im * bt_r

    lrc = lamc_ref[0, :, 0:1]
    lic = lamc_ref[0, :, 1:2]
    lag = (lax.broadcasted_iota(jnp.int32, (P, 256), 1) // S5_GROUP).astype(F32)
    adr, adi = apow(lag, lrc, lic)
    ce_r = ce_r_ref[0]
    ce_i = ce_i_ref[0]
    ca_r = ce_r * adr - ce_i * adi
    ca_i = ce_r * adi + ce_i * adr
    hp = lax.Precision.HIGHEST
    ks = (jnp.dot(bb_r[:, :P], ca_r, precision=hp, preferred_element_type=F32)
          - jnp.dot(bb_i[:, :P], ca_i, precision=hp, preferred_element_type=F32))
    si = lax.broadcasted_iota(jnp.int32, (S5_GROUP, 256), 0)
    lj = lax.broadcasted_iota(jnp.int32, (S5_GROUP, 256), 1)
    ks = ks + jnp.where(si == lj, d_ref[0], 0.0)
    for s in range(S5_TOK):
        sh = s * S5_GROUP
        blk = ks if s == 0 else jnp.where(lj >= sh, pltpu.roll(ks, sh, 1), 0.0)
        t_ref[0, sh:sh + S5_GROUP, :] = blk.astype(t_ref.dtype)

    half = pl.program_id(0) % 2
    lane128 = lax.broadcasted_iota(jnp.int32, (1, 128), 1)
    mine = (lane128 // P) == half

    pw_r, pw_i = apow(lax.broadcasted_iota(jnp.int32, (S5_TOK, 1), 0).astype(F32), lr, li)
    for s in range(S5_TOK):
        e = S5_TOK - 1 - s
        pr, pi = pw_r[e:e + 1], pw_i[e:e + 1]
        rows = slice(s * S5_GROUP, (s + 1) * S5_GROUP)
        w_ref[0, rows, 0:128] = jnp.where(mine, bb_r * pr - bb_i * pi, 0.0).astype(w_ref.dtype)
        w_ref[0, rows, 128:256] = jnp.where(mine, bb_r * pi + bb_i * pr, 0.0).astype(w_ref.dtype)

    a1r, a1i = adr[:, S5_GROUP:S5_GROUP + 1], adi[:, S5_GROUP:S5_GROUP + 1]
    adr1 = adr * a1r - adi * a1i
    adi1 = adr * a1i + adi * a1r
    v_ref[0] =jnp.zeros(v_ref.shape[1:], v_ref.dtype)
    row0 = pl.multiple_of(half * P, P)
    v_ref[0, pl.ds(row0, P), :] = (ce_r * adr1 - ce_i * adi1).astype(v_ref.dtype)
    row1 = pl.multiple_of(2 * P + half * P, P)
    v_ref[0, pl.ds(row1, P), :] = (-(ce_r * adi1 + ce_i * adr1)).astype(v_ref.dtype)

    ek = jnp.left_shift(S5_TOK, lax.broadcasted_iota(jnp.int32, (16, 1), 0)).astype(F32)
    cr, ci = apow(ek, lr, li)
    coef_ref[0, 0:16, :] = jnp.where(mine, cr, 0.0)
    coef_ref[0, 16:32, :] = jnp.where(mine, ci, 0.0)


def s5_prep(lam_re, lam_im, log_dt, b_re, b_im, c_re, c_im, d_skip):
    G, P, I = b_re.shape
    lam = jnp.stack([jnp.tile(lam_re, (1, 2)), jnp.tile(lam_im, (1, 2))], axis=1)
    lam = jnp.pad(lam, ((0, 0), (0, 6), (0, 0)))
    lamc = jnp.stack([lam_re, lam_im], axis=2)
    dt = log_dt.reshape(G, 1, 1)
    bt_r = jnp.tile(jnp.swapaxes(b_re, 1, 2), (1, 1, 2))
    bt_i = jnp.tile(jnp.swapaxes(b_im, 1, 2), (1, 1, 2))
    ce_r = jnp.tile(jnp.swapaxes(c_re, 1, 2), (1, 1, S5_TOK))
    ce_i = jnp.tile(jnp.swapaxes(c_im, 1, 2), (1, 1, S5_TOK))
    d = jnp.pad(d_skip, ((0, 0), (0, 256 - I))).reshape(G, 1, 256)
    blk = lambda *s: pl.BlockSpec((1,) + s, lambda g: (g, 0, 0))
    return pl.pallas_call(
        _s5_prep_body,
        grid=(G,),
        in_specs=[blk(8, 128), blk(P, 2), blk(1, 1), blk(16, 128), blk(16, 128), blk(P, 256), blk(P, 256),
                  blk(1, 256)],
        out_specs=[blk(256, 256), blk(256, 256), blk(256, 256), blk(32, 128)],
        out_shape=[SDS((G, 256, 256), BF16), SDS((G, 256, 256), BF16), SDS((G, 256, 256), BF16),
                   SDS((G, 32, 128), F32)],
        compiler_params=_params(("arbitrary",)),
        name="s5_prep",
    )(lam, lamc, dt, bt_r, bt_i, ce_r, ce_i, d)


def _gelu_tanh(x):
    return 0.5 * x * (1.0 + jnp.tanh(math.sqrt(2.0 / math.pi) * (x + 0.044715 * (x * x * x))))


def _chunk_transpose(arrs, chunk):
    a = list(arrs)
    n = len(a)
    d = n // 2
    while d >= 1:
        bit = (chunk & d) != 0
        nxt = list(a)
        for i in range(n):
            if i & d == 0:
                lo, hi = a[i], a[i + d]
                nxt[i] = jnp.where(bit, pltpu.roll(hi, d * S5_GROUP, 1), lo)
                nxt[i + d] = jnp.where(bit, hi, pltpu.roll(lo, LANES - d * S5_GROUP, 1))
        a = nxt
        d //= 2
    return a


def _s5_body(u_ref, t_ref, w_ref, v_ref, coef_ref, z_ref, nat_ref, uf_ref, zs_ref):
    S = u_ref.shape[0]
    R = S // S5_TOK
    GL = LANES // S5_GROUP
    RC = min(R, 256)
    chunk = lax.broadcasted_iota(jnp.int32, (RC, LANES), 1) // S5_GROUP

    nat_ref[...] = u_ref[...].astype(F32)

    def to_flat(rc, carry):
        r0 = pl.multiple_of(rc * RC, RC)
        for hf in range(S5_TOK // GL):
            arrs = [nat_ref[pl.ds(r0 * S5_TOK + hf * GL + k, RC, stride=S5_TOK), :] for k in range(GL)]
            for gl, a in enumerate(_chunk_transpose(arrs, chunk)):
                uf_ref[gl, pl.ds(r0, RC), hf * LANES:(hf + 1) * LANES] = a.astype(uf_ref.dtype)
        return carry

    lax.fori_loop(0, R // RC, to_flat, 0)

    row = lax.broadcasted_iota(jnp.int32, (R, LANES), 0)

    def shift_down(a, sh):
        if sh % 8 == 0:
            return jnp.concatenate([jnp.zeros((sh, LANES), F32), a[:R - sh]], axis=0)
        return jnp.where(row < sh, 0.0, pltpu.roll(a, sh, 0))

    def pair(p, carry):
        g0 = 2 * p
        g1 = g0 + 1
        u0 = uf_ref[g0]
        u1 = uf_ref[g1]
        s2 = _dot(u0, w_ref[g0]) + _dot(u1, w_ref[g1])
        xr = s2[:, :LANES]
        xi = s2[:, LANES:]
        k = 0
        while (1 << k) < R:
            cr = coef_ref[g0, k:k + 1, :] + coef_ref[g1, k:k + 1, :]
            ci = coef_ref[g0, 16 + k:17 + k, :] + coef_ref[g1, 16 + k:17 + k, :]
            xrs = shift_down(xr, 1 << k)
            xis = shift_down(xi, 1 << k)
            xr, xi = xr + cr * xrs - ci * xis, xi + cr * xis + ci * xrs
            k += 1
        xp = jnp.concatenate([shift_down(xr, 1), shift_down(xi, 1)], axis=1).astype(BF16)
        zs_ref[g0] = _gelu_tanh(_dot(u0, t_ref[g0]) + _dot(xp, v_ref[g0]))
        zs_ref[g1] = _gelu_tanh(_dot(u1, t_ref[g1]) + _dot(xp, v_ref[g1]))
        return carry

    lax.fori_loop(0, GL // 2, pair, 0)

    def to_nat(rc, carry):
        r0 = pl.multiple_of(rc * RC, RC)
        for hf in range(S5_TOK // GL):
            arrs = [zs_ref[gl, pl.ds(r0, RC), hf * LANES:(hf + 1) * LANES] for gl in range(GL)]
            for k, a in enumerate(_chunk_transpose(arrs, chunk)):
                nat_ref[pl.ds(r0 * S5_TOK + hf * GL + k, RC, stride=S5_TOK), :] = a
        return carry

    lax.fori_loop(0, R // RC, to_nat, 0)
    z_ref[...] = nat_ref[...].astype(z_ref.dtype)


def s5_scan(proj, tm, wm, vm, coef, batch, seq):
    t = proj.shape[0]
    GL = LANES // S5_GROUP
    r = seq // S5_TOK
    ub = COL_U // LANES
    per_g = lambda *s: pl.BlockSpec((GL,) + s, lambda bi, si: (si, 0, 0))
    return pl.pallas_call(
        _s5_body,
        grid=(batch, S5_WIDTH // LANES),
        in_specs=[pl.BlockSpec((seq, LANES), lambda bi, si: (bi, ub + si)),
                  per_g(256, 256), per_g(256, 256), per_g(256, 256), per_g(32, 128)],
        out_specs=pl.BlockSpec((seq, LANES), lambda bi, si: (bi, si)),
        out_shape=SDS((t, S5_WIDTH), BF16),
        scratch_shapes=[pltpu.VMEM((seq, LANES), F32), pltpu.VMEM((GL, r, 2 * LANES), BF16),
                        pltpu.VMEM((GL, r, 2 * LANES), F32)],
        compiler_params=_params(("parallel", "parallel")),
        name="s5_scan",
    )(proj, tm, wm, vm, coef)


def _glu_body(z_ref, wl_ref, wg_ref, o_ref):
    z = z_ref[...]
    o_ref[...] = (_dot(z, wl_ref[...]) * _sigmoid(_dot(z, wg_ref[...]))).astype(o_ref.dtype)


def glu(z, w, layer, tm, tn):
    t, k = z.shape
    n = w.shape[2] // 2
    nj = n // tn
    return pl.pallas_call(
        _glu_body,
        grid=(t // tm, nj),
        in_specs=[pl.BlockSpec((tm, k), lambda i, j: (i, 0)),
                  pl.BlockSpec((None, k, tn), lambda i, j: (layer, 0, j)),
                  pl.BlockSpec((None, k, tn), lambda i, j: (layer, 0, nj + j))],
        out_specs=pl.BlockSpec((tm, tn), lambda i, j: (i, j)),
        out_shape=SDS((t, n), BF16),
        compiler_params=_params(("parallel", "arbitrary")),
        name="glu",
    )(z, w, w)


def _mix_body(ya_ref, o0_ref, o1_ref, o2_ref, l0_ref, l1_ref, l2_ref, yc_ref,
              wa_ref, wb_ref, wc_ref, ga_ref, gb_ref, gc_ref, out_ref, yb_ref, wt_ref, acc_ref):
    tm = ya_ref.shape[0]
    E = ATTN_HEAD_DIM

    @pl.when(pl.program_id(1) == 0)
    def _():
        o_refs = (o0_ref, o1_ref, o2_ref)
        l_refs = (l0_ref, l1_ref, l2_ref)
        for g, (_, r) in enumerate(ATTN_GROUPS):
            for c in range(r):
                wt_ref[g, pl.ds(c, tm // r, stride=r), :] = l_refs[g][0, c]
        l0, l1, l2 = wt_ref[0], wt_ref[1], wt_ref[2]
        m = jnp.maximum(jnp.maximum(l0, l1), l2)
        e0, e1, e2 = jnp.exp(l0 - m), jnp.exp(l1 - m), jnp.exp(l2 - m)
        inv = 1.0 / (e0 + e1 + e2)
        wt_ref[0] = e0 * inv
        wt_ref[1] = e1 * inv
        wt_ref[2] = e2 * inv
        order = sorted(range(len(ATTN_GROUPS)), key=lambda g: -ATTN_GROUPS[g][1])
        for pos, g in enumerate(order[:-1]):
            r = ATTN_GROUPS[g][1]
            for c in range(r):
                rows = pl.ds(c, tm // r, stride=r)
                w = wt_ref[g, rows, :]
                for h in range(ATTN_HEADS):
                    part = w[:, h:h + 1] * o_refs[g][0, c, :, h * E:(h + 1) * E].astype(F32)
                    if pos == 0:
                        acc_ref[h, rows, :] = part
                    else:
                        acc_ref[h, rows, :] += part
        g = order[-1]
        assert ATTN_GROUPS[g][1] == 1
        w = wt_ref[g]
        for h in range(ATTN_HEADS):
            sl = slice(h * E, (h + 1) * E)
            yb = acc_ref[h] + w[:, h:h + 1] * o_refs[g][0, 0, :, sl].astype(F32)
            yb_ref[:, sl] = yb.astype(yb_ref.dtype)

    tn = out_ref.shape[1]
    cols = pl.ds(pl.multiple_of(pl.program_id(1) * tn, tn), tn)
    mix = (_sigmoid(ga_ref[...].astype(F32)) * _dot(ya_ref[...], wa_ref[:, cols])
           + _sigmoid(gb_ref[...].astype(F32)) * _dot(yb_ref[...], wb_ref[:, cols])
           + _sigmoid(gc_ref[...].astype(F32)) * _dot(yc_ref[...], wc_ref[:, cols]))
    out_ref[...] = mix.astype(out_ref.dtype)


def gated_mix(ya, outs, lses, yc, wa, wb, wc, layer, proj, seq, tm, tn):
    t, kw = ya.shape
    d = wa.shape[2]
    go = COL_G // tn
    nbt = seq // tm
    row = lambda w: pl.BlockSpec((tm, w), lambda i, j: (i, 0))
    grp = lambda r, w: pl.BlockSpec((1, r, tm // r, w), lambda i, j: (i // nbt, 0, i % nbt, 0))
    wsp = pl.BlockSpec((None, kw, d), lambda i, j: (layer, 0, 0), pipeline_mode=pl.Buffered(1))
    gate = lambda o: pl.BlockSpec((tm, tn), lambda i, j: (i, go + o * (d // tn) + j))
    dils = [r for _, r in ATTN_GROUPS]
    return pl.pallas_call(
        _mix_body,
        grid=(t // tm, d // tn),
        in_specs=[row(kw)] + [grp(r, kw) for r in dils] + [grp(r, LANES) for r in dils] + [row(kw)]
        + [wsp, wsp, wsp, gate(0), gate(1), gate(2)],
        out_specs=pl.BlockSpec((tm, tn), lambda i, j: (i, j)),
        out_shape=SDS((t, d), BF16),
        scratch_shapes=[pltpu.VMEM((tm, kw), BF16), pltpu.VMEM((len(dils), tm, LANES), F32),
                        pltpu.VMEM((ATTN_HEADS, tm, LANES), F32)],
        compiler_params=_params(("parallel", "arbitrary")),
        name="gated_mix",
    )(ya, *outs, *lses, yc, wa, wb, wc, proj, proj, proj)


def _out_ffn_body(final, x_ref, m_ref, wo_ref, g_ref, w1_ref, w2_ref, fg_ref, o_ref, xn_ref):
    j = pl.program_id(1)

    @pl.when(j == 0)
    def _():
        x1 = x_ref[...] + _dot(m_ref[...], wo_ref[...])
        xn_ref[...] = _rms(x1, g_ref[...]).astype(BF16)
        o_ref[...] = x1

    hid = jnp.maximum(_dot(xn_ref[...], w1_ref[...]), 0.0)
    o_ref[...] += _dot((hid * hid).astype(BF16), w2_ref[...])

    if final:
        @pl.when(j == pl.num_programs(1) - 1)
        def _():
            o_ref[...] = _rms(o_ref[...], fg_ref[...])


def out_ffn(x2d, mix, w_out, g, w1, w2, layer, final_g, final, tm, th):
    t, d = x2d.shape
    hdim = w1.shape[2]
    const = lambda shape: pl.BlockSpec(shape, lambda i, j: (0, 0))
    return pl.pallas_call(
        functools.partial(_out_ffn_body, final),
        grid=(t // tm, hdim // th),
        in_specs=[pl.BlockSpec((tm, d), lambda i, j: (i, 0)),
                  pl.BlockSpec((tm, d), lambda i, j: (i, 0)),
                  pl.BlockSpec((None, d, d), lambda i, j: (layer, 0, 0), pipeline_mode=pl.Buffered(1)),
                  const((1, d)),
                  pl.BlockSpec((None, d, th), lambda i, j: (layer, 0, j)),
                  pl.BlockSpec((None, th, d), lambda i, j: (layer, j, 0)),
                  const((1, d))],
        out_specs=pl.BlockSpec((tm, d), lambda i, j: (i, 0)),
        out_shape=SDS((t, d), F32),
        scratch_shapes=[pltpu.VMEM((tm, d), BF16)],
        compiler_params=_params(("parallel", "arbitrary")),
        name="out_ffn",
    )(x2d, mix, w_out, g, w1, w2, final_g)


def _tile(n, want):
    t = min(n, want)
    assert n % t == 0, (n, want)
    return t


def _projection_weights(w_in):
    n_gate0 = 4 * MLSTM_WIDTH
    n_att0 = n_gate0 + 2 * MLSTM_HEADS
    n_att = 3 * ATTN_HEADS_TOTAL * ATTN_HEAD_DIM
    n_u0 = n_att0 + n_att
    w_main =jnp.concatenate([w_in[:, :, :COL_U], w_in[:, :, n_u0:n_u0 + S5_WIDTH], w_in[:, :, COL_U:n_gate0],
                              w_in[:, :, n_u0 + S5_WIDTH:], w_in[:, :, n_att0:n_u0]], axis=2).astype(BF16)
    w_if = jnp.pad(w_in[:, :, n_gate0:n_att0], ((0, 0), (0, 0), (0, N_GATE_PAD - 2 * MLSTM_HEADS))).astype(BF16)
    return w_main, w_if


def _layer(x2d, bias, batch, seq, layer, p, w, final_g, final):
    t, d = x2d.shape
    tm = _tile(seq, 1024)
    n_groups = len(ATTN_GROUPS)
    proj, gif, gif_t, *qkvs = in_proj(x2d, p["norm1_g"].reshape(1, d), w["w_main"], w["w_if"], layer,
                                      batch, seq, tm)

    bif = jnp.concatenate([p["b_igate"], p["b_fgate"]]).astype(F32)
    bcol = jnp.pad(bif, (0, N_GATE_PAD - bif.shape[0])).reshape(1, N_GATE_PAD)
    ya = mlstm(proj, gif, gif_t, bcol, bif.reshape(8, 1), p["conv_w"], p["conv_b"].reshape(1, -1),
               p["mh_norm_g"].reshape(1, -1), batch, seq)

    outs, lses = zip(*[attn_group(qkvs[g], bias, g) for g in range(n_groups)])

    tmat, wmat, vmat, coef = s5_prep(p["lam_re"], p["lam_im"], p["log_dt"], p["b_re"], p["b_im"],
                                     p["c_re"], p["c_im"], p["d_skip"])
    z = s5_scan(proj, tmat, wmat, vmat, coef, batch, seq)
    yc = glu(z, w["w_glu"], layer, tm, 1024)

    mix = gated_mix(ya, outs, lses, yc, w["w_br_a"], w["w_br_b"], w["w_br_c"], layer, proj, seq,
                    _tile(seq, 512), 1024)
    return out_ffn(x2d, mix, w["w_out"], p["norm2_g"].reshape(1, d), w["w_ff1"], w["w_ff2"], layer,
                   final_g.reshape(1, d), final, _tile(t, 512), 1024)


_SMALL = ("norm1_g", "conv_w", "conv_b", "b_igate", "b_fgate", "mh_norm_g", "lam_re", "lam_im", "log_dt",
          "b_re", "b_im", "c_re", "c_im", "d_skip", "norm2_g")


def kernel(x, norm1_g, w_in, conv_w, conv_b, b_igate, b_fgate, mh_norm_g, rel_bias, lam_re, lam_im, log_dt,
           b_re, b_im, c_re, c_im, d_skip, w_glu, w_br_a, w_br_b, w_br_c, w_out, norm2_g, w_ff1, w_ff2,
           final_g):
    small = dict(norm1_g=norm1_g, conv_w=conv_w, conv_b=conv_b, b_igate=b_igate, b_fgate=b_fgate,
                 mh_norm_g=mh_norm_g, lam_re=lam_re, lam_im=lam_im, log_dt=log_dt, b_re=b_re, b_im=b_im,
                 c_re=c_re, c_im=c_im, d_skip=d_skip, norm2_g=norm2_g)
    w_main, w_if = _projection_weights(w_in)
    weights = dict(w_main=w_main, w_if=w_if, w_glu=w_glu.astype(BF16), w_br_a=w_br_a.astype(BF16),
                   w_br_b=w_br_b.astype(BF16), w_br_c=w_br_c.astype(BF16), w_out=w_out.astype(BF16),
                   w_ff1=w_ff1.astype(BF16), w_ff2=w_ff2.astype(BF16))
    batch, seq, d = x.shape
    x2d = x.astype(F32).reshape(batch * seq, d)
    bias = attn_bias(rel_bias.astype(F32))
    depth = w_in.shape[0]
    for l in range(depth):
        x2d = _layer(x2d, bias, batch, seq, l, {k: small[k][l] for k in _SMALL}, weights, final_g,
                     l == depth - 1)
    return x2d.reshape(batch, seq, d).astype(x.dtype)
```

```python
import functools
import math

import jax
import jax.numpy as jnp
from jax import lax
from jax.experimental import pallas as pl
from jax.experimental.pallas import tpu as pltpu

F32 = jnp.float32
BF16 = jnp.bfloat16
SDS = jax.ShapeDtypeStruct

NORM_EPS = 1e-6
NEG = -1e30

MLSTM_HEADS = 4
MLSTM_HEAD_DIM = 256
MLSTM_WIDTH = MLSTM_HEADS * MLSTM_HEAD_DIM
CONV_WIDTH = 4
ATTN_GROUPS = ((128, 1), (512, 4), (2048, 16))
ATTN_HEADS = 8
ATTN_HEAD_DIM = 128
ATTN_WIDTH = ATTN_HEADS * ATTN_HEAD_DIM
ATTN_HEADS_TOTAL = len(ATTN_GROUPS) * ATTN_HEADS
ATTN_BLOCK = 128
ATTN_Q_BLOCKS = 8
REL_BUCKETS = 32
REL_MAX_DISTANCE = 2048
S5_WIDTH = 1024
S5_GROUP = 16
S5_GROUPS = S5_WIDTH // S5_GROUP
S5_STATE = 64
S5_TOK = 16
N_GATE_PAD = 128

COL_U = 3 * MLSTM_WIDTH
COL_O = COL_U + S5_WIDTH
COL_G = COL_O + MLSTM_WIDTH
D_MODEL = 2048
N_MAIN = COL_G + 3 * D_MODEL
LANES = 128

MLSTM_CHUNK = 256
VMEM_LIMIT = 56 * 2**20


def _params(sem):
    return pltpu.CompilerParams(dimension_semantics=sem, vmem_limit_bytes=VMEM_LIMIT)


def _sigmoid(x):
    return 1.0 / (1.0 + jnp.exp(-x))


def _log_sigmoid(x):
    return jnp.minimum(x, 0.0) - jnp.log(1.0 + jnp.exp(-jnp.abs(x)))


def _rms(x, g):
    ms = jnp.mean(x * x, axis=-1, keepdims=True)
    return x * lax.rsqrt(ms + NORM_EPS) * g


def _dot(a, b):
    return jnp.dot(a, b, preferred_element_type=F32)


def _dot_nt(a, b):
    return lax.dot_general(a, b, (((1,), (1,)), ((), ())), preferred_element_type=F32)


def _dot_tn(a, b):
    return lax.dot_general(a, b, (((0,), (0,)), ((), ())), preferred_element_type=F32)


def _inproj_body(ns, nm, x_ref, g_ref, w_ref, wif_ref, o_ref, oif_ref, oift_ref, a0_ref, a1_ref, a2_ref,
                 xn_ref, acc_ref, acc2_ref):
    j = pl.program_id(1)
    tm = x_ref.shape[0]
    tn = w_ref.shape[1]

    @pl.when(j == 0)
    def _():
        xn = _rms(x_ref[...], g_ref[...]).astype(BF16)
        xn_ref[...] = xn
        gates = _dot(xn, wif_ref[...])
        oif_ref[...] = gates
        oift_ref[...] = jnp.transpose(gates)[:8]

    @pl.when(jnp.logical_and(j < nm, j != ns))
    def _():
        o_ref[...] = _dot(xn_ref[...], w_ref[...]).astype(o_ref.dtype)

    @pl.when(j == ns)
    def _():
        o_ref[...] = _sigmoid(_dot(xn_ref[...], w_ref[...])).astype(o_ref.dtype)

    for g, a_ref in enumerate((a0_ref, a1_ref, a2_ref)):
        r = ATTN_GROUPS[g][1]
        lo = nm + 3 * g

        @pl.when(jnp.logical_and(j >= lo, j < lo + 3))
        def _(a_ref=a_ref, r=r):
            res = _dot(xn_ref[...], w_ref[...])
            if r == 1:
                a_ref[0, 0] = res.astype(a_ref.dtype)
                return
            ns_ = acc_ref.shape[0]
            for base in range(0, tn // LANES, ns_):
                lanes = [slice((base + s) * LANES, (base + s + 1) * LANES) for s in range(ns_)]
                for s in range(ns_):
                    acc_ref[s] = res[:, lanes[s]]
                if r <= 4:
                    for c in range(r):
                        for s in range(ns_):
                            a_ref[0, c, :, lanes[s]] = acc_ref[s, pl.ds(c, tm // r, stride=r), :].astype(a_ref.dtype)
                    continue
                q, r2 = 4, r // 4
                for s in range(ns_):
                    for c0 in range(q):
                        acc2_ref[s, c0 * (tm // q):(c0 + 1) * (tm // q), :] = (
                            acc_ref[s, pl.ds(c0, tm // q, stride=q), :])
                for c0 in range(q):
                    for c1 in range(r2):
                        for s in range(ns_):
                            rows = pl.ds(c0 * (tm // q) + c1, tm // r, stride=r2)
                            a_ref[0, c1 * q + c0, :, lanes[s]] = acc2_ref[s, rows, :].astype(a_ref.dtype)


def in_proj(x2d, g, w, wif, layer, batch, seq, tm):
    t, d = x2d.shape
    tn = ATTN_WIDTH
    nm = N_MAIN // tn
    nbt = seq // tm
    n_groups = len(ATTN_GROUPS)

    def a_spec(gi):
        r = ATTN_GROUPS[gi][1]
        return pl.BlockSpec((1, r, tm // r, tn),
                            lambda i, j: (i // nbt, 0, i % nbt, jnp.clip(j - nm - 3 * gi, 0, 2)))

    def w_col(j):
        jj = jnp.maximum(j - nm, 0)
        return jnp.where(j < nm, j, nm + (jj % 3) * n_groups + jj // 3)

    return pl.pallas_call(
        functools.partial(_inproj_body, COL_O // tn, nm),
        grid=(t // tm, nm + 3 * n_groups),
        in_specs=[pl.BlockSpec((tm, d), lambda i, j: (i, 0)),
                  pl.BlockSpec((1, d), lambda i, j: (0, 0)),
                  pl.BlockSpec((None, d, tn), lambda i, j: (layer, 0, w_col(j))),
                  pl.BlockSpec((None, d, N_GATE_PAD), lambda i, j: (layer, 0, 0))],
        out_specs=[pl.BlockSpec((None, tm, tn), lambda i, j: (jnp.minimum(j, nm - 1), i, 0)),
                   pl.BlockSpec((tm, N_GATE_PAD), lambda i, j: (i, 0)),
                   pl.BlockSpec((8, tm), lambda i, j: (0, i))] + [a_spec(gi) for gi in range(n_groups)],
        out_shape=[SDS((nm, t, tn), BF16), SDS((t, N_GATE_PAD), F32), SDS((8, t), F32)]
        + [SDS((batch, r, seq // r, 3 * tn), BF16) for _, r in ATTN_GROUPS],
        scratch_shapes=[pltpu.VMEM((tm, d), BF16), pltpu.VMEM((tn // LANES // 2, tm, LANES), F32),
                        pltpu.VMEM((tn // LANES // 2, tm, LANES), F32)],
        compiler_params=_params(("parallel", "arbitrary")),
        name="in_proj",
    )(x2d, g, w, wif)


def _mlstm_body(q_ref, k_ref, v_ref, og_ref, gcol_ref, grow_ref, bcol_ref, brow_ref,
                cw_ref, cb_ref, ng_ref, y_ref, ct_ref, m_ref, tail_ref):
    c = pl.program_id(1)
    L = q_ref.shape[0]
    E = MLSTM_HEAD_DIM
    H = MLSTM_HEADS

    @pl.when(c == 0)
    def _():
        ct_ref[...] = jnp.zeros_like(ct_ref)
        m_ref[...] = jnp.zeros_like(m_ref)
        tail_ref[...] = jnp.zeros_like(tail_ref)

    row8 = lax.broadcasted_iota(jnp.int32, (8, E), 0)
    tt = lax.broadcasted_iota(jnp.int32, (L, L), 0)
    ss = lax.broadcasted_iota(jnp.int32, (L, L), 1)
    causal = ss <= tt
    gc = gcol_ref[...] + bcol_ref[...]
    gr = grow_ref[...] + brow_ref[...]

    shifts = [jnp.where(tt - ss == d, 1.0, 0.0).astype(BF16) for d in range(1, CONV_WIDTH)]

    def conv_silu(x_ref, h, slot):
        xb = x_ref[:, h * E:(h + 1) * E]
        x = xb.astype(F32)
        tail = tail_ref[slot]
        w = cw_ref[:, slot * E:(slot + 1) * E]
        acc = cb_ref[:, slot * E:(slot + 1) * E] + w[CONV_WIDTH - 1:CONV_WIDTH] * x
        head = jnp.zeros((8, E), F32)
        for d in range(1, CONV_WIDTH):
            wd = w[CONV_WIDTH - 1 - d:CONV_WIDTH - d]
            acc = acc + wd * _dot(shifts[d - 1], xb)
            head = head + wd * jnp.where(row8 < d, pltpu.roll(tail, d, 0), 0.0)
        acc = jnp.concatenate([acc[:8] + head, acc[8:]], axis=0)
        tail_ref[slot] = x[L - 8:]
        return acc * _sigmoid(acc)

    for h in range(H):
        hs = slice(h * E, (h + 1) * E)
        q = conv_silu(q_ref, h, h)
        k = conv_silu(k_ref, h, H + h) * (E ** -0.5)

        ig_col = gc[:, h:h + 1]
        lf_col = _log_sigmoid(gc[:, H + h:H + h + 1])
        ig_row = gr[h:h + 1, :]
        lf_row = _log_sigmoid(gr[H + h:H + h + 1, :])
        bcum_col = jnp.sum(jnp.where(causal, lf_row, 0.0), axis=1, keepdims=True)
        bcum_row = jnp.sum(jnp.where(tt <= ss, lf_col, 0.0), axis=0, keepdims=True)
        a_row = ig_row - bcum_row
        a_col = ig_col - bcum_col

        m_prev = m_ref[h]
        amat = jnp.where(causal, a_row, NEG)
        mrow = jnp.maximum(m_prev, jnp.max(amat, axis=1, keepdims=True))
        w_intra = jnp.exp(amat - mrow)
        w_inter = jnp.exp(m_prev - mrow)

        qb = q.astype(BF16)
        kb = k.astype(BF16)
        vaug = jnp.concatenate([v_ref[:, hs], jnp.ones((L, LANES), BF16)], axis=1)
        s = _dot_nt(qb, kb) * w_intra
        ct = ct_ref[h]
        num_aug = _dot(s.astype(BF16), vaug) + w_inter * _dot(qb, ct.astype(BF16))
        num = num_aug[:, :E]
        den = num_aug[:, E:E + 1]
        m_t = bcum_col + mrow
        hout = num / jnp.maximum(jnp.abs(den), jnp.exp(-m_t))
        hn = _rms(hout, ng_ref[:, hs])
        y_ref[:, hs] = (og_ref[:, hs].astype(F32) * hn).astype(y_ref.dtype)

        b_last = jnp.sum(lf_col, axis=0, keepdims=True)
        g_col = b_last + a_col
        m_new = jnp.maximum(b_last + m_prev, jnp.max(g_col, axis=0, keepdims=True))
        w_s = jnp.exp(g_col - m_new)
        decay = jnp.exp(b_last + m_prev - m_new)
        kw = (k * w_s).astype(BF16)
        ct_ref[h] = decay * ct + _dot_tn(kw, vaug)
        m_ref[h] = m_new


def mlstm(proj, gif, gif_t, bcol, brow, conv_w, conv_b, ng, batch, seq):
    L = min(MLSTM_CHUNK, seq)
    E = MLSTM_HEAD_DIM
    W = MLSTM_WIDTH
    nc = seq // L
    H = MLSTM_HEADS
    t = batch * seq
    row = lambda b, c: b * nc + c
    return pl.pallas_call(
        _mlstm_body,
        grid=(batch, nc),
        in_specs=[pl.BlockSpec((None, L, W), lambda b, c: (0, row(b, c), 0)),
                  pl.BlockSpec((None, L, W), lambda b, c: (1, row(b, c), 0)),
                  pl.BlockSpec((None, L, W), lambda b, c: (2, row(b, c), 0)),
                  pl.BlockSpec((None, L, W), lambda b, c: (COL_O // W, row(b, c), 0)),
                  pl.BlockSpec((L, N_GATE_PAD), lambda b, c: (row(b, c), 0)),
                  pl.BlockSpec((8, L), lambda b, c: (0, row(b, c))),
                  pl.BlockSpec((1, N_GATE_PAD), lambda b, c: (0, 0)),
                  pl.BlockSpec((8, 1), lambda b, c: (0, 0)),
                  pl.BlockSpec((CONV_WIDTH, 2 * W), lambda b, c: (0, 0)),
                  pl.BlockSpec((1, 2 * W), lambda b, c: (0, 0)),
                  pl.BlockSpec((1, W), lambda b, c: (0, 0))],
        out_specs=pl.BlockSpec((L, W), lambda b, c: (row(b, c), 0)),
        out_shape=SDS((t, W), BF16),
        scratch_shapes=[pltpu.VMEM((H, E, E + LANES), F32), pltpu.VMEM((H, 1, 1), F32),
                        pltpu.VMEM((2 * H, 8, E), F32)],
        compiler_params=_params(("parallel", "arbitrary")),
        name="mlstm",
    )(proj, proj, proj, proj, gif, gif_t, bcol, brow, conv_w, conv_b, ng)


def _bias_body(table_ref, o_ref):
    h = pl.program_id(0)
    dil = jnp.where(h < ATTN_HEADS, ATTN_GROUPS[0][1],
                    jnp.where(h < 2 * ATTN_HEADS, ATTN_GROUPS[1][1], ATTN_GROUPS[2][1]))
    shape = (ATTN_BLOCK, 2 * ATTN_BLOCK)
    i = lax.broadcasted_iota(jnp.int32, shape, 0)
    j = lax.broadcasted_iota(jnp.int32, shape, 1)
    rel = ATTN_BLOCK + i - j
    dist = jnp.maximum(rel, 0) * dil
    max_exact = REL_BUCKETS // 2
    nf = jnp.maximum(dist, max_exact).astype(F32)
    large = max_exact + (jnp.log(nf / max_exact) / math.log(REL_MAX_DISTANCE / max_exact)
                         * (REL_BUCKETS - max_exact)).astype(jnp.int32)
    large = jnp.minimum(large, REL_BUCKETS - 1)
    bucket = jnp.where(dist < max_exact, dist, large)
    acc = jnp.zeros(shape, F32)
    for b in range(REL_BUCKETS):
        acc = jnp.where(bucket == b, table_ref[b, h], acc)
    o_ref[0] = jnp.where(rel >= 0, jnp.where(rel <= ATTN_BLOCK, acc, NEG), NEG)


def attn_bias(rel_bias):
    return pl.pallas_call(
        _bias_body,
        grid=(ATTN_HEADS_TOTAL,),
        in_specs=[pl.BlockSpec(memory_space=pltpu.SMEM)],
        out_specs=pl.BlockSpec((1, ATTN_BLOCK, 2 * ATTN_BLOCK), lambda h: (h, 0, 0)),
        out_shape=SDS((ATTN_HEADS_TOTAL, ATTN_BLOCK, 2 * ATTN_BLOCK), F32),
        compiler_params=_params(("arbitrary",)),
        name="attn_bias",
    )(rel_bias)


def _attn_body(q_ref, kp_ref, kc_ref, vp_ref, vc_ref, bias_ref, o_ref, lse_ref):
    n = pl.program_id(2)
    B = ATTN_BLOCK
    E = ATTN_HEAD_DIM
    H = ATTN_HEADS
    NQ = q_ref.shape[2] // B
    scale = E ** -0.5
    hs = [slice(h * E, (h + 1) * E) for h in range(H)]

    def keys(cur_ref, prev_ref, i, sl):
        if i == 0:
            return jnp.concatenate([prev_ref[0, 0, :, sl], cur_ref[0, 0, :B, sl]], axis=0)
        return cur_ref[0, 0, (i - 1) * B:(i + 1) * B, sl]

    key = lax.broadcasted_iota(jnp.int32, (1, 2 * B), 1)
    no_prev = jnp.where(key < B, jnp.where(n > 0, 0.0, NEG), 0.0)
    logits, maxima = [], []
    for i in range(NQ):
        for h, sl in enumerate(hs):
            s = _dot_nt(q_ref[0, 0, i * B:(i + 1) * B, sl], keys(kc_ref, kp_ref, i, sl)) * scale + bias_ref[h]
            if i == 0:
                s = s + no_prev
            logits.append(s)
            maxima.append(jnp.max(s, axis=1, keepdims=True))
    lane = lax.broadcasted_iota(jnp.int32, (B, LANES), 1)
    ones = jnp.ones((2 * B, LANES), BF16)
    for i in range(NQ):
        lse = jnp.zeros((B, LANES), F32)
        for h, sl in enumerate(hs):
            s, m = logits[i * H + h], maxima[i * H + h]
            p = jnp.exp(s - m).astype(BF16)
            v_aug = jnp.concatenate([keys(vc_ref, vp_ref, i, sl), ones], axis=1)
            o_aug = _dot(p, v_aug)
            den = o_aug[:, E:]
            o = o_aug[:, :E] * (1.0 / den)
            o_ref[0, 0, i * B:(i + 1) * B, sl] = o.astype(o_ref.dtype)
            lse = jnp.where(lane == h, m + jnp.log(den), lse)
        lse_ref[0, 0, i * B:(i + 1) * B, :] = lse


def attn_group(qkv, bias, g):
    batch, r, l, _ = qkv.shape
    W = ATTN_WIDTH
    nq = min(ATTN_Q_BLOCKS, l // ATTN_BLOCK)
    rows = nq * ATTN_BLOCK
    cur = (1, 1, rows, W)
    one = (1, 1, ATTN_BLOCK, W)
    prev = lambda n: jnp.maximum(n * nq - 1, 0)
    return pl.pallas_call(
        _attn_body,
        grid=(batch, r, l // rows),
        in_specs=[pl.BlockSpec(cur, lambda b, c, n: (b, c, n, 0)),
                  pl.BlockSpec(one, lambda b, c, n: (b, c, prev(n), 1)),
                  pl.BlockSpec(cur, lambda b, c, n: (b, c, n, 1)),
                  pl.BlockSpec(one, lambda b, c, n: (b, c, prev(n), 2)),
                  pl.BlockSpec(cur, lambda b, c, n: (b, c, n, 2)),
                  pl.BlockSpec((ATTN_HEADS, ATTN_BLOCK, 2 * ATTN_BLOCK), lambda b, c, n: (g, 0, 0))],
        out_specs=[pl.BlockSpec(cur, lambda b, c, n: (b, c, n, 0)),
                   pl.BlockSpec((1, 1, rows, LANES), lambda b, c, n: (b, c, n, 0))],
        out_shape=[SDS((batch, r, l, W), BF16), SDS((batch, r, l, LANES), F32)],
        compiler_params=_params(("parallel", "parallel", "arbitrary")),
        name=f"attn_g{g}",
    )(qkv, qkv, qkv, qkv, qkv, bias)


def _s5_prep_body(lam_ref, lamc_ref, dt_ref, bt_r_ref, bt_i_ref, ce_r_ref, ce_i_ref, d_ref,
                  t_ref, w_ref, v_ref, coef_ref):
    P = S5_STATE
    lr = lam_ref[0, 0:1, :]
    li = lam_ref[0, 1:2, :]
    dt = jnp.exp(dt_ref[0])

    def apow(e, lr_, li_):
        mag = jnp.exp(lr_ * dt * e)
        ang = li_ * dt * e
        return mag * jnp.cos(ang), mag * jnp.sin(ang)

    one = jnp.ones((1, 1), F32)
    ar, ai = apow(one, lr, li)
    nr = ar - 1.0
    den = lr * lr + li * li
    f_re = (nr * lr + ai * li) / den
    f_im = (ai * lr - nr * li) / den
    bt_r = bt_r_ref[0]
    bt_i = bt_i_ref[0]
    bb_r = f_re * bt_r - f_im * bt_i
    bb_i = f_re * bt_i + f_im * bt_r

    lrc = lamc_ref[0, :, 0:1]
    lic = lamc_ref[0, :, 1:2]
    lag = (lax.broadcasted_iota(jnp.int32, (P, 256), 1) // S5_GROUP).astype(F32)
    adr, adi = apow(lag, lrc, lic)
    ce_r = ce_r_ref[0]
    ce_i = ce_i_ref[0]
    ca_r = ce_r * adr - ce_i * adi
    ca_i = ce_r * adi + ce_i * adr
    hp = lax.Precision.HIGHEST
    ks = (jnp.dot(bb_r[:, :P], ca_r, precision=hp, preferred_element_type=F32)
          - jnp.dot(bb_i[:, :P], ca_i, precision=hp, preferred_element_type=F32))
    si = lax.broadcasted_iota(jnp.int32, (S5_GROUP, 256), 0)
    lj = lax.broadcasted_iota(jnp.int32, (S5_GROUP, 256), 1)
    ks = ks + jnp.where(si == lj, d_ref[0], 0.0)
    for s in range(S5_TOK):
        sh = s * S5_GROUP
        blk = ks if s == 0 else jnp.where(lj >= sh, pltpu.roll(ks, sh, 1), 0.0)
        t_ref[0, sh:sh + S5_GROUP, :] = blk.astype(t_ref.dtype)

    half = pl.program_id(0) % 2
    lane128 = lax.broadcasted_iota(jnp.int32, (1, 128), 1)
    mine = (lane128 // P) == half

    pw_r, pw_i = apow(lax.broadcasted_iota(jnp.int32, (S5_TOK, 1), 0).astype(F32), lr, li)
    for s in range(S5_TOK):
        e = S5_TOK - 1 - s
        pr, pi = pw_r[e:e + 1], pw_i[e:e + 1]
        rows = slice(s * S5_GROUP, (s + 1) * S5_GROUP)
        w_ref[0, rows, 0:128] = jnp.where(mine, bb_r * pr - bb_i * pi, 0.0).astype(w_ref.dtype)
        w_ref[0, rows, 128:256] = jnp.where(mine, bb_r * pi + bb_i * pr, 0.0).astype(w_ref.dtype)

    a1r, a1i = adr[:, S5_GROUP:S5_GROUP + 1], adi[:, S5_GROUP:S5_GROUP + 1]
    adr1 = adr * a1r - adi * a1i
    adi1 = adr * a1i + adi * a1r
    v_ref[0] =jnp.zeros(v_ref.shape[1:], v_ref.dtype)
    row0 = pl.multiple_of(half * P, P)
    v_ref[0, pl.ds(row0, P), :] = (ce_r * adr1 - ce_i * adi1).astype(v_ref.dtype)
    row1 = pl.multiple_of(2 * P + half * P, P)
    v_ref[0, pl.ds(row1, P), :] = (-(ce_r * adi1 + ce_i * adr1)).astype(v_ref.dtype)

    ek = jnp.left_shift(S5_TOK, lax.broadcasted_iota(jnp.int32, (16, 1), 0)).astype(F32)
    cr, ci = apow(ek, lr, li)
    coef_ref[0, 0:16, :] = jnp.where(mine, cr, 0.0)
    coef_ref[0, 16:32, :] = jnp.where(mine, ci, 0.0)


def s5_prep(lam_re, lam_im, log_dt, b_re, b_im, c_re, c_im, d_skip):
    G, P, I = b_re.shape
    lam = jnp.stack([jnp.tile(lam_re, (1, 2)), jnp.tile(lam_im, (1, 2))], axis=1)
    lam = jnp.pad(lam, ((0, 0), (0, 6), (0, 0)))
    lamc = jnp.stack([lam_re, lam_im], axis=2)
    dt = log_dt.reshape(G, 1, 1)
    bt_r = jnp.tile(jnp.swapaxes(b_re, 1, 2), (1, 1, 2))
    bt_i = jnp.tile(jnp.swapaxes(b_im, 1, 2), (1, 1, 2))
    ce_r = jnp.tile(jnp.swapaxes(c_re, 1, 2), (1, 1, S5_TOK))
    ce_i = jnp.tile(jnp.swapaxes(c_im, 1, 2), (1, 1, S5_TOK))
    d = jnp.pad(d_skip, ((0, 0), (0, 256 - I))).reshape(G, 1, 256)
    blk = lambda *s: pl.BlockSpec((1,) + s, lambda g: (g, 0, 0))
    return pl.pallas_call(
        _s5_prep_body,
        grid=(G,),
        in_specs=[blk(8, 128), blk(P, 2), blk(1, 1), blk(16, 128), blk(16, 128), blk(P, 256), blk(P, 256),
                  blk(1, 256)],
        out_specs=[blk(256, 256), blk(256, 256), blk(256, 256), blk(32, 128)],
        out_shape=[SDS((G, 256, 256), BF16), SDS((G, 256, 256), BF16), SDS((G, 256, 256), BF16),
                   SDS((G, 32, 128), F32)],
        compiler_params=_params(("arbitrary",)),
        name="s5_prep",
    )(lam, lamc, dt, bt_r, bt_i, ce_r, ce_i, d)


def _gelu_tanh(x):
    return 0.5 * x * (1.0 + jnp.tanh(math.sqrt(2.0 / math.pi) * (x + 0.044715 * (x * x * x))))


def _chunk_transpose(arrs, chunk):
    a = list(arrs)
    n = len(a)
    d = n // 2
    while d >= 1:
        bit = (chunk & d) != 0
        nxt = list(a)
        for i in range(n):
            if i & d == 0:
                lo, hi = a[i], a[i + d]
                nxt[i] = jnp.where(bit, pltpu.roll(hi, d * S5_GROUP, 1), lo)
                nxt[i + d] = jnp.where(bit, hi, pltpu.roll(lo, LANES - d * S5_GROUP, 1))
        a = nxt
        d //= 2
    return a


def _s5_body(u_ref, t_ref, w_ref, v_ref, coef_ref, z_ref, nat_ref, uf_ref, zs_ref):
    S = u_ref.shape[0]
    R = S // S5_TOK
    GL = LANES // S5_GROUP
    RC = min(R, 256)
    chunk = lax.broadcasted_iota(jnp.int32, (RC, LANES), 1) // S5_GROUP

    nat_ref[...] = u_ref[...].astype(F32)

    def to_flat(rc, carry):
        r0 = pl.multiple_of(rc * RC, RC)
        for hf in range(S5_TOK // GL):
            arrs = [nat_ref[pl.ds(r0 * S5_TOK + hf * GL + k, RC, stride=S5_TOK), :] for k in range(GL)]
            for gl, a in enumerate(_chunk_transpose(arrs, chunk)):
                uf_ref[gl, pl.ds(r0, RC), hf * LANES:(hf + 1) * LANES] = a.astype(uf_ref.dtype)
        return carry

    lax.fori_loop(0, R // RC, to_flat, 0)

    row = lax.broadcasted_iota(jnp.int32, (R, LANES), 0)

    def shift_down(a, sh):
        if sh % 8 == 0:
            return jnp.concatenate([jnp.zeros((sh, LANES), F32), a[:R - sh]], axis=0)
        return jnp.where(row < sh, 0.0, pltpu.roll(a, sh, 0))

    def pair(p, carry):
        g0 = 2 * p
        g1 = g0 + 1
        u0 = uf_ref[g0]
        u1 = uf_ref[g1]
        s2 = _dot(u0, w_ref[g0]) + _dot(u1, w_ref[g1])
        xr = s2[:, :LANES]
        xi = s2[:, LANES:]
        k = 0
        while (1 << k) < R:
            cr = coef_ref[g0, k:k + 1, :] + coef_ref[g1, k:k + 1, :]
            ci = coef_ref[g0, 16 + k:17 + k, :] + coef_ref[g1, 16 + k:17 + k, :]
            xrs = shift_down(xr, 1 << k)
            xis = shift_down(xi, 1 << k)
            xr, xi = xr + cr * xrs - ci * xis, xi + cr * xis + ci * xrs
            k += 1
        xp = jnp.concatenate([shift_down(xr, 1), shift_down(xi, 1)], axis=1).astype(BF16)
        zs_ref[g0] = _gelu_tanh(_dot(u0, t_ref[g0]) + _dot(xp, v_ref[g0]))
        zs_ref[g1] = _gelu_tanh(_dot(u1, t_ref[g1]) + _dot(xp, v_ref[g1]))
        return carry

    lax.fori_loop(0, GL // 2, pair, 0)

    def to_nat(rc, carry):
        r0 = pl.multiple_of(rc * RC, RC)
        for hf in range(S5_TOK // GL):
            arrs = [zs_ref[gl, pl.ds(r0, RC), hf * LANES:(hf + 1) * LANES] for gl in range(GL)]
            for k, a in enumerate(_chunk_transpose(arrs, chunk)):
                nat_ref[pl.ds(r0 * S5_TOK + hf * GL + k, RC, stride=S5_TOK), :] = a
        return carry

    lax.fori_loop(0, R // RC, to_nat, 0)
    z_ref[...] = nat_ref[...].astype(z_ref.dtype)


def s5_scan(proj, tm, wm, vm, coef, batch, seq):
    t = proj.shape[1]
    GL = LANES // S5_GROUP
    r = seq // S5_TOK
    ub = COL_U // proj.shape[2]
    assert proj.shape[2] == S5_WIDTH
    per_g = lambda *s: pl.BlockSpec((GL,) + s, lambda bi, si: (si, 0, 0))
    return pl.pallas_call(
        _s5_body,
        grid=(batch, S5_WIDTH // LANES),
        in_specs=[pl.BlockSpec((None, seq, LANES), lambda bi, si: (ub, bi, si)),
                  per_g(256, 256), per_g(256, 256), per_g(256, 256), per_g(32, 128)],
        out_specs=pl.BlockSpec((seq, LANES), lambda bi, si: (bi, si)),
        out_shape=SDS((t, S5_WIDTH), BF16),
        scratch_shapes=[pltpu.VMEM((seq, LANES), F32), pltpu.VMEM((GL, r, 2 * LANES), BF16),
                        pltpu.VMEM((GL, r, 2 * LANES), F32)],
        compiler_params=_params(("parallel", "parallel")),
        name="s5_scan",
    )(proj, tm, wm, vm, coef)


def _glu_body(z_ref, wl_ref, wg_ref, o_ref):
    z = z_ref[...]
    o_ref[...] = (_dot(z, wl_ref[...]) * _sigmoid(_dot(z, wg_ref[...]))).astype(o_ref.dtype)


def glu(z, w, layer, tm, tn):
    t, k = z.shape
    n = w.shape[2] // 2
    nj = n // tn
    return pl.pallas_call(
        _glu_body,
        grid=(t // tm, nj),
        in_specs=[pl.BlockSpec((tm, k), lambda i, j: (i, 0)),
                  pl.BlockSpec((None, k, tn), lambda i, j: (layer, 0, j)),
                  pl.BlockSpec((None, k, tn), lambda i, j: (layer, 0, nj + j))],
        out_specs=pl.BlockSpec((tm, tn), lambda i, j: (i, j)),
        out_shape=SDS((t, n), BF16),
        compiler_params=_params(("parallel", "arbitrary")),
        name="glu",
    )(z, w, w)


def _mix_body(ya_ref, o0_ref, o1_ref, o2_ref, l0_ref, l1_ref, l2_ref, yc_ref,
              wa_ref, wb_ref, wc_ref, ga_ref, gb_ref, gc_ref, out_ref, yb_ref, wt_ref, acc_ref):
    tm = ya_ref.shape[0]
    E = ATTN_HEAD_DIM

    @pl.when(pl.program_id(1) == 0)
    def _():
        o_refs = (o0_ref, o1_ref, o2_ref)
        l_refs = (l0_ref, l1_ref, l2_ref)
        for g, (_, r) in enumerate(ATTN_GROUPS):
            for c in range(r):
                wt_ref[g, pl.ds(c, tm // r, stride=r), :] = l_refs[g][0, c]
        l0, l1, l2 = wt_ref[0], wt_ref[1], wt_ref[2]
        m = jnp.maximum(jnp.maximum(l0, l1), l2)
        e0, e1, e2 = jnp.exp(l0 - m), jnp.exp(l1 - m), jnp.exp(l2 - m)
        inv = 1.0 / (e0 + e1 + e2)
        wt_ref[0] = e0 * inv
        wt_ref[1] = e1 * inv
        wt_ref[2] = e2 * inv
        order = sorted(range(len(ATTN_GROUPS)), key=lambda g: -ATTN_GROUPS[g][1])
        for pos, g in enumerate(order[:-1]):
            r = ATTN_GROUPS[g][1]
            for c in range(r):
                rows = pl.ds(c, tm // r, stride=r)
                w = wt_ref[g, rows, :]
                for h in range(ATTN_HEADS):
                    part = w[:, h:h + 1] * o_refs[g][0, c, :, h * E:(h + 1) * E].astype(F32)
                    if pos == 0:
                        acc_ref[h, rows, :] = part
                    else:
                        acc_ref[h, rows, :] += part
        g = order[-1]
        assert ATTN_GROUPS[g][1] == 1
        w = wt_ref[g]
        for h in range(ATTN_HEADS):
            sl = slice(h * E, (h + 1) * E)
            yb = acc_ref[h] + w[:, h:h + 1] * o_refs[g][0, 0, :, sl].astype(F32)
            yb_ref[:, sl] = yb.astype(yb_ref.dtype)

    tn = out_ref.shape[1]
    cols = pl.ds(pl.multiple_of(pl.program_id(1) * tn, tn), tn)
    mix = (_sigmoid(ga_ref[...].astype(F32)) * _dot(ya_ref[...], wa_ref[:, cols])
           + _sigmoid(gb_ref[...].astype(F32)) * _dot(yb_ref[...], wb_ref[:, cols])
           + _sigmoid(gc_ref[...].astype(F32)) * _dot(yc_ref[...], wc_ref[:, cols]))
    out_ref[...] = mix.astype(out_ref.dtype)


def gated_mix(ya, outs, lses, yc, wa, wb, wc, layer, proj, seq, tm, tn):
    t, kw = ya.shape
    d = wa.shape[2]
    go = COL_G // tn
    nbt = seq // tm
    row = lambda w: pl.BlockSpec((tm, w), lambda i, j: (i, 0))
    grp = lambda r, w: pl.BlockSpec((1, r, tm // r, w), lambda i, j: (i // nbt, 0, i % nbt, 0))
    wsp = pl.BlockSpec((None, kw, d), lambda i, j: (layer, 0, 0), pipeline_mode=pl.Buffered(1))
    assert proj.shape[2] == tn
    gate = lambda o: pl.BlockSpec((None, tm, tn), lambda i, j: (go + o * (d // tn) + j, i, 0))
    dils = [r for _, r in ATTN_GROUPS]
    return pl.pallas_call(
        _mix_body,
        grid=(t // tm, d // tn),
        in_specs=[row(kw)] + [grp(r, kw) for r in dils] + [grp(r, LANES) for r in dils] + [row(kw)]
        + [wsp, wsp, wsp, gate(0), gate(1), gate(2)],
        out_specs=pl.BlockSpec((tm, tn), lambda i, j: (i, j)),
        out_shape=SDS((t, d), BF16),
        scratch_shapes=[pltpu.VMEM((tm, kw), BF16), pltpu.VMEM((len(dils), tm, LANES), F32),
                        pltpu.VMEM((ATTN_HEADS, tm, LANES), F32)],
        compiler_params=_params(("parallel", "arbitrary")),
        name="gated_mix",
    )(ya, *outs, *lses, yc, wa, wb, wc, proj, proj, proj)


def _out_ffn_body(final, x_ref, m_ref, wo_ref, g_ref, w1_ref, w2_ref, fg_ref, o_ref, xn_ref):
    j = pl.program_id(1)

    @pl.when(j == 0)
    def _():
        x1 = x_ref[...] + _dot(m_ref[...], wo_ref[...])
        xn_ref[...] = _rms(x1, g_ref[...]).astype(BF16)
        o_ref[...] = x1

    hid = jnp.maximum(_dot(xn_ref[...], w1_ref[...]), 0.0)
    o_ref[...] += _dot((hid * hid).astype(BF16), w2_ref[...])

    if final:
        @pl.when(j == pl.num_programs(1) - 1)
        def _():
            o_ref[...] = _rms(o_ref[...], fg_ref[...])


def out_ffn(x2d, mix, w_out, g, w1, w2, layer, final_g, final, tm, th):
    t, d = x2d.shape
    hdim = w1.shape[2]
    const = lambda shape: pl.BlockSpec(shape, lambda i, j: (0, 0))
    return pl.pallas_call(
        functools.partial(_out_ffn_body, final),
        grid=(t // tm, hdim // th),
        in_specs=[pl.BlockSpec((tm, d), lambda i, j: (i, 0)),
                  pl.BlockSpec((tm, d), lambda i, j: (i, 0)),
                  pl.BlockSpec((None, d, d), lambda i, j: (layer, 0, 0), pipeline_mode=pl.Buffered(1)),
                  const((1, d)),
                  pl.BlockSpec((None, d, th), lambda i, j: (layer, 0, j)),
                  pl.BlockSpec((None, th, d), lambda i, j: (layer, j, 0)),
                  const((1, d))],
        out_specs=pl.BlockSpec((tm, d), lambda i, j: (i, 0)),
        out_shape=SDS((t, d), F32),
        scratch_shapes=[pltpu.VMEM((tm, d), BF16)],
        compiler_params=_params(("parallel", "arbitrary")),
        name="out_ffn",
    )(x2d, mix, w_out, g, w1, w2, final_g)


def _tile(n, want):
    t = min(n, want)
    assert n % t == 0, (n, want)
    return t


def _projection_weights(w_in):
    n_gate0 = 4 * MLSTM_WIDTH
    n_att0 = n_gate0 + 2 * MLSTM_HEADS
    n_att = 3 * ATTN_HEADS_TOTAL * ATTN_HEAD_DIM
    n_u0 = n_att0 + n_att
    w_main =jnp.concatenate([w_in[:, :, :COL_U], w_in[:, :, n_u0:n_u0 + S5_WIDTH], w_in[:, :, COL_U:n_gate0],
                              w_in[:, :, n_u0 + S5_WIDTH:], w_in[:, :, n_att0:n_u0]], axis=2).astype(BF16)
    w_if = jnp.pad(w_in[:, :, n_gate0:n_att0], ((0, 0), (0, 0), (0, N_GATE_PAD - 2 * MLSTM_HEADS))).astype(BF16)
    return w_main, w_if


def _layer(x2d, bias, batch, seq, layer, p, w, final_g, final):
    t, d = x2d.shape
    tm = _tile(seq, 1024)
    n_groups = len(ATTN_GROUPS)
    proj, gif, gif_t, *qkvs = in_proj(x2d, p["norm1_g"].reshape(1, d), w["w_main"], w["w_if"], layer,
                                      batch, seq, tm)

    bif = jnp.concatenate([p["b_igate"], p["b_fgate"]]).astype(F32)
    bcol = jnp.pad(bif, (0, N_GATE_PAD - bif.shape[0])).reshape(1, N_GATE_PAD)
    ya = mlstm(proj, gif, gif_t, bcol, bif.reshape(8, 1), p["conv_w"], p["conv_b"].reshape(1, -1),
               p["mh_norm_g"].reshape(1, -1), batch, seq)

    outs, lses = zip(*[attn_group(qkvs[g], bias, g) for g in range(n_groups)])

    tmat, wmat, vmat, coef = s5_prep(p["lam_re"], p["lam_im"], p["log_dt"], p["b_re"], p["b_im"],
                                     p["c_re"], p["c_im"], p["d_skip"])
    z = s5_scan(proj, tmat, wmat, vmat, coef, batch, seq)
    yc = glu(z, w["w_glu"], layer, tm, 1024)

    mix = gated_mix(ya, outs, lses, yc, w["w_br_a"], w["w_br_b"], w["w_br_c"], layer, proj, seq,
                    _tile(seq, 512), 1024)
    return out_ffn(x2d, mix, w["w_out"], p["norm2_g"].reshape(1, d), w["w_ff1"], w["w_ff2"], layer,
                   final_g.reshape(1, d), final, _tile(t, 512), 1024)


_SMALL = ("norm1_g", "conv_w", "conv_b", "b_igate", "b_fgate", "mh_norm_g", "lam_re", "lam_im", "log_dt",
          "b_re", "b_im", "c_re", "c_im", "d_skip", "norm2_g")


def kernel(x, norm1_g, w_in, conv_w, conv_b, b_igate, b_fgate, mh_norm_g, rel_bias, lam_re, lam_im, log_dt,
           b_re, b_im, c_re, c_im, d_skip, w_glu, w_br_a, w_br_b, w_br_c, w_out, norm2_g, w_ff1, w_ff2,
           final_g):
    small = dict(norm1_g=norm1_g, conv_w=conv_w, conv_b=conv_b, b_igate=b_igate, b_fgate=b_fgate,
                 mh_norm_g=mh_norm_g, lam_re=lam_re, lam_im=lam_im, log_dt=log_dt, b_re=b_re, b_im=b_im,
                 c_re=c_re, c_im=c_im, d_skip=d_skip, norm2_g=norm2_g)
    w_main, w_if = _projection_weights(w_in)
    weights = dict(w_main=w_main, w_if=w_if, w_glu=w_glu.astype(BF16), w_br_a=w_br_a.astype(BF16),
                   w_br_b=w_br_b.astype(BF16), w_br_c=w_br_c.astype(BF16), w_out=w_out.astype(BF16),
                   w_ff1=w_ff1.astype(BF16), w_ff2=w_ff2.astype(BF16))
    batch, seq, d = x.shape
    x2d = x.astype(F32).reshape(batch * seq, d)
    bias = attn_bias(rel_bias.astype(F32))
    depth = w_in.shape[0]
    for l in range(depth):
        x2d = _layer(x2d, bias, batch, seq, l, {k: small[k][l] for k in _SMALL}, weights, final_g,
                     l == depth - 1)
    return x2d.reshape(batch, seq, d).astype(x.dtype)
```

```python
import functools
import math

import jax
import jax.numpy as jnp
from jax import lax
from jax.experimental import pallas as pl
from jax.experimental.pallas import tpu as pltpu

F32 = jnp.float32
BF16 = jnp.bfloat16
SDS = jax.ShapeDtypeStruct

NORM_EPS = 1e-6
NEG = -1e30

MLSTM_HEADS = 4
MLSTM_HEAD_DIM = 256
MLSTM_WIDTH = MLSTM_HEADS * MLSTM_HEAD_DIM
CONV_WIDTH = 4
ATTN_GROUPS = ((128, 1), (512, 4), (2048, 16))
ATTN_HEADS = 8
ATTN_HEAD_DIM = 128
ATTN_WIDTH = ATTN_HEADS * ATTN_HEAD_DIM
ATTN_HEADS_TOTAL = len(ATTN_GROUPS) * ATTN_HEADS
ATTN_BLOCK = 128
ATTN_Q_BLOCKS = 8
REL_BUCKETS = 32
REL_MAX_DISTANCE = 2048
S5_WIDTH = 1024
S5_GROUP = 16
S5_GROUPS = S5_WIDTH // S5_GROUP
S5_STATE = 64
S5_TOK = 16
N_GATE_PAD = 128

COL_U = 3 * MLSTM_WIDTH
COL_O = COL_U + S5_WIDTH
COL_G = COL_O + MLSTM_WIDTH
D_MODEL = 2048
N_MAIN = COL_G + 3 * D_MODEL
LANES = 128

MLSTM_CHUNK = 256
VMEM_LIMIT = 56 * 2**20


def _params(sem):
    return pltpu.CompilerParams(dimension_semantics=sem, vmem_limit_bytes=VMEM_LIMIT)


def _sigmoid(x):
    return 1.0 / (1.0 + jnp.exp(-x))


def _log_sigmoid(x):
    return jnp.minimum(x, 0.0) - jnp.log(1.0 + jnp.exp(-jnp.abs(x)))


def _rms(x, g):
    ms = jnp.mean(x * x, axis=-1, keepdims=True)
    return x * lax.rsqrt(ms + NORM_EPS) * g


def _dot(a, b):
    return jnp.dot(a, b, preferred_element_type=F32)


def _dot_nt(a, b):
    return lax.dot_general(a, b, (((1,), (1,)), ((), ())), preferred_element_type=F32)


def _dot_tn(a, b):
    return lax.dot_general(a, b, (((0,), (0,)), ((), ())), preferred_element_type=F32)


def _inproj_body(ns, nm, x_ref, g_ref, w_ref, wif_ref, o_ref, oif_ref, oift_ref, a0_ref, a1_ref, a2_ref,
                 xn_ref, acc_ref, acc2_ref):
    j = pl.program_id(1)
    tm = x_ref.shape[0]
    tn = w_ref.shape[1]

    @pl.when(j == 0)
    def _():
        xn = _rms(x_ref[...], g_ref[...]).astype(BF16)
        xn_ref[...] = xn
        gates = _dot(xn, wif_ref[...])
        oif_ref[...] = gates
        oift_ref[...] = jnp.transpose(gates)[:8]

    @pl.when(jnp.logical_and(j < nm, j != ns))
    def _():
        o_ref[...] = _dot(xn_ref[...], w_ref[...]).astype(o_ref.dtype)

    @pl.when(j == ns)
    def _():
        o_ref[...] = _sigmoid(_dot(xn_ref[...], w_ref[...])).astype(o_ref.dtype)

    for g, a_ref in enumerate((a0_ref, a1_ref, a2_ref)):
        r = ATTN_GROUPS[g][1]
        lo = nm + 3 * g

        @pl.when(jnp.logical_and(j >= lo, j < lo + 3))
        def _(a_ref=a_ref, r=r):
            res = _dot(xn_ref[...], w_ref[...])
            if r == 1:
                a_ref[0, 0] = res.astype(a_ref.dtype)
                return
            ns_ = acc_ref.shape[0]
            for base in range(0, tn // LANES, ns_):
                lanes = [slice((base + s) * LANES, (base + s + 1) * LANES) for s in range(ns_)]
                for s in range(ns_):
                    acc_ref[s] = res[:, lanes[s]]
                if r <= 4:
                    for c in range(r):
                        for s in range(ns_):
                            a_ref[0, c, :, lanes[s]] = acc_ref[s, pl.ds(c, tm // r, stride=r), :].astype(a_ref.dtype)
                    continue
                q, r2 = 4, r // 4
                for s in range(ns_):
                    for c0 in range(q):
                        acc2_ref[s, c0 * (tm // q):(c0 + 1) * (tm // q), :] = (
                            acc_ref[s, pl.ds(c0, tm // q, stride=q), :])
                for c0 in range(q):
                    for c1 in range(r2):
                        for s in range(ns_):
                            rows = pl.ds(c0 * (tm // q) + c1, tm // r, stride=r2)
                            a_ref[0, c1 * q + c0, :, lanes[s]] = acc2_ref[s, rows, :].astype(a_ref.dtype)


def in_proj(x2d, g, w, wif, layer, batch, seq, tm):
    t, d = x2d.shape
    tn = ATTN_WIDTH
    nm = N_MAIN // tn
    nbt = seq // tm
    n_groups = len(ATTN_GROUPS)

    def a_spec(gi):
        r = ATTN_GROUPS[gi][1]
        return pl.BlockSpec((1, r, tm // r, tn),
                            lambda i, j: (i // nbt, 0, i % nbt, jnp.clip(j - nm - 3 * gi, 0, 2)))

    def w_col(j):
        jj = jnp.maximum(j - nm, 0)
        return jnp.where(j < nm, j, nm + (jj % 3) * n_groups + jj // 3)

    return pl.pallas_call(
        functools.partial(_inproj_body, COL_O // tn, nm),
        grid=(t // tm, nm + 3 * n_groups),
        in_specs=[pl.BlockSpec((tm, d), lambda i, j: (i, 0)),
                  pl.BlockSpec((1, d), lambda i, j: (0, 0)),
                  pl.BlockSpec((None, d, tn), lambda i, j: (layer, 0, w_col(j))),
                  pl.BlockSpec((None, d, N_GATE_PAD), lambda i, j: (layer, 0, 0))],
        out_specs=[pl.BlockSpec((tm, tn), lambda i, j: (i, jnp.minimum(j, nm - 1))),
                   pl.BlockSpec((tm, N_GATE_PAD), lambda i, j: (i, 0)),
                   pl.BlockSpec((8, tm), lambda i, j: (0, i))] + [a_spec(gi) for gi in range(n_groups)],
        out_shape=[SDS((t, N_MAIN), BF16), SDS((t, N_GATE_PAD), F32), SDS((8, t), F32)]
        + [SDS((batch, r, seq // r, 3 * tn), BF16) for _, r in ATTN_GROUPS],
        scratch_shapes=[pltpu.VMEM((tm, d), BF16), pltpu.VMEM((tn // LANES // 2, tm, LANES), F32),
                        pltpu.VMEM((tn // LANES // 2, tm, LANES), F32)],
        compiler_params=_params(("parallel", "arbitrary")),
        name="in_proj",
    )(x2d, g, w, wif)


def _mlstm_body(q_ref, k_ref, v_ref, og_ref, gcol_ref, grow_ref, bcol_ref, brow_ref,
                cw_ref, cb_ref, ng_ref, y_ref, ct_ref, m_ref, tail_ref):
    c = pl.program_id(1)
    L = q_ref.shape[0]
    E = MLSTM_HEAD_DIM
    H = MLSTM_HEADS

    @pl.when(c == 0)
    def _():
        ct_ref[...] = jnp.zeros_like(ct_ref)
        m_ref[...] = jnp.zeros_like(m_ref)
        tail_ref[...] = jnp.zeros_like(tail_ref)

    row8 = lax.broadcasted_iota(jnp.int32, (8, E), 0)
    tt = lax.broadcasted_iota(jnp.int32, (L, L), 0)
    ss = lax.broadcasted_iota(jnp.int32, (L, L), 1)
    causal = ss <= tt
    gc = gcol_ref[...] + bcol_ref[...]
    gr = grow_ref[...] + brow_ref[...]

    shifts = [jnp.where(tt - ss == d, 1.0, 0.0).astype(BF16) for d in range(1, CONV_WIDTH)]

    def conv_silu(x_ref, h, slot):
        xb = x_ref[:, h * E:(h + 1) * E]
        x = xb.astype(F32)
        tail = tail_ref[slot]
        w = cw_ref[:, slot * E:(slot + 1) * E]
        acc = cb_ref[:, slot * E:(slot + 1) * E] + w[CONV_WIDTH - 1:CONV_WIDTH] * x
        head = jnp.zeros((8, E), F32)
        for d in range(1, CONV_WIDTH):
            wd = w[CONV_WIDTH - 1 - d:CONV_WIDTH - d]
            acc = acc + wd * _dot(shifts[d - 1], xb)
            head = head + wd * jnp.where(row8 < d, pltpu.roll(tail, d, 0), 0.0)
        acc = jnp.concatenate([acc[:8] + head, acc[8:]], axis=0)
        tail_ref[slot] = x[L - 8:]
        return acc * _sigmoid(acc)

    for h in range(H):
        hs = slice(h * E, (h + 1) * E)
        q = conv_silu(q_ref, h, h)
        k = conv_silu(k_ref, h, H + h) * (E ** -0.5)

        ig_col = gc[:, h:h + 1]
        lf_col = _log_sigmoid(gc[:, H + h:H + h + 1])
        ig_row = gr[h:h + 1, :]
        lf_row = _log_sigmoid(gr[H + h:H + h + 1, :])
        bcum_col = jnp.sum(jnp.where(causal, lf_row, 0.0), axis=1, keepdims=True)
        bcum_row = jnp.sum(jnp.where(tt <= ss, lf_col, 0.0), axis=0, keepdims=True)
        a_row = ig_row - bcum_row
        a_col = ig_col - bcum_col

        m_prev = m_ref[h]
        amat = jnp.where(causal, a_row, NEG)
        mrow = jnp.maximum(m_prev, jnp.max(amat, axis=1, keepdims=True))
        w_intra = jnp.exp(amat - mrow)
        w_inter = jnp.exp(m_prev - mrow)

        qb = q.astype(BF16)
        kb = k.astype(BF16)
        vaug = jnp.concatenate([v_ref[:, hs], jnp.ones((L, LANES), BF16)], axis=1)
        s = _dot_nt(qb, kb) * w_intra
        ct = ct_ref[h]
        num_aug = _dot(s.astype(BF16), vaug) + w_inter * _dot(qb, ct.astype(BF16))
        num = num_aug[:, :E]
        den = num_aug[:, E:E + 1]
        m_t = bcum_col + mrow
        hout = num / jnp.maximum(jnp.abs(den), jnp.exp(-m_t))
        hn = _rms(hout, ng_ref[:, hs])
        y_ref[:, hs] = (og_ref[:, hs].astype(F32) * hn).astype(y_ref.dtype)

        b_last = jnp.sum(lf_col, axis=0, keepdims=True)
        g_col = b_last + a_col
        m_new = jnp.maximum(b_last + m_prev, jnp.max(g_col, axis=0, keepdims=True))
        w_s = jnp.exp(g_col - m_new)
        decay = jnp.exp(b_last + m_prev - m_new)
        kw = (k * w_s).astype(BF16)
        ct_ref[h] = decay * ct + _dot_tn(kw, vaug)
        m_ref[h] = m_new


def mlstm(proj, gif, gif_t, bcol, brow, conv_w, conv_b, ng, batch, seq):
    L = min(MLSTM_CHUNK, seq)
    E = MLSTM_HEAD_DIM
    W = MLSTM_WIDTH
    nc = seq // L
    H = MLSTM_HEADS
    t = batch * seq
    row = lambda b, c: b * nc + c
    return pl.pallas_call(
        _mlstm_body,
        grid=(batch, nc),
        in_specs=[pl.BlockSpec((L, W), lambda b, c: (row(b, c), 0)),
                  pl.BlockSpec((L, W), lambda b, c: (row(b, c), 1)),
                  pl.BlockSpec((L, W), lambda b, c: (row(b, c), 2)),
                  pl.BlockSpec((L, W), lambda b, c: (row(b, c), COL_O // W)),
                  pl.BlockSpec((L, N_GATE_PAD), lambda b, c: (row(b, c), 0)),
                  pl.BlockSpec((8, L), lambda b, c: (0, row(b, c))),
                  pl.BlockSpec((1, N_GATE_PAD), lambda b, c: (0, 0)),
                  pl.BlockSpec((8, 1), lambda b, c: (0, 0)),
                  pl.BlockSpec((CONV_WIDTH, 2 * W), lambda b, c: (0, 0)),
                  pl.BlockSpec((1, 2 * W), lambda b, c: (0, 0)),
                  pl.BlockSpec((1, W), lambda b, c: (0, 0))],
        out_specs=pl.BlockSpec((L, W), lambda b, c: (row(b, c), 0)),
        out_shape=SDS((t, W), BF16),
        scratch_shapes=[pltpu.VMEM((H, E, E + LANES), F32), pltpu.VMEM((H, 1, 1), F32),
                        pltpu.VMEM((2 * H, 8, E), F32)],
        compiler_params=_params(("parallel", "arbitrary")),
        name="mlstm",
    )(proj, proj, proj, proj, gif, gif_t, bcol, brow, conv_w, conv_b, ng)


def _bias_body(table_ref, o_ref):
    h = pl.program_id(0)
    dil = jnp.where(h < ATTN_HEADS, ATTN_GROUPS[0][1],
                    jnp.where(h < 2 * ATTN_HEADS, ATTN_GROUPS[1][1], ATTN_GROUPS[2][1]))
    shape = (ATTN_BLOCK, 2 * ATTN_BLOCK)
    i = lax.broadcasted_iota(jnp.int32, shape, 0)
    j = lax.broadcasted_iota(jnp.int32, shape, 1)
    rel = ATTN_BLOCK + i - j
    dist = jnp.maximum(rel, 0) * dil
    max_exact = REL_BUCKETS // 2
    nf = jnp.maximum(dist, max_exact).astype(F32)
    large = max_exact + (jnp.log(nf / max_exact) / math.log(REL_MAX_DISTANCE / max_exact)
                         * (REL_BUCKETS - max_exact)).astype(jnp.int32)
    large = jnp.minimum(large, REL_BUCKETS - 1)
    bucket = jnp.where(dist < max_exact, dist, large)
    acc = jnp.zeros(shape, F32)
    for b in range(REL_BUCKETS):
        acc = jnp.where(bucket == b, table_ref[b, h], acc)
    o_ref[0] = jnp.where(rel >= 0, jnp.where(rel <= ATTN_BLOCK, acc, NEG), NEG)


def attn_bias(rel_bias):
    return pl.pallas_call(
        _bias_body,
        grid=(ATTN_HEADS_TOTAL,),
        in_specs=[pl.BlockSpec(memory_space=pltpu.SMEM)],
        out_specs=pl.BlockSpec((1, ATTN_BLOCK, 2 * ATTN_BLOCK), lambda h: (h, 0, 0)),
        out_shape=SDS((ATTN_HEADS_TOTAL, ATTN_BLOCK, 2 * ATTN_BLOCK), F32),
        compiler_params=_params(("arbitrary",)),
        name="attn_bias",
    )(rel_bias)


def _attn_body(q_ref, kp_ref, kc_ref, vp_ref, vc_ref, bias_ref, o_ref, lse_ref):
    n = pl.program_id(2)
    B = ATTN_BLOCK
    E = ATTN_HEAD_DIM
    H = ATTN_HEADS
    NQ = q_ref.shape[2] // B
    scale = E ** -0.5
    hs = [slice(h * E, (h + 1) * E) for h in range(H)]

    def keys(cur_ref, prev_ref, i, sl):
        if i == 0:
            return jnp.concatenate([prev_ref[0, 0, :, sl], cur_ref[0, 0, :B, sl]], axis=0)
        return cur_ref[0, 0, (i - 1) * B:(i + 1) * B, sl]

    key = lax.broadcasted_iota(jnp.int32, (1, 2 * B), 1)
    no_prev = jnp.where(key < B, jnp.where(n > 0, 0.0, NEG), 0.0)
    logits, maxima = [], []
    for i in range(NQ):
        for h, sl in enumerate(hs):
            s = _dot_nt(q_ref[0, 0, i * B:(i + 1) * B, sl], keys(kc_ref, kp_ref, i, sl)) * scale + bias_ref[h]
            if i == 0:
                s = s + no_prev
            logits.append(s)
            maxima.append(jnp.max(s, axis=1, keepdims=True))
    lane = lax.broadcasted_iota(jnp.int32, (B, LANES), 1)
    ones = jnp.ones((2 * B, LANES), BF16)
    for i in range(NQ):
        lse = jnp.zeros((B, LANES), F32)
        for h, sl in enumerate(hs):
            s, m = logits[i * H + h], maxima[i * H + h]
            p = jnp.exp(s - m).astype(BF16)
            v_aug = jnp.concatenate([keys(vc_ref, vp_ref, i, sl), ones], axis=1)
            o_aug = _dot(p, v_aug)
            den = o_aug[:, E:]
            o = o_aug[:, :E] * (1.0 / den)
            o_ref[0, 0, i * B:(i + 1) * B, sl] = o.astype(o_ref.dtype)
            lse = jnp.where(lane == h, m + jnp.log(den), lse)
        lse_ref[0, 0, i * B:(i + 1) * B, :] = lse


def attn_group(qkv, bias, g):
    batch, r, l, _ = qkv.shape
    W = ATTN_WIDTH
    nq = min(ATTN_Q_BLOCKS, l // ATTN_BLOCK)
    rows = nq * ATTN_BLOCK
    cur = (1, 1, rows, W)
    one = (1, 1, ATTN_BLOCK, W)
    prev = lambda n: jnp.maximum(n * nq - 1, 0)
    return pl.pallas_call(
        _attn_body,
        grid=(batch, r, l // rows),
        in_specs=[pl.BlockSpec(cur, lambda b, c, n: (b, c, n, 0)),
                  pl.BlockSpec(one, lambda b, c, n: (b, c, prev(n), 1)),
                  pl.BlockSpec(cur, lambda b, c, n: (b, c, n, 1)),
                  pl.BlockSpec(one, lambda b, c, n: (b, c, prev(n), 2)),
                  pl.BlockSpec(cur, lambda b, c, n: (b, c, n, 2)),
                  pl.BlockSpec((ATTN_HEADS, ATTN_BLOCK, 2 * ATTN_BLOCK), lambda b, c, n: (g, 0, 0))],
        out_specs=[pl.BlockSpec(cur, lambda b, c, n: (b, c, n, 0)),
                   pl.BlockSpec((1, 1, rows, LANES), lambda b, c, n: (b, c, n, 0))],
        out_shape=[SDS((batch, r, l, W), BF16), SDS((batch, r, l, LANES), F32)],
        compiler_params=_params(("parallel", "parallel", "arbitrary")),
        name=f"attn_g{g}",
    )(qkv, qkv, qkv, qkv, qkv, bias)


def _s5_prep_body(lam_ref, lamc_ref, dt_ref, bt_r_ref, bt_i_ref, ce_r_ref, ce_i_ref, d_ref,
                  t_ref, w_ref, v_ref, coef_ref):
    P = S5_STATE
    lr = lam_ref[0, 0:1, :]
    li = lam_ref[0, 1:2, :]
    dt = jnp.exp(dt_ref[0])

    def apow(e, lr_, li_):
        mag = jnp.exp(lr_ * dt * e)
        ang = li_ * dt * e
        return mag * jnp.cos(ang), mag * jnp.sin(ang)

    one = jnp.ones((1, 1), F32)
    ar, ai = apow(one, lr, li)
    nr = ar - 1.0
    den = lr * lr + li * li
    f_re = (nr * lr + ai * li) / den
    f_im = (ai * lr - nr * li) / den
    bt_r = bt_r_ref[0]
    bt_i = bt_i_ref[0]
    bb_r = f_re * bt_r - f_im * bt_i
    bb_i = f_re * bt_i + f_im * bt_r

    lrc = lamc_ref[0, :, 0:1]
    lic = lamc_ref[0, :, 1:2]
    lag = (lax.broadcasted_iota(jnp.int32, (P, 256), 1) // S5_GROUP).astype(F32)
    adr, adi = apow(lag, lrc, lic)
    ce_r = ce_r_ref[0]
    ce_i = ce_i_ref[0]
    ca_r = ce_r * adr - ce_i * adi
    ca_i = ce_r * adi + ce_i * adr
    hp = lax.Precision.HIGHEST
    ks = (jnp.dot(bb_r[:, :P], ca_r, precision=hp, preferred_element_type=F32)
          - jnp.dot(bb_i[:, :P], ca_i, precision=hp, preferred_element_type=F32))
    si = lax.broadcasted_iota(jnp.int32, (S5_GROUP, 256), 0)
    lj = lax.broadcasted_iota(jnp.int32, (S5_GROUP, 256), 1)
    ks = ks + jnp.where(si == lj, d_ref[0], 0.0)
    for s in range(S5_TOK):
        sh = s * S5_GROUP
        blk = ks if s == 0 else jnp.where(lj >= sh, pltpu.roll(ks, sh, 1), 0.0)
        t_ref[0, sh:sh + S5_GROUP, :] = blk.astype(t_ref.dtype)

    half = pl.program_id(0) % 2
    lane128 = lax.broadcasted_iota(jnp.int32, (1, 128), 1)
    mine = (lane128 // P) == half

    pw_r, pw_i = apow(lax.broadcasted_iota(jnp.int32, (S5_TOK, 1), 0).astype(F32), lr, li)
    for s in range(S5_TOK):
        e = S5_TOK - 1 - s
        pr, pi = pw_r[e:e + 1], pw_i[e:e + 1]
        rows = slice(s * S5_GROUP, (s + 1) * S5_GROUP)
        w_ref[0, rows, 0:128] = jnp.where(mine, bb_r * pr - bb_i * pi, 0.0).astype(w_ref.dtype)
        w_ref[0, rows, 128:256] = jnp.where(mine, bb_r * pi + bb_i * pr, 0.0).astype(w_ref.dtype)

    a1r, a1i = adr[:, S5_GROUP:S5_GROUP + 1], adi[:, S5_GROUP:S5_GROUP + 1]
    adr1 = adr * a1r - adi * a1i
    adi1 = adr * a1i + adi * a1r
    v_ref[0] =jnp.zeros(v_ref.shape[1:], v_ref.dtype)
    row0 = pl.multiple_of(half * P, P)
    v_ref[0, pl.ds(row0, P), :] = (ce_r * adr1 - ce_i * adi1).astype(v_ref.dtype)
    row1 = pl.multiple_of(2 * P + half * P, P)
    v_ref[0, pl.ds(row1, P), :] = (-(ce_r * adi1 + ce_i * adr1)).astype(v_ref.dtype)

    ek = jnp.left_shift(S5_TOK, lax.broadcasted_iota(jnp.int32, (16, 1), 0)).astype(F32)
    cr, ci = apow(ek, lr, li)
    coef_ref[0, 0:16, :] = jnp.where(mine, cr, 0.0)
    coef_ref[0, 16:32, :] = jnp.where(mine, ci, 0.0)


def s5_prep(lam_re, lam_im, log_dt, b_re, b_im, c_re, c_im, d_skip):
    G, P, I = b_re.shape
    lam = jnp.stack([jnp.tile(lam_re, (1, 2)), jnp.tile(lam_im, (1, 2))], axis=1)
    lam = jnp.pad(lam, ((0, 0), (0, 6), (0, 0)))
    lamc = jnp.stack([lam_re, lam_im], axis=2)
    dt = log_dt.reshape(G, 1, 1)
    bt_r = jnp.tile(jnp.swapaxes(b_re, 1, 2), (1, 1, 2))
    bt_i = jnp.tile(jnp.swapaxes(b_im, 1, 2), (1, 1, 2))
    ce_r = jnp.tile(jnp.swapaxes(c_re, 1, 2), (1, 1, S5_TOK))
    ce_i = jnp.tile(jnp.swapaxes(c_im, 1, 2), (1, 1, S5_TOK))
    d = jnp.pad(d_skip, ((0, 0), (0, 256 - I))).reshape(G, 1, 256)
    blk = lambda *s: pl.BlockSpec((1,) + s, lambda g: (g, 0, 0))
    return pl.pallas_call(
        _s5_prep_body,
        grid=(G,),
        in_specs=[blk(8, 128), blk(P, 2), blk(1, 1), blk(16, 128), blk(16, 128), blk(P, 256), blk(P, 256),
                  blk(1, 256)],
        out_specs=[blk(256, 256), blk(256, 256), blk(256, 256), blk(32, 128)],
        out_shape=[SDS((G, 256, 256), BF16), SDS((G, 256, 256), BF16), SDS((G, 256, 256), BF16),
                   SDS((G, 32, 128), F32)],
        compiler_params=_params(("arbitrary",)),
        name="s5_prep",
    )(lam, lamc, dt, bt_r, bt_i, ce_r, ce_i, d)


def _gelu_tanh(x):
    return 0.5 * x * (1.0 + jnp.tanh(math.sqrt(2.0 / math.pi) * (x + 0.044715 * (x * x * x))))


def _chunk_transpose(arrs, chunk):
    a = list(arrs)
    n = len(a)
    d = n // 2
    while d >= 1:
        bit = (chunk & d) != 0
        nxt = list(a)
        for i in range(n):
            if i & d == 0:
                lo, hi = a[i], a[i + d]
                nxt[i] = jnp.where(bit, pltpu.roll(hi, d * S5_GROUP, 1), lo)
                nxt[i + d] = jnp.where(bit, hi, pltpu.roll(lo, LANES - d * S5_GROUP, 1))
        a = nxt
        d //= 2
    return a


def _s5_body(u_ref, t_ref, w_ref, v_ref, coef_ref, z_ref, nat_ref, uf_ref, zs_ref):
    S = u_ref.shape[0]
    R = S // S5_TOK
    GL = LANES // S5_GROUP
    RC = min(R, 256)
    chunk = lax.broadcasted_iota(jnp.int32, (RC, LANES), 1) // S5_GROUP

    nat_ref[...] = u_ref[...].astype(F32)

    def to_flat(rc, carry):
        r0 = pl.multiple_of(rc * RC, RC)
        for hf in range(S5_TOK // GL):
            arrs = [nat_ref[pl.ds(r0 * S5_TOK + hf * GL + k, RC, stride=S5_TOK), :] for k in range(GL)]
            for gl, a in enumerate(_chunk_transpose(arrs, chunk)):
                uf_ref[gl, pl.ds(r0, RC), hf * LANES:(hf + 1) * LANES] = a.astype(uf_ref.dtype)
        return carry

    lax.fori_loop(0, R // RC, to_flat, 0)

    row = lax.broadcasted_iota(jnp.int32, (R, LANES), 0)

    def shift_down(a, sh):
        if sh % 8 == 0:
            return jnp.concatenate([jnp.zeros((sh, LANES), F32), a[:R - sh]], axis=0)
        return jnp.where(row < sh, 0.0, pltpu.roll(a, sh, 0))

    def pair(p, carry):
        g0 = 2 * p
        g1 = g0 + 1
        u0 = uf_ref[g0]
        u1 = uf_ref[g1]
        s2 = _dot(u0, w_ref[g0]) + _dot(u1, w_ref[g1])
        xr = s2[:, :LANES]
        xi = s2[:, LANES:]
        k = 0
        while (1 << k) < R:
            cr = coef_ref[g0, k:k + 1, :] + coef_ref[g1, k:k + 1, :]
            ci = coef_ref[g0, 16 + k:17 + k, :] + coef_ref[g1, 16 + k:17 + k, :]
            xrs = shift_down(xr, 1 << k)
            xis = shift_down(xi, 1 << k)
            xr, xi = xr + cr * xrs - ci * xis, xi + cr * xis + ci * xrs
            k += 1
        xp = jnp.concatenate([shift_down(xr, 1), shift_down(xi, 1)], axis=1).astype(BF16)
        zs_ref[g0] = _gelu_tanh(_dot(u0, t_ref[g0]) + _dot(xp, v_ref[g0]))
        zs_ref[g1] = _gelu_tanh(_dot(u1, t_ref[g1]) + _dot(xp, v_ref[g1]))
        return carry

    lax.fori_loop(0, GL // 2, pair, 0)

    def to_nat(rc, carry):
        r0 = pl.multiple_of(rc * RC, RC)
        for hf in range(S5_TOK // GL):
            arrs = [zs_ref[gl, pl.ds(r0, RC), hf * LANES:(hf + 1) * LANES] for gl in range(GL)]
            for k, a in enumerate(_chunk_transpose(arrs, chunk)):
                nat_ref[pl.ds(r0 * S5_TOK + hf * GL + k, RC, stride=S5_TOK), :] = a
        return carry

    lax.fori_loop(0, R // RC, to_nat, 0)
    z_ref[...] = nat_ref[...].astype(z_ref.dtype)


def s5_scan(proj, tm, wm, vm, coef, batch, seq):
    t = proj.shape[0]
    GL = LANES // S5_GROUP
    r = seq // S5_TOK
    ub = COL_U // LANES
    per_g = lambda *s: pl.BlockSpec((GL,) + s, lambda bi, si: (si, 0, 0))
    return pl.pallas_call(
        _s5_body,
        grid=(batch, S5_WIDTH // LANES),
        in_specs=[pl.BlockSpec((seq, LANES), lambda bi, si: (bi, ub + si)),
                  per_g(256, 256), per_g(256, 256), per_g(256, 256), per_g(32, 128)],
        out_specs=pl.BlockSpec((seq, LANES), lambda bi, si: (bi, si)),
        out_shape=SDS((t, S5_WIDTH), BF16),
        scratch_shapes=[pltpu.VMEM((seq, LANES), F32), pltpu.VMEM((GL, r, 2 * LANES), BF16),
                        pltpu.VMEM((GL, r, 2 * LANES), F32)],
        compiler_params=_params(("parallel", "parallel")),
        name="s5_scan",
    )(proj, tm, wm, vm, coef)


def _glu_body(z_ref, wl_ref, wg_ref, o_ref):
    z = z_ref[...]
    o_ref[...] = (_dot(z, wl_ref[...]) * _sigmoid(_dot(z, wg_ref[...]))).astype(o_ref.dtype)


def glu(z, w, layer, tm, tn):
    t, k = z.shape
    n = w.shape[2] // 2
    nj = n // tn
    return pl.pallas_call(
        _glu_body,
        grid=(t // tm, nj),
        in_specs=[pl.BlockSpec((tm, k), lambda i, j: (i, 0)),
                  pl.BlockSpec((None, k, tn), lambda i, j: (layer, 0, j)),
                  pl.BlockSpec((None, k, tn), lambda i, j: (layer, 0, nj + j))],
        out_specs=pl.BlockSpec((tm, tn), lambda i, j: (i, j)),
        out_shape=SDS((t, n), BF16),
        compiler_params=_params(("parallel", "arbitrary")),
        name="glu",
    )(z, w, w)


def _mix_body(ya_ref, o0_ref, o1_ref, o2_ref, l0_ref, l1_ref, l2_ref, yc_ref,
              wa_ref, wb_ref, wc_ref, ga_ref, gb_ref, gc_ref, out_ref, yb_ref, wt_ref, acc_ref):
    tm = ya_ref.shape[0]
    E = ATTN_HEAD_DIM

    @pl.when(pl.program_id(1) == 0)
    def _():
        o_refs = (o0_ref, o1_ref, o2_ref)
        l_refs = (l0_ref, l1_ref, l2_ref)
        for g, (_, r) in enumerate(ATTN_GROUPS):
            for c in range(r):
                wt_ref[g, pl.ds(c, tm // r, stride=r), :] = l_refs[g][0, c]
        l0, l1, l2 = wt_ref[0], wt_ref[1], wt_ref[2]
        m = jnp.maximum(jnp.maximum(l0, l1), l2)
        e0, e1, e2 = jnp.exp(l0 - m), jnp.exp(l1 - m), jnp.exp(l2 - m)
        inv = 1.0 / (e0 + e1 + e2)
        wt_ref[0] = e0 * inv
        wt_ref[1] = e1 * inv
        wt_ref[2] = e2 * inv
        order = sorted(range(len(ATTN_GROUPS)), key=lambda g: -ATTN_GROUPS[g][1])
        for pos, g in enumerate(order[:-1]):
            r = ATTN_GROUPS[g][1]
            for c in range(r):
                rows = pl.ds(c, tm // r, stride=r)
                w = wt_ref[g, rows, :]
                for h in range(ATTN_HEADS):
                    part = w[:, h:h + 1] * o_refs[g][0, c, :, h * E:(h + 1) * E].astype(F32)
                    if pos == 0:
                        acc_ref[h, rows, :] = part
                    else:
                        acc_ref[h, rows, :] += part
        g = order[-1]
        assert ATTN_GROUPS[g][1] == 1
        w = wt_ref[g]
        for h in range(ATTN_HEADS):
            sl = slice(h * E, (h + 1) * E)
            yb = acc_ref[h] + w[:, h:h + 1] * o_refs[g][0, 0, :, sl].astype(F32)
            yb_ref[:, sl] = yb.astype(yb_ref.dtype)

    tn = out_ref.shape[1]
    cols = pl.ds(pl.multiple_of(pl.program_id(1) * tn, tn), tn)
    mix = (_sigmoid(ga_ref[...].astype(F32)) * _dot(ya_ref[...], wa_ref[:, cols])
           + _sigmoid(gb_ref[...].astype(F32)) * _dot(yb_ref[...], wb_ref[:, cols])
           + _sigmoid(gc_ref[...].astype(F32)) * _dot(yc_ref[...], wc_ref[:, cols]))
    out_ref[...] = mix.astype(out_ref.dtype)


def gated_mix(ya, outs, lses, yc, wa, wb, wc, layer, proj, seq, tm, tn):
    t, kw = ya.shape
    d = wa.shape[2]
    go = COL_G // tn
    nbt = seq // tm
    row = lambda w: pl.BlockSpec((tm, w), lambda i, j: (i, 0))
    grp = lambda r, w: pl.BlockSpec((1, r, tm // r, w), lambda i, j: (i // nbt, 0, i % nbt, 0))
    wsp = pl.BlockSpec((None, kw, d), lambda i, j: (layer, 0, 0), pipeline_mode=pl.Buffered(1))
    gate = lambda o: pl.BlockSpec((tm, tn), lambda i, j: (i, go + o * (d // tn) + j))
    dils = [r for _, r in ATTN_GROUPS]
    return pl.pallas_call(
        _mix_body,
        grid=(t // tm, d // tn),
        in_specs=[row(kw)] + [grp(r, kw) for r in dils] + [grp(r, LANES) for r in dils] + [row(kw)]
        + [wsp, wsp, wsp, gate(0), gate(1), gate(2)],
        out_specs=pl.BlockSpec((tm, tn), lambda i, j: (i, j)),
        out_shape=SDS((t, d), BF16),
        scratch_shapes=[pltpu.VMEM((tm, kw), BF16), pltpu.VMEM((len(dils), tm, LANES), F32),
                        pltpu.VMEM((ATTN_HEADS, tm, LANES), F32)],
        compiler_params=_params(("parallel", "arbitrary")),
        name="gated_mix",
    )(ya, *outs, *lses, yc, wa, wb, wc, proj, proj, proj)


def _out_ffn_body(final, x_ref, m_ref, wo_ref, g_ref, w1_ref, w2_ref, fg_ref, o_ref, xn_ref):
    j = pl.program_id(1)

    @pl.when(j == 0)
    def _():
        x1 = x_ref[...] + _dot(m_ref[...], wo_ref[...])
        xn_ref[...] = _rms(x1, g_ref[...]).astype(BF16)
        o_ref[...] = x1

    hid = jnp.maximum(_dot(xn_ref[...], w1_ref[...]), 0.0)
    o_ref[...] += _dot((hid * hid).astype(BF16), w2_ref[...])

    if final:
        @pl.when(j == pl.num_programs(1) - 1)
        def _():
            o_ref[...] = _rms(o_ref[...], fg_ref[...])


def out_ffn(x2d, mix, w_out, g, w1, w2, layer, final_g, final, tm, th):
    t, d = x2d.shape
    hdim = w1.shape[2]
    const = lambda shape: pl.BlockSpec(shape, lambda i, j: (0, 0))
    return pl.pallas_call(
        functools.partial(_out_ffn_body, final),
        grid=(t // tm, hdim // th),
        in_specs=[pl.BlockSpec((tm, d), lambda i, j: (i, 0)),
                  pl.BlockSpec((tm, d), lambda i, j: (i, 0)),
                  pl.BlockSpec((None, d, d), lambda i, j: (layer, 0, 0), pipeline_mode=pl.Buffered(1)),
                  const((1, d)),
                  pl.BlockSpec((None, d, th), lambda i, j: (layer, 0, j)),
                  pl.BlockSpec((None, th, d), lambda i, j: (layer, j, 0)),
                  const((1, d))],
        out_specs=pl.BlockSpec((tm, d), lambda i, j: (i, 0)),
        out_shape=SDS((t, d), F32),
        scratch_shapes=[pltpu.VMEM((tm, d), BF16)],
        compiler_params=_params(("parallel", "arbitrary")),
        name="out_ffn",
    )(x2d, mix, w_out, g, w1, w2, final_g)


def _tile(n, want):
    t = min(n, want)
    assert n % t == 0, (n, want)
    return t


def _projection_weights(w_in):
    n_gate0 = 4 * MLSTM_WIDTH
    n_att0 = n_gate0 + 2 * MLSTM_HEADS
    n_att = 3 * ATTN_HEADS_TOTAL * ATTN_HEAD_DIM
    n_u0 = n_att0 + n_att
    wb = w_in.astype(BF16)
    w_main = jnp.concatenate([wb[:, :, :COL_U], wb[:, :, n_u0:n_u0 + S5_WIDTH], wb[:, :, COL_U:n_gate0],
                              wb[:, :, n_u0 + S5_WIDTH:], wb[:, :, n_att0:n_u0]], axis=2)
    w_if = jnp.pad(wb[:, :, n_gate0:n_att0], ((0, 0), (0, 0), (0, N_GATE_PAD - 2 * MLSTM_HEADS)))
    return w_main, w_if


def _layer(x2d, bias, batch, seq, layer, p, w, final_g, final):
    t, d = x2d.shape
    tm = _tile(seq, 1024)
    n_groups = len(ATTN_GROUPS)
    proj, gif, gif_t, *qkvs = in_proj(x2d, p["norm1_g"].reshape(1, d), w["w_main"], w["w_if"], layer,
                                      batch, seq, tm)

    bif = jnp.concatenate([p["b_igate"], p["b_fgate"]]).astype(F32)
    bcol = jnp.pad(bif, (0, N_GATE_PAD - bif.shape[0])).reshape(1, N_GATE_PAD)
    ya = mlstm(proj, gif, gif_t, bcol, bif.reshape(8, 1), p["conv_w"], p["conv_b"].reshape(1, -1),
               p["mh_norm_g"].reshape(1, -1), batch, seq)

    outs, lses = zip(*[attn_group(qkvs[g], bias, g) for g in range(n_groups)])

    tmat, wmat, vmat, coef = s5_prep(p["lam_re"], p["lam_im"], p["log_dt"], p["b_re"], p["b_im"],
                                     p["c_re"], p["c_im"], p["d_skip"])
    z = s5_scan(proj, tmat, wmat, vmat, coef, batch, seq)
    yc = glu(z, w["w_glu"], layer, tm, 1024)

    mix = gated_mix(ya, outs, lses, yc, w["w_br_a"], w["w_br_b"], w["w_br_c"], layer, proj, seq,
                    _tile(seq, 512), 1024)
    return out_ffn(x2d, mix, w["w_out"], p["norm2_g"].reshape(1, d), w["w_ff1"], w["w_ff2"], layer,
                   final_g.reshape(1, d), final, _tile(t, 512), 1024)


_SMALL = ("norm1_g", "conv_w", "conv_b", "b_igate", "b_fgate", "mh_norm_g", "lam_re", "lam_im", "log_dt",
          "b_re", "b_im", "c_re", "c_im", "d_skip", "norm2_g")


def kernel(x, norm1_g, w_in, conv_w, conv_b, b_igate, b_fgate, mh_norm_g, rel_bias, lam_re, lam_im, log_dt,
           b_re, b_im, c_re, c_im, d_skip, w_glu, w_br_a, w_br_b, w_br_c, w_out, norm2_g, w_ff1, w_ff2,
           final_g):
    small = dict(norm1_g=norm1_g, conv_w=conv_w, conv_b=conv_b, b_igate=b_igate, b_fgate=b_fgate,
                 mh_norm_g=mh_norm_g, lam_re=lam_re, lam_im=lam_im, log_dt=log_dt, b_re=b_re, b_im=b_im,
                 c_re=c_re, c_im=c_im, d_skip=d_skip, norm2_g=norm2_g)
    w_main, w_if = _projection_weights(w_in)
    weights = dict(w_main=w_main, w_if=w_if, w_glu=w_glu.astype(BF16), w_br_a=w_br_a.astype(BF16),
                   w_br_b=w_br_b.astype(BF16), w_br_c=w_br_c.astype(BF16), w_out=w_out.astype(BF16),
                   w_ff1=w_ff1.astype(BF16), w_ff2=w_ff2.astype(BF16))
    batch, seq, d = x.shape
    x2d = x.astype(F32).reshape(batch * seq, d)
    bias = attn_bias(rel_bias.astype(F32))
    depth = w_in.shape[0]
    for l in range(depth):
        x2d = _layer(x2d, bias, batch, seq, l, {k: small[k][l] for k in _SMALL}, weights, final_g,
                     l == depth - 1)
    return x2d.reshape(batch, seq, d).astype(x.dtype)
```

```python
import functools
import math

import jax
import jax.numpy as jnp
from jax import lax
from jax.experimental import pallas as pl
from jax.experimental.pallas import tpu as pltpu

F32 = jnp.float32
BF16 = jnp.bfloat16
SDS = jax.ShapeDtypeStruct

NORM_EPS = 1e-6
NEG = -1e30

MLSTM_HEADS = 4
MLSTM_HEAD_DIM = 256
MLSTM_WIDTH = MLSTM_HEADS * MLSTM_HEAD_DIM
CONV_WIDTH = 4
ATTN_GROUPS = ((128, 1), (512, 4), (2048, 16))
ATTN_HEADS = 8
ATTN_HEAD_DIM = 128
ATTN_WIDTH = ATTN_HEADS * ATTN_HEAD_DIM
ATTN_HEADS_TOTAL = len(ATTN_GROUPS) * ATTN_HEADS
ATTN_BLOCK = 128
ATTN_Q_BLOCKS = 8
REL_BUCKETS = 32
REL_MAX_DISTANCE = 2048
S5_WIDTH = 1024
S5_GROUP = 16
S5_GROUPS = S5_WIDTH // S5_GROUP
S5_STATE = 64
S5_TOK = 16
N_GATE_PAD = 128

COL_U = 3 * MLSTM_WIDTH
COL_O = COL_U + S5_WIDTH
COL_G = COL_O + MLSTM_WIDTH
D_MODEL = 2048
N_MAIN = COL_G + 3 * D_MODEL
LANES = 128

MLSTM_CHUNK = 256
VMEM_LIMIT = 56 * 2**20


def _params(sem):
    return pltpu.CompilerParams(dimension_semantics=sem, vmem_limit_bytes=VMEM_LIMIT)


def _sigmoid(x):
    return 1.0 / (1.0 + jnp.exp(-x))


def _log_sigmoid(x):
    return jnp.minimum(x, 0.0) - jnp.log(1.0 + jnp.exp(-jnp.abs(x)))


def _rms(x, g):
    ms = jnp.mean(x * x, axis=-1, keepdims=True)
    return x * lax.rsqrt(ms + NORM_EPS) * g


def _dot(a, b):
    return jnp.dot(a, b, preferred_element_type=F32)


def _dot_nt(a, b):
    return lax.dot_general(a, b, (((1,), (1,)), ((), ())), preferred_element_type=F32)


def _dot_tn(a, b):
    return lax.dot_general(a, b, (((0,), (0,)), ((), ())), preferred_element_type=F32)


def _early_rows(n_tiles, switch):
    return lambda i, j: (jnp.minimum(jnp.where(j >= switch, i + 1, i), n_tiles - 1), 0)


def _inproj_body(ns, nm, x_ref, g_ref, w_ref, wif_ref, o_ref, oif_ref, oift_ref, a0_ref, a1_ref, a2_ref,
                 xn_ref, acc_ref, acc2_ref):
    j = pl.program_id(1)
    tm = x_ref.shape[0]
    tn = w_ref.shape[1]

    @pl.when(j == 0)
    def _():
        xn = _rms(x_ref[...], g_ref[...]).astype(BF16)
        xn_ref[...] = xn
        gates = _dot(xn, wif_ref[...])
        oif_ref[...] = gates
        oift_ref[...] = jnp.transpose(gates)[:8]

    @pl.when(jnp.logical_and(j < nm, j != ns))
    def _():
        o_ref[...] = _dot(xn_ref[...], w_ref[...]).astype(o_ref.dtype)

    @pl.when(j == ns)
    def _():
        o_ref[...] = _sigmoid(_dot(xn_ref[...], w_ref[...])).astype(o_ref.dtype)

    for g, a_ref in enumerate((a0_ref, a1_ref, a2_ref)):
        r = ATTN_GROUPS[g][1]
        lo = nm + 3 * g

        @pl.when(jnp.logical_and(j >= lo, j < lo + 3))
        def _(a_ref=a_ref, r=r):
            res = _dot(xn_ref[...], w_ref[...])
            if r == 1:
                a_ref[0, 0] = res.astype(a_ref.dtype)
                return
            ns_ = acc_ref.shape[0]
            for base in range(0, tn // LANES, ns_):
                lanes = [slice((base + s) * LANES, (base + s + 1) * LANES) for s in range(ns_)]
                for s in range(ns_):
                    acc_ref[s] = res[:, lanes[s]]
                if r <= 4:
                    for c in range(r):
                        for s in range(ns_):
                            a_ref[0, c, :, lanes[s]] = acc_ref[s, pl.ds(c, tm // r, stride=r), :].astype(a_ref.dtype)
                    continue
                q, r2 = 4, r // 4
                for s in range(ns_):
                    for c0 in range(q):
                        acc2_ref[s, c0 * (tm // q):(c0 + 1) * (tm // q), :] = (
                            acc_ref[s, pl.ds(c0, tm // q, stride=q), :])
                for c0 in range(q):
                    for c1 in range(r2):
                        for s in range(ns_):
                            rows = pl.ds(c0 * (tm // q) + c1, tm // r, stride=r2)
                            a_ref[0, c1 * q + c0, :, lanes[s]] = acc2_ref[s, rows, :].astype(a_ref.dtype)


def in_proj(x2d, g, w, wif, layer, batch, seq, tm):
    t, d = x2d.shape
    tn = ATTN_WIDTH
    nm = N_MAIN // tn
    nbt = seq // tm
    n_groups = len(ATTN_GROUPS)

    def a_spec(gi):
        r = ATTN_GROUPS[gi][1]
        return pl.BlockSpec((1, r, tm // r, tn),
                            lambda i, j: (i // nbt, 0, i % nbt, jnp.clip(j - nm - 3 * gi, 0, 2)))

    def w_col(j):
        jj = jnp.maximum(j - nm, 0)
        return jnp.where(j < nm, j, nm + (jj % 3) * n_groups + jj // 3)

    return pl.pallas_call(
        functools.partial(_inproj_body, COL_O // tn, nm),
        grid=(t // tm, nm + 3 * n_groups),
        in_specs=[pl.BlockSpec((tm, d), _early_rows(t // tm, (nm + 3 * n_groups) // 2)),
                  pl.BlockSpec((1, d), lambda i, j: (0, 0)),
                  pl.BlockSpec((None, d, tn), lambda i, j: (layer, 0, w_col(j))),
                  pl.BlockSpec((None, d, N_GATE_PAD), lambda i, j: (layer, 0, 0))],
        out_specs=[pl.BlockSpec((tm, tn), lambda i, j: (i, jnp.minimum(j, nm - 1))),
                   pl.BlockSpec((tm, N_GATE_PAD), lambda i, j: (i, 0)),
                   pl.BlockSpec((8, tm), lambda i, j: (0, i))] + [a_spec(gi) for gi in range(n_groups)],
        out_shape=[SDS((t, N_MAIN), BF16), SDS((t, N_GATE_PAD), F32), SDS((8, t), F32)]
        + [SDS((batch, r, seq // r, 3 * tn), BF16) for _, r in ATTN_GROUPS],
        scratch_shapes=[pltpu.VMEM((tm, d), BF16), pltpu.VMEM((tn // LANES // 2, tm, LANES), F32),
                        pltpu.VMEM((tn // LANES // 2, tm, LANES), F32)],
        compiler_params=_params(("parallel", "arbitrary")),
        name="in_proj",
    )(x2d, g, w, wif)


def _mlstm_body(q_ref, k_ref, v_ref, og_ref, gcol_ref, grow_ref, bcol_ref, brow_ref,
                cw_ref, cb_ref, ng_ref, y_ref, ct_ref, m_ref, tail_ref):
    c = pl.program_id(1)
    L = q_ref.shape[0]
    E = MLSTM_HEAD_DIM
    H = MLSTM_HEADS

    @pl.when(c == 0)
    def _():
        ct_ref[...] = jnp.zeros_like(ct_ref)
        m_ref[...] = jnp.zeros_like(m_ref)
        tail_ref[...] = jnp.zeros_like(tail_ref)

    row8 = lax.broadcasted_iota(jnp.int32, (8, E), 0)
    tt = lax.broadcasted_iota(jnp.int32, (L, L), 0)
    ss = lax.broadcasted_iota(jnp.int32, (L, L), 1)
    causal = ss <= tt
    gc = gcol_ref[...] + bcol_ref[...]
    gr = grow_ref[...] + brow_ref[...]

    shifts = [jnp.where(tt - ss == d, 1.0, 0.0).astype(BF16) for d in range(1, CONV_WIDTH)]

    def conv_silu(x_ref, h, slot):
        xb = x_ref[:, h * E:(h + 1) * E]
        x = xb.astype(F32)
        tail = tail_ref[slot]
        w = cw_ref[:, slot * E:(slot + 1) * E]
        acc = cb_ref[:, slot * E:(slot + 1) * E] + w[CONV_WIDTH - 1:CONV_WIDTH] * x
        head = jnp.zeros((8, E), F32)
        for d in range(1, CONV_WIDTH):
            wd = w[CONV_WIDTH - 1 - d:CONV_WIDTH - d]
            acc = acc + wd * _dot(shifts[d - 1], xb)
            head = head + wd * jnp.where(row8 < d, pltpu.roll(tail, d, 0), 0.0)
        acc = jnp.concatenate([acc[:8] + head, acc[8:]], axis=0)
        tail_ref[slot] = x[L - 8:]
        return acc * _sigmoid(acc)

    for h in range(H):
        hs = slice(h * E, (h + 1) * E)
        q = conv_silu(q_ref, h, h)
        k = conv_silu(k_ref, h, H + h) * (E ** -0.5)

        ig_col = gc[:, h:h + 1]
        lf_col = _log_sigmoid(gc[:, H + h:H + h + 1])
        ig_row = gr[h:h + 1, :]
        lf_row = _log_sigmoid(gr[H + h:H + h + 1, :])
        bcum_col = jnp.sum(jnp.where(causal, lf_row, 0.0), axis=1, keepdims=True)
        bcum_row = jnp.sum(jnp.where(tt <= ss, lf_col, 0.0), axis=0, keepdims=True)
        a_row = ig_row - bcum_row
        a_col = ig_col - bcum_col

        m_prev = m_ref[h]
        amat = jnp.where(causal, a_row, NEG)
        mrow = jnp.maximum(m_prev, jnp.max(amat, axis=1, keepdims=True))
        w_intra = jnp.exp(amat - mrow)
        w_inter = jnp.exp(m_prev - mrow)

        qb = q.astype(BF16)
        kb = k.astype(BF16)
        vaug = jnp.concatenate([v_ref[:, hs], jnp.ones((L, LANES), BF16)], axis=1)
        s = _dot_nt(qb, kb) * w_intra
        ct = ct_ref[h]
        num_aug = _dot(s.astype(BF16), vaug) + w_inter * _dot(qb, ct.astype(BF16))
        num = num_aug[:, :E]
        den = num_aug[:, E:E + 1]
        m_t = bcum_col + mrow
        hout = num / jnp.maximum(jnp.abs(den), jnp.exp(-m_t))
        hn = _rms(hout, ng_ref[:, hs])
        y_ref[:, hs] = (og_ref[:, hs].astype(F32) * hn).astype(y_ref.dtype)

        b_last = jnp.sum(lf_col, axis=0, keepdims=True)
        g_col = b_last + a_col
        m_new = jnp.maximum(b_last + m_prev, jnp.max(g_col, axis=0, keepdims=True))
        w_s = jnp.exp(g_col - m_new)
        decay = jnp.exp(b_last + m_prev - m_new)
        kw = (k * w_s).astype(BF16)
        ct_ref[h] = decay * ct + _dot_tn(kw, vaug)
        m_ref[h] = m_new


def mlstm(proj, gif, gif_t, bcol, brow, conv_w, conv_b, ng, batch, seq):
    L = min(MLSTM_CHUNK, seq)
    E = MLSTM_HEAD_DIM
    W = MLSTM_WIDTH
    nc = seq // L
    H = MLSTM_HEADS
    t = batch * seq
    row = lambda b, c: b * nc + c
    return pl.pallas_call(
        _mlstm_body,
        grid=(batch, nc),
        in_specs=[pl.BlockSpec((L, W), lambda b, c: (row(b, c), 0)),
                  pl.BlockSpec((L, W), lambda b, c: (row(b, c), 1)),
                  pl.BlockSpec((L, W), lambda b, c: (row(b, c), 2)),
                  pl.BlockSpec((L, W), lambda b, c: (row(b, c), COL_O // W)),
                  pl.BlockSpec((L, N_GATE_PAD), lambda b, c: (row(b, c), 0)),
                  pl.BlockSpec((8, L), lambda b, c: (0, row(b, c))),
                  pl.BlockSpec((1, N_GATE_PAD), lambda b, c: (0, 0)),
                  pl.BlockSpec((8, 1), lambda b, c: (0, 0)),
                  pl.BlockSpec((CONV_WIDTH, 2 * W), lambda b, c: (0, 0)),
                  pl.BlockSpec((1, 2 * W), lambda b, c: (0, 0)),
                  pl.BlockSpec((1, W), lambda b, c: (0, 0))],
        out_specs=pl.BlockSpec((L, W), lambda b, c: (row(b, c), 0)),
        out_shape=SDS((t, W), BF16),
        scratch_shapes=[pltpu.VMEM((H, E, E + LANES), F32), pltpu.VMEM((H, 1, 1), F32),
                        pltpu.VMEM((2 * H, 8, E), F32)],
        compiler_params=_params(("parallel", "arbitrary")),
        name="mlstm",
    )(proj, proj, proj, proj, gif, gif_t, bcol, brow, conv_w, conv_b, ng)


def _bias_body(table_ref, o_ref):
    h = pl.program_id(0)
    dil = jnp.where(h < ATTN_HEADS, ATTN_GROUPS[0][1],
                    jnp.where(h < 2 * ATTN_HEADS, ATTN_GROUPS[1][1], ATTN_GROUPS[2][1]))
    shape = (ATTN_BLOCK, 2 * ATTN_BLOCK)
    i = lax.broadcasted_iota(jnp.int32, shape, 0)
    j = lax.broadcasted_iota(jnp.int32, shape, 1)
    rel = ATTN_BLOCK + i - j
    dist = jnp.maximum(rel, 0) * dil
    max_exact = REL_BUCKETS // 2
    nf = jnp.maximum(dist, max_exact).astype(F32)
    large = max_exact + (jnp.log(nf / max_exact) / math.log(REL_MAX_DISTANCE / max_exact)
                         * (REL_BUCKETS - max_exact)).astype(jnp.int32)
    large = jnp.minimum(large, REL_BUCKETS - 1)
    bucket = jnp.where(dist < max_exact, dist, large)
    acc = jnp.zeros(shape, F32)
    for b in range(REL_BUCKETS):
        acc = jnp.where(bucket == b, table_ref[b, h], acc)
    o_ref[0] = jnp.where(rel >= 0, jnp.where(rel <= ATTN_BLOCK, acc, NEG), NEG)


def attn_bias(rel_bias):
    return pl.pallas_call(
        _bias_body,
        grid=(ATTN_HEADS_TOTAL,),
        in_specs=[pl.BlockSpec(memory_space=pltpu.SMEM)],
        out_specs=pl.BlockSpec((1, ATTN_BLOCK, 2 * ATTN_BLOCK), lambda h: (h, 0, 0)),
        out_shape=SDS((ATTN_HEADS_TOTAL, ATTN_BLOCK, 2 * ATTN_BLOCK), F32),
        compiler_params=_params(("arbitrary",)),
        name="attn_bias",
    )(rel_bias)


def _attn_body(q_ref, kp_ref, kc_ref, vp_ref, vc_ref, bias_ref, o_ref, lse_ref):
    n = pl.program_id(2)
    B = ATTN_BLOCK
    E = ATTN_HEAD_DIM
    H = ATTN_HEADS
    NQ = q_ref.shape[2] // B
    scale = E ** -0.5
    hs = [slice(h * E, (h + 1) * E) for h in range(H)]

    def keys(cur_ref, prev_ref, i, sl):
        if i == 0:
            return jnp.concatenate([prev_ref[0, 0, :, sl], cur_ref[0, 0, :B, sl]], axis=0)
        return cur_ref[0, 0, (i - 1) * B:(i + 1) * B, sl]

    key = lax.broadcasted_iota(jnp.int32, (1, 2 * B), 1)
    no_prev = jnp.where(key < B, jnp.where(n > 0, 0.0, NEG), 0.0)
    logits, maxima = [], []
    for i in range(NQ):
        for h, sl in enumerate(hs):
            s = _dot_nt(q_ref[0, 0, i * B:(i + 1) * B, sl], keys(kc_ref, kp_ref, i, sl)) * scale + bias_ref[h]
            if i == 0:
                s = s + no_prev
            logits.append(s)
            maxima.append(jnp.max(s, axis=1, keepdims=True))
    lane = lax.broadcasted_iota(jnp.int32, (B, LANES), 1)
    ones = jnp.ones((2 * B, LANES), BF16)
    for i in range(NQ):
        lse = jnp.zeros((B, LANES), F32)
        for h, sl in enumerate(hs):
            s, m = logits[i * H + h], maxima[i * H + h]
            p = jnp.exp(s - m).astype(BF16)
            v_aug = jnp.concatenate([keys(vc_ref, vp_ref, i, sl), ones], axis=1)
            o_aug = _dot(p, v_aug)
            den = o_aug[:, E:]
            o = o_aug[:, :E] * (1.0 / den)
            o_ref[0, 0, i * B:(i + 1) * B, sl] = o.astype(o_ref.dtype)
            lse = jnp.where(lane == h, m + jnp.log(den), lse)
        lse_ref[0, 0, i * B:(i + 1) * B, :] = lse


def attn_group(qkv, bias, g):
    batch, r, l, _ = qkv.shape
    W = ATTN_WIDTH
    nq = min(ATTN_Q_BLOCKS, l // ATTN_BLOCK)
    rows = nq * ATTN_BLOCK
    cur = (1, 1, rows, W)
    one = (1, 1, ATTN_BLOCK, W)
    prev = lambda n: jnp.maximum(n * nq - 1, 0)
    return pl.pallas_call(
        _attn_body,
        grid=(batch, r, l // rows),
        in_specs=[pl.BlockSpec(cur, lambda b, c, n: (b, c, n, 0)),
                  pl.BlockSpec(one, lambda b, c, n: (b, c, prev(n), 1)),
                  pl.BlockSpec(cur, lambda b, c, n: (b, c, n, 1)),
                  pl.BlockSpec(one, lambda b, c, n: (b, c, prev(n), 2)),
                  pl.BlockSpec(cur, lambda b, c, n: (b, c, n, 2)),
                  pl.BlockSpec((ATTN_HEADS, ATTN_BLOCK, 2 * ATTN_BLOCK), lambda b, c, n: (g, 0, 0))],
        out_specs=[pl.BlockSpec(cur, lambda b, c, n: (b, c, n, 0)),
                   pl.BlockSpec((1, 1, rows, LANES), lambda b, c, n: (b, c, n, 0))],
        out_shape=[SDS((batch, r, l, W), BF16), SDS((batch, r, l, LANES), F32)],
        compiler_params=_params(("parallel", "parallel", "arbitrary")),
        name=f"attn_g{g}",
    )(qkv, qkv, qkv, qkv, qkv, bias)


def _s5_prep_body(lam_ref, lamc_ref, dt_ref, bt_r_ref, bt_i_ref, ce_r_ref, ce_i_ref, d_ref,
                  t_ref, w_ref, v_ref, coef_ref):
    P = S5_STATE
    lr = lam_ref[0, 0:1, :]
    li = lam_ref[0, 1:2, :]
    dt = jnp.exp(dt_ref[0])

    def apow(e, lr_, li_):
        mag = jnp.exp(lr_ * dt * e)
        ang = li_ * dt * e
        return mag * jnp.cos(ang), mag * jnp.sin(ang)

    one = jnp.ones((1, 1), F32)
    ar, ai = apow(one, lr, li)
    nr = ar - 1.0
    den = lr * lr + li * li
    f_re = (nr * lr + ai * li) / den
    f_im = (ai * lr - nr * li) / den
    bt_r = bt_r_ref[0]
    bt_i = bt_i_ref[0]
    bb_r = f_re * bt_r - f_im * bt_i
    bb_i = f_re * bt_i + f_im * bt_r

    lrc = lamc_ref[0, :, 0:1]
    lic = lamc_ref[0, :, 1:2]
    lag = (lax.broadcasted_iota(jnp.int32, (P, 256), 1) // S5_GROUP).astype(F32)
    adr, adi = apow(lag, lrc, lic)
    ce_r = ce_r_ref[0]
    ce_i = ce_i_ref[0]
    ca_r = ce_r * adr - ce_i * adi
    ca_i = ce_r * adi + ce_i * adr
    hp = lax.Precision.HIGHEST
    ks = (jnp.dot(bb_r[:, :P], ca_r, precision=hp, preferred_element_type=F32)
          - jnp.dot(bb_i[:, :P], ca_i, precision=hp, preferred_element_type=F32))
    si = lax.broadcasted_iota(jnp.int32, (S5_GROUP, 256), 0)
    lj = lax.broadcasted_iota(jnp.int32, (S5_GROUP, 256), 1)
    ks = ks + jnp.where(si == lj, d_ref[0], 0.0)
    for s in range(S5_TOK):
        sh = s * S5_GROUP
        blk = ks if s == 0 else jnp.where(lj >= sh, pltpu.roll(ks, sh, 1), 0.0)
        t_ref[0, sh:sh + S5_GROUP, :] = blk.astype(t_ref.dtype)

    half = pl.program_id(0) % 2
    lane128 = lax.broadcasted_iota(jnp.int32, (1, 128), 1)
    mine = (lane128 // P) == half

    pw_r, pw_i = apow(lax.broadcasted_iota(jnp.int32, (S5_TOK, 1), 0).astype(F32), lr, li)
    for s in range(S5_TOK):
        e = S5_TOK - 1 - s
        pr, pi = pw_r[e:e + 1], pw_i[e:e + 1]
        rows = slice(s * S5_GROUP, (s + 1) * S5_GROUP)
        w_ref[0, rows, 0:128] = jnp.where(mine, bb_r * pr - bb_i * pi, 0.0).astype(w_ref.dtype)
        w_ref[0, rows, 128:256] = jnp.where(mine, bb_r * pi + bb_i * pr, 0.0).astype(w_ref.dtype)

    a1r, a1i = adr[:, S5_GROUP:S5_GROUP + 1], adi[:, S5_GROUP:S5_GROUP + 1]
    adr1 = adr * a1r - adi * a1i
    adi1 = adr * a1i + adi * a1r
    v_ref[0] =jnp.zeros(v_ref.shape[1:], v_ref.dtype)
    row0 = pl.multiple_of(half * P, P)
    v_ref[0, pl.ds(row0, P), :] = (ce_r * adr1 - ce_i * adi1).astype(v_ref.dtype)
    row1 = pl.multiple_of(2 * P + half * P, P)
    v_ref[0, pl.ds(row1, P), :] = (-(ce_r * adi1 + ce_i * adr1)).astype(v_ref.dtype)

    ek = jnp.left_shift(S5_TOK, lax.broadcasted_iota(jnp.int32, (16, 1), 0)).astype(F32)
    cr, ci = apow(ek, lr, li)
    coef_ref[0, 0:16, :] = jnp.where(mine, cr, 0.0)
    coef_ref[0, 16:32, :] = jnp.where(mine, ci, 0.0)


def s5_prep(lam_re, lam_im, log_dt, b_re, b_im, c_re, c_im, d_skip):
    G, P, I = b_re.shape
    lam = jnp.stack([jnp.tile(lam_re, (1, 2)), jnp.tile(lam_im, (1, 2))], axis=1)
    lam = jnp.pad(lam, ((0, 0), (0, 6), (0, 0)))
    lamc = jnp.stack([lam_re, lam_im], axis=2)
    dt = log_dt.reshape(G, 1, 1)
    bt_r = jnp.tile(jnp.swapaxes(b_re, 1, 2), (1, 1, 2))
    bt_i = jnp.tile(jnp.swapaxes(b_im, 1, 2), (1, 1, 2))
    ce_r = jnp.tile(jnp.swapaxes(c_re, 1, 2), (1, 1, S5_TOK))
    ce_i = jnp.tile(jnp.swapaxes(c_im, 1, 2), (1, 1, S5_TOK))
    d = jnp.pad(d_skip, ((0, 0), (0, 256 - I))).reshape(G, 1, 256)
    blk = lambda *s: pl.BlockSpec((1,) + s, lambda g: (g, 0, 0))
    return pl.pallas_call(
        _s5_prep_body,
        grid=(G,),
        in_specs=[blk(8, 128), blk(P, 2), blk(1, 1), blk(16, 128), blk(16, 128), blk(P, 256), blk(P, 256),
                  blk(1, 256)],
        out_specs=[blk(256, 256), blk(256, 256), blk(256, 256), blk(32, 128)],
        out_shape=[SDS((G, 256, 256), BF16), SDS((G, 256, 256), BF16), SDS((G, 256, 256), BF16),
                   SDS((G, 32, 128), F32)],
        compiler_params=_params(("arbitrary",)),
        name="s5_prep",
    )(lam, lamc, dt, bt_r, bt_i, ce_r, ce_i, d)


def _gelu_tanh(x):
    return 0.5 * x * (1.0 + jnp.tanh(math.sqrt(2.0 / math.pi) * (x + 0.044715 * (x * x * x))))


def _chunk_transpose(arrs, chunk):
    a = list(arrs)
    n = len(a)
    d = n // 2
    while d >= 1:
        bit = (chunk & d) != 0
        nxt = list(a)
        for i in range(n):
            if i & d == 0:
                lo, hi = a[i], a[i + d]
                nxt[i] = jnp.where(bit, pltpu.roll(hi, d * S5_GROUP, 1), lo)
                nxt[i + d] = jnp.where(bit, hi, pltpu.roll(lo, LANES - d * S5_GROUP, 1))
        a = nxt
        d //= 2
    return a


def _s5_body(u_ref, t_ref, w_ref, v_ref, coef_ref, z_ref, nat_ref, uf_ref, zs_ref):
    S = u_ref.shape[0]
    R = S // S5_TOK
    GL = LANES // S5_GROUP
    RC = min(R, 256)
    chunk = lax.broadcasted_iota(jnp.int32, (RC, LANES), 1) // S5_GROUP

    nat_ref[...] = u_ref[...].astype(F32)

    def to_flat(rc, carry):
        r0 = pl.multiple_of(rc * RC, RC)
        for hf in range(S5_TOK // GL):
            arrs = [nat_ref[pl.ds(r0 * S5_TOK + hf * GL + k, RC, stride=S5_TOK), :] for k in range(GL)]
            for gl, a in enumerate(_chunk_transpose(arrs, chunk)):
                uf_ref[gl, pl.ds(r0, RC), hf * LANES:(hf + 1) * LANES] = a.astype(uf_ref.dtype)
        return carry

    lax.fori_loop(0, R // RC, to_flat, 0)

    row = lax.broadcasted_iota(jnp.int32, (R, LANES), 0)

    def shift_down(a, sh):
        if sh % 8 == 0:
            return jnp.concatenate([jnp.zeros((sh, LANES), F32), a[:R - sh]], axis=0)
        return jnp.where(row < sh, 0.0, pltpu.roll(a, sh, 0))

    def pair(p, carry):
        g0 = 2 * p
        g1 = g0 + 1
        u0 = uf_ref[g0]
        u1 = uf_ref[g1]
        s2 = _dot(u0, w_ref[g0]) + _dot(u1, w_ref[g1])
        xr = s2[:, :LANES]
        xi = s2[:, LANES:]
        k = 0
        while (1 << k) < R:
            cr = coef_ref[g0, k:k + 1, :] + coef_ref[g1, k:k + 1, :]
            ci = coef_ref[g0, 16 + k:17 + k, :] + coef_ref[g1, 16 + k:17 + k, :]
            xrs = shift_down(xr, 1 << k)
            xis = shift_down(xi, 1 << k)
            xr, xi = xr + cr * xrs - ci * xis, xi + cr * xis + ci * xrs
            k += 1
        xp = jnp.concatenate([shift_down(xr, 1), shift_down(xi, 1)], axis=1).astype(BF16)
        zs_ref[g0] = _gelu_tanh(_dot(u0, t_ref[g0]) + _dot(xp, v_ref[g0]))
        zs_ref[g1] = _gelu_tanh(_dot(u1, t_ref[g1]) + _dot(xp, v_ref[g1]))
        return carry

    lax.fori_loop(0, GL // 2, pair, 0)

    def to_nat(rc, carry):
        r0 = pl.multiple_of(rc * RC, RC)
        for hf in range(S5_TOK // GL):
            arrs = [zs_ref[gl, pl.ds(r0, RC), hf * LANES:(hf + 1) * LANES] for gl in range(GL)]
            for k, a in enumerate(_chunk_transpose(arrs, chunk)):
                nat_ref[pl.ds(r0 * S5_TOK + hf * GL + k, RC, stride=S5_TOK), :] = a
        return carry

    lax.fori_loop(0, R // RC, to_nat, 0)
    z_ref[...] = nat_ref[...].astype(z_ref.dtype)


def s5_scan(proj, tm, wm, vm, coef, batch, seq):
    t = proj.shape[0]
    GL = LANES // S5_GROUP
    r = seq // S5_TOK
    ub = COL_U // LANES
    per_g = lambda *s: pl.BlockSpec((GL,) + s, lambda bi, si: (si, 0, 0))
    return pl.pallas_call(
        _s5_body,
        grid=(batch, S5_WIDTH // LANES),
        in_specs=[pl.BlockSpec((seq, LANES), lambda bi, si: (bi, ub + si)),
                  per_g(256, 256), per_g(256, 256), per_g(256, 256), per_g(32, 128)],
        out_specs=pl.BlockSpec((seq, LANES), lambda bi, si: (bi, si)),
        out_shape=SDS((t, S5_WIDTH), BF16),
        scratch_shapes=[pltpu.VMEM((seq, LANES), F32), pltpu.VMEM((GL, r, 2 * LANES), BF16),
                        pltpu.VMEM((GL, r, 2 * LANES), F32)],
        compiler_params=_params(("parallel", "parallel")),
        name="s5_scan",
    )(proj, tm, wm, vm, coef)


def _glu_body(z_ref, wl_ref, wg_ref, o_ref):
    z = z_ref[...]
    o_ref[...] = (_dot(z, wl_ref[...]) * _sigmoid(_dot(z, wg_ref[...]))).astype(o_ref.dtype)


def glu(z, w, layer, tm, tn):
    t, k = z.shape
    n = w.shape[2] // 2
    nj = n // tn
    return pl.pallas_call(
        _glu_body,
        grid=(t // tm, nj),
        in_specs=[pl.BlockSpec((tm, k), lambda i, j: (i, 0)),
                  pl.BlockSpec((None, k, tn), lambda i, j: (layer, 0, j)),
                  pl.BlockSpec((None, k, tn), lambda i, j: (layer, 0, nj + j))],
        out_specs=pl.BlockSpec((tm, tn), lambda i, j: (i, j)),
        out_shape=SDS((t, n), BF16),
        compiler_params=_params(("parallel", "arbitrary")),
        name="glu",
    )(z, w, w)


def _mix_body(ya_ref, o0_ref, o1_ref, o2_ref, l0_ref, l1_ref, l2_ref, yc_ref,
              wa_ref, wb_ref, wc_ref, ga_ref, gb_ref, gc_ref, out_ref, yb_ref, wt_ref, acc_ref):
    tm = ya_ref.shape[0]
    E = ATTN_HEAD_DIM

    @pl.when(pl.program_id(1) == 0)
    def _():
        o_refs = (o0_ref, o1_ref, o2_ref)
        l_refs = (l0_ref, l1_ref, l2_ref)
        for g, (_, r) in enumerate(ATTN_GROUPS):
            for c in range(r):
                wt_ref[g, pl.ds(c, tm // r, stride=r), :] = l_refs[g][0, c]
        l0, l1, l2 = wt_ref[0], wt_ref[1], wt_ref[2]
        m = jnp.maximum(jnp.maximum(l0, l1), l2)
        e0, e1, e2 = jnp.exp(l0 - m), jnp.exp(l1 - m), jnp.exp(l2 - m)
        inv = 1.0 / (e0 + e1 + e2)
        wt_ref[0] = e0 * inv
        wt_ref[1] = e1 * inv
        wt_ref[2] = e2 * inv
        order = sorted(range(len(ATTN_GROUPS)), key=lambda g: -ATTN_GROUPS[g][1])
        for pos, g in enumerate(order[:-1]):
            r = ATTN_GROUPS[g][1]
            for c in range(r):
                rows = pl.ds(c, tm // r, stride=r)
                w = wt_ref[g, rows, :]
                for h in range(ATTN_HEADS):
                    part = w[:, h:h + 1] * o_refs[g][0, c, :, h * E:(h + 1) * E].astype(F32)
                    if pos == 0:
                        acc_ref[h, rows, :] = part
                    else:
                        acc_ref[h, rows, :] += part
        g = order[-1]
        assert ATTN_GROUPS[g][1] == 1
        w = wt_ref[g]
        for h in range(ATTN_HEADS):
            sl = slice(h * E, (h + 1) * E)
            yb = acc_ref[h] + w[:, h:h + 1] * o_refs[g][0, 0, :, sl].astype(F32)
            yb_ref[:, sl] = yb.astype(yb_ref.dtype)

    tn = out_ref.shape[1]
    cols = pl.ds(pl.multiple_of(pl.program_id(1) * tn, tn), tn)
    mix = (_sigmoid(ga_ref[...].astype(F32)) * _dot(ya_ref[...], wa_ref[:, cols])
           + _sigmoid(gb_ref[...].astype(F32)) * _dot(yb_ref[...], wb_ref[:, cols])
           + _sigmoid(gc_ref[...].astype(F32)) * _dot(yc_ref[...], wc_ref[:, cols]))
    out_ref[...] = mix.astype(out_ref.dtype)


def gated_mix(ya, outs, lses, yc, wa, wb, wc, layer, proj, seq, tm, tn):
    t, kw = ya.shape
    d = wa.shape[2]
    go = COL_G // tn
    nbt = seq // tm
    row = lambda w: pl.BlockSpec((tm, w), lambda i, j: (i, 0))
    grp = lambda r, w: pl.BlockSpec((1, r, tm // r, w), lambda i, j: (i // nbt, 0, i % nbt, 0))
    wsp = pl.BlockSpec((None, kw, d), lambda i, j: (layer, 0, 0), pipeline_mode=pl.Buffered(1))
    gate = lambda o: pl.BlockSpec((tm, tn), lambda i, j: (i, go + o * (d // tn) + j))
    dils = [r for _, r in ATTN_GROUPS]
    return pl.pallas_call(
        _mix_body,
        grid=(t // tm, d // tn),
        in_specs=[row(kw)] + [grp(r, kw) for r in dils] + [grp(r, LANES) for r in dils] + [row(kw)]
        + [wsp, wsp, wsp, gate(0), gate(1), gate(2)],
        out_specs=pl.BlockSpec((tm, tn), lambda i, j: (i, j)),
        out_shape=SDS((t, d), BF16),
        scratch_shapes=[pltpu.VMEM((tm, kw), BF16), pltpu.VMEM((len(dils), tm, LANES), F32),
                        pltpu.VMEM((ATTN_HEADS, tm, LANES), F32)],
        compiler_params=_params(("parallel", "arbitrary")),
        name="gated_mix",
    )(ya, *outs, *lses, yc, wa, wb, wc, proj, proj, proj)


def _out_ffn_body(final, x_ref, m_ref, wo_ref, g_ref, w1_ref, w2_ref, fg_ref, o_ref, xn_ref):
    j = pl.program_id(1)

    @pl.when(j == 0)
    def _():
        x1 = x_ref[...] + _dot(m_ref[...], wo_ref[...])
        xn_ref[...] = _rms(x1, g_ref[...]).astype(BF16)
        o_ref[...] = x1

    hid = jnp.maximum(_dot(xn_ref[...], w1_ref[...]), 0.0)
    o_ref[...] += _dot((hid * hid).astype(BF16), w2_ref[...])

    if final:
        @pl.when(j == pl.num_programs(1) - 1)
        def _():
            o_ref[...] = _rms(o_ref[...], fg_ref[...])


def out_ffn(x2d, mix, w_out, g, w1, w2, layer, final_g, final, tm, th):
    t, d = x2d.shape
    hdim = w1.shape[2]
    const = lambda shape: pl.BlockSpec(shape, lambda i, j: (0, 0))
    return pl.pallas_call(
        functools.partial(_out_ffn_body, final),
        grid=(t // tm, hdim // th),
        in_specs=[pl.BlockSpec((tm, d), _early_rows(t // tm, hdim // th // 3)),
                  pl.BlockSpec((tm, d), _early_rows(t // tm, 2 * (hdim // th) // 3)),
                  pl.BlockSpec((None, d, d), lambda i, j: (layer, 0, 0), pipeline_mode=pl.Buffered(1)),
                  const((1, d)),
                  pl.BlockSpec((None, d, th), lambda i, j: (layer, 0, j)),
                  pl.BlockSpec((None, th, d), lambda i, j: (layer, j, 0)),
                  const((1, d))],
        out_specs=pl.BlockSpec((tm, d), lambda i, j: (i, 0)),
        out_shape=SDS((t, d), F32),
        scratch_shapes=[pltpu.VMEM((tm, d), BF16)],
        compiler_params=_params(("parallel", "arbitrary")),
        name="out_ffn",
    )(x2d, mix, w_out, g, w1, w2, final_g)


def _tile(n, want):
    t = min(n, want)
    assert n % t == 0, (n, want)
    return t


def _projection_weights(w_in):
    n_gate0 = 4 * MLSTM_WIDTH
    n_att0 = n_gate0 + 2 * MLSTM_HEADS
    n_att = 3 * ATTN_HEADS_TOTAL * ATTN_HEAD_DIM
    n_u0 = n_att0 + n_att
    w_main =jnp.concatenate([w_in[:, :, :COL_U], w_in[:, :, n_u0:n_u0 + S5_WIDTH], w_in[:, :, COL_U:n_gate0],
                              w_in[:, :, n_u0 + S5_WIDTH:], w_in[:, :, n_att0:n_u0]], axis=2).astype(BF16)
    w_if = jnp.pad(w_in[:, :, n_gate0:n_att0], ((0, 0), (0, 0), (0, N_GATE_PAD - 2 * MLSTM_HEADS))).astype(BF16)
    return w_main, w_if


def _layer(x2d, bias, batch, seq, layer, p, w, final_g, final):
    t, d = x2d.shape
    tm = _tile(seq, 1024)
    n_groups = len(ATTN_GROUPS)
    proj, gif, gif_t, *qkvs = in_proj(x2d, p["norm1_g"].reshape(1, d), w["w_main"], w["w_if"], layer,
                                      batch, seq, tm)

    bif = jnp.concatenate([p["b_igate"], p["b_fgate"]]).astype(F32)
    bcol = jnp.pad(bif, (0, N_GATE_PAD - bif.shape[0])).reshape(1, N_GATE_PAD)
    ya = mlstm(proj, gif, gif_t, bcol, bif.reshape(8, 1), p["conv_w"], p["conv_b"].reshape(1, -1),
               p["mh_norm_g"].reshape(1, -1), batch, seq)

    outs, lses = zip(*[attn_group(qkvs[g], bias, g) for g in range(n_groups)])

    tmat, wmat, vmat, coef = s5_prep(p["lam_re"], p["lam_im"], p["log_dt"], p["b_re"], p["b_im"],
                                     p["c_re"], p["c_im"], p["d_skip"])
    z = s5_scan(proj, tmat, wmat, vmat, coef, batch, seq)
    yc = glu(z, w["w_glu"], layer, tm, 1024)

    mix = gated_mix(ya, outs, lses, yc, w["w_br_a"], w["w_br_b"], w["w_br_c"], layer, proj, seq,
                    _tile(seq, 512), 1024)
    return out_ffn(x2d, mix, w["w_out"], p["norm2_g"].reshape(1, d), w["w_ff1"], w["w_ff2"], layer,
                   final_g.reshape(1, d), final, _tile(t, 512), 1024)


_SMALL = ("norm1_g", "conv_w", "conv_b", "b_igate", "b_fgate", "mh_norm_g", "lam_re", "lam_im", "log_dt",
          "b_re", "b_im", "c_re", "c_im", "d_skip", "norm2_g")


def kernel(x, norm1_g, w_in, conv_w, conv_b, b_igate, b_fgate, mh_norm_g, rel_bias, lam_re, lam_im, log_dt,
           b_re, b_im, c_re, c_im, d_skip, w_glu, w_br_a, w_br_b, w_br_c, w_out, norm2_g, w_ff1, w_ff2,
           final_g):
    small = dict(norm1_g=norm1_g, conv_w=conv_w, conv_b=conv_b, b_igate=b_igate, b_fgate=b_fgate,
                 mh_norm_g=mh_norm_g, lam_re=lam_re, lam_im=lam_im, log_dt=log_dt, b_re=b_re, b_im=b_im,
                 c_re=c_re, c_im=c_im, d_skip=d_skip, norm2_g=norm2_g)
    w_main, w_if = _projection_weights(w_in)
    weights = dict(w_main=w_main, w_if=w_if, w_glu=w_glu.astype(BF16), w_br_a=w_br_a.astype(BF16),
                   w_br_b=w_br_b.astype(BF16), w_br_c=w_br_c.astype(BF16), w_out=w_out.astype(BF16),
                   w_ff1=w_ff1.astype(BF16), w_ff2=w_ff2.astype(BF16))
    batch, seq, d = x.shape
    x2d = x.astype(F32).reshape(batch * seq, d)
    bias = attn_bias(rel_bias.astype(F32))
    depth = w_in.shape[0]
    for l in range(depth):
        x2d = _layer(x2d, bias, batch, seq, l, {k: small[k][l] for k in _SMALL}, weights, final_g,
                     l == depth - 1)
    return x2d.reshape(batch, seq, d).astype(x.dtype)
```
